```python
import jax, jax.numpy as jnp
from jax import lax
import numpy as np


D_MODEL = 1024
BATCH = 2
SEQ = 8192
DEPTH = 2

RMS_EPS = 1e-6
BRANCH_WIDTH = 512
N_BRANCH = 3
A_HEADS = 8
A_HEAD_DIM = 64
A_WIDTH = A_HEADS * A_HEAD_DIM
Q_BLOCK = 128
POOL_WINDOWS = (2, 4, 8, 16)
B_GROUPS = 4
B_GROUP_DIM = 128
B_WIDTH = B_GROUPS * B_GROUP_DIM
C_GROUPS = 4
C_GROUP_DIM = 128
C_WIDTH = C_GROUPS * C_GROUP_DIM
C_CHUNK = 128
OFF_Q = 0
OFF_K = OFF_Q + A_WIDTH
OFF_V = OFF_K + A_WIDTH
OFF_F = OFF_V + A_WIDTH
OFF_P = OFF_F + A_HEADS
OFF_U = OFF_P + B_WIDTH
OFF_SV = OFF_U + C_WIDTH
OFF_G = OFF_SV + C_WIDTH
D_IN = OFF_G + N_BRANCH * D_MODEL
N_EXPERT_GROUPS = 4
EXPERTS_PER_GROUP = 8
N_EXPERTS = N_EXPERT_GROUPS * EXPERTS_PER_GROUP
TOP_K = 2
D_EXPERT = 256
MOE_BLOCK = 256

kernel_name = 'hybrid_fox_pool_sgu_hiermoe'


def rms_norm(x, g):
    xf = x.astype(jnp.float32)
    y = xf * lax.rsqrt(jnp.mean(xf * xf, axis=-1, keepdims=True) + RMS_EPS)
    return (y * g.astype(jnp.float32)).astype(x.dtype)


def forgetting_attention(q, k, v, log_f):
    S = q.shape[2]
    c = jnp.cumsum(log_f, axis=-1)
    scale = A_HEAD_DIM ** -0.5
    outs = []
    for blk in range(S // Q_BLOCK):
        q0 = blk * Q_BLOCK
        q1 = q0 + Q_BLOCK
        qb = q[:, :, q0:q1]
        kb = k[:, :, :q1]
        vb = v[:, :, :q1]
        s = jnp.einsum('bhqd,bhkd->bhqk', qb, kb).astype(jnp.float32) * scale
        s = s + c[:, :, q0:q1, None] - c[:, :, None, :q1]
        mask = (q0 + jnp.arange(Q_BLOCK))[:, None] >= jnp.arange(q1)[None, :]
        s = jnp.where(mask, s, -jnp.inf)
        p = jax.nn.softmax(s, axis=-1)
        outs.append(jnp.einsum('bhqk,bhkd->bhqd', p.astype(vb.dtype), vb))
    return jnp.concatenate(outs, axis=2)


def multiscale_pool(p, w_pool, s_pool):
    B, S, _ = p.shape
    pf = p.astype(jnp.float32)
    cs = jnp.concatenate([jnp.zeros((B, 1, B_WIDTH), jnp.float32), jnp.cumsum(pf, axis=1)], axis=1)
    t = jnp.arange(S)
    outs = []
    for g, w in enumerate(POOL_WINDOWS):
        sl = slice(g * B_GROUP_DIM, (g + 1) * B_GROUP_DIM)
        lo = jnp.maximum(t + 1 - w, 0)
        cnt = jnp.minimum(t + 1, w).astype(jnp.float32)
        win_mean = (cs[:, 1:, sl] - cs[:, lo, sl]) / cnt[None, :, None]
        pooled = (win_mean - pf[:, :, sl]).astype(p.dtype)
        outs.append(jnp.einsum('bsc,cd->bsd', pooled, w_pool[g]))
    return jnp.concatenate(outs, axis=-1) * s_pool


def spatial_gating(z, g_v, w_s, b_s):
    B, S, _ = z.shape
    u = z[..., :C_WIDTH]
    vv = z[..., C_WIDTH:].reshape(B, S // C_CHUNK, C_CHUNK, C_GROUPS, C_GROUP_DIM)
    vv = rms_norm(vv, g_v.reshape(C_GROUPS, C_GROUP_DIM))
    causal = jnp.tril(jnp.ones((C_CHUNK, C_CHUNK), w_s.dtype))
    w_c = w_s * causal
    mixed = jnp.einsum('gts,bnsgc->bntgc', w_c, vv) + b_s.T[None, None, :, :, None]
    return u * mixed.reshape(B, S, C_WIDTH)


def hybrid_mixer(h, w_in, b_fgate, b_gate, g_q, g_k, w_pool, s_pool, g_sgu, w_sgu, b_sgu, w_branch, w_out):
    B, S, _ = h.shape
    z = h @ w_in

    def heads(t):
        return t.reshape(B, S, A_HEADS, A_HEAD_DIM).transpose(0, 2, 1, 3)

    q = rms_norm(heads(z[..., OFF_Q:OFF_K]), g_q)
    k = rms_norm(heads(z[..., OFF_K:OFF_V]), g_k)
    v = heads(z[..., OFF_V:OFF_F])
    log_f = jax.nn.log_sigmoid((z[..., OFF_F:OFF_P] + b_fgate).astype(jnp.float32)).transpose(0, 2, 1)
    y_a = forgetting_attention(q, k, v, log_f).transpose(0, 2, 1, 3).reshape(B, S, A_WIDTH)
    y_b = multiscale_pool(z[..., OFF_P:OFF_U], w_pool, s_pool)
    y_c = spatial_gating(jax.nn.gelu(z[..., OFF_U:OFF_G]), g_sgu, w_sgu, b_sgu)
    gates = jax.nn.sigmoid(z[..., OFF_G:].reshape(B, S, N_BRANCH, D_MODEL) + b_gate)
    merged = (gates[:, :, 0] * (y_a @ w_branch[0])
              + gates[:, :, 1] * (y_b @ w_branch[1])
              + gates[:, :, 2] * (y_c @ w_branch[2]))
    return merged @ w_out


def hierarchical_moe(h, w_rg, b_rg, w_re, b_re, w1, w3, w2):
    B, S, D = h.shape
    N = B * S
    xt = h.reshape(N, D)
    lg = (xt @ w_rg).astype(jnp.float32) + b_rg
    pg = jax.nn.softmax(lg, axis=-1)
    grp = jnp.argmax(lg, axis=-1)
    p_grp = jnp.take_along_axis(pg, grp[:, None], axis=-1)
    le = ((xt @ w_re).astype(jnp.float32) + b_re).reshape(N, N_EXPERT_GROUPS, EXPERTS_PER_GROUP)
    le = jnp.take_along_axis(le, grp[:, None, None], axis=1)[:, 0]
    top_p, top_i = lax.top_k(jax.nn.softmax(le, axis=-1), TOP_K)
    gate = p_grp * (top_p / jnp.sum(top_p, axis=-1, keepdims=True))
    expert = grp[:, None] * EXPERTS_PER_GROUP + top_i

    A = N * TOP_K
    e_flat = expert.reshape(A).astype(jnp.int32)
    g_flat = gate.reshape(A)
    tok_flat = (jnp.arange(A, dtype=jnp.int32) // TOP_K)
    order = jnp.argsort(e_flat)
    e_sorted = e_flat[order]
    counts = jnp.bincount(e_flat, length=N_EXPERTS).astype(jnp.int32)
    start = jnp.cumsum(counts) - counts
    padded = (counts + MOE_BLOCK - 1) // MOE_BLOCK * MOE_BLOCK
    pend = jnp.cumsum(padded)
    pstart = pend - padded
    dest = pstart[e_sorted] + (jnp.arange(A, dtype=jnp.int32) - start[e_sorted])
    n_blocks = -(-(A + N_EXPERTS * (MOE_BLOCK - 1)) // MOE_BLOCK)
    P = n_blocks * MOE_BLOCK
    slot_tok = jnp.full((P,), N, jnp.int32).at[dest].set(tok_flat[order])
    slot_gate = jnp.zeros((P,), g_flat.dtype).at[dest].set(g_flat[order])
    block_start = jnp.arange(n_blocks, dtype=jnp.int32) * MOE_BLOCK
    block_expert = jnp.minimum(jnp.searchsorted(pend, block_start, side='right'), N_EXPERTS - 1)
    x_pad = jnp.concatenate([xt, jnp.zeros((1, D), xt.dtype)], axis=0)
    xs = x_pad[slot_tok].reshape(n_blocks, MOE_BLOCK, D)

    def expert_block(args):
        xb, e = args
        return (jax.nn.silu(xb @ w1[e]) * (xb @ w3[e])) @ w2[e]

    yb = lax.map(expert_block, (xs, block_expert))
    y = yb.reshape(P, D) * slot_gate[:, None].astype(yb.dtype)
    out = jax.ops.segment_sum(y, slot_tok, num_segments=N + 1)[:N]
    return out.reshape(B, S, D)


def setup_inputs(seed: int = 0) -> dict:
    key = jax.random.key(seed)
    ks = jax.random.split(key, 24)
    f32 = jnp.float32

    def nrm(k, shape, fan_in):
        return jax.random.normal(k, shape, f32) * (fan_in ** -0.5)

    def gain(k, shape):
        return 1.0 + 0.02 * jax.random.normal(k, shape, f32)

    return {
        'x': jax.random.normal(ks[0], (BATCH, SEQ, D_MODEL), f32),
        'g_mix': gain(ks[1], (DEPTH, D_MODEL)),
        'w_in': nrm(ks[2], (DEPTH, D_MODEL, D_IN), D_MODEL),
        'b_fgate': jax.random.uniform(ks[3], (DEPTH, A_HEADS), f32, 1.0, 5.0),
        'b_gate': 0.02 * jax.random.normal(ks[4], (DEPTH, N_BRANCH, D_MODEL), f32),
        'g_q': gain(ks[5], (DEPTH, A_HEAD_DIM)),
        'g_k': gain(ks[6], (DEPTH, A_HEAD_DIM)),
        'w_pool': nrm(ks[7], (DEPTH, B_GROUPS, B_GROUP_DIM, B_GROUP_DIM), B_GROUP_DIM),
        's_pool': gain(ks[8], (DEPTH, B_WIDTH)),
        'g_sgu': gain(ks[9], (DEPTH, C_WIDTH)),
        'w_sgu': nrm(ks[10], (DEPTH, C_GROUPS, C_CHUNK, C_CHUNK), C_CHUNK),
        'b_sgu': gain(ks[11], (DEPTH, C_GROUPS, C_CHUNK)),
        'w_branch': nrm(ks[12], (DEPTH, N_BRANCH, BRANCH_WIDTH, D_MODEL), BRANCH_WIDTH),
        'w_out': nrm(ks[13], (DEPTH, D_MODEL, D_MODEL), D_MODEL),
        'g_ffn': gain(ks[14], (DEPTH, D_MODEL)),
        'w_rg': nrm(ks[15], (DEPTH, D_MODEL, N_EXPERT_GROUPS), D_MODEL),
        'b_rg': 0.01 * jax.random.normal(ks[16], (DEPTH, N_EXPERT_GROUPS), f32),
        'w_re': nrm(ks[17], (DEPTH, D_MODEL, N_EXPERTS), D_MODEL),
        'b_re': 0.01 * jax.random.normal(ks[18], (DEPTH, N_EXPERTS), f32),
        'w1': nrm(ks[19], (DEPTH, N_EXPERTS, D_MODEL, D_EXPERT), D_MODEL),
        'w3': nrm(ks[20], (DEPTH, N_EXPERTS, D_MODEL, D_EXPERT), D_MODEL),
        'w2': nrm(ks[21], (DEPTH, N_EXPERTS, D_EXPERT, D_MODEL), D_EXPERT),
    }


def reference(x, g_mix, w_in, b_fgate, b_gate, g_q, g_k, w_pool, s_pool, g_sgu, w_sgu, b_sgu,
              w_branch, w_out, g_ffn, w_rg, b_rg, w_re, b_re, w1, w3, w2):
    for l in range(DEPTH):
        h = rms_norm(x, g_mix[l])
        x = x + hybrid_mixer(h, w_in[l], b_fgate[l], b_gate[l], g_q[l], g_k[l], w_pool[l], s_pool[l],
                             g_sgu[l], w_sgu[l], b_sgu[l], w_branch[l], w_out[l])
        h = rms_norm(x, g_ffn[l])
        x = x + hierarchical_moe(h, w_rg[l], b_rg[l], w_re[l], b_re[l], w1[l], w3[l], w2[l])
    return x
```

```python
import functools

import jax
import jax.numpy as jnp
from jax import lax
from jax.experimental import pallas as pl
from jax.experimental.pallas import tpu as pltpu

F32 = jnp.float32
BF16 = jnp.bfloat16
I32 = jnp.int32

D_MODEL = 1024
A_HEADS = 8
A_HEAD_DIM = 64
BRANCH_WIDTH = 512
N_BRANCH = 3
POOL_WINDOWS = (2, 4, 8, 16)
GROUP_DIM = 128
N_GROUPS = 4
C_CHUNK = 128
N_EXPERT_GROUPS = 4
EXPERTS_PER_GROUP = 8
N_EXPERTS = N_EXPERT_GROUPS * EXPERTS_PER_GROUP
TOP_K = 2
D_EXPERT = 256
RMS_EPS = 1e-6
OFF_F = 3 * BRANCH_WIDTH
OFF_P = OFF_F + A_HEADS
OFF_U = OFF_P + BRANCH_WIDTH
OFF_SV = OFF_U + BRANCH_WIDTH
OFF_G = OFF_SV + BRANCH_WIDTH

LANES = 128
V7X_VMEM_BYTES = 64 * 1024 * 1024
VMEM_LIMIT = 48 * 1024 * 1024

PK_Q, PK_K, PK_V = 0, 512, 1024
PK_F = 1536
PK_P = PK_F + LANES
PK_U = PK_P + BRANCH_WIDTH
PK_SV = PK_U + BRANCH_WIDTH
PK_W = PK_SV + BRANCH_WIDTH
F_REP = 6

NEG = -1e30
HALO = 16


def _tiles(seq_len):
    t_proj = min(256, seq_len)
    t_q = min(512, seq_len)
    t_k = min(512, seq_len)
    moe_block = 256
    return t_proj, t_q, t_k, moe_block


def _rms(x, g):
    return x * lax.rsqrt(jnp.mean(x * x, axis=-1, keepdims=True) + RMS_EPS) * g


def _gelu_tanh(x):
    cdf = 0.5 * (1.0 + jnp.tanh(0.7978845608028654 * (x + 0.044715 * (x * x * x))))
    return x * cdf


def _log_sigmoid(x):
    return jnp.minimum(x, 0.0) - jnp.log1p(jnp.exp(-jnp.abs(x)))


def _split3(c):
    hi = c.astype(BF16).astype(F32)
    r = c - hi
    lo = r.astype(BF16).astype(F32)
    lolo = (r - lo).astype(BF16).astype(F32)
    return hi, lo, lolo


def _proj_kernel(x_ref, gmix_ref, w_ref, bf_ref, gq_ref, gk_ref, bd_ref, wpool_ref, spool_ref,
                 gsgu_ref, wsgu_ref, bsgu_ref,
                 qT_ref, k_ref, vT_ref, yb_ref, yc_ref,
                 carry_ref, halo_ref, *, T):
    i = pl.program_id(1)

    @pl.when(i == 0)
    def _():
        carry_ref[...] = jnp.zeros_like(carry_ref)
        halo_ref[...] = jnp.zeros_like(halo_ref)

    hb = _rms(x_ref[...], gmix_ref[...]).astype(BF16)

    def proj(lo, width):
        return jnp.dot(hb, w_ref[:, lo:lo + width], preferred_element_type=F32)

    bd = bd_ref[...]

    def head_norm(z, g):
        sq = z * z
        hi = sq.astype(BF16)
        lo = (sq - hi.astype(F32)).astype(BF16)
        ss = (jnp.dot(hi, bd, preferred_element_type=F32)
              + jnp.dot(lo, bd, preferred_element_type=F32))
        return z * lax.rsqrt(ss * (1.0 / A_HEAD_DIM) + RMS_EPS) * g

    qn = head_norm(proj(PK_Q, BRANCH_WIDTH), gq_ref[...]) * (A_HEAD_DIM ** -0.5)
    kn = head_norm(proj(PK_K, BRANCH_WIDTH), gk_ref[...])
    zv = proj(PK_V, BRANCH_WIDTH)

    logf = _log_sigmoid(proj(PK_F, LANES) + bf_ref[...])
    r_i = lax.broadcasted_iota(I32, (T, T), 0)
    c_i = lax.broadcasted_iota(I32, (T, T), 1)
    tri = (r_i >= c_i).astype(F32)
    c = jnp.dot(tri, logf, preferred_element_type=F32,
                precision=lax.Precision.HIGHEST) + carry_ref[...]
    carry_ref[...] = c[T - 1:T, :]

    lane = lax.broadcasted_iota(I32, (T, LANES), 1)
    lm = lane & 63
    grp = lm >> 3
    hsel = lm & 7
    hi, lo, lolo = _split3(c)
    part = jnp.where((grp == 0) | (grp == 3), hi, jnp.where((grp == 1) | (grp == 4), lo, lolo))
    first3 = grp < 3
    second3 = (grp >= 3) & (grp < F_REP)
    k_aug = jnp.where(first3, -part, 0.0)
    q_aug = jnp.where(second3, part, 0.0)

    for g in range(A_HEADS // 2):
        sl = slice(g * LANES, (g + 1) * LANES)
        zq2, zk2, zv2 = qn[:, sl], kn[:, sl], zv[:, sl]
        for par in range(2):
            h = 2 * g + par
            keep = (lane < 64) if par == 0 else (lane >= 64)
            oh_k = (second3 & (hsel == h)).astype(F32)
            oh_q = (first3 & (hsel == h)).astype(F32)
            k_ref[h] = jnp.where(keep, zk2, k_aug + oh_k).astype(BF16)
            qT_ref[h] = jnp.where(keep, zq2, q_aug + oh_q).T.astype(BF16)
            ones_lane = 64 if par == 0 else 0
            vT_ref[h] = jnp.where(keep, zv2, (lane == ones_lane).astype(F32)).T.astype(BF16)

    p = proj(PK_P, BRANCH_WIDTH)
    row8 = lax.broadcasted_iota(I32, (8, GROUP_DIM), 0)
    pos = i * T + lax.broadcasted_iota(I32, (T, GROUP_DIM), 0)

    def shift_down(v, tail, d):
        r = pltpu.roll(v, d, 0)
        rt = pltpu.roll(tail, d, 0)
        top = jnp.where(row8 < d, rt[0:8], r[0:8])
        return jnp.concatenate([top, r[8:]], axis=0)

    pooled = []
    for gi, w in enumerate(POOL_WINDOWS):
        sl = slice(gi * GROUP_DIM, (gi + 1) * GROUP_DIM)
        s = p[:, sl]
        for lv in range(gi + 1):
            tail = halo_ref[lv, :, sl]
            halo_ref[lv, :, sl] = s[T - HALO:T, :]
            s = s + shift_down(s, tail, 1 << lv)
        cnt = jnp.minimum(pos + 1, w).astype(F32)
        pooled.append(s / cnt - p[:, sl])
    pooled = jnp.concatenate(pooled, axis=1).astype(BF16)
    yb = jnp.dot(pooled, wpool_ref[...], preferred_element_type=F32) * spool_ref[...]
    yb_ref[...] = yb.astype(yb_ref.dtype)

    gu = _gelu_tanh(proj(PK_U, BRANCH_WIDTH))
    gv = _gelu_tanh(proj(PK_SV, BRANCH_WIDTH))
    t_r = lax.broadcasted_iota(I32, (C_CHUNK, C_CHUNK), 0)
    t_c = lax.broadcasted_iota(I32, (C_CHUNK, C_CHUNK), 1)
    causal = t_r >= t_c
    for g in range(N_GROUPS):
        sl = slice(g * GROUP_DIM, (g + 1) * GROUP_DIM)
        vn = _rms(gv[:, sl], gsgu_ref[:, sl]).astype(BF16)
        wc = jnp.where(causal, wsgu_ref[g], 0.0).astype(BF16)
        for ch in range(T // C_CHUNK):
            rows = slice(ch * C_CHUNK, (ch + 1) * C_CHUNK)
            mixed = jnp.dot(wc, vn[rows], preferred_element_type=F32) + bsgu_ref[g]
            yc_ref[rows, sl] = (gu[rows, sl] * mixed).astype(yc_ref.dtype)


def _proj_call(x, w, T):
    B, S, D = x.shape
    H = A_HEADS
    const = lambda *shape: pl.BlockSpec(shape, lambda b, i: (0,) * len(shape))
    return pl.pallas_call(
        functools.partial(_proj_kernel, T=T),
        grid=(B, S // T),
        in_specs=[
            pl.BlockSpec((None, T, D), lambda b, i: (b, i, 0)),
            const(1, D), const(D, PK_W), const(1, LANES), const(1, BRANCH_WIDTH), const(1, BRANCH_WIDTH),
            const(BRANCH_WIDTH, BRANCH_WIDTH), const(BRANCH_WIDTH, BRANCH_WIDTH), const(1, BRANCH_WIDTH),
            const(1, BRANCH_WIDTH), const(N_GROUPS, C_CHUNK, C_CHUNK), const(N_GROUPS, C_CHUNK, GROUP_DIM),
        ],
        out_specs=[
            pl.BlockSpec((None, H, LANES, T), lambda b, i: (b, 0, 0, i)),
            pl.BlockSpec((None, H, T, LANES), lambda b, i: (b, 0, i, 0)),
            pl.BlockSpec((None, H, LANES, T), lambda b, i: (b, 0, 0, i)),
            pl.BlockSpec((None, T, BRANCH_WIDTH), lambda b, i: (b, i, 0)),
            pl.BlockSpec((None, T, BRANCH_WIDTH), lambda b, i: (b, i, 0)),
        ],
        out_shape=[
            jax.ShapeDtypeStruct((B, H, LANES, S), BF16),
            jax.ShapeDtypeStruct((B, H, S, LANES), BF16),
            jax.ShapeDtypeStruct((B, H, LANES, S), BF16),
            jax.ShapeDtypeStruct((B, S, BRANCH_WIDTH), BF16),
            jax.ShapeDtypeStruct((B, S, BRANCH_WIDTH), BF16),
        ],
        scratch_shapes=[pltpu.VMEM((1, LANES), F32), pltpu.VMEM((4, HALO, BRANCH_WIDTH), F32)],
        compiler_params=pltpu.CompilerParams(
            dimension_semantics=("arbitrary", "arbitrary"), vmem_limit_bytes=VMEM_LIMIT),
        name="proj",
    )(x, w["g_mix"], w["w_pack"], w["b_f"], w["g_q"], w["g_k"], w["bd"], w["w_pool"], w["s_pool"],
      w["g_sgu"], w["w_sgu"], w["b_sgu"])


def _attn_kernel(qT_ref, k_ref, vT_ref, o_ref, m_ref, acc_ref, *, tq, tk):
    qi = pl.program_id(2)
    q0 = qi * tq
    n_full = q0 // tk
    lane = lax.broadcasted_iota(I32, (tq, LANES), 1)
    outs = []
    for par in range(2):
        qT = qT_ref[par]
        m_ref[...] = jnp.full(m_ref.shape, NEG, F32)
        acc_ref[...] = jnp.zeros_like(acc_ref)

        def step(j, masked, par=par, qT=qT):
            k0 = pl.multiple_of(j * tk, tk)
            s = jnp.dot(k_ref[par, pl.ds(k0, tk), :], qT, preferred_element_type=F32)
            if masked:
                kpos = k0 + lax.broadcasted_iota(I32, (tk, tq), 0)
                qpos = q0 + lax.broadcasted_iota(I32, (tk, tq), 1)
                s = jnp.where(kpos <= qpos, s, NEG)
            m_old = m_ref[...]
            m_new = jnp.maximum(m_old, jnp.max(s, axis=0, keepdims=True))
            p = jnp.exp(s - m_new).astype(BF16)
            pv = jnp.dot(vT_ref[par, :, pl.ds(k0, tk)], p, preferred_element_type=F32)
            acc_ref[...] = acc_ref[...] * jnp.exp(m_old - m_new) + pv
            m_ref[...] = m_new

        def body(j, carry):
            step(j, False)
            return carry

        lax.fori_loop(0, n_full, body, 0)
        for dj in range(tq // tk):
            step(n_full + dj, True)

        acc = acc_ref[...]
        l = acc[64:65, :] if par == 0 else acc[0:1, :]
        outs.append((acc * (1.0 / l)).T)
    o_ref[...] = jnp.where(lane < 64, outs[0], outs[1]).astype(o_ref.dtype)


def _attn_call(qT, k, vT, tq, tk):
    B, H, _, S = qT.shape
    return pl.pallas_call(
        functools.partial(_attn_kernel, tq=tq, tk=tk),
        grid=(B, H // 2, S // tq),
        in_specs=[
            pl.BlockSpec((None, 2, LANES, tq), lambda b, g, q: (b, g, 0, q)),
            pl.BlockSpec((None, 2, S, LANES), lambda b, g, q: (b, g, 0, 0)),
            pl.BlockSpec((None, 2, LANES, S), lambda b, g, q: (b, g, 0, 0)),
        ],
        out_specs=pl.BlockSpec((None, tq, LANES), lambda b, g, q: (b, q, g)),
        out_shape=jax.ShapeDtypeStruct((B, S, BRANCH_WIDTH), BF16),
        scratch_shapes=[pltpu.VMEM((1, tq), F32), pltpu.VMEM((LANES, tq), F32)],
        compiler_params=pltpu.CompilerParams(
            dimension_semantics=("parallel", "parallel", "arbitrary"), vmem_limit_bytes=VMEM_LIMIT),
        name="attn",
    )(qT, k, vT)


def _merge_kernel(x_ref, ya_ref, yb_ref, yc_ref, gmix_ref, wg_ref, bg_ref, wb_ref, wo_ref,
                  gffn_ref, wr_ref, br_ref,
                  x1_ref, h2_ref, route_ref, cnt_ref, carry_ref, *, T):
    i = pl.program_id(0)

    @pl.when(i == 0)
    def _():
        carry_ref[...] = jnp.zeros_like(carry_ref)

    x = x_ref[...]
    hb = _rms(x, gmix_ref[...]).astype(BF16)
    merged = None
    for bi, y_ref in enumerate((ya_ref, yb_ref, yc_ref)):
        sl = slice(bi * D_MODEL, (bi + 1) * D_MODEL)
        gate = jax.nn.sigmoid(jnp.dot(hb, wg_ref[:, sl], preferred_element_type=F32) + bg_ref[:, sl])
        term = gate * jnp.dot(y_ref[...], wb_ref[bi], preferred_element_type=F32)
        merged = term if merged is None else merged + term
    x1 = x + jnp.dot(merged.astype(BF16), wo_ref[...], preferred_element_type=F32)
    x1_ref[...] = x1
    h2 = _rms(x1, gffn_ref[...])
    h2_ref[...] = h2

    logits = jnp.dot(h2, wr_ref[...], preferred_element_type=F32,
                     precision=lax.Precision.HIGHEST) + br_ref[...]
    lane = lax.broadcasted_iota(I32, (T, LANES), 1).astype(F32)
    big = float(LANES)

    def first_argmax(v):
        m = jnp.max(v, axis=-1, keepdims=True)
        return m, jnp.min(jnp.where(v == m, lane, big), axis=-1, keepdims=True)

    lg = jnp.where(lane < N_EXPERT_GROUPS, logits, NEG)
    mg, grp = first_argmax(lg)
    p_grp = 1.0 / jnp.sum(jnp.exp(lg - mg), axis=-1, keepdims=True)
    lo_lane = N_EXPERT_GROUPS + grp * EXPERTS_PER_GROUP
    le = jnp.where((lane >= lo_lane) & (lane < lo_lane + EXPERTS_PER_GROUP), logits, NEG)
    m1, i1 = first_argmax(le)
    m2, i2 = first_argmax(jnp.where(lane == i1, NEG, le))
    e21 = jnp.exp(m2 - m1)
    g1 = p_grp / (1.0 + e21)
    g2 = p_grp * e21 / (1.0 + e21)
    e1 = i1 - N_EXPERT_GROUPS
    e2 = i2 - N_EXPERT_GROUPS

    oh1 = lane == e1
    oh2 = lane == e2
    sel = (oh1 | oh2).astype(F32)
    r_i = lax.broadcasted_iota(I32, (T, T), 0)
    c_i = lax.broadcasted_iota(I32, (T, T), 1)
    before = (r_i > c_i).astype(BF16)
    seen = jnp.dot(before, sel.astype(BF16), preferred_element_type=F32) + carry_ref[...]
    r1 = jnp.sum(jnp.where(oh1, seen, 0.0), axis=-1, keepdims=True)
    r2 = jnp.sum(jnp.where(oh2, seen, 0.0), axis=-1, keepdims=True)
    carry_ref[...] = carry_ref[...] + jnp.sum(sel, axis=0, keepdims=True)
    cnt_ref[...] = carry_ref[...]

    route = jnp.zeros((T, LANES), F32)
    for idx, val in enumerate((e1, e2, g1, g2, r1, r2)):
        route = jnp.where(lane == idx, val, route)
    route_ref[...] = route


def _merge_call(x2d, ya, yb, yc, w, T):
    N, D = x2d.shape
    const = lambda *shape: pl.BlockSpec(shape, lambda i: (0,) * len(shape))
    tile = lambda width: pl.BlockSpec((T, width), lambda i: (i, 0))
    return pl.pallas_call(
        functools.partial(_merge_kernel, T=T),
        grid=(N // T,),
        in_specs=[
            tile(D), tile(BRANCH_WIDTH), tile(BRANCH_WIDTH), tile(BRANCH_WIDTH),
            const(1, D), const(D, N_BRANCH * D), const(1, N_BRANCH * D),
            const(N_BRANCH, BRANCH_WIDTH, D), const(D, D), const(1, D), const(D, LANES), const(1, LANES),
        ],
        out_specs=[tile(D), tile(D), tile(LANES), const(1, LANES)],
        out_shape=[
            jax.ShapeDtypeStruct((N, D), F32),
            jax.ShapeDtypeStruct((N, D), F32),
            jax.ShapeDtypeStruct((N, LANES), F32),
            jax.ShapeDtypeStruct((1, LANES), F32),
        ],
        scratch_shapes=[pltpu.VMEM((1, LANES), F32)],
        compiler_params=pltpu.CompilerParams(
            dimension_semantics=("arbitrary",), vmem_limit_bytes=VMEM_LIMIT),
        name="merge",
    )(x2d, ya, yb, yc, w["g_mix"], w["w_gate"], w["b_gate"], w["w_branch"], w["w_out"],
      w["g_ffn"], w["w_router"], w["b_router"])


def _dispatch_kernel(dest_ref, h2_ref, xs_ref, sem, *, T):
    def row_copy(t, d):
        return pltpu.make_async_copy(h2_ref.at[pl.ds(t, 1)], xs_ref.at[pl.ds(d, 1)], sem)

    def body(t, carry):
        for kk in range(TOP_K):
            row_copy(t, dest_ref[0, TOP_K * t + kk]).start()
        return carry

    lax.fori_loop(0, T, body, 0)
    for kk in range(TOP_K):
        pltpu.make_async_copy(h2_ref, xs_ref.at[pl.ds(0, T)], sem).wait()


def _dispatch_call(dest3, h2, T):
    N, D = h2.shape
    return pl.pallas_call(
        functools.partial(_dispatch_kernel, T=T),
        grid=(N // T,),
        in_specs=[
            pl.BlockSpec((None, 1, TOP_K * T), lambda i: (i, 0, 0), memory_space=pltpu.SMEM),
            pl.BlockSpec((T, D), lambda i: (i, 0)),
        ],
        out_specs=pl.BlockSpec(memory_space=pl.ANY),
        out_shape=jax.ShapeDtypeStruct((TOP_K * N, D), F32),
        scratch_shapes=[pltpu.SemaphoreType.DMA],
        compiler_params=pltpu.CompilerParams(
            dimension_semantics=("arbitrary",), vmem_limit_bytes=VMEM_LIMIT),
        name="dispatch",
    )(dest3, h2)


def _gmm_kernel(blk_ref, exp_ref, lo_ref, hi_ref, xs_ref, w1_ref, w3_ref, w2_ref, y_ref, *, bm):
    i = pl.program_id(0)
    lo = lo_ref[i]
    hi = hi_ref[i]
    first = jnp.logical_or(i == 0, blk_ref[i] != blk_ref[jnp.maximum(i - 1, 0)])

    @pl.when(first)
    def _():
        y_ref[...] = jnp.zeros_like(y_ref)

    @pl.when(hi > lo)
    def _():
        x = xs_ref[...].astype(BF16)
        a = jnp.dot(x, w1_ref[...].astype(BF16), preferred_element_type=F32)
        b = jnp.dot(x, w3_ref[...].astype(BF16), preferred_element_type=F32)
        mid = (a * jax.nn.sigmoid(a) * b).astype(BF16)
        y = jnp.dot(mid, w2_ref[...].astype(BF16), preferred_element_type=F32)
        row = lax.broadcasted_iota(I32, y.shape, 0)
        y_ref[...] = jnp.where((row >= lo) & (row < hi), y, y_ref[...])


def _gmm_call(plan, xs, w1, w3, w2, bm):
    A, D = xs.shape
    n_items = plan[0].shape[0]
    grid_spec = pltpu.PrefetchScalarGridSpec(
        num_scalar_prefetch=4,
        grid=(n_items,),
        in_specs=[
            pl.BlockSpec((bm, D), lambda i, blk, ex, lo, hi: (blk[i], 0)),
            pl.BlockSpec((None, D, D_EXPERT), lambda i, blk, ex, lo, hi: (ex[i], 0, 0)),
            pl.BlockSpec((None, D, D_EXPERT), lambda i, blk, ex, lo, hi: (ex[i], 0, 0)),
            pl.BlockSpec((None, D_EXPERT, D), lambda i, blk, ex, lo, hi: (ex[i], 0, 0)),
        ],
        out_specs=pl.BlockSpec((bm, D), lambda i, blk, ex, lo, hi: (blk[i], 0)),
    )
    return pl.pallas_call(
        functools.partial(_gmm_kernel, bm=bm),
        grid_spec=grid_spec,
        out_shape=jax.ShapeDtypeStruct((A, D), F32),
        compiler_params=pltpu.CompilerParams(
            dimension_semantics=("arbitrary",), vmem_limit_bytes=VMEM_LIMIT),
        name="gmm",
    )(*plan, xs, w1, w3, w2)


def _gmm_plan(counts, n_rows, bm):
    n_blk = n_rows // bm
    n_items = n_blk + N_EXPERTS - 1
    ends = jnp.cumsum(counts)
    starts = ends - counts
    first_blk = starts // bm
    n_it = jnp.where(counts > 0, (ends - 1) // bm - first_blk + 1, 0)
    item_end = jnp.cumsum(n_it)
    item_start = item_end - n_it
    total = item_end[-1]
    ids = jnp.arange(n_items, dtype=I32)
    valid = ids < total
    ex = jnp.minimum(jnp.searchsorted(item_end, ids, side="right").astype(I32), N_EXPERTS - 1)
    ex = jnp.where(valid, ex, ex[jnp.maximum(total - 1, 0)])
    blk = jnp.where(valid, first_blk[ex] + ids - item_start[ex], n_blk - 1)
    lo = jnp.where(valid, jnp.maximum(starts[ex], blk * bm) - blk * bm, 0)
    hi = jnp.where(valid, jnp.minimum(ends[ex], (blk + 1) * bm) - blk * bm, 0)
    return blk.astype(I32), ex.astype(I32), lo.astype(I32), hi.astype(I32)


def _combine_kernel(dest_ref, x1_ref, route_ref, y_hbm, o_ref, buf, sem, *, T):
    def row_copy(kk, t, d):
        return pltpu.make_async_copy(y_hbm.at[pl.ds(d, 1)], buf.at[kk, pl.ds(t, 1)], sem)

    def body(t, carry):
        for kk in range(TOP_K):
            row_copy(kk, t, dest_ref[0, TOP_K * t + kk]).start()
        return carry

    lax.fori_loop(0, T, body, 0)
    for kk in range(TOP_K):
        pltpu.make_async_copy(y_hbm.at[pl.ds(0, T)], buf.at[kk], sem).wait()
    g1 = route_ref[:, 2:3]
    g2 = route_ref[:, 3:4]
    o_ref[...] = x1_ref[...] + (g1 * buf[0] + g2 * buf[1])


def _combine_call(dest3, x1, route, y, T):
    N, D = x1.shape
    return pl.pallas_call(
        functools.partial(_combine_kernel, T=T),
        grid=(N // T,),
        in_specs=[
            pl.BlockSpec((None, 1, TOP_K * T), lambda i: (i, 0, 0), memory_space=pltpu.SMEM),
            pl.BlockSpec((T, D), lambda i: (i, 0)),
            pl.BlockSpec((T, LANES), lambda i: (i, 0)),
            pl.BlockSpec(memory_space=pl.ANY),
        ],
        out_specs=pl.BlockSpec((T, D), lambda i: (i, 0)),
        out_shape=jax.ShapeDtypeStruct((N, D), F32),
        scratch_shapes=[pltpu.VMEM((TOP_K, T, D), F32), pltpu.SemaphoreType.DMA],
        compiler_params=pltpu.CompilerParams(
            dimension_semantics=("arbitrary",), vmem_limit_bytes=VMEM_LIMIT),
        name="combine",
    )(dest3, x1, route, y)


def _block_diag(blocks):
    n = len(blocks)
    rows = []
    for i, b in enumerate(blocks):
        rows.append(jnp.concatenate(
            [b if j == i else jnp.zeros((b.shape[0], blocks[j].shape[1]), b.dtype) for j in range(n)], axis=1))
    return jnp.concatenate(rows, axis=0)


def _rep_forget(cols):
    half = jnp.concatenate(
        [jnp.tile(cols, (1, F_REP)), jnp.zeros((cols.shape[0], 64 - F_REP * A_HEADS), cols.dtype)], axis=1)
    return jnp.concatenate([half, half], axis=1)


def _prep_layer(l, g_mix, w_in, b_fgate, b_gate, g_q, g_k, w_pool, s_pool, g_sgu, w_sgu, b_sgu,
                w_branch, w_out, g_ffn, w_rg, b_rg, w_re, b_re):
    wi = w_in[l]
    w_pack = jnp.concatenate(
        [wi[:, 0:OFF_F], _rep_forget(wi[:, OFF_F:OFF_P]), wi[:, OFF_P:OFF_G]], axis=1).astype(BF16)
    pad_r = LANES - N_EXPERT_GROUPS - N_EXPERTS
    return dict(
        g_mix=g_mix[l][None, :],
        w_pack=w_pack,
        b_f=_rep_forget(b_fgate[l][None, :]),
        g_q=jnp.tile(g_q[l], A_HEADS)[None, :],
        g_k=jnp.tile(g_k[l], A_HEADS)[None, :],
        bd=_block_diag([jnp.ones((A_HEAD_DIM, A_HEAD_DIM), BF16)] * A_HEADS),
        w_pool=_block_diag([w_pool[l][g] for g in range(N_GROUPS)]).astype(BF16),
        s_pool=s_pool[l][None, :],
        g_sgu=g_sgu[l][None, :],
        w_sgu=w_sgu[l],
        b_sgu=jnp.broadcast_to(b_sgu[l][:, :, None], (N_GROUPS, C_CHUNK, GROUP_DIM)),
        w_gate=wi[:, OFF_G:].astype(BF16),
        b_gate=b_gate[l].reshape(1, N_BRANCH * D_MODEL),
        w_branch=w_branch[l].astype(BF16),
        w_out=w_out[l].astype(BF16),
        g_ffn=g_ffn[l][None, :],
        w_router=jnp.concatenate([w_rg[l], w_re[l], jnp.zeros((D_MODEL, pad_r), F32)], axis=1),
        b_router=jnp.concatenate([b_rg[l], b_re[l], jnp.zeros((pad_r,), F32)])[None, :],
    )


def kernel(x, g_mix, w_in, b_fgate, b_gate, g_q, g_k, w_pool, s_pool, g_sgu, w_sgu, b_sgu, w_branch, w_out,
           g_ffn, w_rg, b_rg, w_re, b_re, w1, w3, w2):
    B, S, D = x.shape
    assert D == D_MODEL and x.dtype == F32
    N = B * S
    T, tq, tk, bm = _tiles(S)
    assert S % T == 0 and S % tq == 0 and tq % tk == 0 and T % C_CHUNK == 0 and (TOP_K * N) % bm == 0
    depth = w_in.shape[0]
    for l in range(depth):
        w = _prep_layer(l, g_mix, w_in, b_fgate, b_gate, g_q, g_k, w_pool, s_pool, g_sgu, w_sgu, b_sgu,
                        w_branch, w_out, g_ffn, w_rg, b_rg, w_re, b_re)
        qT, k, vT, yb, yc = _proj_call(x, w, T)
        ya = _attn_call(qT, k, vT, tq, tk)
        x1, h2, route, cnt = _merge_call(
            x.reshape(N, D), ya.reshape(N, -1), yb.reshape(N, -1), yc.reshape(N, -1), w, T)
        counts = cnt[0, :N_EXPERTS].astype(I32)
        starts = jnp.cumsum(counts) - counts
        experts = route[:, 0:TOP_K].astype(I32)
        dest = starts[experts] + route[:, 4:4 + TOP_K].astype(I32)
        dest3 = dest.reshape(N // T, 1, TOP_K * T)
        xs = _dispatch_call(dest3, h2, T)
        y = _gmm_call(_gmm_plan(counts, TOP_K * N, bm), xs, w1[l], w3[l], w2[l], bm)
        x = _combine_call(dest3, x1, route, y, T).reshape(B, S, D)
    return x
```

```python
import functools

import jax
import jax.numpy as jnp
from jax import lax
from jax.experimental import pallas as pl
from jax.experimental.pallas import tpu as pltpu

F32 = jnp.float32
BF16 = jnp.bfloat16
I32 = jnp.int32

D_MODEL = 1024
A_HEADS = 8
A_HEAD_DIM = 64
BRANCH_WIDTH = 512
N_BRANCH = 3
POOL_WINDOWS = (2, 4, 8, 16)
GROUP_DIM = 128
N_GROUPS = 4
C_CHUNK = 128
N_EXPERT_GROUPS = 4
EXPERTS_PER_GROUP = 8
N_EXPERTS = N_EXPERT_GROUPS * EXPERTS_PER_GROUP
TOP_K = 2
D_EXPERT = 256
RMS_EPS = 1e-6
OFF_F = 3 * BRANCH_WIDTH
OFF_P = OFF_F + A_HEADS
OFF_U = OFF_P + BRANCH_WIDTH
OFF_SV = OFF_U + BRANCH_WIDTH
OFF_G = OFF_SV + BRANCH_WIDTH

LANES = 128
V7X_VMEM_BYTES = 64 * 1024 * 1024
VMEM_LIMIT = 48 * 1024 * 1024

PK_Q, PK_K, PK_V = 0, 512, 1024
PK_F = 1536
PK_P = PK_F + LANES
PK_U = PK_P + BRANCH_WIDTH
PK_SV = PK_U + BRANCH_WIDTH
PK_W = PK_SV + BRANCH_WIDTH
F_REP = 6

LOG2E = 1.4426950408889634
SCORE_CAP = 96.0
NEG = -1e30
HALO = 16


def _tiles(seq_len):
    t_proj = min(256, seq_len)
    t_q = min(512, seq_len)
    t_k = min(512, seq_len)
    moe_block = 256
    return t_proj, t_q, t_k, moe_block


def _rms(x, g):
    return x * lax.rsqrt(jnp.mean(x * x, axis=-1, keepdims=True) + RMS_EPS) * g


def _gelu_tanh(x):
    cdf = 0.5 * (1.0 + jnp.tanh(0.7978845608028654 * (x + 0.044715 * (x * x * x))))
    return x * cdf


def _log_sigmoid(x):
    return jnp.minimum(x, 0.0) - jnp.log1p(jnp.exp(-jnp.abs(x)))


def _split3(c):
    hi = c.astype(BF16).astype(F32)
    r = c - hi
    lo = r.astype(BF16).astype(F32)
    lolo = (r - lo).astype(BF16).astype(F32)
    return hi, lo, lolo


def _proj_kernel(x_ref, gmix_ref, w_ref, bf_ref, gq_ref, gk_ref, bd_ref, sel_ref, wpool_ref, spool_ref,
                 gsgu_ref, wsgu_ref, bsgu_ref,
                 qT_ref, k_ref, vT_ref, yb_ref, yc_ref,
                 carry_ref, halo_ref, *, T):
    i = pl.program_id(1)

    @pl.when(i == 0)
    def _():
        carry_ref[...] = jnp.zeros_like(carry_ref)
        halo_ref[...] = jnp.zeros_like(halo_ref)

    hb = _rms(x_ref[...], gmix_ref[...]).astype(BF16)

    def proj(lo, width):
        return jnp.dot(hb, w_ref[:, lo:lo + width], preferred_element_type=F32)

    bd = bd_ref[...]

    def head_norm(z, g):
        sq = z * z
        hi = sq.astype(BF16)
        lo = (sq - hi.astype(F32)).astype(BF16)
        ss = (jnp.dot(hi, bd, preferred_element_type=F32)
              + jnp.dot(lo, bd, preferred_element_type=F32))
        return z * lax.rsqrt(ss * (1.0 / A_HEAD_DIM) + RMS_EPS) * g

    qn = head_norm(proj(PK_Q, BRANCH_WIDTH), gq_ref[...]) * (A_HEAD_DIM ** -0.5 * LOG2E)
    kn = head_norm(proj(PK_K, BRANCH_WIDTH), gk_ref[...])
    zv = proj(PK_V, BRANCH_WIDTH)
    qk = qn * kn
    qk_hi = qk.astype(BF16)
    qk_lo = (qk - qk_hi.astype(F32)).astype(BF16)
    diag = (jnp.dot(qk_hi, sel_ref[...], preferred_element_type=F32)
            + jnp.dot(qk_lo, sel_ref[...], preferred_element_type=F32))

    logf = _log_sigmoid(proj(PK_F, LANES) + bf_ref[...])
    r_i = lax.broadcasted_iota(I32, (T, T), 0)
    c_i = lax.broadcasted_iota(I32, (T, T), 1)
    tri = (r_i >= c_i).astype(F32)
    c = jnp.dot(tri, logf, preferred_element_type=F32,
                precision=lax.Precision.HIGHEST) + carry_ref[...]
    carry_ref[...] = c[T - 1:T, :]

    lane = lax.broadcasted_iota(I32, (T, LANES), 1)
    lm = lane & 63
    grp = lm >> 3
    hsel = lm & 7
    first3 = grp < 3
    second3 = (grp >= 3) & (grp < F_REP)

    def parts(v):
        hi, lo, lolo = _split3(v)
        return jnp.where((grp == 0) | (grp == 3), hi, jnp.where((grp == 1) | (grp == 4), lo, lolo))

    k_aug = jnp.where(first3, -parts(c * LOG2E), 0.0)
    q_aug = jnp.where(second3, parts(c * LOG2E - diag), 0.0)

    for g in range(A_HEADS // 2):
        sl = slice(g * LANES, (g + 1) * LANES)
        zq2, zk2, zv2 = qn[:, sl], kn[:, sl], zv[:, sl]
        for par in range(2):
            h = 2 * g + par
            keep = (lane < 64) if par == 0 else (lane >= 64)
            oh_k = (second3 & (hsel == h)).astype(F32)
            oh_q = (first3 & (hsel == h)).astype(F32)
            k_ref[h] = jnp.where(keep, zk2, k_aug + oh_k).astype(BF16)
            qT_ref[h] = jnp.where(keep, zq2, q_aug + oh_q).T.astype(BF16)
            ones_lane = 64 if par == 0 else 0
            vT_ref[h] = jnp.where(keep, zv2, (lane == ones_lane).astype(F32)).T.astype(BF16)

    p = proj(PK_P, BRANCH_WIDTH)
    row8 = lax.broadcasted_iota(I32, (8, GROUP_DIM), 0)
    pos = i * T + lax.broadcasted_iota(I32, (T, GROUP_DIM), 0)

    def shift_down(v, tail, d):
        r = pltpu.roll(v, d, 0)
        rt = pltpu.roll(tail, d, 0)
        top = jnp.where(row8 < d, rt[0:8], r[0:8])
        return jnp.concatenate([top, r[8:]], axis=0)

    pooled = []
    for gi, w in enumerate(POOL_WINDOWS):
        sl = slice(gi * GROUP_DIM, (gi + 1) * GROUP_DIM)
        s = p[:, sl]
        for lv in range(gi + 1):
            tail = halo_ref[lv, :, sl]
            halo_ref[lv, :, sl] = s[T - HALO:T, :]
            s = s + shift_down(s, tail, 1 << lv)
        cnt = jnp.minimum(pos + 1, w).astype(F32)
        pooled.append(s / cnt - p[:, sl])
    pooled = jnp.concatenate(pooled, axis=1).astype(BF16)
    yb = jnp.dot(pooled, wpool_ref[...], preferred_element_type=F32) * spool_ref[...]
    yb_ref[...] = yb.astype(yb_ref.dtype)

    gu = _gelu_tanh(proj(PK_U, BRANCH_WIDTH))
    gv = _gelu_tanh(proj(PK_SV, BRANCH_WIDTH))
    t_r = lax.broadcasted_iota(I32, (C_CHUNK, C_CHUNK), 0)
    t_c = lax.broadcasted_iota(I32, (C_CHUNK, C_CHUNK), 1)
    causal = t_r >= t_c
    for g in range(N_GROUPS):
        sl = slice(g * GROUP_DIM, (g + 1) * GROUP_DIM)
        vn = _rms(gv[:, sl], gsgu_ref[:, sl]).astype(BF16)
        wc = jnp.where(causal, wsgu_ref[g], 0.0).astype(BF16)
        for ch in range(T // C_CHUNK):
            rows = slice(ch * C_CHUNK, (ch + 1) * C_CHUNK)
            mixed = jnp.dot(wc, vn[rows], preferred_element_type=F32) + bsgu_ref[g]
            yc_ref[rows, sl] = (gu[rows, sl] * mixed).astype(yc_ref.dtype)


def _proj_call(x, w, T):
    B, S, D = x.shape
    H = A_HEADS
    const = lambda *shape: pl.BlockSpec(shape, lambda b, i: (0,) * len(shape))
    return pl.pallas_call(
        functools.partial(_proj_kernel, T=T),
        grid=(B, S // T),
        in_specs=[
            pl.BlockSpec((None, T, D), lambda b, i: (b, i, 0)),
            const(1, D), const(D, PK_W), const(1, LANES), const(1, BRANCH_WIDTH), const(1, BRANCH_WIDTH),
            const(BRANCH_WIDTH, BRANCH_WIDTH), const(BRANCH_WIDTH, LANES), const(BRANCH_WIDTH, BRANCH_WIDTH),
            const(1, BRANCH_WIDTH), const(1, BRANCH_WIDTH), const(N_GROUPS, C_CHUNK, C_CHUNK), const(N_GROUPS, C_CHUNK, GROUP_DIM),
        ],
        out_specs=[
            pl.BlockSpec((None, H, LANES, T), lambda b, i: (b, 0, 0, i)),
            pl.BlockSpec((None, H, T, LANES), lambda b, i: (b, 0, i, 0)),
            pl.BlockSpec((None, H, LANES, T), lambda b, i: (b, 0, 0, i)),
            pl.BlockSpec((None, T, BRANCH_WIDTH), lambda b, i: (b, i, 0)),
            pl.BlockSpec((None, T, BRANCH_WIDTH), lambda b, i: (b, i, 0)),
        ],
        out_shape=[
            jax.ShapeDtypeStruct((B, H, LANES, S), BF16),
            jax.ShapeDtypeStruct((B, H, S, LANES), BF16),
            jax.ShapeDtypeStruct((B, H, LANES, S), BF16),
            jax.ShapeDtypeStruct((B, S, BRANCH_WIDTH), BF16),
            jax.ShapeDtypeStruct((B, S, BRANCH_WIDTH), BF16),
        ],
        scratch_shapes=[pltpu.VMEM((1, LANES), F32), pltpu.VMEM((4, HALO, BRANCH_WIDTH), F32)],
        compiler_params=pltpu.CompilerParams(
            dimension_semantics=("arbitrary", "arbitrary"), vmem_limit_bytes=VMEM_LIMIT),
        name="proj",
    )(x, w["g_mix"], w["w_pack"], w["b_f"], w["g_q"], w["g_k"], w["bd"], w["sel"], w["w_pool"], w["s_pool"],
      w["g_sgu"], w["w_sgu"], w["b_sgu"])


def _attn_kernel(qT_ref, k_ref, vT_ref, o_ref, m_ref, acc_ref, s_ref, *, tq, tk, online_max):
    qi = pl.program_id(2)
    q0 = qi * tq
    n_full = q0 // tk
    n_diag = tq // tk
    acc_ref[...] = jnp.zeros_like(acc_ref)
    if online_max:
        m_ref[...] = jnp.full(m_ref.shape, NEG, F32)

    def scores(j, slot):
        k0 = pl.multiple_of(j * tk, tk)
        for par in range(2):
            s_ref[slot, par] = jnp.dot(k_ref[par, pl.ds(k0, tk), :], qT_ref[par],
                                       preferred_element_type=F32)

    def consume(j, slot, masked):
        k0 = pl.multiple_of(j * tk, tk)
        for par in range(2):
            s = s_ref[slot, par]
            if masked:
                kpos = k0 + lax.broadcasted_iota(I32, (tk, tq), 0)
                qpos = q0 + lax.broadcasted_iota(I32, (tk, tq), 1)
                s = jnp.where(kpos <= qpos, s, NEG)
            v_blk = vT_ref[par, :, pl.ds(k0, tk)]
            if online_max:
                m_old = m_ref[par]
                m_new = jnp.maximum(m_old, jnp.max(s, axis=0, keepdims=True))
                p = jnp.exp2(s - m_new).astype(BF16)
                pv = jnp.dot(v_blk, p, preferred_element_type=F32)
                acc_ref[par] = acc_ref[par] * jnp.exp2(m_old - m_new) + pv
                m_ref[par] = m_new
            else:
                p = jnp.exp2(s).astype(BF16)
                acc_ref[par] += jnp.dot(v_blk, p, preferred_element_type=F32)

    assert n_diag == 1
    scores(0, 0)

    def body(j2, carry):
        j = 2 * j2
        scores(j + 1, 1)
        consume(j, 0, False)
        scores(j + 2, 0)
        consume(j + 1, 1, False)
        return carry

    lax.fori_loop(0, n_full // 2, body, 0)

    @pl.when(n_full % 2 == 0)
    def _():
        consume(n_full, 0, True)

    @pl.when(n_full % 2 == 1)
    def _():
        scores(n_full, 1)
        consume(n_full - 1, 0, False)
        consume(n_full, 1, True)

    lane = lax.broadcasted_iota(I32, (tq, LANES), 1)
    outs = []
    for par in range(2):
        acc = acc_ref[par]
        l = acc[64:65, :] if par == 0 else acc[0:1, :]
        outs.append((acc * (1.0 / l)).T)
    o_ref[...] = jnp.where(lane < 64, outs[0], outs[1]).astype(o_ref.dtype)


def _attn_call(qT, k, vT, tq, tk, online_max):
    B, H, _, S = qT.shape
    return pl.pallas_call(
        functools.partial(_attn_kernel, tq=tq, tk=tk, online_max=online_max),
        grid=(B, H // 2, S // tq),
        in_specs=[
            pl.BlockSpec((None, 2, LANES, tq), lambda b, g, q: (b, g, 0, q)),
            pl.BlockSpec((None, 2, S, LANES), lambda b, g, q: (b, g, 0, 0)),
            pl.BlockSpec((None, 2, LANES, S), lambda b, g, q: (b, g, 0, 0)),
        ],
        out_specs=pl.BlockSpec((None, tq, LANES), lambda b, g, q: (b, q, g)),
        out_shape=jax.ShapeDtypeStruct((B, S, BRANCH_WIDTH), BF16),
        scratch_shapes=[pltpu.VMEM((2, 1, tq), F32), pltpu.VMEM((2, LANES, tq), F32),
                        pltpu.VMEM((2, 2, tk, tq), F32)],
        compiler_params=pltpu.CompilerParams(
            dimension_semantics=("parallel", "parallel", "arbitrary"), vmem_limit_bytes=VMEM_LIMIT),
        name="attn_online" if online_max else "attn",
    )(qT, k, vT)


def _attention(qT, k, vT, g_q, g_k, tq, tk):
    bound = 16.0 * LOG2E * jnp.max(jnp.abs(g_q)) * jnp.max(jnp.abs(g_k))
    return lax.cond(bound <= SCORE_CAP,
                    lambda: _attn_call(qT, k, vT, tq, tk, False),
                    lambda: _attn_call(qT, k, vT, tq, tk, True))


def _merge_kernel(x_ref, ya_ref, yb_ref, yc_ref, gmix_ref, wg_ref, bg_ref, wb_ref, wo_ref,
                  gffn_ref, wr_ref, br_ref,
                  x1_ref, h2_ref, route_ref, cnt_ref, carry_ref, *, T):
    i = pl.program_id(0)

    @pl.when(i == 0)
    def _():
        carry_ref[...] = jnp.zeros_like(carry_ref)

    x = x_ref[...]
    hb = _rms(x, gmix_ref[...]).astype(BF16)
    merged = None
    for bi, y_ref in enumerate((ya_ref, yb_ref, yc_ref)):
        sl = slice(bi * D_MODEL, (bi + 1) * D_MODEL)
        gate = jax.nn.sigmoid(jnp.dot(hb, wg_ref[:, sl], preferred_element_type=F32) + bg_ref[:, sl])
        term = gate * jnp.dot(y_ref[...], wb_ref[bi], preferred_element_type=F32)
        merged = term if merged is None else merged + term
    x1 = x + jnp.dot(merged.astype(BF16), wo_ref[...], preferred_element_type=F32)
    x1_ref[...] = x1
    h2 = _rms(x1, gffn_ref[...])
    h2_ref[...] = h2

    logits = jnp.dot(h2, wr_ref[...], preferred_element_type=F32,
                     precision=lax.Precision.HIGHEST) + br_ref[...]
    lane = lax.broadcasted_iota(I32, (T, LANES), 1).astype(F32)
    big = float(LANES)

    def first_argmax(v):
        m = jnp.max(v, axis=-1, keepdims=True)
        return m, jnp.min(jnp.where(v == m, lane, big), axis=-1, keepdims=True)

    lg = jnp.where(lane < N_EXPERT_GROUPS, logits, NEG)
    mg, grp = first_argmax(lg)
    p_grp = 1.0 / jnp.sum(jnp.exp(lg - mg), axis=-1, keepdims=True)
    lo_lane = N_EXPERT_GROUPS + grp * EXPERTS_PER_GROUP
    le = jnp.where((lane >= lo_lane) & (lane < lo_lane + EXPERTS_PER_GROUP), logits, NEG)
    m1, i1 = first_argmax(le)
    m2, i2 = first_argmax(jnp.where(lane == i1, NEG, le))
    e21 = jnp.exp(m2 - m1)
    g1 = p_grp / (1.0 + e21)
    g2 = p_grp * e21 / (1.0 + e21)
    e1 = i1 - N_EXPERT_GROUPS
    e2 = i2 - N_EXPERT_GROUPS

    oh1 = lane == e1
    oh2 = lane == e2
    sel = (oh1 | oh2).astype(F32)
    r_i = lax.broadcasted_iota(I32, (T, T), 0)
    c_i = lax.broadcasted_iota(I32, (T, T), 1)
    before = (r_i > c_i).astype(BF16)
    seen = jnp.dot(before, sel.astype(BF16), preferred_element_type=F32) + carry_ref[...]
    r1 = jnp.sum(jnp.where(oh1, seen, 0.0), axis=-1, keepdims=True)
    r2 = jnp.sum(jnp.where(oh2, seen, 0.0), axis=-1, keepdims=True)
    carry_ref[...] = carry_ref[...] + jnp.sum(sel, axis=0, keepdims=True)
    cnt_ref[...] = carry_ref[...]

    route = jnp.zeros((T, LANES), F32)
    for idx, val in enumerate((e1, e2, g1, g2, r1, r2)):
        route = jnp.where(lane == idx, val, route)
    route_ref[...] = route


def _merge_call(x2d, ya, yb, yc, w, T):
    N, D = x2d.shape
    const = lambda *shape: pl.BlockSpec(shape, lambda i: (0,) * len(shape))
    tile = lambda width: pl.BlockSpec((T, width), lambda i: (i, 0))
    return pl.pallas_call(
        functools.partial(_merge_kernel, T=T),
        grid=(N // T,),
        in_specs=[
            tile(D), tile(BRANCH_WIDTH), tile(BRANCH_WIDTH), tile(BRANCH_WIDTH),
            const(1, D), const(D, N_BRANCH * D), const(1, N_BRANCH * D),
            const(N_BRANCH, BRANCH_WIDTH, D), const(D, D), const(1, D), const(D, LANES), const(1, LANES),
        ],
        out_specs=[tile(D), tile(D), tile(LANES), const(1, LANES)],
        out_shape=[
            jax.ShapeDtypeStruct((N, D), F32),
            jax.ShapeDtypeStruct((N, D), F32),
            jax.ShapeDtypeStruct((N, LANES), F32),
            jax.ShapeDtypeStruct((1, LANES), F32),
        ],
        scratch_shapes=[pltpu.VMEM((1, LANES), F32)],
        compiler_params=pltpu.CompilerParams(
            dimension_semantics=("arbitrary",), vmem_limit_bytes=VMEM_LIMIT),
        name="merge",
    )(x2d, ya, yb, yc, w["g_mix"], w["w_gate"], w["b_gate"], w["w_branch"], w["w_out"],
      w["g_ffn"], w["w_router"], w["b_router"])


def _dispatch_kernel(dest_ref, h2_ref, xs_ref, sem, *, T):
    def row_copy(t, d):
        return pltpu.make_async_copy(h2_ref.at[pl.ds(t, 1)], xs_ref.at[pl.ds(d, 1)], sem)

    def body(t, carry):
        for kk in range(TOP_K):
            row_copy(t, dest_ref[0, TOP_K * t + kk]).start()
        return carry

    lax.fori_loop(0, T, body, 0)
    for kk in range(TOP_K):
        pltpu.make_async_copy(h2_ref, xs_ref.at[pl.ds(0, T)], sem).wait()


def _dispatch_call(dest3, h2, T):
    N, D = h2.shape
    return pl.pallas_call(
        functools.partial(_dispatch_kernel, T=T),
        grid=(N // T,),
        in_specs=[
            pl.BlockSpec((None, 1, TOP_K * T), lambda i: (i, 0, 0), memory_space=pltpu.SMEM),
            pl.BlockSpec((T, D), lambda i: (i, 0)),
        ],
        out_specs=pl.BlockSpec(memory_space=pl.ANY),
        out_shape=jax.ShapeDtypeStruct((TOP_K * N, D), F32),
        scratch_shapes=[pltpu.SemaphoreType.DMA],
        compiler_params=pltpu.CompilerParams(
            dimension_semantics=("arbitrary",), vmem_limit_bytes=VMEM_LIMIT),
        name="dispatch",
    )(dest3, h2)


def _gmm_kernel(blk_ref, exp_ref, lo_ref, hi_ref, xs_ref, w1_ref, w3_ref, w2_ref, y_ref, *, bm):
    i = pl.program_id(0)
    lo = lo_ref[i]
    hi = hi_ref[i]
    first = jnp.logical_or(i == 0, blk_ref[i] != blk_ref[jnp.maximum(i - 1, 0)])

    @pl.when(first)
    def _():
        y_ref[...] = jnp.zeros_like(y_ref)

    @pl.when(hi > lo)
    def _():
        x = xs_ref[...].astype(BF16)
        a = jnp.dot(x, w1_ref[...].astype(BF16), preferred_element_type=F32)
        b = jnp.dot(x, w3_ref[...].astype(BF16), preferred_element_type=F32)
        mid = (a * jax.nn.sigmoid(a) * b).astype(BF16)
        y = jnp.dot(mid, w2_ref[...].astype(BF16), preferred_element_type=F32)
        row = lax.broadcasted_iota(I32, y.shape, 0)
        y_ref[...] = jnp.where((row >= lo) & (row < hi), y, y_ref[...])


def _gmm_call(plan, xs, w1, w3, w2, bm):
    A, D = xs.shape
    n_items = plan[0].shape[0]
    grid_spec = pltpu.PrefetchScalarGridSpec(
        num_scalar_prefetch=4,
        grid=(n_items,),
        in_specs=[
            pl.BlockSpec((bm, D), lambda i, blk, ex, lo, hi: (blk[i], 0)),
            pl.BlockSpec((None, D, D_EXPERT), lambda i, blk, ex, lo, hi: (ex[i], 0, 0)),
            pl.BlockSpec((None, D, D_EXPERT), lambda i, blk, ex, lo, hi: (ex[i], 0, 0)),
            pl.BlockSpec((None, D_EXPERT, D), lambda i, blk, ex, lo, hi: (ex[i], 0, 0)),
        ],
        out_specs=pl.BlockSpec((bm, D), lambda i, blk, ex, lo, hi: (blk[i], 0)),
    )
    return pl.pallas_call(
        functools.partial(_gmm_kernel, bm=bm),
        grid_spec=grid_spec,
        out_shape=jax.ShapeDtypeStruct((A, D), F32),
        compiler_params=pltpu.CompilerParams(
            dimension_semantics=("arbitrary",), vmem_limit_bytes=VMEM_LIMIT),
        name="gmm",
    )(*plan, xs, w1, w3, w2)


def _gmm_plan(counts, n_rows, bm):
    n_blk = n_rows // bm
    n_items = n_blk + N_EXPERTS - 1
    ends = jnp.cumsum(counts)
    starts = ends - counts
    first_blk = starts // bm
    n_it = jnp.where(counts > 0, (ends - 1) // bm - first_blk + 1, 0)
    item_end = jnp.cumsum(n_it)
    item_start = item_end - n_it
    total = item_end[-1]
    ids = jnp.arange(n_items, dtype=I32)
    valid = ids < total
    ex = jnp.minimum(jnp.searchsorted(item_end, ids, side="right").astype(I32), N_EXPERTS - 1)
    ex = jnp.where(valid, ex, ex[jnp.maximum(total - 1, 0)])
    blk = jnp.where(valid, first_blk[ex] + ids - item_start[ex], n_blk - 1)
    lo = jnp.where(valid, jnp.maximum(starts[ex], blk * bm) - blk * bm, 0)
    hi = jnp.where(valid, jnp.minimum(ends[ex], (blk + 1) * bm) - blk * bm, 0)
    return blk.astype(I32), ex.astype(I32), lo.astype(I32), hi.astype(I32)


def _combine_kernel(dest_ref, x1_ref, route_ref, y_hbm, o_ref, buf, sem, *, T):
    def row_copy(kk, t, d):
        return pltpu.make_async_copy(y_hbm.at[pl.ds(d, 1)], buf.at[kk, pl.ds(t, 1)], sem)

    def body(t, carry):
        for kk in range(TOP_K):
            row_copy(kk, t, dest_ref[0, TOP_K * t + kk]).start()
        return carry

    lax.fori_loop(0, T, body, 0)
    for kk in range(TOP_K):
        pltpu.make_async_copy(y_hbm.at[pl.ds(0, T)], buf.at[kk], sem).wait()
    g1 = route_ref[:, 2:3]
    g2 = route_ref[:, 3:4]
    o_ref[...] = x1_ref[...] + (g1 * buf[0] + g2 * buf[1])


def _combine_call(dest3, x1, route, y, T):
    N, D = x1.shape
    return pl.pallas_call(
        functools.partial(_combine_kernel, T=T),
        grid=(N // T,),
        in_specs=[
            pl.BlockSpec((None, 1, TOP_K * T), lambda i: (i, 0, 0), memory_space=pltpu.SMEM),
            pl.BlockSpec((T, D), lambda i: (i, 0)),
            pl.BlockSpec((T, LANES), lambda i: (i, 0)),
            pl.BlockSpec(memory_space=pl.ANY),
        ],
        out_specs=pl.BlockSpec((T, D), lambda i: (i, 0)),
        out_shape=jax.ShapeDtypeStruct((N, D), F32),
        scratch_shapes=[pltpu.VMEM((TOP_K, T, D), F32), pltpu.SemaphoreType.DMA],
        compiler_params=pltpu.CompilerParams(
            dimension_semantics=("arbitrary",), vmem_limit_bytes=VMEM_LIMIT),
        name="combine",
    )(dest3, x1, route, y)


def _block_diag(blocks):
    n = len(blocks)
    rows = []
    for i, b in enumerate(blocks):
        rows.append(jnp.concatenate(
            [b if j == i else jnp.zeros((b.shape[0], blocks[j].shape[1]), b.dtype) for j in range(n)], axis=1))
    return jnp.concatenate(rows, axis=0)


def _rep_forget(cols):
    half = jnp.concatenate(
        [jnp.tile(cols, (1, F_REP)), jnp.zeros((cols.shape[0], 64 - F_REP * A_HEADS), cols.dtype)], axis=1)
    return jnp.concatenate([half, half], axis=1)


def _head_to_gate_lanes():
    head = jnp.arange(BRANCH_WIDTH, dtype=I32)[:, None] // A_HEAD_DIM
    lane = jnp.arange(LANES, dtype=I32)[None, :]
    hit = ((lane & 7) == head) & (((lane & 63) >> 3) < F_REP)
    return hit.astype(BF16)


def _prep_layer(l, g_mix, w_in, b_fgate, b_gate, g_q, g_k, w_pool, s_pool, g_sgu, w_sgu, b_sgu,
                w_branch, w_out, g_ffn, w_rg, b_rg, w_re, b_re):
    wi = w_in[l]
    w_pack = jnp.concatenate(
        [wi[:, 0:OFF_F], _rep_forget(wi[:, OFF_F:OFF_P]), wi[:, OFF_P:OFF_G]], axis=1).astype(BF16)
    pad_r = LANES - N_EXPERT_GROUPS - N_EXPERTS
    return dict(
        g_mix=g_mix[l][None, :],
        w_pack=w_pack,
        b_f=_rep_forget(b_fgate[l][None, :]),
        g_q=jnp.tile(g_q[l], A_HEADS)[None, :],
        g_k=jnp.tile(g_k[l], A_HEADS)[None, :],
        bd=_block_diag([jnp.ones((A_HEAD_DIM, A_HEAD_DIM), BF16)] * A_HEADS),
        sel=_head_to_gate_lanes(),
        w_pool=_block_diag([w_pool[l][g] for g in range(N_GROUPS)]).astype(BF16),
        s_pool=s_pool[l][None, :],
        g_sgu=g_sgu[l][None, :],
        w_sgu=w_sgu[l],
        b_sgu=jnp.broadcast_to(b_sgu[l][:, :, None], (N_GROUPS, C_CHUNK, GROUP_DIM)),
        w_gate=wi[:, OFF_G:].astype(BF16),
        b_gate=b_gate[l].reshape(1, N_BRANCH * D_MODEL),
        w_branch=w_branch[l].astype(BF16),
        w_out=w_out[l].astype(BF16),
        g_ffn=g_ffn[l][None, :],
        w_router=jnp.concatenate([w_rg[l], w_re[l], jnp.zeros((D_MODEL, pad_r), F32)], axis=1),
        b_router=jnp.concatenate([b_rg[l], b_re[l], jnp.zeros((pad_r,), F32)])[None, :],
    )


def kernel(x, g_mix, w_in, b_fgate, b_gate, g_q, g_k, w_pool, s_pool, g_sgu, w_sgu, b_sgu, w_branch, w_out,
           g_ffn, w_rg, b_rg, w_re, b_re, w1, w3, w2):
    B, S, D = x.shape
    assert D == D_MODEL and x.dtype == F32
    N = B * S
    T, tq, tk, bm = _tiles(S)
    assert S % T == 0 and S % tq == 0 and tq % tk == 0 and T % C_CHUNK == 0 and (TOP_K * N) % bm == 0
    depth = w_in.shape[0]
    for l in range(depth):
        w = _prep_layer(l, g_mix, w_in, b_fgate, b_gate, g_q, g_k, w_pool, s_pool, g_sgu, w_sgu, b_sgu,
                        w_branch, w_out, g_ffn, w_rg, b_rg, w_re, b_re)
        qT, k, vT, yb, yc = _proj_call(x, w, T)
        ya = _attention(qT, k, vT, g_q[l], g_k[l], tq, tk)
        x1, h2, route, cnt = _merge_call(
            x.reshape(N, D), ya.reshape(N, -1), yb.reshape(N, -1), yc.reshape(N, -1), w, T)
        counts = cnt[0, :N_EXPERTS].astype(I32)
        starts = jnp.cumsum(counts) - counts
        experts = route[:, 0:TOP_K].astype(I32)
        dest = starts[experts] + route[:, 4:4 + TOP_K].astype(I32)
        dest3 = dest.reshape(N // T, 1, TOP_K * T)
        xs = _dispatch_call(dest3, h2, T)
        y = _gmm_call(_gmm_plan(counts, TOP_K * N, bm), xs, w1[l], w3[l], w2[l], bm)
        x = _combine_call(dest3, x1, route, y, T).reshape(B, S, D)
    return x
```

```python
import functools

import jax
import jax.numpy as jnp
from jax import lax
from jax.experimental import pallas as pl
from jax.experimental.pallas import tpu as pltpu

F32 = jnp.float32
BF16 = jnp.bfloat16
I32 = jnp.int32

D_MODEL = 1024
A_HEADS = 8
A_HEAD_DIM = 64
BRANCH_WIDTH = 512
N_BRANCH = 3
POOL_WINDOWS = (2, 4, 8, 16)
GROUP_DIM = 128
N_GROUPS = 4
C_CHUNK = 128
N_EXPERT_GROUPS = 4
EXPERTS_PER_GROUP = 8
N_EXPERTS = N_EXPERT_GROUPS * EXPERTS_PER_GROUP
TOP_K = 2
D_EXPERT = 256
RMS_EPS = 1e-6
OFF_F = 3 * BRANCH_WIDTH
OFF_P = OFF_F + A_HEADS
OFF_U = OFF_P + BRANCH_WIDTH
OFF_SV = OFF_U + BRANCH_WIDTH
OFF_G = OFF_SV + BRANCH_WIDTH

LANES = 128
SUBLANES = 8
V7X_VMEM_BYTES = 64 * 1024 * 1024
VMEM_LIMIT = 56 * 1024 * 1024
ROW_TILE = D_MODEL // LANES
ROUTE_ROWS = SUBLANES
ISSUE_UNROLL = 8

PK_Q, PK_K, PK_V = 0, 512, 1024
PK_F = 1536
PK_P = PK_F + LANES
PK_U = PK_P + BRANCH_WIDTH
PK_SV = PK_U + BRANCH_WIDTH
PK_W = PK_SV + BRANCH_WIDTH
F_REP = 6

LOG2E = 1.4426950408889634
SCORE_CAP = 96.0
NEG = -1e30
HALO = 16


def _tiles(seq_len):
    t_proj = min(512, seq_len)
    t_q = min(512, seq_len)
    t_k = min(512, seq_len)
    moe_block = 512
    t_move = min(1024, seq_len)
    return t_proj, t_q, t_k, moe_block, t_move


def _const_spec(*shape):
    zeros = (0,) * len(shape)
    return pl.BlockSpec(shape, lambda *_: zeros, pipeline_mode=pl.Buffered(1))


def _rms(x, g):
    return x * lax.rsqrt(jnp.mean(x * x, axis=-1, keepdims=True) + RMS_EPS) * g


def _gelu_tanh(x):
    cdf = 0.5 * (1.0 + jnp.tanh(0.7978845608028654 * (x + 0.044715 * (x * x * x))))
    return x * cdf


def _log_sigmoid(x):
    return jnp.minimum(x, 0.0) - jnp.log1p(jnp.exp(-jnp.abs(x)))


def _split3(c):
    hi = c.astype(BF16).astype(F32)
    r = c - hi
    lo = r.astype(BF16).astype(F32)
    lolo = (r - lo).astype(BF16).astype(F32)
    return hi, lo, lolo


def _proj_kernel(x_ref, gmix_ref, w_ref, bf_ref, gq_ref, gk_ref, bd_ref, sel_ref, wpool_ref, spool_ref,
                 gsgu_ref, wsgu_ref, bsgu_ref,
                 qT_ref, k_ref, vT_ref, yb_ref, yc_ref,
                 carry_ref, halo_ref, *, T):
    i = pl.program_id(1)

    @pl.when(i == 0)
    def _():
        carry_ref[...] = jnp.zeros_like(carry_ref)
        halo_ref[...] = jnp.zeros_like(halo_ref)

    hb = _rms(x_ref[...], gmix_ref[...]).astype(BF16)

    def proj(lo, width):
        return jnp.dot(hb, w_ref[:, lo:lo + width], preferred_element_type=F32)

    bd = bd_ref[...]

    def head_norm(z, g):
        sq = z * z
        hi = sq.astype(BF16)
        lo = (sq - hi.astype(F32)).astype(BF16)
        ss = (jnp.dot(hi, bd, preferred_element_type=F32)
              + jnp.dot(lo, bd, preferred_element_type=F32))
        return z * lax.rsqrt(ss * (1.0 / A_HEAD_DIM) + RMS_EPS) * g

    qn = head_norm(proj(PK_Q, BRANCH_WIDTH), gq_ref[...]) * (A_HEAD_DIM ** -0.5 * LOG2E)
    kn = head_norm(proj(PK_K, BRANCH_WIDTH), gk_ref[...])
    zv = proj(PK_V, BRANCH_WIDTH)
    qk = qn * kn
    qk_hi = qk.astype(BF16)
    qk_lo = (qk - qk_hi.astype(F32)).astype(BF16)
    diag = (jnp.dot(qk_hi, sel_ref[...], preferred_element_type=F32)
            + jnp.dot(qk_lo, sel_ref[...], preferred_element_type=F32))

    logf = _log_sigmoid(proj(PK_F, LANES) + bf_ref[...])
    r_i = lax.broadcasted_iota(I32, (T, T), 0)
    c_i = lax.broadcasted_iota(I32, (T, T), 1)
    tri = (r_i >= c_i).astype(F32)
    c = jnp.dot(tri, logf, preferred_element_type=F32,
                precision=lax.Precision.HIGHEST) + carry_ref[...]
    carry_ref[...] = c[T - 1:T, :]

    lane = lax.broadcasted_iota(I32, (T, LANES), 1)
    lm = lane & 63
    grp = lm >> 3
    hsel = lm & 7
    first3 = grp < 3
    second3 = (grp >= 3) & (grp < F_REP)

    def parts(v):
        hi, lo, lolo = _split3(v)
        return jnp.where((grp == 0) | (grp == 3), hi, jnp.where((grp == 1) | (grp == 4), lo, lolo))

    k_aug = jnp.where(first3, -parts(c * LOG2E), 0.0)
    q_aug = jnp.where(second3, parts(c * LOG2E - diag), 0.0)

    for g in range(A_HEADS // 2):
        sl = slice(g * LANES, (g + 1) * LANES)
        zq2, zk2, zv2 = qn[:, sl], kn[:, sl], zv[:, sl]
        for par in range(2):
            h = 2 * g + par
            keep = (lane < 64) if par == 0 else (lane >= 64)
            oh_k = (second3 & (hsel == h)).astype(F32)
            oh_q = (first3 & (hsel == h)).astype(F32)
            k_ref[h] = jnp.where(keep, zk2, k_aug + oh_k).astype(BF16)
            qT_ref[h] = jnp.where(keep, zq2, q_aug + oh_q).T.astype(BF16)
            ones_lane = 64 if par == 0 else 0
            vT_ref[h] = jnp.where(keep, zv2, (lane == ones_lane).astype(F32)).T.astype(BF16)

    p = proj(PK_P, BRANCH_WIDTH)
    row8 = lax.broadcasted_iota(I32, (8, GROUP_DIM), 0)
    pos = i * T + lax.broadcasted_iota(I32, (T, GROUP_DIM), 0)

    def shift_down(v, tail, d):
        r = pltpu.roll(v, d, 0)
        rt = pltpu.roll(tail, d, 0)
        top = jnp.where(row8 < d, rt[0:8], r[0:8])
        return jnp.concatenate([top, r[8:]], axis=0)

    pooled = []
    for gi, w in enumerate(POOL_WINDOWS):
        sl = slice(gi * GROUP_DIM, (gi + 1) * GROUP_DIM)
        s = p[:, sl]
        for lv in range(gi + 1):
            tail = halo_ref[lv, :, sl]
            halo_ref[lv, :, sl] = s[T - HALO:T, :]
            s = s + shift_down(s, tail, 1 << lv)
        cnt = jnp.minimum(pos + 1, w).astype(F32)
        pooled.append(s / cnt - p[:, sl])
    pooled = jnp.concatenate(pooled, axis=1).astype(BF16)
    yb = jnp.dot(pooled, wpool_ref[...], preferred_element_type=F32) * spool_ref[...]
    yb_ref[...] = yb.astype(yb_ref.dtype)

    gu = _gelu_tanh(proj(PK_U, BRANCH_WIDTH))
    gv = _gelu_tanh(proj(PK_SV, BRANCH_WIDTH))
    t_r = lax.broadcasted_iota(I32, (C_CHUNK, C_CHUNK), 0)
    t_c = lax.broadcasted_iota(I32, (C_CHUNK, C_CHUNK), 1)
    causal = t_r >= t_c
    for g in range(N_GROUPS):
        sl = slice(g * GROUP_DIM, (g + 1) * GROUP_DIM)
        vn = _rms(gv[:, sl], gsgu_ref[:, sl]).astype(BF16)
        wc = jnp.where(causal, wsgu_ref[g], 0.0).astype(BF16)
        for ch in range(T // C_CHUNK):
            rows = slice(ch * C_CHUNK, (ch + 1) * C_CHUNK)
            mixed = jnp.dot(wc, vn[rows], preferred_element_type=F32) + bsgu_ref[g]
            yc_ref[rows, sl] = (gu[rows, sl] * mixed).astype(yc_ref.dtype)


def _proj_call(x, w, T):
    B, S, D = x.shape
    H = A_HEADS
    const = _const_spec
    return pl.pallas_call(
        functools.partial(_proj_kernel, T=T),
        grid=(B, S // T),
        in_specs=[
            pl.BlockSpec((None, T, D), lambda b, i: (b, i, 0)),
            const(1, D), const(D, PK_W), const(1, LANES), const(1, BRANCH_WIDTH), const(1, BRANCH_WIDTH),
            const(BRANCH_WIDTH, BRANCH_WIDTH), const(BRANCH_WIDTH, LANES), const(BRANCH_WIDTH, BRANCH_WIDTH),
            const(1, BRANCH_WIDTH), const(1, BRANCH_WIDTH), const(N_GROUPS, C_CHUNK, C_CHUNK), const(N_GROUPS, C_CHUNK, GROUP_DIM),
        ],
        out_specs=[
            pl.BlockSpec((None, H, LANES, T), lambda b, i: (b, 0, 0, i)),
            pl.BlockSpec((None, H, T, LANES), lambda b, i: (b, 0, i, 0)),
            pl.BlockSpec((None, H, LANES, T), lambda b, i: (b, 0, 0, i)),
            pl.BlockSpec((None, T, BRANCH_WIDTH), lambda b, i: (b, i, 0)),
            pl.BlockSpec((None, T, BRANCH_WIDTH), lambda b, i: (b, i, 0)),
        ],
        out_shape=[
            jax.ShapeDtypeStruct((B, H, LANES, S), BF16),
            jax.ShapeDtypeStruct((B, H, S, LANES), BF16),
            jax.ShapeDtypeStruct((B, H, LANES, S), BF16),
            jax.ShapeDtypeStruct((B, S, BRANCH_WIDTH), BF16),
            jax.ShapeDtypeStruct((B, S, BRANCH_WIDTH), BF16),
        ],
        scratch_shapes=[pltpu.VMEM((1, LANES), F32), pltpu.VMEM((4, HALO, BRANCH_WIDTH), F32)],
        compiler_params=pltpu.CompilerParams(
            dimension_semantics=("arbitrary", "arbitrary"), vmem_limit_bytes=VMEM_LIMIT),
        name="proj",
    )(x, w["g_mix"], w["w_pack"], w["b_f"], w["g_q"], w["g_k"], w["bd"], w["sel"], w["w_pool"], w["s_pool"],
      w["g_sgu"], w["w_sgu"], w["b_sgu"])


def _attn_kernel(qT_ref, k_ref, vT_ref, o_ref, m_ref, acc_ref, s_ref, *, tq, tk, online_max):
    qi = pl.program_id(2)
    q0 = qi * tq
    n_full = q0 // tk
    n_diag = tq // tk
    acc_ref[...] = jnp.zeros_like(acc_ref)
    if online_max:
        m_ref[...] = jnp.full(m_ref.shape, NEG, F32)

    def scores(j, slot):
        k0 = pl.multiple_of(j * tk, tk)
        for par in range(2):
            s_ref[slot, par] = jnp.dot(k_ref[par, pl.ds(k0, tk), :], qT_ref[par],
                                       preferred_element_type=F32)

    def consume(j, slot, masked):
        k0 = pl.multiple_of(j * tk, tk)
        for par in range(2):
            s = s_ref[slot, par]
            if masked:
                kpos = k0 + lax.broadcasted_iota(I32, (tk, tq), 0)
                qpos = q0 + lax.broadcasted_iota(I32, (tk, tq), 1)
                s = jnp.where(kpos <= qpos, s, NEG)
            v_blk = vT_ref[par, :, pl.ds(k0, tk)]
            if online_max:
                m_old = m_ref[par]
                m_new = jnp.maximum(m_old, jnp.max(s, axis=0, keepdims=True))
                p = jnp.exp2(s - m_new).astype(BF16)
                pv = jnp.dot(v_blk, p, preferred_element_type=F32)
                acc_ref[par] = acc_ref[par] * jnp.exp2(m_old - m_new) + pv
                m_ref[par] = m_new
            else:
                p = jnp.exp2(s).astype(BF16)
                acc_ref[par] += jnp.dot(v_blk, p, preferred_element_type=F32)

    assert n_diag == 1
    scores(0, 0)

    def body(j2, carry):
        j = 2 * j2
        scores(j + 1, 1)
        consume(j, 0, False)
        scores(j + 2, 0)
        consume(j + 1, 1, False)
        return carry

    lax.fori_loop(0, n_full // 2, body, 0)

    @pl.when(n_full % 2 == 0)
    def _():
        consume(n_full, 0, True)

    @pl.when(n_full % 2 == 1)
    def _():
        scores(n_full, 1)
        consume(n_full - 1, 0, False)
        consume(n_full, 1, True)

    lane = lax.broadcasted_iota(I32, (tq, LANES), 1)
    outs = []
    for par in range(2):
        acc = acc_ref[par]
        l = acc[64:65, :] if par == 0 else acc[0:1, :]
        outs.append((acc * (1.0 / l)).T)
    o_ref[...] = jnp.where(lane < 64, outs[0], outs[1]).astype(o_ref.dtype)


def _attn_call(qT, k, vT, tq, tk, online_max):
    B, H, _, S = qT.shape
    return pl.pallas_call(
        functools.partial(_attn_kernel, tq=tq, tk=tk, online_max=online_max),
        grid=(B, H // 2, S // tq),
        in_specs=[
            pl.BlockSpec((None, 2, LANES, tq), lambda b, g, q: (b, g, 0, q)),
            pl.BlockSpec((None, 2, S, LANES), lambda b, g, q: (b, g, 0, 0)),
            pl.BlockSpec((None, 2, LANES, S), lambda b, g, q: (b, g, 0, 0)),
        ],
        out_specs=pl.BlockSpec((None, tq, LANES), lambda b, g, q: (b, q, g)),
        out_shape=jax.ShapeDtypeStruct((B, S, BRANCH_WIDTH), BF16),
        scratch_shapes=[pltpu.VMEM((2, 1, tq), F32), pltpu.VMEM((2, LANES, tq), F32),
                        pltpu.VMEM((2, 2, tk, tq), F32)],
        compiler_params=pltpu.CompilerParams(
            dimension_semantics=("parallel", "parallel", "arbitrary"), vmem_limit_bytes=VMEM_LIMIT),
        name="attn_online" if online_max else "attn",
    )(qT, k, vT)


def _attention(qT, k, vT, g_q, g_k, tq, tk):
    bound = 16.0 * LOG2E * jnp.max(jnp.abs(g_q)) * jnp.max(jnp.abs(g_k))
    return lax.cond(bound <= SCORE_CAP,
                    lambda: _attn_call(qT, k, vT, tq, tk, False),
                    lambda: _attn_call(qT, k, vT, tq, tk, True))


def _merge_kernel(x_ref, ya_ref, yb_ref, yc_ref, gmix_ref, wg_ref, bg_ref, wb_ref, wo_ref,
                  gffn_ref, wr_ref, br_ref,
                  x1_ref, h2_ref, route_ref, route_t_ref, cnt_ref, carry_ref, *, T):
    i = pl.program_id(0)

    @pl.when(i == 0)
    def _():
        carry_ref[...] = jnp.zeros_like(carry_ref)

    x = x_ref[...]
    hb = _rms(x, gmix_ref[...]).astype(BF16)
    merged = None
    for bi, y_ref in enumerate((ya_ref, yb_ref, yc_ref)):
        sl = slice(bi * D_MODEL, (bi + 1) * D_MODEL)
        gate = jax.nn.sigmoid(jnp.dot(hb, wg_ref[:, sl], preferred_element_type=F32) + bg_ref[:, sl])
        term = gate * jnp.dot(y_ref[...], wb_ref[bi], preferred_element_type=F32)
        merged = term if merged is None else merged + term
    x1 = x + jnp.dot(merged.astype(BF16), wo_ref[...], preferred_element_type=F32)
    x1_ref[...] = x1
    h2 = _rms(x1, gffn_ref[...])
    _store_row_tiles(h2_ref, h2)

    h_hi = h2.astype(BF16)
    h_lo = (h2 - h_hi.astype(F32)).astype(BF16)
    hw = jnp.dot(h_hi, wr_ref[...], preferred_element_type=F32)
    lw = jnp.dot(h_lo, wr_ref[:, 0:LANES], preferred_element_type=F32)
    logits = hw[:, 0:LANES] + (hw[:, LANES:] + lw) + br_ref[...]
    lane = lax.broadcasted_iota(I32, (T, LANES), 1).astype(F32)
    big = float(LANES)

    def first_argmax(v):
        m = jnp.max(v, axis=-1, keepdims=True)
        return m, jnp.min(jnp.where(v == m, lane, big), axis=-1, keepdims=True)

    lg = jnp.where(lane < N_EXPERT_GROUPS, logits, NEG)
    mg, grp = first_argmax(lg)
    p_grp = 1.0 / jnp.sum(jnp.exp(lg - mg), axis=-1, keepdims=True)
    lo_lane = N_EXPERT_GROUPS + grp * EXPERTS_PER_GROUP
    le = jnp.where((lane >= lo_lane) & (lane < lo_lane + EXPERTS_PER_GROUP), logits, NEG)
    m1, i1 = first_argmax(le)
    m2, i2 = first_argmax(jnp.where(lane == i1, NEG, le))
    e21 = jnp.exp(m2 - m1)
    g1 = p_grp / (1.0 + e21)
    g2 = p_grp * e21 / (1.0 + e21)
    e1 = i1 - N_EXPERT_GROUPS
    e2 = i2 - N_EXPERT_GROUPS

    oh1 = lane == e1
    oh2 = lane == e2
    sel = (oh1 | oh2).astype(F32)
    r_i = lax.broadcasted_iota(I32, (T, T), 0)
    c_i = lax.broadcasted_iota(I32, (T, T), 1)
    before = (r_i > c_i).astype(BF16)
    seen = jnp.dot(before, sel.astype(BF16), preferred_element_type=F32) + carry_ref[...]
    r1 = jnp.sum(jnp.where(oh1, seen, 0.0), axis=-1, keepdims=True)
    r2 = jnp.sum(jnp.where(oh2, seen, 0.0), axis=-1, keepdims=True)
    carry_ref[...] = carry_ref[...] + jnp.sum(sel, axis=0, keepdims=True)
    cnt_ref[...] = carry_ref[...]

    route = jnp.zeros((T, LANES), F32)
    for idx, val in enumerate((e1, e2, g1, g2, r1, r2)):
        route = jnp.where(lane == idx, val, route)
    route_ref[...] = route
    route_t_ref[...] = route.T[0:ROUTE_ROWS]


def _row_tile_rows(s, n_rows):
    return pl.ds(s, n_rows, stride=ROW_TILE)


def _store_row_tiles(ref, v):
    for s in range(ROW_TILE):
        ref[_row_tile_rows(s, v.shape[0]), :] = v[:, s * LANES:(s + 1) * LANES]


def _load_row_tiles(ref):
    n_rows = ref.shape[0] // ROW_TILE
    return jnp.concatenate([ref[_row_tile_rows(s, n_rows), :] for s in range(ROW_TILE)], axis=-1)


def _merge_call(x2d, ya, yb, yc, w, T):
    N, D = x2d.shape
    const = _const_spec
    tile = lambda width: pl.BlockSpec((T, width), lambda i: (i, 0))
    return pl.pallas_call(
        functools.partial(_merge_kernel, T=T),
        grid=(N // T,),
        in_specs=[
            tile(D), tile(BRANCH_WIDTH), tile(BRANCH_WIDTH), tile(BRANCH_WIDTH),
            const(1, D), const(D, N_BRANCH * D), const(1, N_BRANCH * D),
            const(N_BRANCH, BRANCH_WIDTH, D), const(D, D), const(1, D), const(D, 2 * LANES), const(1, LANES),
        ],
        out_specs=[tile(D), pl.BlockSpec((T * ROW_TILE, LANES), lambda i: (i, 0)), tile(LANES),
                   pl.BlockSpec((ROUTE_ROWS, T), lambda i: (0, i)),
                   pl.BlockSpec((1, LANES), lambda i: (0, 0))],
        out_shape=[
            jax.ShapeDtypeStruct((N, D), F32),
            jax.ShapeDtypeStruct((N * ROW_TILE, LANES), F32),
            jax.ShapeDtypeStruct((N, LANES), F32),
            jax.ShapeDtypeStruct((ROUTE_ROWS, N), F32),
            jax.ShapeDtypeStruct((1, LANES), F32),
        ],
        scratch_shapes=[pltpu.VMEM((1, LANES), F32)],
        compiler_params=pltpu.CompilerParams(
            dimension_semantics=("arbitrary",), vmem_limit_bytes=VMEM_LIMIT),
        name="merge",
    )(x2d, ya, yb, yc, w["g_mix"], w["w_gate"], w["b_gate"], w["w_branch"], w["w_out"],
      w["g_ffn"], w["w_router"], w["b_router"])


def _dispatch_kernel(dest_ref, h2_ref, xs_ref, sem, *, T):
    def row_copy(t, d):
        return pltpu.make_async_copy(h2_ref.at[_row_tile(t)], xs_ref.at[_row_tile(d)], sem)

    def body(tb, carry):
        t0 = tb * ISSUE_UNROLL
        for u in range(ISSUE_UNROLL):
            for kk in range(TOP_K):
                row_copy(t0 + u, dest_ref[0, kk * T + t0 + u]).start()
        return carry

    lax.fori_loop(0, T // ISSUE_UNROLL, body, 0)
    for kk in range(TOP_K):
        pltpu.make_async_copy(h2_ref, xs_ref.at[pl.ds(0, T * ROW_TILE)], sem).wait()


def _row_tile(r):
    return pl.ds(pl.multiple_of(r * ROW_TILE, ROW_TILE), ROW_TILE)


def _dispatch_call(dest3, h2, T):
    N = h2.shape[0] // ROW_TILE
    return pl.pallas_call(
        functools.partial(_dispatch_kernel, T=T),
        grid=(N // T,),
        in_specs=[
            pl.BlockSpec((None, 1, TOP_K * T), lambda i: (i, 0, 0), memory_space=pltpu.SMEM),
            pl.BlockSpec((T * ROW_TILE, LANES), lambda i: (i, 0)),
        ],
        out_specs=pl.BlockSpec(memory_space=pl.ANY),
        out_shape=jax.ShapeDtypeStruct((TOP_K * N * ROW_TILE, LANES), F32),
        scratch_shapes=[pltpu.SemaphoreType.DMA],
        compiler_params=pltpu.CompilerParams(
            dimension_semantics=("arbitrary",), vmem_limit_bytes=VMEM_LIMIT),
        name="dispatch",
    )(dest3, h2)


def _gmm_kernel(blk_ref, exp_ref, lo_ref, hi_ref, xs_ref, w1_ref, w3_ref, w2_ref, y_ref,
                w13_bf, w2_bf, *, bm):
    i = pl.program_id(0)
    lo = lo_ref[i]
    hi = hi_ref[i]
    prev = jnp.maximum(i - 1, 0)
    first = jnp.logical_or(i == 0, blk_ref[i] != blk_ref[prev])
    new_expert = jnp.logical_or(i == 0, exp_ref[i] != exp_ref[prev])

    @pl.when(first)
    def _():
        y_ref[...] = jnp.zeros_like(y_ref)

    @pl.when(new_expert)
    def _():
        w13_bf[:, 0:D_EXPERT] = w1_ref[...].astype(BF16)
        w13_bf[:, D_EXPERT:] = w3_ref[...].astype(BF16)
        w2_bf[...] = w2_ref[...].astype(BF16)

    @pl.when(hi > lo)
    def _():
        x = _load_row_tiles(xs_ref).astype(BF16)
        ab = jnp.dot(x, w13_bf[...], preferred_element_type=F32)
        a = ab[:, 0:D_EXPERT]
        b = ab[:, D_EXPERT:]
        mid = (a * jax.nn.sigmoid(a) * b).astype(BF16)
        y = jnp.dot(mid, w2_bf[...], preferred_element_type=F32)
        row = lax.broadcasted_iota(I32, (bm, LANES), 0)
        mine = (row >= lo) & (row < hi)
        for s in range(ROW_TILE):
            rows = _row_tile_rows(s, bm)
            y_ref[rows, :] = jnp.where(mine, y[:, s * LANES:(s + 1) * LANES], y_ref[rows, :])


def _gmm_call(plan, xs, w1, w3, w2, layer, bm):
    A = xs.shape[0] // ROW_TILE
    D = D_MODEL
    n_items = plan[0].shape[0]
    grid_spec = pltpu.PrefetchScalarGridSpec(
        num_scalar_prefetch=4,
        grid=(n_items,),
        in_specs=[
            pl.BlockSpec((bm * ROW_TILE, LANES), lambda i, blk, ex, lo, hi: (blk[i], 0)),
            pl.BlockSpec((None, None, D, D_EXPERT), lambda i, blk, ex, lo, hi: (layer, ex[i], 0, 0)),
            pl.BlockSpec((None, None, D, D_EXPERT), lambda i, blk, ex, lo, hi: (layer, ex[i], 0, 0)),
            pl.BlockSpec((None, None, D_EXPERT, D), lambda i, blk, ex, lo, hi: (layer, ex[i], 0, 0)),
        ],
        out_specs=pl.BlockSpec((bm * ROW_TILE, LANES), lambda i, blk, ex, lo, hi: (blk[i], 0)),
        scratch_shapes=[pltpu.VMEM((D, 2 * D_EXPERT), BF16), pltpu.VMEM((D_EXPERT, D), BF16)],
    )
    return pl.pallas_call(
        functools.partial(_gmm_kernel, bm=bm),
        grid_spec=grid_spec,
        out_shape=jax.ShapeDtypeStruct((A * ROW_TILE, LANES), F32),
        compiler_params=pltpu.CompilerParams(
            dimension_semantics=("arbitrary",), vmem_limit_bytes=VMEM_LIMIT),
        name="gmm",
    )(*plan, xs, w1, w3, w2)


def _gmm_plan(counts, n_rows, bm):
    n_blk = n_rows // bm
    n_items = n_blk + N_EXPERTS - 1
    ends = jnp.cumsum(counts)
    starts = ends - counts
    first_blk = starts // bm
    n_it = jnp.where(counts > 0, (ends - 1) // bm - first_blk + 1, 0)
    item_end = jnp.cumsum(n_it)
    item_start = item_end - n_it
    total = item_end[-1]
    ids = jnp.arange(n_items, dtype=I32)
    valid = ids < total
    ex = jnp.minimum(jnp.searchsorted(item_end, ids, side="right").astype(I32), N_EXPERTS - 1)
    ex = jnp.where(valid, ex, ex[jnp.maximum(total - 1, 0)])
    blk = jnp.where(valid, first_blk[ex] + ids - item_start[ex], n_blk - 1)
    lo = jnp.where(valid, jnp.maximum(starts[ex], blk * bm) - blk * bm, 0)
    hi = jnp.where(valid, jnp.minimum(ends[ex], (blk + 1) * bm) - blk * bm, 0)
    return blk.astype(I32), ex.astype(I32), lo.astype(I32), hi.astype(I32)


def _combine_kernel(dest_ref, x1_ref, route_ref, y_hbm, o_ref, buf, sem, *, T):
    def row_copy(kk, t, d):
        return pltpu.make_async_copy(y_hbm.at[_row_tile(d)], buf.at[kk, _row_tile(t)], sem)

    def body(tb, carry):
        t0 = tb * ISSUE_UNROLL
        for u in range(ISSUE_UNROLL):
            for kk in range(TOP_K):
                row_copy(kk, t0 + u, dest_ref[0, kk * T + t0 + u]).start()
        return carry

    lax.fori_loop(0, T // ISSUE_UNROLL, body, 0)
    for kk in range(TOP_K):
        pltpu.make_async_copy(y_hbm.at[pl.ds(0, T * ROW_TILE)], buf.at[kk], sem).wait()
    g1 = route_ref[:, 2:3]
    g2 = route_ref[:, 3:4]
    for s in range(ROW_TILE):
        sl = slice(s * LANES, (s + 1) * LANES)
        rows = _row_tile_rows(s, T)
        o_ref[:, sl] = x1_ref[:, sl] + (g1 * buf[0, rows, :] + g2 * buf[1, rows, :])


def _combine_call(dest3, x1, route, y, T):
    N, D = x1.shape
    return pl.pallas_call(
        functools.partial(_combine_kernel, T=T),
        grid=(N // T,),
        in_specs=[
            pl.BlockSpec((None, 1, TOP_K * T), lambda i: (i, 0, 0), memory_space=pltpu.SMEM),
            pl.BlockSpec((T, D), lambda i: (i, 0)),
            pl.BlockSpec((T, LANES), lambda i: (i, 0)),
            pl.BlockSpec(memory_space=pl.ANY),
        ],
        out_specs=pl.BlockSpec((T, D), lambda i: (i, 0)),
        out_shape=jax.ShapeDtypeStruct((N, D), F32),
        scratch_shapes=[pltpu.VMEM((TOP_K, T * ROW_TILE, LANES), F32), pltpu.SemaphoreType.DMA],
        compiler_params=pltpu.CompilerParams(
            dimension_semantics=("arbitrary",), vmem_limit_bytes=VMEM_LIMIT),
        name="combine",
    )(dest3, x1, route, y)


def _block_diag(blocks):
    n = len(blocks)
    rows = []
    for i, b in enumerate(blocks):
        rows.append(jnp.concatenate(
            [b if j == i else jnp.zeros((b.shape[0], blocks[j].shape[1]), b.dtype) for j in range(n)], axis=1))
    return jnp.concatenate(rows, axis=0)


def _rep_forget(cols):
    half = jnp.concatenate(
        [jnp.tile(cols, (1, F_REP)), jnp.zeros((cols.shape[0], 64 - F_REP * A_HEADS), cols.dtype)], axis=1)
    return jnp.concatenate([half, half], axis=1)


def _head_to_gate_lanes():
    head = jnp.arange(BRANCH_WIDTH, dtype=I32)[:, None] // A_HEAD_DIM
    lane = jnp.arange(LANES, dtype=I32)[None, :]
    hit = ((lane & 7) == head) & (((lane & 63) >> 3) < F_REP)
    return hit.astype(BF16)


def _prep_layer(l, g_mix, w_in, b_fgate, b_gate, g_q, g_k, w_pool, s_pool, g_sgu, w_sgu, b_sgu,
                w_branch, w_out, g_ffn, w_rg, b_rg, w_re, b_re):
    wi = w_in[l]
    w_pack = jnp.concatenate(
        [wi[:, 0:OFF_F], _rep_forget(wi[:, OFF_F:OFF_P]), wi[:, OFF_P:OFF_G]], axis=1).astype(BF16)
    pad_r = LANES - N_EXPERT_GROUPS - N_EXPERTS
    w_r = jnp.concatenate([w_rg[l], w_re[l], jnp.zeros((D_MODEL, pad_r), F32)], axis=1)
    w_r_hi = w_r.astype(BF16)
    w_r_lo = (w_r - w_r_hi.astype(F32)).astype(BF16)
    return dict(
        g_mix=g_mix[l][None, :],
        w_pack=w_pack,
        b_f=_rep_forget(b_fgate[l][None, :]),
        g_q=jnp.tile(g_q[l], A_HEADS)[None, :],
        g_k=jnp.tile(g_k[l], A_HEADS)[None, :],
        bd=_block_diag([jnp.ones((A_HEAD_DIM, A_HEAD_DIM), BF16)] * A_HEADS),
        sel=_head_to_gate_lanes(),
        w_pool=_block_diag([w_pool[l][g] for g in range(N_GROUPS)]).astype(BF16),
        s_pool=s_pool[l][None, :],
        g_sgu=g_sgu[l][None, :],
        w_sgu=w_sgu[l],
        b_sgu=jnp.broadcast_to(b_sgu[l][:, :, None], (N_GROUPS, C_CHUNK, GROUP_DIM)),
        w_gate=wi[:, OFF_G:].astype(BF16),
        b_gate=b_gate[l].reshape(1, N_BRANCH * D_MODEL),
        w_branch=w_branch[l].astype(BF16),
        w_out=w_out[l].astype(BF16),
        g_ffn=g_ffn[l][None, :],
        w_router=jnp.concatenate([w_r_hi, w_r_lo], axis=1),
        b_router=jnp.concatenate([b_rg[l], b_re[l], jnp.zeros((pad_r,), F32)])[None, :],
    )


def kernel(x, g_mix, w_in, b_fgate, b_gate, g_q, g_k, w_pool, s_pool, g_sgu, w_sgu, b_sgu, w_branch, w_out,
           g_ffn, w_rg, b_rg, w_re, b_re, w1, w3, w2):
    B, S, D = x.shape
    assert D == D_MODEL and x.dtype == F32
    N = B * S
    T, tq, tk, bm, tm = _tiles(S)
    assert S % T == 0 and S % tq == 0 and tq == tk and T % C_CHUNK == 0
    assert (TOP_K * N) % bm == 0 and N % tm == 0 and tm % ISSUE_UNROLL == 0
    depth = w_in.shape[0]
    for l in range(depth):
        w = _prep_layer(l, g_mix, w_in, b_fgate, b_gate, g_q, g_k, w_pool, s_pool, g_sgu, w_sgu, b_sgu,
                        w_branch, w_out, g_ffn, w_rg, b_rg, w_re, b_re)
        qT, k, vT, yb, yc = _proj_call(x, w, T)
        ya = _attention(qT, k, vT, g_q[l], g_k[l], tq, tk)
        x1, h2, route, route_t, cnt = _merge_call(
            x.reshape(N, D), ya.reshape(N, -1), yb.reshape(N, -1), yc.reshape(N, -1), w, T)
        counts = cnt[0, :N_EXPERTS].astype(I32)
        starts = jnp.cumsum(counts) - counts
        experts = route_t[0:TOP_K].astype(I32)
        dest = starts[experts] + route_t[4:4 + TOP_K].astype(I32)
        dest3 = dest.reshape(TOP_K, N // tm, tm).transpose(1, 0, 2).reshape(N // tm, 1, TOP_K * tm)
        xs = _dispatch_call(dest3, h2, tm)
        y = _gmm_call(_gmm_plan(counts, TOP_K * N, bm), xs, w1, w3, w2, l, bm)
        x = _combine_call(dest3, x1, route, y, tm).reshape(B, S, D)
    return x
```

```python
import functools

import jax
import jax.numpy as jnp
from jax import lax
from jax.experimental import pallas as pl
from jax.experimental.pallas import tpu as pltpu

F32 = jnp.float32
BF16 = jnp.bfloat16
I32 = jnp.int32

D_MODEL = 1024
A_HEADS = 8
A_HEAD_DIM = 64
BRANCH_WIDTH = 512
N_BRANCH = 3
POOL_WINDOWS = (2, 4, 8, 16)
GROUP_DIM = 128
N_GROUPS = 4
C_CHUNK = 128
N_EXPERT_GROUPS = 4
EXPERTS_PER_GROUP = 8
N_EXPERTS = N_EXPERT_GROUPS * EXPERTS_PER_GROUP
TOP_K = 2
D_EXPERT = 256
RMS_EPS = 1e-6
OFF_F = 3 * BRANCH_WIDTH
OFF_P = OFF_F + A_HEADS
OFF_U = OFF_P + BRANCH_WIDTH
OFF_SV = OFF_U + BRANCH_WIDTH
OFF_G = OFF_SV + BRANCH_WIDTH

LANES = 128
SUBLANES = 8
V7X_VMEM_BYTES = 64 * 1024 * 1024
VMEM_LIMIT = 56 * 1024 * 1024
ROW_TILE = D_MODEL // LANES
ROUTE_ROWS = SUBLANES
ISSUE_UNROLL = 8
N_DMA_PRIORITIES = 2

PK_Q, PK_K, PK_V = 0, 512, 1024
PK_F = 1536
PK_P = PK_F + LANES
PK_U = PK_P + BRANCH_WIDTH
PK_SV = PK_U + BRANCH_WIDTH
PK_W = PK_SV + BRANCH_WIDTH
F_REP = 6

LOG2E = 1.4426950408889634
SCORE_CAP = 96.0
NEG = -1e30
HALO = 16


def _tiles(seq_len):
    t_proj = min(512, seq_len)
    t_q = min(512, seq_len)
    t_k = min(512, seq_len)
    moe_block = 512
    t_move = min(1024, seq_len)
    return t_proj, t_q, t_k, moe_block, t_move


def _const_spec(*shape):
    zeros = (0,) * len(shape)
    return pl.BlockSpec(shape, lambda *_: zeros, pipeline_mode=pl.Buffered(1))


def _rms(x, g):
    return x * lax.rsqrt(jnp.mean(x * x, axis=-1, keepdims=True) + RMS_EPS) * g


def _gelu_tanh(x):
    cdf = 0.5 * (1.0 + jnp.tanh(0.7978845608028654 * (x + 0.044715 * (x * x * x))))
    return x * cdf


def _log_sigmoid(x):
    return jnp.minimum(x, 0.0) - jnp.log1p(jnp.exp(-jnp.abs(x)))


def _split3(c):
    hi = c.astype(BF16).astype(F32)
    r = c - hi
    lo = r.astype(BF16).astype(F32)
    lolo = (r - lo).astype(BF16).astype(F32)
    return hi, lo, lolo


def _proj_kernel(x_ref, gmix_ref, w_ref, bf_ref, gq_ref, gk_ref, bd_ref, sel_ref, wpool_ref, spool_ref,
                 gsgu_ref, wsgu_ref, bsgu_ref,
                 qT_ref, k_ref, vT_ref, yb_ref, yc_ref,
                 carry_ref, halo_ref, *, T):
    i = pl.program_id(1)

    @pl.when(i == 0)
    def _():
        carry_ref[...] = jnp.zeros_like(carry_ref)
        halo_ref[...] = jnp.zeros_like(halo_ref)

    hb = _rms(x_ref[...], gmix_ref[...]).astype(BF16)

    def proj(lo, width):
        return jnp.dot(hb, w_ref[:, lo:lo + width], preferred_element_type=F32)

    bd = bd_ref[...]

    def head_norm(z, g):
        sq = z * z
        hi = sq.astype(BF16)
        lo = (sq - hi.astype(F32)).astype(BF16)
        ss = (jnp.dot(hi, bd, preferred_element_type=F32)
              + jnp.dot(lo, bd, preferred_element_type=F32))
        return z * lax.rsqrt(ss * (1.0 / A_HEAD_DIM) + RMS_EPS) * g

    qn = head_norm(proj(PK_Q, BRANCH_WIDTH), gq_ref[...]) * (A_HEAD_DIM ** -0.5 * LOG2E)
    kn = head_norm(proj(PK_K, BRANCH_WIDTH), gk_ref[...])
    zv = proj(PK_V, BRANCH_WIDTH)
    qk = qn * kn
    qk_hi = qk.astype(BF16)
    qk_lo = (qk - qk_hi.astype(F32)).astype(BF16)
    diag = (jnp.dot(qk_hi, sel_ref[...], preferred_element_type=F32)
            + jnp.dot(qk_lo, sel_ref[...], preferred_element_type=F32))

    logf = _log_sigmoid(proj(PK_F, LANES) + bf_ref[...])
    r_i = lax.broadcasted_iota(I32, (T, T), 0)
    c_i = lax.broadcasted_iota(I32, (T, T), 1)
    tri = (r_i >= c_i).astype(F32)
    c = jnp.dot(tri, logf, preferred_element_type=F32,
                precision=lax.Precision.HIGHEST) + carry_ref[...]
    carry_ref[...] = c[T - 1:T, :]

    lane = lax.broadcasted_iota(I32, (T, LANES), 1)
    lm = lane & 63
    grp = lm >> 3
    hsel = lm & 7
    first3 = grp < 3
    second3 = (grp >= 3) & (grp < F_REP)

    def parts(v):
        hi, lo, lolo = _split3(v)
        return jnp.where((grp == 0) | (grp == 3), hi, jnp.where((grp == 1) | (grp == 4), lo, lolo))

    k_aug = jnp.where(first3, -parts(c * LOG2E), 0.0)
    q_aug = jnp.where(second3, parts(c * LOG2E - diag), 0.0)

    for g in range(A_HEADS // 2):
        sl = slice(g * LANES, (g + 1) * LANES)
        zq2, zk2, zv2 = qn[:, sl], kn[:, sl], zv[:, sl]
        for par in range(2):
            h = 2 * g + par
            keep = (lane < 64) if par == 0 else (lane >= 64)
            oh_k = (second3 & (hsel == h)).astype(F32)
            oh_q = (first3 & (hsel == h)).astype(F32)
            k_ref[h] = jnp.where(keep, zk2, k_aug + oh_k).astype(BF16)
            qT_ref[h] = jnp.where(keep, zq2, q_aug + oh_q).T.astype(BF16)
            ones_lane = 64 if par == 0 else 0
            vT_ref[h] = jnp.where(keep, zv2, (lane == ones_lane).astype(F32)).T.astype(BF16)

    p = proj(PK_P, BRANCH_WIDTH)
    row8 = lax.broadcasted_iota(I32, (8, GROUP_DIM), 0)
    pos = i * T + lax.broadcasted_iota(I32, (T, GROUP_DIM), 0)

    def shift_down(v, tail, d):
        r = pltpu.roll(v, d, 0)
        rt = pltpu.roll(tail, d, 0)
        top = jnp.where(row8 < d, rt[0:8], r[0:8])
        return jnp.concatenate([top, r[8:]], axis=0)

    pooled = []
    for gi, w in enumerate(POOL_WINDOWS):
        sl = slice(gi * GROUP_DIM, (gi + 1) * GROUP_DIM)
        s = p[:, sl]
        for lv in range(gi + 1):
            tail = halo_ref[lv, :, sl]
            halo_ref[lv, :, sl] = s[T - HALO:T, :]
            s = s + shift_down(s, tail, 1 << lv)
        cnt = jnp.minimum(pos + 1, w).astype(F32)
        pooled.append(s / cnt - p[:, sl])
    pooled = jnp.concatenate(pooled, axis=1).astype(BF16)
    yb = jnp.dot(pooled, wpool_ref[...], preferred_element_type=F32) * spool_ref[...]
    yb_ref[...] = yb.astype(yb_ref.dtype)

    gu = _gelu_tanh(proj(PK_U, BRANCH_WIDTH))
    gv = _gelu_tanh(proj(PK_SV, BRANCH_WIDTH))
    t_r = lax.broadcasted_iota(I32, (C_CHUNK, C_CHUNK), 0)
    t_c = lax.broadcasted_iota(I32, (C_CHUNK, C_CHUNK), 1)
    causal = t_r >= t_c
    for g in range(N_GROUPS):
        sl = slice(g * GROUP_DIM, (g + 1) * GROUP_DIM)
        vn = _rms(gv[:, sl], gsgu_ref[:, sl]).astype(BF16)
        wc = jnp.where(causal, wsgu_ref[g], 0.0).astype(BF16)
        for ch in range(T // C_CHUNK):
            rows = slice(ch * C_CHUNK, (ch + 1) * C_CHUNK)
            mixed = jnp.dot(wc, vn[rows], preferred_element_type=F32) + bsgu_ref[g]
            yc_ref[rows, sl] = (gu[rows, sl] * mixed).astype(yc_ref.dtype)


def _proj_call(x, w, T):
    B, S, D = x.shape
    H = A_HEADS
    const = _const_spec
    return pl.pallas_call(
        functools.partial(_proj_kernel, T=T),
        grid=(B, S // T),
        in_specs=[
            pl.BlockSpec((None, T, D), lambda b, i: (b, i, 0)),
            const(1, D), const(D, PK_W), const(1, LANES), const(1, BRANCH_WIDTH), const(1, BRANCH_WIDTH),
            const(BRANCH_WIDTH, BRANCH_WIDTH), const(BRANCH_WIDTH, LANES), const(BRANCH_WIDTH, BRANCH_WIDTH),
            const(1, BRANCH_WIDTH), const(1, BRANCH_WIDTH), const(N_GROUPS, C_CHUNK, C_CHUNK), const(N_GROUPS, C_CHUNK, GROUP_DIM),
        ],
        out_specs=[
            pl.BlockSpec((None, H, LANES, T), lambda b, i: (b, 0, 0, i)),
            pl.BlockSpec((None, H, T, LANES), lambda b, i: (b, 0, i, 0)),
            pl.BlockSpec((None, H, LANES, T), lambda b, i: (b, 0, 0, i)),
            pl.BlockSpec((None, T, BRANCH_WIDTH), lambda b, i: (b, i, 0)),
            pl.BlockSpec((None, T, BRANCH_WIDTH), lambda b, i: (b, i, 0)),
        ],
        out_shape=[
            jax.ShapeDtypeStruct((B, H, LANES, S), BF16),
            jax.ShapeDtypeStruct((B, H, S, LANES), BF16),
            jax.ShapeDtypeStruct((B, H, LANES, S), BF16),
            jax.ShapeDtypeStruct((B, S, BRANCH_WIDTH), BF16),
            jax.ShapeDtypeStruct((B, S, BRANCH_WIDTH), BF16),
        ],
        scratch_shapes=[pltpu.VMEM((1, LANES), F32), pltpu.VMEM((4, HALO, BRANCH_WIDTH), F32)],
        compiler_params=pltpu.CompilerParams(
            dimension_semantics=("arbitrary", "arbitrary"), vmem_limit_bytes=VMEM_LIMIT),
        name="proj",
    )(x, w["g_mix"], w["w_pack"], w["b_f"], w["g_q"], w["g_k"], w["bd"], w["sel"], w["w_pool"], w["s_pool"],
      w["g_sgu"], w["w_sgu"], w["b_sgu"])


def _attn_kernel(qT_ref, k_ref, vT_ref, o_ref, m_ref, acc_ref, s_ref, *, tq, tk, online_max):
    qi = pl.program_id(2)
    q0 = qi * tq
    n_full = q0 // tk
    n_diag = tq // tk
    acc_ref[...] = jnp.zeros_like(acc_ref)
    if online_max:
        m_ref[...] = jnp.full(m_ref.shape, NEG, F32)

    def scores(j, slot):
        k0 = pl.multiple_of(j * tk, tk)
        for par in range(2):
            s_ref[slot, par] = jnp.dot(k_ref[par, pl.ds(k0, tk), :], qT_ref[par],
                                       preferred_element_type=F32)

    def consume(j, slot, masked):
        k0 = pl.multiple_of(j * tk, tk)
        for par in range(2):
            s = s_ref[slot, par]
            if masked:
                kpos = k0 + lax.broadcasted_iota(I32, (tk, tq), 0)
                qpos = q0 + lax.broadcasted_iota(I32, (tk, tq), 1)
                s = jnp.where(kpos <= qpos, s, NEG)
            v_blk = vT_ref[par, :, pl.ds(k0, tk)]
            if online_max:
                m_old = m_ref[par]
                m_new = jnp.maximum(m_old, jnp.max(s, axis=0, keepdims=True))
                p = jnp.exp2(s - m_new).astype(BF16)
                pv = jnp.dot(v_blk, p, preferred_element_type=F32)
                acc_ref[par] = acc_ref[par] * jnp.exp2(m_old - m_new) + pv
                m_ref[par] = m_new
            else:
                p = jnp.exp2(s).astype(BF16)
                acc_ref[par] += jnp.dot(v_blk, p, preferred_element_type=F32)

    assert n_diag == 1
    scores(0, 0)

    def body(j2, carry):
        j = 2 * j2
        scores(j + 1, 1)
        consume(j, 0, False)
        scores(j + 2, 0)
        consume(j + 1, 1, False)
        return carry

    lax.fori_loop(0, n_full // 2, body, 0)

    @pl.when(n_full % 2 == 0)
    def _():
        consume(n_full, 0, True)

    @pl.when(n_full % 2 == 1)
    def _():
        scores(n_full, 1)
        consume(n_full - 1, 0, False)
        consume(n_full, 1, True)

    lane = lax.broadcasted_iota(I32, (tq, LANES), 1)
    outs = []
    for par in range(2):
        acc = acc_ref[par]
        l = acc[64:65, :] if par == 0 else acc[0:1, :]
        outs.append((acc * (1.0 / l)).T)
    o_ref[...] = jnp.where(lane < 64, outs[0], outs[1]).astype(o_ref.dtype)


def _attn_call(qT, k, vT, tq, tk, online_max):
    B, H, _, S = qT.shape
    return pl.pallas_call(
        functools.partial(_attn_kernel, tq=tq, tk=tk, online_max=online_max),
        grid=(B, H // 2, S // tq),
        in_specs=[
            pl.BlockSpec((None, 2, LANES, tq), lambda b, g, q: (b, g, 0, q)),
            pl.BlockSpec((None, 2, S, LANES), lambda b, g, q: (b, g, 0, 0)),
            pl.BlockSpec((None, 2, LANES, S), lambda b, g, q: (b, g, 0, 0)),
        ],
        out_specs=pl.BlockSpec((None, tq, LANES), lambda b, g, q: (b, q, g)),
        out_shape=jax.ShapeDtypeStruct((B, S, BRANCH_WIDTH), BF16),
        scratch_shapes=[pltpu.VMEM((2, 1, tq), F32), pltpu.VMEM((2, LANES, tq), F32),
                        pltpu.VMEM((2, 2, tk, tq), F32)],
        compiler_params=pltpu.CompilerParams(
            dimension_semantics=("parallel", "parallel", "arbitrary"), vmem_limit_bytes=VMEM_LIMIT),
        name="attn_online" if online_max else "attn",
    )(qT, k, vT)


def _attention(qT, k, vT, g_q, g_k, tq, tk):
    bound = 16.0 * LOG2E * jnp.max(jnp.abs(g_q)) * jnp.max(jnp.abs(g_k))
    return lax.cond(bound <= SCORE_CAP,
                    lambda: _attn_call(qT, k, vT, tq, tk, False),
                    lambda: _attn_call(qT, k, vT, tq, tk, True))


def _merge_kernel(x_ref, ya_ref, yb_ref, yc_ref, gmix_ref, wg_ref, bg_ref, wb_ref, wo_ref,
                  gffn_ref, wr_ref, br_ref,
                  x1_ref, h2_ref, route_ref, route_t_ref, cnt_ref, carry_ref, *, T):
    i = pl.program_id(0)

    @pl.when(i == 0)
    def _():
        carry_ref[...] = jnp.zeros_like(carry_ref)

    x = x_ref[...]
    hb = _rms(x, gmix_ref[...]).astype(BF16)
    merged = None
    for bi, y_ref in enumerate((ya_ref, yb_ref, yc_ref)):
        sl = slice(bi * D_MODEL, (bi + 1) * D_MODEL)
        gate = jax.nn.sigmoid(jnp.dot(hb, wg_ref[:, sl], preferred_element_type=F32) + bg_ref[:, sl])
        term = gate * jnp.dot(y_ref[...], wb_ref[bi], preferred_element_type=F32)
        merged = term if merged is None else merged + term
    x1 = x + jnp.dot(merged.astype(BF16), wo_ref[...], preferred_element_type=F32)
    x1_ref[...] = x1
    h2 = _rms(x1, gffn_ref[...])
    _store_row_tiles(h2_ref, h2)

    h_hi = h2.astype(BF16)
    h_lo = (h2 - h_hi.astype(F32)).astype(BF16)
    hw = jnp.dot(h_hi, wr_ref[...], preferred_element_type=F32)
    lw = jnp.dot(h_lo, wr_ref[:, 0:LANES], preferred_element_type=F32)
    logits = hw[:, 0:LANES] + (hw[:, LANES:] + lw) + br_ref[...]
    lane = lax.broadcasted_iota(I32, (T, LANES), 1).astype(F32)
    big = float(LANES)

    def first_argmax(v):
        m = jnp.max(v, axis=-1, keepdims=True)
        return m, jnp.min(jnp.where(v == m, lane, big), axis=-1, keepdims=True)

    lg = jnp.where(lane < N_EXPERT_GROUPS, logits, NEG)
    mg, grp = first_argmax(lg)
    p_grp = 1.0 / jnp.sum(jnp.exp(lg - mg), axis=-1, keepdims=True)
    lo_lane = N_EXPERT_GROUPS + grp * EXPERTS_PER_GROUP
    le = jnp.where((lane >= lo_lane) & (lane < lo_lane + EXPERTS_PER_GROUP), logits, NEG)
    m1, i1 = first_argmax(le)
    m2, i2 = first_argmax(jnp.where(lane == i1, NEG, le))
    e21 = jnp.exp(m2 - m1)
    g1 = p_grp / (1.0 + e21)
    g2 = p_grp * e21 / (1.0 + e21)
    e1 = i1 - N_EXPERT_GROUPS
    e2 = i2 - N_EXPERT_GROUPS

    oh1 = lane == e1
    oh2 = lane == e2
    sel = (oh1 | oh2).astype(F32)
    r_i = lax.broadcasted_iota(I32, (T, T), 0)
    c_i = lax.broadcasted_iota(I32, (T, T), 1)
    before = (r_i > c_i).astype(BF16)
    seen = jnp.dot(before, sel.astype(BF16), preferred_element_type=F32) + carry_ref[...]
    r1 = jnp.sum(jnp.where(oh1, seen, 0.0), axis=-1, keepdims=True)
    r2 = jnp.sum(jnp.where(oh2, seen, 0.0), axis=-1, keepdims=True)
    carry_ref[...] = carry_ref[...] + jnp.sum(sel, axis=0, keepdims=True)
    cnt_ref[...] = carry_ref[...]

    route = jnp.zeros((T, LANES), F32)
    for idx, val in enumerate((e1, e2, g1, g2, r1, r2)):
        route = jnp.where(lane == idx, val, route)
    route_ref[...] = route
    route_t_ref[...] = route.T[0:ROUTE_ROWS]


def _row_tile_rows(s, n_rows):
    return pl.ds(s, n_rows, stride=ROW_TILE)


def _store_row_tiles(ref, v):
    for s in range(ROW_TILE):
        ref[_row_tile_rows(s, v.shape[0]), :] = v[:, s * LANES:(s + 1) * LANES]


def _load_row_tiles(ref):
    n_rows = ref.shape[0] // ROW_TILE
    return jnp.concatenate([ref[_row_tile_rows(s, n_rows), :] for s in range(ROW_TILE)], axis=-1)


def _merge_call(x2d, ya, yb, yc, w, T):
    N, D = x2d.shape
    const = _const_spec
    tile = lambda width: pl.BlockSpec((T, width), lambda i: (i, 0))
    return pl.pallas_call(
        functools.partial(_merge_kernel, T=T),
        grid=(N // T,),
        in_specs=[
            tile(D), tile(BRANCH_WIDTH), tile(BRANCH_WIDTH), tile(BRANCH_WIDTH),
            const(1, D), const(D, N_BRANCH * D), const(1, N_BRANCH * D),
            const(N_BRANCH, BRANCH_WIDTH, D), const(D, D), const(1, D), const(D, 2 * LANES), const(1, LANES),
        ],
        out_specs=[tile(D), pl.BlockSpec((T * ROW_TILE, LANES), lambda i: (i, 0)), tile(LANES),
                   pl.BlockSpec((ROUTE_ROWS, T), lambda i: (0, i)),
                   pl.BlockSpec((1, LANES), lambda i: (0, 0))],
        out_shape=[
            jax.ShapeDtypeStruct((N, D), F32),
            jax.ShapeDtypeStruct((N * ROW_TILE, LANES), F32),
            jax.ShapeDtypeStruct((N, LANES), F32),
            jax.ShapeDtypeStruct((ROUTE_ROWS, N), F32),
            jax.ShapeDtypeStruct((1, LANES), F32),
        ],
        scratch_shapes=[pltpu.VMEM((1, LANES), F32)],
        compiler_params=pltpu.CompilerParams(
            dimension_semantics=("arbitrary",), vmem_limit_bytes=VMEM_LIMIT),
        name="merge",
    )(x2d, ya, yb, yc, w["g_mix"], w["w_gate"], w["b_gate"], w["w_branch"], w["w_out"],
      w["g_ffn"], w["w_router"], w["b_router"])


def _dispatch_kernel(dest_ref, h2_ref, xs_ref, sem, *, T):
    def row_copy(t, d):
        return pltpu.make_async_copy(h2_ref.at[_row_tile(t)], xs_ref.at[_row_tile(d)], sem)

    def body(tb, carry):
        t0 = tb * ISSUE_UNROLL
        for u in range(ISSUE_UNROLL):
            for kk in range(TOP_K):
                row_copy(t0 + u, dest_ref[0, kk * T + t0 + u]).start(priority=kk % N_DMA_PRIORITIES)
        return carry

    lax.fori_loop(0, T // ISSUE_UNROLL, body, 0)
    for kk in range(TOP_K):
        pltpu.make_async_copy(h2_ref, xs_ref.at[pl.ds(0, T * ROW_TILE)], sem).wait()


def _row_tile(r):
    return pl.ds(pl.multiple_of(r * ROW_TILE, ROW_TILE), ROW_TILE)


def _dispatch_call(dest3, h2, T):
    N = h2.shape[0] // ROW_TILE
    return pl.pallas_call(
        functools.partial(_dispatch_kernel, T=T),
        grid=(N // T,),
        in_specs=[
            pl.BlockSpec((None, 1, TOP_K * T), lambda i: (i, 0, 0), memory_space=pltpu.SMEM),
            pl.BlockSpec((T * ROW_TILE, LANES), lambda i: (i, 0)),
        ],
        out_specs=pl.BlockSpec(memory_space=pl.ANY),
        out_shape=jax.ShapeDtypeStruct((TOP_K * N * ROW_TILE, LANES), F32),
        scratch_shapes=[pltpu.SemaphoreType.DMA],
        compiler_params=pltpu.CompilerParams(
            dimension_semantics=("arbitrary",), vmem_limit_bytes=VMEM_LIMIT),
        name="dispatch",
    )(dest3, h2)


def _gmm_kernel(blk_ref, exp_ref, lo_ref, hi_ref, xs_ref, w1_ref, w3_ref, w2_ref, y_ref,
                w13_bf, w2_bf, *, bm):
    i = pl.program_id(0)
    lo = lo_ref[i]
    hi = hi_ref[i]
    prev = jnp.maximum(i - 1, 0)
    first = jnp.logical_or(i == 0, blk_ref[i] != blk_ref[prev])
    new_expert = jnp.logical_or(i == 0, exp_ref[i] != exp_ref[prev])

    @pl.when(first)
    def _():
        y_ref[...] = jnp.zeros_like(y_ref)

    @pl.when(new_expert)
    def _():
        w13_bf[:, 0:D_EXPERT] = w1_ref[...].astype(BF16)
        w13_bf[:, D_EXPERT:] = w3_ref[...].astype(BF16)
        w2_bf[...] = w2_ref[...].astype(BF16)

    @pl.when(hi > lo)
    def _():
        x = _load_row_tiles(xs_ref).astype(BF16)
        ab = jnp.dot(x, w13_bf[...], preferred_element_type=F32)
        a = ab[:, 0:D_EXPERT]
        b = ab[:, D_EXPERT:]
        mid = (a * jax.nn.sigmoid(a) * b).astype(BF16)
        y = jnp.dot(mid, w2_bf[...], preferred_element_type=F32)
        row = lax.broadcasted_iota(I32, (bm, LANES), 0)
        mine = (row >= lo) & (row < hi)
        for s in range(ROW_TILE):
            rows = _row_tile_rows(s, bm)
            y_ref[rows, :] = jnp.where(mine, y[:, s * LANES:(s + 1) * LANES], y_ref[rows, :])


def _gmm_call(plan, xs, w1, w3, w2, layer, bm):
    A = xs.shape[0] // ROW_TILE
    D = D_MODEL
    n_items = plan[0].shape[0]
    grid_spec = pltpu.PrefetchScalarGridSpec(
        num_scalar_prefetch=4,
        grid=(n_items,),
        in_specs=[
            pl.BlockSpec((bm * ROW_TILE, LANES), lambda i, blk, ex, lo, hi: (blk[i], 0)),
            pl.BlockSpec((None, None, D, D_EXPERT), lambda i, blk, ex, lo, hi: (layer, ex[i], 0, 0)),
            pl.BlockSpec((None, None, D, D_EXPERT), lambda i, blk, ex, lo, hi: (layer, ex[i], 0, 0)),
            pl.BlockSpec((None, None, D_EXPERT, D), lambda i, blk, ex, lo, hi: (layer, ex[i], 0, 0)),
        ],
        out_specs=pl.BlockSpec((bm * ROW_TILE, LANES), lambda i, blk, ex, lo, hi: (blk[i], 0)),
        scratch_shapes=[pltpu.VMEM((D, 2 * D_EXPERT), BF16), pltpu.VMEM((D_EXPERT, D), BF16)],
    )
    return pl.pallas_call(
        functools.partial(_gmm_kernel, bm=bm),
        grid_spec=grid_spec,
        out_shape=jax.ShapeDtypeStruct((A * ROW_TILE, LANES), F32),
        compiler_params=pltpu.CompilerParams(
            dimension_semantics=("arbitrary",), vmem_limit_bytes=VMEM_LIMIT),
        name="gmm",
    )(*plan, xs, w1, w3, w2)


def _gmm_plan(counts, n_rows, bm):
    n_blk = n_rows // bm
    n_items = n_blk + N_EXPERTS - 1
    ends = jnp.cumsum(counts)
    starts = ends - counts
    first_blk = starts // bm
    n_it = jnp.where(counts > 0, (ends - 1) // bm - first_blk + 1, 0)
    item_end = jnp.cumsum(n_it)
    item_start = item_end - n_it
    total = item_end[-1]
    ids = jnp.arange(n_items, dtype=I32)
    valid = ids < total
    ex = jnp.minimum(jnp.searchsorted(item_end, ids, side="right").astype(I32), N_EXPERTS - 1)
    ex = jnp.where(valid, ex, ex[jnp.maximum(total - 1, 0)])
    blk = jnp.where(valid, first_blk[ex] + ids - item_start[ex], n_blk - 1)
    lo = jnp.where(valid, jnp.maximum(starts[ex], blk * bm) - blk * bm, 0)
    hi = jnp.where(valid, jnp.minimum(ends[ex], (blk + 1) * bm) - blk * bm, 0)
    return blk.astype(I32), ex.astype(I32), lo.astype(I32), hi.astype(I32)


def _combine_kernel(dest_ref, x1_ref, route_ref, y_hbm, o_ref, buf, sem, *, T):
    def row_copy(kk, t, d):
        return pltpu.make_async_copy(y_hbm.at[_row_tile(d)], buf.at[kk, _row_tile(t)], sem)

    def body(tb, carry):
        t0 = tb * ISSUE_UNROLL
        for u in range(ISSUE_UNROLL):
            for kk in range(TOP_K):
                row_copy(kk, t0 + u, dest_ref[0, kk * T + t0 + u]).start(priority=kk % N_DMA_PRIORITIES)
        return carry

    lax.fori_loop(0, T // ISSUE_UNROLL, body, 0)
    for kk in range(TOP_K):
        pltpu.make_async_copy(y_hbm.at[pl.ds(0, T * ROW_TILE)], buf.at[kk], sem).wait()
    g1 = route_ref[:, 2:3]
    g2 = route_ref[:, 3:4]
    for s in range(ROW_TILE):
        sl = slice(s * LANES, (s + 1) * LANES)
        rows = _row_tile_rows(s, T)
        o_ref[:, sl] = x1_ref[:, sl] + (g1 * buf[0, rows, :] + g2 * buf[1, rows, :])


def _combine_call(dest3, x1, route, y, T):
    N, D = x1.shape
    return pl.pallas_call(
        functools.partial(_combine_kernel, T=T),
        grid=(N // T,),
        in_specs=[
            pl.BlockSpec((None, 1, TOP_K * T), lambda i: (i, 0, 0), memory_space=pltpu.SMEM),
            pl.BlockSpec((T, D), lambda i: (i, 0)),
            pl.BlockSpec((T, LANES), lambda i: (i, 0)),
            pl.BlockSpec(memory_space=pl.ANY),
        ],
        out_specs=pl.BlockSpec((T, D), lambda i: (i, 0)),
        out_shape=jax.ShapeDtypeStruct((N, D), F32),
        scratch_shapes=[pltpu.VMEM((TOP_K, T * ROW_TILE, LANES), F32), pltpu.SemaphoreType.DMA],
        compiler_params=pltpu.CompilerParams(
            dimension_semantics=("arbitrary",), vmem_limit_bytes=VMEM_LIMIT),
        name="combine",
    )(dest3, x1, route, y)


def _block_diag(blocks):
    n = len(blocks)
    rows = []
    for i, b in enumerate(blocks):
        rows.append(jnp.concatenate(
            [b if j == i else jnp.zeros((b.shape[0], blocks[j].shape[1]), b.dtype) for j in range(n)], axis=1))
    return jnp.concatenate(rows, axis=0)


def _rep_forget(cols):
    half = jnp.concatenate(
        [jnp.tile(cols, (1, F_REP)), jnp.zeros((cols.shape[0], 64 - F_REP * A_HEADS), cols.dtype)], axis=1)
    return jnp.concatenate([half, half], axis=1)


def _head_to_gate_lanes():
    head = jnp.arange(BRANCH_WIDTH, dtype=I32)[:, None] // A_HEAD_DIM
    lane = jnp.arange(LANES, dtype=I32)[None, :]
    hit = ((lane & 7) == head) & (((lane & 63) >> 3) < F_REP)
    return hit.astype(BF16)


def _prep_layer(l, g_mix, w_in, b_fgate, b_gate, g_q, g_k, w_pool, s_pool, g_sgu, w_sgu, b_sgu,
                w_branch, w_out, g_ffn, w_rg, b_rg, w_re, b_re):
    wi = w_in[l]
    w_pack = jnp.concatenate(
        [wi[:, 0:OFF_F], _rep_forget(wi[:, OFF_F:OFF_P]), wi[:, OFF_P:OFF_G]], axis=1).astype(BF16)
    pad_r = LANES - N_EXPERT_GROUPS - N_EXPERTS
    w_r = jnp.concatenate([w_rg[l], w_re[l], jnp.zeros((D_MODEL, pad_r), F32)], axis=1)
    w_r_hi = w_r.astype(BF16)
    w_r_lo = (w_r - w_r_hi.astype(F32)).astype(BF16)
    return dict(
        g_mix=g_mix[l][None, :],
        w_pack=w_pack,
        b_f=_rep_forget(b_fgate[l][None, :]),
        g_q=jnp.tile(g_q[l], A_HEADS)[None, :],
        g_k=jnp.tile(g_k[l], A_HEADS)[None, :],
        bd=_block_diag([jnp.ones((A_HEAD_DIM, A_HEAD_DIM), BF16)] * A_HEADS),
        sel=_head_to_gate_lanes(),
        w_pool=_block_diag([w_pool[l][g] for g in range(N_GROUPS)]).astype(BF16),
        s_pool=s_pool[l][None, :],
        g_sgu=g_sgu[l][None, :],
        w_sgu=w_sgu[l],
        b_sgu=jnp.broadcast_to(b_sgu[l][:, :, None], (N_GROUPS, C_CHUNK, GROUP_DIM)),
        w_gate=wi[:, OFF_G:].astype(BF16),
        b_gate=b_gate[l].reshape(1, N_BRANCH * D_MODEL),
        w_branch=w_branch[l].astype(BF16),
        w_out=w_out[l].astype(BF16),
        g_ffn=g_ffn[l][None, :],
        w_router=jnp.concatenate([w_r_hi, w_r_lo], axis=1),
        b_router=jnp.concatenate([b_rg[l], b_re[l], jnp.zeros((pad_r,), F32)])[None, :],
    )


def kernel(x, g_mix, w_in, b_fgate, b_gate, g_q, g_k, w_pool, s_pool, g_sgu, w_sgu, b_sgu, w_branch, w_out,
           g_ffn, w_rg, b_rg, w_re, b_re, w1, w3, w2):
    B, S, D = x.shape
    assert D == D_MODEL and x.dtype == F32
    N = B * S
    T, tq, tk, bm, tm = _tiles(S)
    assert S % T == 0 and S % tq == 0 and tq == tk and T % C_CHUNK == 0
    assert (TOP_K * N) % bm == 0 and N % tm == 0 and tm % ISSUE_UNROLL == 0
    depth = w_in.shape[0]
    for l in range(depth):
        w = _prep_layer(l, g_mix, w_in, b_fgate, b_gate, g_q, g_k, w_pool, s_pool, g_sgu, w_sgu, b_sgu,
                        w_branch, w_out, g_ffn, w_rg, b_rg, w_re, b_re)
        qT, k, vT, yb, yc = _proj_call(x, w, T)
        ya = _attention(qT, k, vT, g_q[l], g_k[l], tq, tk)
        x1, h2, route, route_t, cnt = _merge_call(
            x.reshape(N, D), ya.reshape(N, -1), yb.reshape(N, -1), yc.reshape(N, -1), w, T)
        counts = cnt[0, :N_EXPERTS].astype(I32)
        starts = jnp.cumsum(counts) - counts
        experts = route_t[0:TOP_K].astype(I32)
        expert_ids = jnp.arange(N_EXPERTS, dtype=I32)[:, None, None]
        start_of = jnp.sum(jnp.where(experts[None] == expert_ids, starts[:, None, None], 0), axis=0)
        dest = start_of + route_t[4:4 + TOP_K].astype(I32)
        dest3 = dest.reshape(TOP_K, N // tm, tm).transpose(1, 0, 2).reshape(N // tm, 1, TOP_K * tm)
        xs = _dispatch_call(dest3, h2, tm)
        y = _gmm_call(_gmm_plan(counts, TOP_K * N, bm), xs, w1, w3, w2, l, bm)
        x = _combine_call(dest3, x1, route, y, tm).reshape(B, S, D)
    return x
```

```python
import functools

import jax
import jax.numpy as jnp
from jax import lax
from jax.experimental import pallas as pl
from jax.experimental.pallas import tpu as pltpu

F32 = jnp.float32
BF16 = jnp.bfloat16
I32 = jnp.int32

D_MODEL = 1024
A_HEADS = 8
A_HEAD_DIM = 64
BRANCH_WIDTH = 512
N_BRANCH = 3
POOL_WINDOWS = (2, 4, 8, 16)
GROUP_DIM = 128
N_GROUPS = 4
C_CHUNK = 128
N_EXPERT_GROUPS = 4
EXPERTS_PER_GROUP = 8
N_EXPERTS = N_EXPERT_GROUPS * EXPERTS_PER_GROUP
TOP_K = 2
D_EXPERT = 256
RMS_EPS = 1e-6
OFF_F = 3 * BRANCH_WIDTH
OFF_P = OFF_F + A_HEADS
OFF_U = OFF_P + BRANCH_WIDTH
OFF_SV = OFF_U + BRANCH_WIDTH
OFF_G = OFF_SV + BRANCH_WIDTH

LANES = 128
SUBLANES = 8
V7X_VMEM_BYTES = 64 * 1024 * 1024
VMEM_LIMIT = 56 * 1024 * 1024
ROW_TILE = D_MODEL // LANES
ROUTE_ROWS = SUBLANES
ISSUE_UNROLL = 8
N_DMA_PRIORITIES = 2
ATTN_BLOCKS_PER_TRIP = 4

PK_Q, PK_K, PK_V = 0, 512, 1024
PK_F = 1536
PK_P = PK_F + LANES
PK_U = PK_P + BRANCH_WIDTH
PK_SV = PK_U + BRANCH_WIDTH
PK_W = PK_SV + BRANCH_WIDTH
F_REP = 6

LOG2E = 1.4426950408889634
SCORE_CAP = 96.0
NEG = -1e30
HALO = 16


def _tiles(seq_len):
    t_proj = min(512, seq_len)
    t_q = min(512, seq_len)
    t_k = min(512, seq_len)
    moe_block = 512
    t_move = min(1024, seq_len)
    return t_proj, t_q, t_k, moe_block, t_move


def _const_spec(*shape):
    zeros = (0,) * len(shape)
    return pl.BlockSpec(shape, lambda *_: zeros, pipeline_mode=pl.Buffered(1))


def _rms(x, g):
    return x * lax.rsqrt(jnp.mean(x * x, axis=-1, keepdims=True) + RMS_EPS) * g


def _gelu_tanh(x):
    cdf = 0.5 * (1.0 + jnp.tanh(0.7978845608028654 * (x + 0.044715 * (x * x * x))))
    return x * cdf


def _log_sigmoid(x):
    return jnp.minimum(x, 0.0) - jnp.log1p(jnp.exp(-jnp.abs(x)))


def _split3(c):
    hi = c.astype(BF16).astype(F32)
    r = c - hi
    lo = r.astype(BF16).astype(F32)
    lolo = (r - lo).astype(BF16).astype(F32)
    return hi, lo, lolo


def _proj_kernel(x_ref, gmix_ref, w_ref, bf_ref, gq_ref, gk_ref, bd_ref, sel_ref, wpool_ref, spool_ref,
                 gsgu_ref, wsgu_ref, bsgu_ref,
                 qT_ref, k_ref, vT_ref, yb_ref, yc_ref,
                 carry_ref, halo_ref, *, T):
    i = pl.program_id(1)

    @pl.when(i == 0)
    def _():
        carry_ref[...] = jnp.zeros_like(carry_ref)
        halo_ref[...] = jnp.zeros_like(halo_ref)

    hb = _rms(x_ref[...], gmix_ref[...]).astype(BF16)

    def proj(lo, width):
        return jnp.dot(hb, w_ref[:, lo:lo + width], preferred_element_type=F32)

    bd = bd_ref[...]

    def head_norm(z, g):
        sq = z * z
        hi = sq.astype(BF16)
        lo = (sq - hi.astype(F32)).astype(BF16)
        ss = (jnp.dot(hi, bd, preferred_element_type=F32)
              + jnp.dot(lo, bd, preferred_element_type=F32))
        return z * lax.rsqrt(ss * (1.0 / A_HEAD_DIM) + RMS_EPS) * g

    qn = head_norm(proj(PK_Q, BRANCH_WIDTH), gq_ref[...]) * (A_HEAD_DIM ** -0.5 * LOG2E)
    kn = head_norm(proj(PK_K, BRANCH_WIDTH), gk_ref[...])
    zv = proj(PK_V, BRANCH_WIDTH)
    qk = qn * kn
    qk_hi = qk.astype(BF16)
    qk_lo = (qk - qk_hi.astype(F32)).astype(BF16)
    diag = (jnp.dot(qk_hi, sel_ref[...], preferred_element_type=F32)
            + jnp.dot(qk_lo, sel_ref[...], preferred_element_type=F32))

    logf = _log_sigmoid(proj(PK_F, LANES) + bf_ref[...])
    r_i = lax.broadcasted_iota(I32, (T, T), 0)
    c_i = lax.broadcasted_iota(I32, (T, T), 1)
    tri = (r_i >= c_i).astype(F32)
    c = jnp.dot(tri, logf, preferred_element_type=F32,
                precision=lax.Precision.HIGHEST) + carry_ref[...]
    carry_ref[...] = c[T - 1:T, :]

    lane = lax.broadcasted_iota(I32, (T, LANES), 1)
    lm = lane & 63
    grp = lm >> 3
    hsel = lm & 7
    first3 = grp < 3
    second3 = (grp >= 3) & (grp < F_REP)

    def parts(v):
        hi, lo, lolo = _split3(v)
        return jnp.where((grp == 0) | (grp == 3), hi, jnp.where((grp == 1) | (grp == 4), lo, lolo))

    k_aug = jnp.where(first3, -parts(c * LOG2E), 0.0)
    q_aug = jnp.where(second3, parts(c * LOG2E - diag), 0.0)

    for g in range(A_HEADS // 2):
        sl = slice(g * LANES, (g + 1) * LANES)
        zq2, zk2, zv2 = qn[:, sl], kn[:, sl], zv[:, sl]
        for par in range(2):
            h = 2 * g + par
            keep = (lane < 64) if par == 0 else (lane >= 64)
            oh_k = (second3 & (hsel == h)).astype(F32)
            oh_q = (first3 & (hsel == h)).astype(F32)
            k_ref[h] = jnp.where(keep, zk2, k_aug + oh_k).astype(BF16)
            qT_ref[h] = jnp.where(keep, zq2, q_aug + oh_q).T.astype(BF16)
            ones_lane = 64 if par == 0 else 0
            vT_ref[h] = jnp.where(keep, zv2, (lane == ones_lane).astype(F32)).T.astype(BF16)

    p = proj(PK_P, BRANCH_WIDTH)
    row8 = lax.broadcasted_iota(I32, (8, GROUP_DIM), 0)
    pos = i * T + lax.broadcasted_iota(I32, (T, GROUP_DIM), 0)

    def shift_down(v, tail, d):
        r = pltpu.roll(v, d, 0)
        rt = pltpu.roll(tail, d, 0)
        top = jnp.where(row8 < d, rt[0:8], r[0:8])
        return jnp.concatenate([top, r[8:]], axis=0)

    pooled = []
    for gi, w in enumerate(POOL_WINDOWS):
        sl = slice(gi * GROUP_DIM, (gi + 1) * GROUP_DIM)
        s = p[:, sl]
        for lv in range(gi + 1):
            tail = halo_ref[lv, :, sl]
            halo_ref[lv, :, sl] = s[T - HALO:T, :]
            s = s + shift_down(s, tail, 1 << lv)
        cnt = jnp.minimum(pos + 1, w).astype(F32)
        pooled.append(s / cnt - p[:, sl])
    pooled = jnp.concatenate(pooled, axis=1).astype(BF16)
    yb = jnp.dot(pooled, wpool_ref[...], preferred_element_type=F32) * spool_ref[...]
    yb_ref[...] = yb.astype(yb_ref.dtype)

    gu = _gelu_tanh(proj(PK_U, BRANCH_WIDTH))
    gv = _gelu_tanh(proj(PK_SV, BRANCH_WIDTH))
    t_r = lax.broadcasted_iota(I32, (C_CHUNK, C_CHUNK), 0)
    t_c = lax.broadcasted_iota(I32, (C_CHUNK, C_CHUNK), 1)
    causal = t_r >= t_c
    for g in range(N_GROUPS):
        sl = slice(g * GROUP_DIM, (g + 1) * GROUP_DIM)
        vn = _rms(gv[:, sl], gsgu_ref[:, sl]).astype(BF16)
        wc = jnp.where(causal, wsgu_ref[g], 0.0).astype(BF16)
        for ch in range(T // C_CHUNK):
            rows = slice(ch * C_CHUNK, (ch + 1) * C_CHUNK)
            mixed = jnp.dot(wc, vn[rows], preferred_element_type=F32) + bsgu_ref[g]
            yc_ref[rows, sl] = (gu[rows, sl] * mixed).astype(yc_ref.dtype)


def _proj_call(x, w, T):
    B, S, D = x.shape
    H = A_HEADS
    const = _const_spec
    return pl.pallas_call(
        functools.partial(_proj_kernel, T=T),
        grid=(B, S // T),
        in_specs=[
            pl.BlockSpec((None, T, D), lambda b, i: (b, i, 0)),
            const(1, D), const(D, PK_W), const(1, LANES), const(1, BRANCH_WIDTH), const(1, BRANCH_WIDTH),
            const(BRANCH_WIDTH, BRANCH_WIDTH), const(BRANCH_WIDTH, LANES), const(BRANCH_WIDTH, BRANCH_WIDTH),
            const(1, BRANCH_WIDTH), const(1, BRANCH_WIDTH), const(N_GROUPS, C_CHUNK, C_CHUNK), const(N_GROUPS, C_CHUNK, GROUP_DIM),
        ],
        out_specs=[
            pl.BlockSpec((None, H, LANES, T), lambda b, i: (b, 0, 0, i)),
            pl.BlockSpec((None, H, T, LANES), lambda b, i: (b, 0, i, 0)),
            pl.BlockSpec((None, H, LANES, T), lambda b, i: (b, 0, 0, i)),
            pl.BlockSpec((None, T, BRANCH_WIDTH), lambda b, i: (b, i, 0)),
            pl.BlockSpec((None, T, BRANCH_WIDTH), lambda b, i: (b, i, 0)),
        ],
        out_shape=[
            jax.ShapeDtypeStruct((B, H, LANES, S), BF16),
            jax.ShapeDtypeStruct((B, H, S, LANES), BF16),
            jax.ShapeDtypeStruct((B, H, LANES, S), BF16),
            jax.ShapeDtypeStruct((B, S, BRANCH_WIDTH), BF16),
            jax.ShapeDtypeStruct((B, S, BRANCH_WIDTH), BF16),
        ],
        scratch_shapes=[pltpu.VMEM((1, LANES), F32), pltpu.VMEM((4, HALO, BRANCH_WIDTH), F32)],
        compiler_params=pltpu.CompilerParams(
            dimension_semantics=("arbitrary", "arbitrary"), vmem_limit_bytes=VMEM_LIMIT),
        name="proj",
    )(x, w["g_mix"], w["w_pack"], w["b_f"], w["g_q"], w["g_k"], w["bd"], w["sel"], w["w_pool"], w["s_pool"],
      w["g_sgu"], w["w_sgu"], w["b_sgu"])


def _attn_kernel(qT_ref, k_ref, vT_ref, o_ref, m_ref, acc_ref, s_ref, *, tq, tk, online_max):
    qi = pl.program_id(2)
    q0 = qi * tq
    n_full = q0 // tk
    n_diag = tq // tk
    acc_ref[...] = jnp.zeros_like(acc_ref)
    if online_max:
        m_ref[...] = jnp.full(m_ref.shape, NEG, F32)

    def scores(j, slot):
        k0 = pl.multiple_of(j * tk, tk)
        for par in range(2):
            s_ref[slot, par] = jnp.dot(k_ref[par, pl.ds(k0, tk), :], qT_ref[par],
                                       preferred_element_type=F32)

    def consume(j, slot, masked):
        k0 = pl.multiple_of(j * tk, tk)
        for par in range(2):
            s = s_ref[slot, par]
            if masked:
                kpos = k0 + lax.broadcasted_iota(I32, (tk, tq), 0)
                qpos = q0 + lax.broadcasted_iota(I32, (tk, tq), 1)
                s = jnp.where(kpos <= qpos, s, NEG)
            v_blk = vT_ref[par, :, pl.ds(k0, tk)]
            if online_max:
                m_old = m_ref[par]
                m_new = jnp.maximum(m_old, jnp.max(s, axis=0, keepdims=True))
                p = jnp.exp2(s - m_new).astype(BF16)
                pv = jnp.dot(v_blk, p, preferred_element_type=F32)
                acc_ref[par] = acc_ref[par] * jnp.exp2(m_old - m_new) + pv
                m_ref[par] = m_new
            else:
                p = jnp.exp2(s).astype(BF16)
                acc_ref[par] += jnp.dot(v_blk, p, preferred_element_type=F32)

    assert n_diag == 1
    scores(0, 0)

    def chain(j0, n_unmasked, then_diag, feeds_next):
        total = n_unmasked + (1 if then_diag else 0)
        for t in range(total):
            slot = t % 2
            if t + 1 < total or feeds_next:
                scores(j0 + t + 1, 1 - slot)
            consume(j0 + t, slot, then_diag and t == total - 1)

    def body(jb, carry):
        chain(ATTN_BLOCKS_PER_TRIP * jb, ATTN_BLOCKS_PER_TRIP, False, True)
        return carry

    lax.fori_loop(0, n_full // ATTN_BLOCKS_PER_TRIP, body, 0)
    for rem in range(ATTN_BLOCKS_PER_TRIP):
        @pl.when(n_full % ATTN_BLOCKS_PER_TRIP == rem)
        def _(rem=rem):
            chain(n_full - rem, rem, True, False)

    lane = lax.broadcasted_iota(I32, (tq, LANES), 1)
    outs = []
    for par in range(2):
        acc = acc_ref[par]
        l = acc[64:65, :] if par == 0 else acc[0:1, :]
        outs.append((acc * (1.0 / l)).T)
    o_ref[...] = jnp.where(lane < 64, outs[0], outs[1]).astype(o_ref.dtype)


def _attn_call(qT, k, vT, tq, tk, online_max):
    B, H, _, S = qT.shape
    return pl.pallas_call(
        functools.partial(_attn_kernel, tq=tq, tk=tk, online_max=online_max),
        grid=(B, H // 2, S // tq),
        in_specs=[
            pl.BlockSpec((None, 2, LANES, tq), lambda b, g, q: (b, g, 0, q)),
            pl.BlockSpec((None, 2, S, LANES), lambda b, g, q: (b, g, 0, 0)),
            pl.BlockSpec((None, 2, LANES, S), lambda b, g, q: (b, g, 0, 0)),
        ],
        out_specs=pl.BlockSpec((None, tq, LANES), lambda b, g, q: (b, q, g)),
        out_shape=jax.ShapeDtypeStruct((B, S, BRANCH_WIDTH), BF16),
        scratch_shapes=[pltpu.VMEM((2, 1, tq), F32), pltpu.VMEM((2, LANES, tq), F32),
                        pltpu.VMEM((2, 2, tk, tq), F32)],
        compiler_params=pltpu.CompilerParams(
            dimension_semantics=("parallel", "parallel", "arbitrary"), vmem_limit_bytes=VMEM_LIMIT),
        name="attn_online" if online_max else "attn",
    )(qT, k, vT)


def _attention(qT, k, vT, g_q, g_k, tq, tk):
    bound = 16.0 * LOG2E * jnp.max(jnp.abs(g_q)) * jnp.max(jnp.abs(g_k))
    return lax.cond(bound <= SCORE_CAP,
                    lambda: _attn_call(qT, k, vT, tq, tk, False),
                    lambda: _attn_call(qT, k, vT, tq, tk, True))


def _merge_kernel(x_ref, ya_ref, yb_ref, yc_ref, gmix_ref, wg_ref, bg_ref, wb_ref, wo_ref,
                  gffn_ref, wr_ref, br_ref,
                  x1_ref, h2_ref, route_ref, route_t_ref, cnt_ref, carry_ref, *, T):
    i = pl.program_id(0)

    @pl.when(i == 0)
    def _():
        carry_ref[...] = jnp.zeros_like(carry_ref)

    x = x_ref[...]
    hb = _rms(x, gmix_ref[...]).astype(BF16)
    merged = None
    for bi, y_ref in enumerate((ya_ref, yb_ref, yc_ref)):
        sl = slice(bi * D_MODEL, (bi + 1) * D_MODEL)
        gate = jax.nn.sigmoid(jnp.dot(hb, wg_ref[:, sl], preferred_element_type=F32) + bg_ref[:, sl])
        term = gate * jnp.dot(y_ref[...], wb_ref[bi], preferred_element_type=F32)
        merged = term if merged is None else merged + term
    x1 = x + jnp.dot(merged.astype(BF16), wo_ref[...], preferred_element_type=F32)
    x1_ref[...] = x1
    h2 = _rms(x1, gffn_ref[...])
    _store_row_tiles(h2_ref, h2)

    h_hi = h2.astype(BF16)
    h_lo = (h2 - h_hi.astype(F32)).astype(BF16)
    hw = jnp.dot(h_hi, wr_ref[...], preferred_element_type=F32)
    lw = jnp.dot(h_lo, wr_ref[:, 0:LANES], preferred_element_type=F32)
    logits = hw[:, 0:LANES] + (hw[:, LANES:] + lw) + br_ref[...]
    lane = lax.broadcasted_iota(I32, (T, LANES), 1).astype(F32)
    big = float(LANES)

    def first_argmax(v):
        m = jnp.max(v, axis=-1, keepdims=True)
        return m, jnp.min(jnp.where(v == m, lane, big), axis=-1, keepdims=True)

    lg = jnp.where(lane < N_EXPERT_GROUPS, logits, NEG)
    mg, grp = first_argmax(lg)
    p_grp = 1.0 / jnp.sum(jnp.exp(lg - mg), axis=-1, keepdims=True)
    lo_lane = N_EXPERT_GROUPS + grp * EXPERTS_PER_GROUP
    le = jnp.where((lane >= lo_lane) & (lane < lo_lane + EXPERTS_PER_GROUP), logits, NEG)
    m1, i1 = first_argmax(le)
    m2, i2 = first_argmax(jnp.where(lane == i1, NEG, le))
    e21 = jnp.exp(m2 - m1)
    g1 = p_grp / (1.0 + e21)
    g2 = p_grp * e21 / (1.0 + e21)
    e1 = i1 - N_EXPERT_GROUPS
    e2 = i2 - N_EXPERT_GROUPS

    oh1 = lane == e1
    oh2 = lane == e2
    sel = (oh1 | oh2).astype(F32)
    r_i = lax.broadcasted_iota(I32, (T, T), 0)
    c_i = lax.broadcasted_iota(I32, (T, T), 1)
    before = (r_i > c_i).astype(BF16)
    seen = jnp.dot(before, sel.astype(BF16), preferred_element_type=F32) + carry_ref[...]
    r1 = jnp.sum(jnp.where(oh1, seen, 0.0), axis=-1, keepdims=True)
    r2 = jnp.sum(jnp.where(oh2, seen, 0.0), axis=-1, keepdims=True)
    carry_ref[...] = carry_ref[...] + jnp.sum(sel, axis=0, keepdims=True)
    cnt_ref[...] = carry_ref[...]

    route = jnp.zeros((T, LANES), F32)
    for idx, val in enumerate((e1, e2, g1, g2, r1, r2)):
        route = jnp.where(lane == idx, val, route)
    route_ref[...] = route
    route_t_ref[...] = route.T[0:ROUTE_ROWS]


def _row_tile_rows(s, n_rows):
    return pl.ds(s, n_rows, stride=ROW_TILE)


def _store_row_tiles(ref, v):
    for s in range(ROW_TILE):
        ref[_row_tile_rows(s, v.shape[0]), :] = v[:, s * LANES:(s + 1) * LANES]


def _load_row_tiles(ref):
    n_rows = ref.shape[0] // ROW_TILE
    return jnp.concatenate([ref[_row_tile_rows(s, n_rows), :] for s in range(ROW_TILE)], axis=-1)


def _merge_call(x2d, ya, yb, yc, w, T):
    N, D = x2d.shape
    const = _const_spec
    tile = lambda width: pl.BlockSpec((T, width), lambda i: (i, 0))
    return pl.pallas_call(
        functools.partial(_merge_kernel, T=T),
        grid=(N // T,),
        in_specs=[
            tile(D), tile(BRANCH_WIDTH), tile(BRANCH_WIDTH), tile(BRANCH_WIDTH),
            const(1, D), const(D, N_BRANCH * D), const(1, N_BRANCH * D),
            const(N_BRANCH, BRANCH_WIDTH, D), const(D, D), const(1, D), const(D, 2 * LANES), const(1, LANES),
        ],
        out_specs=[tile(D), pl.BlockSpec((T * ROW_TILE, LANES), lambda i: (i, 0)), tile(LANES),
                   pl.BlockSpec((ROUTE_ROWS, T), lambda i: (0, i)),
                   pl.BlockSpec((1, LANES), lambda i: (0, 0))],
        out_shape=[
            jax.ShapeDtypeStruct((N, D), F32),
            jax.ShapeDtypeStruct((N * ROW_TILE, LANES), F32),
            jax.ShapeDtypeStruct((N, LANES), F32),
            jax.ShapeDtypeStruct((ROUTE_ROWS, N), F32),
            jax.ShapeDtypeStruct((1, LANES), F32),
        ],
        scratch_shapes=[pltpu.VMEM((1, LANES), F32)],
        compiler_params=pltpu.CompilerParams(
            dimension_semantics=("arbitrary",), vmem_limit_bytes=VMEM_LIMIT),
        name="merge",
    )(x2d, ya, yb, yc, w["g_mix"], w["w_gate"], w["b_gate"], w["w_branch"], w["w_out"],
      w["g_ffn"], w["w_router"], w["b_router"])


def _dispatch_kernel(dest_ref, h2_ref, xs_ref, sem, *, T):
    def row_copy(t, d):
        return pltpu.make_async_copy(h2_ref.at[_row_tile(t)], xs_ref.at[_row_tile(d)], sem)

    def body(tb, carry):
        t0 = tb * ISSUE_UNROLL
        for u in range(ISSUE_UNROLL):
            for kk in range(TOP_K):
                row_copy(t0 + u, dest_ref[0, kk * T + t0 + u]).start(priority=kk % N_DMA_PRIORITIES)
        return carry

    lax.fori_loop(0, T // ISSUE_UNROLL, body, 0)
    for kk in range(TOP_K):
        pltpu.make_async_copy(h2_ref, xs_ref.at[pl.ds(0, T * ROW_TILE)], sem).wait()


def _row_tile(r):
    return pl.ds(pl.multiple_of(r * ROW_TILE, ROW_TILE), ROW_TILE)


def _dispatch_call(dest3, h2, T):
    N = h2.shape[0] // ROW_TILE
    return pl.pallas_call(
        functools.partial(_dispatch_kernel, T=T),
        grid=(N // T,),
        in_specs=[
            pl.BlockSpec((None, 1, TOP_K * T), lambda i: (i, 0, 0), memory_space=pltpu.SMEM),
            pl.BlockSpec((T * ROW_TILE, LANES), lambda i: (i, 0)),
        ],
        out_specs=pl.BlockSpec(memory_space=pl.ANY),
        out_shape=jax.ShapeDtypeStruct((TOP_K * N * ROW_TILE, LANES), F32),
        scratch_shapes=[pltpu.SemaphoreType.DMA],
        compiler_params=pltpu.CompilerParams(
            dimension_semantics=("arbitrary",), vmem_limit_bytes=VMEM_LIMIT),
        name="dispatch",
    )(dest3, h2)


def _gmm_kernel(blk_ref, exp_ref, lo_ref, hi_ref, xs_ref, w1_ref, w3_ref, w2_ref, y_ref,
                w13_bf, w2_bf, *, bm):
    i = pl.program_id(0)
    lo = lo_ref[i]
    hi = hi_ref[i]
    prev = jnp.maximum(i - 1, 0)
    first = jnp.logical_or(i == 0, blk_ref[i] != blk_ref[prev])
    new_expert = jnp.logical_or(i == 0, exp_ref[i] != exp_ref[prev])

    @pl.when(first)
    def _():
        y_ref[...] = jnp.zeros_like(y_ref)

    @pl.when(new_expert)
    def _():
        w13_bf[:, 0:D_EXPERT] = w1_ref[...].astype(BF16)
        w13_bf[:, D_EXPERT:] = w3_ref[...].astype(BF16)
        w2_bf[...] = w2_ref[...].astype(BF16)

    @pl.when(hi > lo)
    def _():
        x = _load_row_tiles(xs_ref).astype(BF16)
        ab = jnp.dot(x, w13_bf[...], preferred_element_type=F32)
        a = ab[:, 0:D_EXPERT]
        b = ab[:, D_EXPERT:]
        mid = (a * jax.nn.sigmoid(a) * b).astype(BF16)
        y = jnp.dot(mid, w2_bf[...], preferred_element_type=F32)
        row = lax.broadcasted_iota(I32, (bm, LANES), 0)
        mine = (row >= lo) & (row < hi)
        for s in range(ROW_TILE):
            rows = _row_tile_rows(s, bm)
            y_ref[rows, :] = jnp.where(mine, y[:, s * LANES:(s + 1) * LANES], y_ref[rows, :])


def _gmm_call(plan, xs, w1, w3, w2, layer, bm):
    A = xs.shape[0] // ROW_TILE
    D = D_MODEL
    n_items = plan[0].shape[0]
    grid_spec = pltpu.PrefetchScalarGridSpec(
        num_scalar_prefetch=4,
        grid=(n_items,),
        in_specs=[
            pl.BlockSpec((bm * ROW_TILE, LANES), lambda i, blk, ex, lo, hi: (blk[i], 0)),
            pl.BlockSpec((None, None, D, D_EXPERT), lambda i, blk, ex, lo, hi: (layer, ex[i], 0, 0)),
            pl.BlockSpec((None, None, D, D_EXPERT), lambda i, blk, ex, lo, hi: (layer, ex[i], 0, 0)),
            pl.BlockSpec((None, None, D_EXPERT, D), lambda i, blk, ex, lo, hi: (layer, ex[i], 0, 0)),
        ],
        out_specs=pl.BlockSpec((bm * ROW_TILE, LANES), lambda i, blk, ex, lo, hi: (blk[i], 0)),
        scratch_shapes=[pltpu.VMEM((D, 2 * D_EXPERT), BF16), pltpu.VMEM((D_EXPERT, D), BF16)],
    )
    return pl.pallas_call(
        functools.partial(_gmm_kernel, bm=bm),
        grid_spec=grid_spec,
        out_shape=jax.ShapeDtypeStruct((A * ROW_TILE, LANES), F32),
        compiler_params=pltpu.CompilerParams(
            dimension_semantics=("arbitrary",), vmem_limit_bytes=VMEM_LIMIT),
        name="gmm",
    )(*plan, xs, w1, w3, w2)


def _gmm_plan(counts, n_rows, bm):
    n_blk = n_rows // bm
    n_items = n_blk + N_EXPERTS - 1
    ends = jnp.cumsum(counts)
    starts = ends - counts
    first_blk = starts // bm
    n_it = jnp.where(counts > 0, (ends - 1) // bm - first_blk + 1, 0)
    item_end = jnp.cumsum(n_it)
    item_start = item_end - n_it
    total = item_end[-1]
    ids = jnp.arange(n_items, dtype=I32)
    valid = ids < total
    ex = jnp.minimum(jnp.sum(item_end[None, :] <= ids[:, None], axis=1).astype(I32), N_EXPERTS - 1)
    ex = jnp.where(valid, ex, ex[jnp.maximum(total - 1, 0)])
    blk = jnp.where(valid, first_blk[ex] + ids - item_start[ex], n_blk - 1)
    lo = jnp.where(valid, jnp.maximum(starts[ex], blk * bm) - blk * bm, 0)
    hi = jnp.where(valid, jnp.minimum(ends[ex], (blk + 1) * bm) - blk * bm, 0)
    return blk.astype(I32), ex.astype(I32), lo.astype(I32), hi.astype(I32)


def _combine_kernel(dest_ref, dest_next_ref, x1_ref, route_ref, y_hbm, o_ref, buf, sem, *, T):
    i = pl.program_id(0)
    n_steps = pl.num_programs(0)

    def issue(d_ref, slot):
        def body(tb, carry):
            t0 = tb * ISSUE_UNROLL
            for u in range(ISSUE_UNROLL):
                for kk in range(TOP_K):
                    d = d_ref[0, kk * T + t0 + u]
                    pltpu.make_async_copy(y_hbm.at[_row_tile(d)], buf.at[slot * TOP_K + kk, _row_tile(t0 + u)],
                                          sem.at[slot]).start(priority=kk % N_DMA_PRIORITIES)
            return carry

        lax.fori_loop(0, T // ISSUE_UNROLL, body, 0)

    def finish(slot):
        for kk in range(TOP_K):
            pltpu.make_async_copy(y_hbm.at[pl.ds(0, T * ROW_TILE)], buf.at[slot * TOP_K + kk],
                                  sem.at[slot]).wait()
        g1 = route_ref[:, 2:3]
        g2 = route_ref[:, 3:4]
        for s in range(ROW_TILE):
            sl = slice(s * LANES, (s + 1) * LANES)
            rows = _row_tile_rows(s, T)
            o_ref[:, sl] = x1_ref[:, sl] + (g1 * buf[slot * TOP_K, rows, :] + g2 * buf[slot * TOP_K + 1, rows, :])

    @pl.when(i == 0)
    def _():
        issue(dest_ref, 0)

    for slot in range(2):
        @pl.when(i % 2 == slot)
        def _(slot=slot):
            @pl.when(i + 1 < n_steps)
            def _():
                issue(dest_next_ref, 1 - slot)

            finish(slot)


def _combine_call(dest3, x1, route, y, T):
    N, D = x1.shape
    n_steps = N // T
    dest_spec = lambda index: pl.BlockSpec((None, 1, TOP_K * T), index, memory_space=pltpu.SMEM)
    return pl.pallas_call(
        functools.partial(_combine_kernel, T=T),
        grid=(n_steps,),
        in_specs=[
            dest_spec(lambda i: (i, 0, 0)),
            dest_spec(lambda i: (jnp.minimum(i + 1, n_steps - 1), 0, 0)),
            pl.BlockSpec((T, D), lambda i: (i, 0)),
            pl.BlockSpec((T, LANES), lambda i: (i, 0)),
            pl.BlockSpec(memory_space=pl.ANY),
        ],
        out_specs=pl.BlockSpec((T, D), lambda i: (i, 0)),
        out_shape=jax.ShapeDtypeStruct((N, D), F32),
        scratch_shapes=[pltpu.VMEM((2 * TOP_K, T * ROW_TILE, LANES), F32), pltpu.SemaphoreType.DMA((2,))],
        compiler_params=pltpu.CompilerParams(
            dimension_semantics=("arbitrary",), vmem_limit_bytes=VMEM_LIMIT),
        name="combine",
    )(dest3, dest3, x1, route, y)


def _block_diag(blocks):
    n = len(blocks)
    rows = []
    for i, b in enumerate(blocks):
        rows.append(jnp.concatenate(
            [b if j == i else jnp.zeros((b.shape[0], blocks[j].shape[1]), b.dtype) for j in range(n)], axis=1))
    return jnp.concatenate(rows, axis=0)


def _rep_forget(cols):
    half = jnp.concatenate(
        [jnp.tile(cols, (1, F_REP)), jnp.zeros((cols.shape[0], 64 - F_REP * A_HEADS), cols.dtype)], axis=1)
    return jnp.concatenate([half, half], axis=1)


def _head_to_gate_lanes():
    head = jnp.arange(BRANCH_WIDTH, dtype=I32)[:, None] // A_HEAD_DIM
    lane = jnp.arange(LANES, dtype=I32)[None, :]
    hit = ((lane & 7) == head) & (((lane & 63) >> 3) < F_REP)
    return hit.astype(BF16)


def _prep_layer(l, g_mix, w_in, b_fgate, b_gate, g_q, g_k, w_pool, s_pool, g_sgu, w_sgu, b_sgu,
                w_branch, w_out, g_ffn, w_rg, b_rg, w_re, b_re):
    wi = w_in[l]
    w_pack = jnp.concatenate(
        [wi[:, 0:OFF_F], _rep_forget(wi[:, OFF_F:OFF_P]), wi[:, OFF_P:OFF_G]], axis=1).astype(BF16)
    pad_r = LANES - N_EXPERT_GROUPS - N_EXPERTS
    w_r = jnp.concatenate([w_rg[l], w_re[l], jnp.zeros((D_MODEL, pad_r), F32)], axis=1)
    w_r_hi = w_r.astype(BF16)
    w_r_lo = (w_r - w_r_hi.astype(F32)).astype(BF16)
    return dict(
        g_mix=g_mix[l][None, :],
        w_pack=w_pack,
        b_f=_rep_forget(b_fgate[l][None, :]),
        g_q=jnp.tile(g_q[l], A_HEADS)[None, :],
        g_k=jnp.tile(g_k[l], A_HEADS)[None, :],
        bd=_block_diag([jnp.ones((A_HEAD_DIM, A_HEAD_DIM), BF16)] * A_HEADS),
        sel=_head_to_gate_lanes(),
        w_pool=_block_diag([w_pool[l][g] for g in range(N_GROUPS)]).astype(BF16),
        s_pool=s_pool[l][None, :],
        g_sgu=g_sgu[l][None, :],
        w_sgu=w_sgu[l],
        b_sgu=jnp.broadcast_to(b_sgu[l][:, :, None], (N_GROUPS, C_CHUNK, GROUP_DIM)),
        w_gate=wi[:, OFF_G:].astype(BF16),
        b_gate=b_gate[l].reshape(1, N_BRANCH * D_MODEL),
        w_branch=w_branch[l].astype(BF16),
        w_out=w_out[l].astype(BF16),
        g_ffn=g_ffn[l][None, :],
        w_router=jnp.concatenate([w_r_hi, w_r_lo], axis=1),
        b_router=jnp.concatenate([b_rg[l], b_re[l], jnp.zeros((pad_r,), F32)])[None, :],
    )


def kernel(x, g_mix, w_in, b_fgate, b_gate, g_q, g_k, w_pool, s_pool, g_sgu, w_sgu, b_sgu, w_branch, w_out,
           g_ffn, w_rg, b_rg, w_re, b_re, w1, w3, w2):
    B, S, D = x.shape
    assert D == D_MODEL and x.dtype == F32
    N = B * S
    T, tq, tk, bm, tm = _tiles(S)
    assert S % T == 0 and S % tq == 0 and tq == tk and T % C_CHUNK == 0
    assert (TOP_K * N) % bm == 0 and N % tm == 0 and tm % ISSUE_UNROLL == 0
    depth = w_in.shape[0]
    for l in range(depth):
        w = _prep_layer(l, g_mix, w_in, b_fgate, b_gate, g_q, g_k, w_pool, s_pool, g_sgu, w_sgu, b_sgu,
                        w_branch, w_out, g_ffn, w_rg, b_rg, w_re, b_re)
        qT, k, vT, yb, yc = _proj_call(x, w, T)
        ya = _attention(qT, k, vT, g_q[l], g_k[l], tq, tk)
        x1, h2, route, route_t, cnt = _merge_call(
            x.reshape(N, D), ya.reshape(N, -1), yb.reshape(N, -1), yc.reshape(N, -1), w, T)
        counts = cnt[0, :N_EXPERTS].astype(I32)
        starts = jnp.cumsum(counts) - counts
        experts = route_t[0:TOP_K].astype(I32)
        expert_ids = jnp.arange(N_EXPERTS, dtype=I32)[:, None, None]
        start_of = jnp.sum(jnp.where(experts[None] == expert_ids, starts[:, None, None], 0), axis=0)
        dest = start_of + route_t[4:4 + TOP_K].astype(I32)
        dest3 = dest.reshape(TOP_K, N // tm, tm).transpose(1, 0, 2).reshape(N // tm, 1, TOP_K * tm)
        xs = _dispatch_call(dest3, h2, tm)
        y = _gmm_call(_gmm_plan(counts, TOP_K * N, bm), xs, w1, w3, w2, l, bm)
        x = _combine_call(dest3, x1, route, y, tm).reshape(B, S, D)
    return x
```

```python
import functools

import jax
import jax.numpy as jnp
from jax import lax
from jax.experimental import pallas as pl
from jax.experimental.pallas import tpu as pltpu

F32 = jnp.float32
BF16 = jnp.bfloat16
I32 = jnp.int32

D_MODEL = 1024
A_HEADS = 8
A_HEAD_DIM = 64
BRANCH_WIDTH = 512
N_BRANCH = 3
POOL_WINDOWS = (2, 4, 8, 16)
GROUP_DIM = 128
N_GROUPS = 4
C_CHUNK = 128
N_EXPERT_GROUPS = 4
EXPERTS_PER_GROUP = 8
N_EXPERTS = N_EXPERT_GROUPS * EXPERTS_PER_GROUP
TOP_K = 2
D_EXPERT = 256
RMS_EPS = 1e-6
OFF_F = 3 * BRANCH_WIDTH
OFF_P = OFF_F + A_HEADS
OFF_U = OFF_P + BRANCH_WIDTH
OFF_SV = OFF_U + BRANCH_WIDTH
OFF_G = OFF_SV + BRANCH_WIDTH

LANES = 128
SUBLANES = 8
V7X_VMEM_BYTES = 64 * 1024 * 1024
VMEM_LIMIT = 56 * 1024 * 1024
ROW_TILE = D_MODEL // LANES
ROUTE_ROWS = SUBLANES
ISSUE_UNROLL = 8
N_DMA_PRIORITIES = 2
ATTN_BLOCKS_PER_TRIP = 4

PK_Q, PK_K, PK_V = 0, 512, 1024
PK_F = 1536
PK_P = PK_F + LANES
PK_U = PK_P + BRANCH_WIDTH
PK_SV = PK_U + BRANCH_WIDTH
PK_W = PK_SV + BRANCH_WIDTH
F_REP = 6

LOG2E = 1.4426950408889634
SCORE_CAP = 96.0
NEG = -1e30
HALO = 16


def _tiles(seq_len):
    t_proj = min(512, seq_len)
    t_q = min(512, seq_len)
    t_k = min(512, seq_len)
    moe_block = 512
    t_move = min(1024, seq_len)
    return t_proj, t_q, t_k, moe_block, t_move


def _const_spec(*shape):
    zeros = (0,) * len(shape)
    return pl.BlockSpec(shape, lambda *_: zeros, pipeline_mode=pl.Buffered(1))


def _rms(x, g):
    return x * lax.rsqrt(jnp.mean(x * x, axis=-1, keepdims=True) + RMS_EPS) * g


def _gelu_tanh(x):
    cdf = 0.5 * (1.0 + jnp.tanh(0.7978845608028654 * (x + 0.044715 * (x * x * x))))
    return x * cdf


def _log_sigmoid(x):
    return jnp.minimum(x, 0.0) - jnp.log1p(jnp.exp(-jnp.abs(x)))


def _split3(c):
    hi = c.astype(BF16).astype(F32)
    r = c - hi
    lo = r.astype(BF16).astype(F32)
    lolo = (r - lo).astype(BF16).astype(F32)
    return hi, lo, lolo


def _proj_kernel(x_ref, gmix_ref, w_ref, bf_ref, gq_ref, gk_ref, bd_ref, sel_ref, wpool_ref, spool_ref,
                 gsgu_ref, wsgu_ref, bsgu_ref,
                 qT_ref, k_ref, vT_ref, yb_ref, yc_ref,
                 carry_ref, halo_ref, *, T):
    i = pl.program_id(1)

    @pl.when(i == 0)
    def _():
        carry_ref[...] = jnp.zeros_like(carry_ref)
        halo_ref[...] = jnp.zeros_like(halo_ref)

    hb = _rms(x_ref[...], gmix_ref[...]).astype(BF16)

    def proj(lo, width):
        return jnp.dot(hb, w_ref[:, lo:lo + width], preferred_element_type=F32)

    bd = bd_ref[...]

    def head_norm(z, g):
        sq = z * z
        hi = sq.astype(BF16)
        lo = (sq - hi.astype(F32)).astype(BF16)
        ss = (jnp.dot(hi, bd, preferred_element_type=F32)
              + jnp.dot(lo, bd, preferred_element_type=F32))
        return z * lax.rsqrt(ss * (1.0 / A_HEAD_DIM) + RMS_EPS) * g

    qn = head_norm(proj(PK_Q, BRANCH_WIDTH), gq_ref[...]) * (A_HEAD_DIM ** -0.5 * LOG2E)
    kn = head_norm(proj(PK_K, BRANCH_WIDTH), gk_ref[...])
    zv = proj(PK_V, BRANCH_WIDTH)
    qk = qn * kn
    qk_hi = qk.astype(BF16)
    qk_lo = (qk - qk_hi.astype(F32)).astype(BF16)
    diag = (jnp.dot(qk_hi, sel_ref[...], preferred_element_type=F32)
            + jnp.dot(qk_lo, sel_ref[...], preferred_element_type=F32))

    logf = _log_sigmoid(proj(PK_F, LANES) + bf_ref[...])
    r_i = lax.broadcasted_iota(I32, (T, T), 0)
    c_i = lax.broadcasted_iota(I32, (T, T), 1)
    tri = (r_i >= c_i).astype(F32)
    c = jnp.dot(tri, logf, preferred_element_type=F32,
                precision=lax.Precision.HIGHEST) + carry_ref[...]
    carry_ref[...] = c[T - 1:T, :]

    lane = lax.broadcasted_iota(I32, (T, LANES), 1)
    lm = lane & 63
    grp = lm >> 3
    hsel = lm & 7
    first3 = grp < 3
    second3 = (grp >= 3) & (grp < F_REP)

    def parts(v):
        hi, lo, lolo = _split3(v)
        return jnp.where((grp == 0) | (grp == 3), hi, jnp.where((grp == 1) | (grp == 4), lo, lolo))

    k_aug = jnp.where(first3, -parts(c * LOG2E), 0.0)
    q_aug = jnp.where(second3, parts(c * LOG2E - diag), 0.0)

    for g in range(A_HEADS // 2):
        sl = slice(g * LANES, (g + 1) * LANES)
        zq2, zk2, zv2 = qn[:, sl], kn[:, sl], zv[:, sl]
        for par in range(2):
            h = 2 * g + par
            keep = (lane < 64) if par == 0 else (lane >= 64)
            oh_k = (second3 & (hsel == h)).astype(F32)
            oh_q = (first3 & (hsel == h)).astype(F32)
            k_ref[h] = jnp.where(keep, zk2, k_aug + oh_k).astype(BF16)
            qT_ref[h] = jnp.where(keep, zq2, q_aug + oh_q).T.astype(BF16)
            ones_lane = 64 if par == 0 else 0
            vT_ref[h] = jnp.where(keep, zv2, (lane == ones_lane).astype(F32)).T.astype(BF16)

    p = proj(PK_P, BRANCH_WIDTH)
    row8 = lax.broadcasted_iota(I32, (8, GROUP_DIM), 0)
    pos = i * T + lax.broadcasted_iota(I32, (T, GROUP_DIM), 0)

    def shift_down(v, tail, d):
        r = pltpu.roll(v, d, 0)
        rt = pltpu.roll(tail, d, 0)
        top = jnp.where(row8 < d, rt[0:8], r[0:8])
        return jnp.concatenate([top, r[8:]], axis=0)

    pooled = []
    for gi, w in enumerate(POOL_WINDOWS):
        sl = slice(gi * GROUP_DIM, (gi + 1) * GROUP_DIM)
        s = p[:, sl]
        for lv in range(gi + 1):
            tail = halo_ref[lv, :, sl]
            halo_ref[lv, :, sl] = s[T - HALO:T, :]
            s = s + shift_down(s, tail, 1 << lv)
        cnt = jnp.minimum(pos + 1, w).astype(F32)
        pooled.append(s / cnt - p[:, sl])
    pooled = jnp.concatenate(pooled, axis=1).astype(BF16)
    yb = jnp.dot(pooled, wpool_ref[...], preferred_element_type=F32) * spool_ref[...]
    yb_ref[...] = yb.astype(yb_ref.dtype)

    gu = _gelu_tanh(proj(PK_U, BRANCH_WIDTH))
    gv = _gelu_tanh(proj(PK_SV, BRANCH_WIDTH))
    t_r = lax.broadcasted_iota(I32, (C_CHUNK, C_CHUNK), 0)
    t_c = lax.broadcasted_iota(I32, (C_CHUNK, C_CHUNK), 1)
    causal = t_r >= t_c
    for g in range(N_GROUPS):
        sl = slice(g * GROUP_DIM, (g + 1) * GROUP_DIM)
        vn = _rms(gv[:, sl], gsgu_ref[:, sl]).astype(BF16)
        wc = jnp.where(causal, wsgu_ref[g], 0.0).astype(BF16)
        for ch in range(T // C_CHUNK):
            rows = slice(ch * C_CHUNK, (ch + 1) * C_CHUNK)
            mixed = jnp.dot(wc, vn[rows], preferred_element_type=F32) + bsgu_ref[g]
            yc_ref[rows, sl] = (gu[rows, sl] * mixed).astype(yc_ref.dtype)


def _proj_call(x, w, T):
    B, S, D = x.shape
    H = A_HEADS
    const = _const_spec
    return pl.pallas_call(
        functools.partial(_proj_kernel, T=T),
        grid=(B, S // T),
        in_specs=[
            pl.BlockSpec((None, T, D), lambda b, i: (b, i, 0)),
            const(1, D), const(D, PK_W), const(1, LANES), const(1, BRANCH_WIDTH), const(1, BRANCH_WIDTH),
            const(BRANCH_WIDTH, BRANCH_WIDTH), const(BRANCH_WIDTH, LANES), const(BRANCH_WIDTH, BRANCH_WIDTH),
            const(1, BRANCH_WIDTH), const(1, BRANCH_WIDTH), const(N_GROUPS, C_CHUNK, C_CHUNK), const(N_GROUPS, C_CHUNK, GROUP_DIM),
        ],
        out_specs=[
            pl.BlockSpec((None, H, LANES, T), lambda b, i: (b, 0, 0, i)),
            pl.BlockSpec((None, H, T, LANES), lambda b, i: (b, 0, i, 0)),
            pl.BlockSpec((None, H, LANES, T), lambda b, i: (b, 0, 0, i)),
            pl.BlockSpec((None, T, BRANCH_WIDTH), lambda b, i: (b, i, 0)),
            pl.BlockSpec((None, T, BRANCH_WIDTH), lambda b, i: (b, i, 0)),
        ],
        out_shape=[
            jax.ShapeDtypeStruct((B, H, LANES, S), BF16),
            jax.ShapeDtypeStruct((B, H, S, LANES), BF16),
            jax.ShapeDtypeStruct((B, H, LANES, S), BF16),
            jax.ShapeDtypeStruct((B, S, BRANCH_WIDTH), BF16),
            jax.ShapeDtypeStruct((B, S, BRANCH_WIDTH), BF16),
        ],
        scratch_shapes=[pltpu.VMEM((1, LANES), F32), pltpu.VMEM((4, HALO, BRANCH_WIDTH), F32)],
        compiler_params=pltpu.CompilerParams(
            dimension_semantics=("arbitrary", "arbitrary"), vmem_limit_bytes=VMEM_LIMIT),
        name="proj",
    )(x, w["g_mix"], w["w_pack"], w["b_f"], w["g_q"], w["g_k"], w["bd"], w["sel"], w["w_pool"], w["s_pool"],
      w["g_sgu"], w["w_sgu"], w["b_sgu"])


def _attn_kernel(qT_ref, k_ref, vT_ref, o_ref, m_ref, acc_ref, s_ref, *, tq, tk, n_q, online_max):
    assert tq == tk
    FIRST = 2

    def reset():
        acc_ref[...] = jnp.zeros_like(acc_ref)
        if online_max:
            m_ref[...] = jnp.full(m_ref.shape, NEG, F32)

    def scores(qi, j, slot):
        k0 = pl.multiple_of(j * tk, tk)
        q0 = pl.multiple_of(qi * tq, tq)
        for par in range(2):
            s_ref[slot, par] = jnp.dot(k_ref[par, pl.ds(k0, tk), :], qT_ref[par, :, pl.ds(q0, tq)],
                                       preferred_element_type=F32)

    def consume(qi, j, slot, masked):
        k0 = pl.multiple_of(j * tk, tk)
        q0 = qi * tq
        for par in range(2):
            s = s_ref[slot, par]
            if masked:
                kpos = k0 + lax.broadcasted_iota(I32, (tk, tq), 0)
                qpos = q0 + lax.broadcasted_iota(I32, (tk, tq), 1)
                s = jnp.where(kpos <= qpos, s, NEG)
            v_blk = vT_ref[par, :, pl.ds(k0, tk)]
            if online_max:
                m_old = m_ref[par]
                m_new = jnp.maximum(m_old, jnp.max(s, axis=0, keepdims=True))
                p = jnp.exp2(s - m_new).astype(BF16)
                pv = jnp.dot(v_blk, p, preferred_element_type=F32)
                acc_ref[par] = acc_ref[par] * jnp.exp2(m_old - m_new) + pv
                m_ref[par] = m_new
            else:
                p = jnp.exp2(s).astype(BF16)
                acc_ref[par] += jnp.dot(v_blk, p, preferred_element_type=F32)

    def finish(qi):
        lane = lax.broadcasted_iota(I32, (tq, LANES), 1)
        outs = []
        for par in range(2):
            acc = acc_ref[par]
            l = acc[64:65, :] if par == 0 else acc[0:1, :]
            outs.append((acc * (1.0 / l)).T)
        q0 = pl.multiple_of(qi * tq, tq)
        o_ref[pl.ds(q0, tq), :] = jnp.where(lane < 64, outs[0], outs[1]).astype(o_ref.dtype)
        reset()

    reset()
    scores(0, 0, FIRST)
    consume(0, 0, FIRST, True)
    finish(0)
    if n_q > 1:
        scores(1, 0, FIRST)

    def query_block(qi, carry):
        n_mid = qi - 1
        nxt = jnp.minimum(qi + 1, n_q - 1)
        scores(qi, 1, 1)
        consume(qi, 0, FIRST, False)

        def trip(t, c):
            j0 = 1 + ATTN_BLOCKS_PER_TRIP * t
            for u in range(ATTN_BLOCKS_PER_TRIP):
                scores(qi, j0 + u + 1, u % 2)
                consume(qi, j0 + u, (u + 1) % 2, False)
            return c

        lax.fori_loop(0, n_mid // ATTN_BLOCKS_PER_TRIP, trip, 0)
        j0 = 1 + (n_mid // ATTN_BLOCKS_PER_TRIP) * ATTN_BLOCKS_PER_TRIP
        for rem in range(ATTN_BLOCKS_PER_TRIP):
            @pl.when(n_mid % ATTN_BLOCKS_PER_TRIP == rem)
            def _(rem=rem):
                for u in range(rem):
                    scores(qi, j0 + u + 1, u % 2)
                    consume(qi, j0 + u, (u + 1) % 2, False)
                scores(nxt, 0, FIRST)
                consume(qi, j0 + rem, (rem + 1) % 2, True)
                finish(qi)
        return carry

    lax.fori_loop(1, n_q, query_block, 0)


def _attn_call(qT, k, vT, tq, tk, online_max):
    B, H, _, S = qT.shape
    return pl.pallas_call(
        functools.partial(_attn_kernel, tq=tq, tk=tk, n_q=S // tq, online_max=online_max),
        grid=(B, H // 2),
        in_specs=[
            pl.BlockSpec((None, 2, LANES, S), lambda b, g: (b, g, 0, 0)),
            pl.BlockSpec((None, 2, S, LANES), lambda b, g: (b, g, 0, 0)),
            pl.BlockSpec((None, 2, LANES, S), lambda b, g: (b, g, 0, 0)),
        ],
        out_specs=pl.BlockSpec((None, S, LANES), lambda b, g: (b, 0, g)),
        out_shape=jax.ShapeDtypeStruct((B, S, BRANCH_WIDTH), BF16),
        scratch_shapes=[pltpu.VMEM((2, 1, tq), F32), pltpu.VMEM((2, LANES, tq), F32),
                        pltpu.VMEM((3, 2, tk, tq), F32)],
        compiler_params=pltpu.CompilerParams(
            dimension_semantics=("parallel", "parallel"), vmem_limit_bytes=VMEM_LIMIT),
        name="attn_online" if online_max else "attn",
    )(qT, k, vT)


def _attention(qT, k, vT, g_q, g_k, tq, tk):
    bound = 16.0 * LOG2E * jnp.max(jnp.abs(g_q)) * jnp.max(jnp.abs(g_k))
    return lax.cond(bound <= SCORE_CAP,
                    lambda: _attn_call(qT, k, vT, tq, tk, False),
                    lambda: _attn_call(qT, k, vT, tq, tk, True))


def _merge_kernel(x_ref, ya_ref, yb_ref, yc_ref, gmix_ref, wg_ref, bg_ref, wb_ref, wo_ref,
                  gffn_ref, wr_ref, br_ref,
                  x1_ref, h2_ref, route_ref, route_t_ref, cnt_ref, carry_ref, *, T):
    i = pl.program_id(0)

    @pl.when(i == 0)
    def _():
        carry_ref[...] = jnp.zeros_like(carry_ref)

    x = x_ref[...]
    hb = _rms(x, gmix_ref[...]).astype(BF16)
    merged = None
    for bi, y_ref in enumerate((ya_ref, yb_ref, yc_ref)):
        sl = slice(bi * D_MODEL, (bi + 1) * D_MODEL)
        gate = jax.nn.sigmoid(jnp.dot(hb, wg_ref[:, sl], preferred_element_type=F32) + bg_ref[:, sl])
        term = gate * jnp.dot(y_ref[...], wb_ref[bi], preferred_element_type=F32)
        merged = term if merged is None else merged + term
    x1 = x + jnp.dot(merged.astype(BF16), wo_ref[...], preferred_element_type=F32)
    x1_ref[...] = x1
    h2 = _rms(x1, gffn_ref[...])
    _store_row_tiles(h2_ref, h2)

    h_hi = h2.astype(BF16)
    h_lo = (h2 - h_hi.astype(F32)).astype(BF16)
    hw = jnp.dot(h_hi, wr_ref[...], preferred_element_type=F32)
    lw = jnp.dot(h_lo, wr_ref[:, 0:LANES], preferred_element_type=F32)
    logits = hw[:, 0:LANES] + (hw[:, LANES:] + lw) + br_ref[...]
    lane = lax.broadcasted_iota(I32, (T, LANES), 1).astype(F32)
    big = float(LANES)

    def first_argmax(v):
        m = jnp.max(v, axis=-1, keepdims=True)
        return m, jnp.min(jnp.where(v == m, lane, big), axis=-1, keepdims=True)

    lg = jnp.where(lane < N_EXPERT_GROUPS, logits, NEG)
    mg, grp = first_argmax(lg)
    p_grp = 1.0 / jnp.sum(jnp.exp(lg - mg), axis=-1, keepdims=True)
    lo_lane = N_EXPERT_GROUPS + grp * EXPERTS_PER_GROUP
    le = jnp.where((lane >= lo_lane) & (lane < lo_lane + EXPERTS_PER_GROUP), logits, NEG)
    m1, i1 = first_argmax(le)
    m2, i2 = first_argmax(jnp.where(lane == i1, NEG, le))
    e21 = jnp.exp(m2 - m1)
    g1 = p_grp / (1.0 + e21)
    g2 = p_grp * e21 / (1.0 + e21)
    e1 = i1 - N_EXPERT_GROUPS
    e2 = i2 - N_EXPERT_GROUPS

    oh1 = lane == e1
    oh2 = lane == e2
    sel = (oh1 | oh2).astype(F32)
    r_i = lax.broadcasted_iota(I32, (T, T), 0)
    c_i = lax.broadcasted_iota(I32, (T, T), 1)
    before = (r_i > c_i).astype(BF16)
    seen = jnp.dot(before, sel.astype(BF16), preferred_element_type=F32) + carry_ref[...]
    r1 = jnp.sum(jnp.where(oh1, seen, 0.0), axis=-1, keepdims=True)
    r2 = jnp.sum(jnp.where(oh2, seen, 0.0), axis=-1, keepdims=True)
    carry_ref[...] = carry_ref[...] + jnp.sum(sel, axis=0, keepdims=True)
    cnt_ref[...] = carry_ref[...]

    route = jnp.zeros((T, LANES), F32)
    for idx, val in enumerate((e1, e2, g1, g2, r1, r2)):
        route = jnp.where(lane == idx, val, route)
    route_ref[...] = route
    route_t_ref[...] = route.T[0:ROUTE_ROWS]


def _row_tile_rows(s, n_rows):
    return pl.ds(s, n_rows, stride=ROW_TILE)


def _store_row_tiles(ref, v):
    for s in range(ROW_TILE):
        ref[_row_tile_rows(s, v.shape[0]), :] = v[:, s * LANES:(s + 1) * LANES]


def _load_row_tiles(ref):
    n_rows = ref.shape[0] // ROW_TILE
    return jnp.concatenate([ref[_row_tile_rows(s, n_rows), :] for s in range(ROW_TILE)], axis=-1)


def _merge_call(x2d, ya, yb, yc, w, T):
    N, D = x2d.shape
    const = _const_spec
    tile = lambda width: pl.BlockSpec((T, width), lambda i: (i, 0))
    return pl.pallas_call(
        functools.partial(_merge_kernel, T=T),
        grid=(N // T,),
        in_specs=[
            tile(D), tile(BRANCH_WIDTH), tile(BRANCH_WIDTH), tile(BRANCH_WIDTH),
            const(1, D), const(D, N_BRANCH * D), const(1, N_BRANCH * D),
            const(N_BRANCH, BRANCH_WIDTH, D), const(D, D), const(1, D), const(D, 2 * LANES), const(1, LANES),
        ],
        out_specs=[tile(D), pl.BlockSpec((T * ROW_TILE, LANES), lambda i: (i, 0)), tile(LANES),
                   pl.BlockSpec((ROUTE_ROWS, T), lambda i: (0, i)),
                   pl.BlockSpec((1, LANES), lambda i: (0, 0))],
        out_shape=[
            jax.ShapeDtypeStruct((N, D), F32),
            jax.ShapeDtypeStruct((N * ROW_TILE, LANES), F32),
            jax.ShapeDtypeStruct((N, LANES), F32),
            jax.ShapeDtypeStruct((ROUTE_ROWS, N), F32),
            jax.ShapeDtypeStruct((1, LANES), F32),
        ],
        scratch_shapes=[pltpu.VMEM((1, LANES), F32)],
        compiler_params=pltpu.CompilerParams(
            dimension_semantics=("arbitrary",), vmem_limit_bytes=VMEM_LIMIT),
        name="merge",
    )(x2d, ya, yb, yc, w["g_mix"], w["w_gate"], w["b_gate"], w["w_branch"], w["w_out"],
      w["g_ffn"], w["w_router"], w["b_router"])


def _dispatch_kernel(dest_ref, h2_ref, xs_ref, sem, *, T):
    def row_copy(t, d):
        return pltpu.make_async_copy(h2_ref.at[_row_tile(t)], xs_ref.at[_row_tile(d)], sem)

    def body(tb, carry):
        t0 = tb * ISSUE_UNROLL
        for u in range(ISSUE_UNROLL):
            for kk in range(TOP_K):
                row_copy(t0 + u, dest_ref[0, kk * T + t0 + u]).start(priority=kk % N_DMA_PRIORITIES)
        return carry

    lax.fori_loop(0, T // ISSUE_UNROLL, body, 0)
    for kk in range(TOP_K):
        pltpu.make_async_copy(h2_ref, xs_ref.at[pl.ds(0, T * ROW_TILE)], sem).wait()


def _row_tile(r):
    return pl.ds(pl.multiple_of(r * ROW_TILE, ROW_TILE), ROW_TILE)


def _dispatch_call(dest3, h2, T):
    N = h2.shape[0] // ROW_TILE
    return pl.pallas_call(
        functools.partial(_dispatch_kernel, T=T),
        grid=(N // T,),
        in_specs=[
            pl.BlockSpec((None, 1, TOP_K * T), lambda i: (i, 0, 0), memory_space=pltpu.SMEM),
            pl.BlockSpec((T * ROW_TILE, LANES), lambda i: (i, 0)),
        ],
        out_specs=pl.BlockSpec(memory_space=pl.ANY),
        out_shape=jax.ShapeDtypeStruct((TOP_K * N * ROW_TILE, LANES), F32),
        scratch_shapes=[pltpu.SemaphoreType.DMA],
        compiler_params=pltpu.CompilerParams(
            dimension_semantics=("arbitrary",), vmem_limit_bytes=VMEM_LIMIT),
        name="dispatch",
    )(dest3, h2)


def _gmm_kernel(blk_ref, exp_ref, lo_ref, hi_ref, xs_ref, w1_ref, w3_ref, w2_ref, y_ref,
                w13_bf, w2_bf, *, bm):
    i = pl.program_id(0)
    lo = lo_ref[i]
    hi = hi_ref[i]
    prev = jnp.maximum(i - 1, 0)
    first = jnp.logical_or(i == 0, blk_ref[i] != blk_ref[prev])
    new_expert = jnp.logical_or(i == 0, exp_ref[i] != exp_ref[prev])

    @pl.when(first)
    def _():
        y_ref[...] = jnp.zeros_like(y_ref)

    @pl.when(new_expert)
    def _():
        w13_bf[:, 0:D_EXPERT] = w1_ref[...].astype(BF16)
        w13_bf[:, D_EXPERT:] = w3_ref[...].astype(BF16)
        w2_bf[...] = w2_ref[...].astype(BF16)

    @pl.when(hi > lo)
    def _():
        x = _load_row_tiles(xs_ref).astype(BF16)
        ab = jnp.dot(x, w13_bf[...], preferred_element_type=F32)
        a = ab[:, 0:D_EXPERT]
        b = ab[:, D_EXPERT:]
        mid = (a * jax.nn.sigmoid(a) * b).astype(BF16)
        y = jnp.dot(mid, w2_bf[...], preferred_element_type=F32)
        row = lax.broadcasted_iota(I32, (bm, LANES), 0)
        mine = (row >= lo) & (row < hi)
        for s in range(ROW_TILE):
            rows = _row_tile_rows(s, bm)
            y_ref[rows, :] = jnp.where(mine, y[:, s * LANES:(s + 1) * LANES], y_ref[rows, :])


def _gmm_call(plan, xs, w1, w3, w2, layer, bm):
    A = xs.shape[0] // ROW_TILE
    D = D_MODEL
    n_items = plan[0].shape[0]
    grid_spec = pltpu.PrefetchScalarGridSpec(
        num_scalar_prefetch=4,
        grid=(n_items,),
        in_specs=[
            pl.BlockSpec((bm * ROW_TILE, LANES), lambda i, blk, ex, lo, hi: (blk[i], 0)),
            pl.BlockSpec((None, None, D, D_EXPERT), lambda i, blk, ex, lo, hi: (layer, ex[i], 0, 0)),
            pl.BlockSpec((None, None, D, D_EXPERT), lambda i, blk, ex, lo, hi: (layer, ex[i], 0, 0)),
            pl.BlockSpec((None, None, D_EXPERT, D), lambda i, blk, ex, lo, hi: (layer, ex[i], 0, 0)),
        ],
        out_specs=pl.BlockSpec((bm * ROW_TILE, LANES), lambda i, blk, ex, lo, hi: (blk[i], 0)),
        scratch_shapes=[pltpu.VMEM((D, 2 * D_EXPERT), BF16), pltpu.VMEM((D_EXPERT, D), BF16)],
    )
    return pl.pallas_call(
        functools.partial(_gmm_kernel, bm=bm),
        grid_spec=grid_spec,
        out_shape=jax.ShapeDtypeStruct((A * ROW_TILE, LANES), F32),
        compiler_params=pltpu.CompilerParams(
            dimension_semantics=("arbitrary",), vmem_limit_bytes=VMEM_LIMIT),
        name="gmm",
    )(*plan, xs, w1, w3, w2)


def _gmm_plan(counts, n_rows, bm):
    n_blk = n_rows // bm
    n_items = n_blk + N_EXPERTS - 1
    ends = jnp.cumsum(counts)
    starts = ends - counts
    first_blk = starts // bm
    n_it = jnp.where(counts > 0, (ends - 1) // bm - first_blk + 1, 0)
    item_end = jnp.cumsum(n_it)
    item_start = item_end - n_it
    total = item_end[-1]
    ids = jnp.arange(n_items, dtype=I32)
    valid = ids < total
    ex = jnp.minimum(jnp.sum(item_end[None, :] <= ids[:, None], axis=1).astype(I32), N_EXPERTS - 1)
    ex = jnp.where(valid, ex, ex[jnp.maximum(total - 1, 0)])
    blk = jnp.where(valid, first_blk[ex] + ids - item_start[ex], n_blk - 1)
    lo = jnp.where(valid, jnp.maximum(starts[ex], blk * bm) - blk * bm, 0)
    hi = jnp.where(valid, jnp.minimum(ends[ex], (blk + 1) * bm) - blk * bm, 0)
    return blk.astype(I32), ex.astype(I32), lo.astype(I32), hi.astype(I32)


def _combine_kernel(dest_ref, dest_next_ref, x1_ref, route_ref, y_hbm, o_ref, buf, sem, *, T):
    i = pl.program_id(0)
    n_steps = pl.num_programs(0)

    def issue(d_ref, slot):
        def body(tb, carry):
            t0 = tb * ISSUE_UNROLL
            for u in range(ISSUE_UNROLL):
                for kk in range(TOP_K):
                    d = d_ref[0, kk * T + t0 + u]
                    pltpu.make_async_copy(y_hbm.at[_row_tile(d)], buf.at[slot * TOP_K + kk, _row_tile(t0 + u)],
                                          sem.at[slot]).start(priority=kk % N_DMA_PRIORITIES)
            return carry

        lax.fori_loop(0, T // ISSUE_UNROLL, body, 0)

    def finish(slot):
        for kk in range(TOP_K):
            pltpu.make_async_copy(y_hbm.at[pl.ds(0, T * ROW_TILE)], buf.at[slot * TOP_K + kk],
                                  sem.at[slot]).wait()
        g1 = route_ref[:, 2:3]
        g2 = route_ref[:, 3:4]
        for s in range(ROW_TILE):
            sl = slice(s * LANES, (s + 1) * LANES)
            rows = _row_tile_rows(s, T)
            o_ref[:, sl] = x1_ref[:, sl] + (g1 * buf[slot * TOP_K, rows, :] + g2 * buf[slot * TOP_K + 1, rows, :])

    @pl.when(i == 0)
    def _():
        issue(dest_ref, 0)

    for slot in range(2):
        @pl.when(i % 2 == slot)
        def _(slot=slot):
            @pl.when(i + 1 < n_steps)
            def _():
                issue(dest_next_ref, 1 - slot)

            finish(slot)


def _combine_call(dest3, x1, route, y, T):
    N, D = x1.shape
    n_steps = N // T
    dest_spec = lambda index: pl.BlockSpec((None, 1, TOP_K * T), index, memory_space=pltpu.SMEM)
    return pl.pallas_call(
        functools.partial(_combine_kernel, T=T),
        grid=(n_steps,),
        in_specs=[
            dest_spec(lambda i: (i, 0, 0)),
            dest_spec(lambda i: (jnp.minimum(i + 1, n_steps - 1), 0, 0)),
            pl.BlockSpec((T, D), lambda i: (i, 0)),
            pl.BlockSpec((T, LANES), lambda i: (i, 0)),
            pl.BlockSpec(memory_space=pl.ANY),
        ],
        out_specs=pl.BlockSpec((T, D), lambda i: (i, 0)),
        out_shape=jax.ShapeDtypeStruct((N, D), F32),
        scratch_shapes=[pltpu.VMEM((2 * TOP_K, T * ROW_TILE, LANES), F32), pltpu.SemaphoreType.DMA((2,))],
        compiler_params=pltpu.CompilerParams(
            dimension_semantics=("arbitrary",), vmem_limit_bytes=VMEM_LIMIT),
        name="combine",
    )(dest3, dest3, x1, route, y)


def _block_diag(blocks):
    n = len(blocks)
    rows = []
    for i, b in enumerate(blocks):
        rows.append(jnp.concatenate(
            [b if j == i else jnp.zeros((b.shape[0], blocks[j].shape[1]), b.dtype) for j in range(n)], axis=1))
    return jnp.concatenate(rows, axis=0)


def _rep_forget(cols):
    half = jnp.concatenate(
        [jnp.tile(cols, (1, F_REP)), jnp.zeros((cols.shape[0], 64 - F_REP * A_HEADS), cols.dtype)], axis=1)
    return jnp.concatenate([half, half], axis=1)


def _head_to_gate_lanes():
    head = jnp.arange(BRANCH_WIDTH, dtype=I32)[:, None] // A_HEAD_DIM
    lane = jnp.arange(LANES, dtype=I32)[None, :]
    hit = ((lane & 7) == head) & (((lane & 63) >> 3) < F_REP)
    return hit.astype(BF16)


def _prep_layer(l, g_mix, w_in, b_fgate, b_gate, g_q, g_k, w_pool, s_pool, g_sgu, w_sgu, b_sgu,
                w_branch, w_out, g_ffn, w_rg, b_rg, w_re, b_re):
    wi = w_in[l]
    w_pack = jnp.concatenate(
        [wi[:, 0:OFF_F], _rep_forget(wi[:, OFF_F:OFF_P]), wi[:, OFF_P:OFF_G]], axis=1).astype(BF16)
    pad_r = LANES - N_EXPERT_GROUPS - N_EXPERTS
    w_r = jnp.concatenate([w_rg[l], w_re[l], jnp.zeros((D_MODEL, pad_r), F32)], axis=1)
    w_r_hi = w_r.astype(BF16)
    w_r_lo = (w_r - w_r_hi.astype(F32)).astype(BF16)
    return dict(
        g_mix=g_mix[l][None, :],
        w_pack=w_pack,
        b_f=_rep_forget(b_fgate[l][None, :]),
        g_q=jnp.tile(g_q[l], A_HEADS)[None, :],
        g_k=jnp.tile(g_k[l], A_HEADS)[None, :],
        bd=_block_diag([jnp.ones((A_HEAD_DIM, A_HEAD_DIM), BF16)] * A_HEADS),
        sel=_head_to_gate_lanes(),
        w_pool=_block_diag([w_pool[l][g] for g in range(N_GROUPS)]).astype(BF16),
        s_pool=s_pool[l][None, :],
        g_sgu=g_sgu[l][None, :],
        w_sgu=w_sgu[l],
        b_sgu=jnp.broadcast_to(b_sgu[l][:, :, None], (N_GROUPS, C_CHUNK, GROUP_DIM)),
        w_gate=wi[:, OFF_G:].astype(BF16),
        b_gate=b_gate[l].reshape(1, N_BRANCH * D_MODEL),
        w_branch=w_branch[l].astype(BF16),
        w_out=w_out[l].astype(BF16),
        g_ffn=g_ffn[l][None, :],
        w_router=jnp.concatenate([w_r_hi, w_r_lo], axis=1),
        b_router=jnp.concatenate([b_rg[l], b_re[l], jnp.zeros((pad_r,), F32)])[None, :],
    )


def kernel(x, g_mix, w_in, b_fgate, b_gate, g_q, g_k, w_pool, s_pool, g_sgu, w_sgu, b_sgu, w_branch, w_out,
           g_ffn, w_rg, b_rg, w_re, b_re, w1, w3, w2):
    B, S, D = x.shape
    assert D == D_MODEL and x.dtype == F32
    N = B * S
    T, tq, tk, bm, tm = _tiles(S)
    assert S % T == 0 and S % tq == 0 and tq == tk and T % C_CHUNK == 0
    assert (TOP_K * N) % bm == 0 and N % tm == 0 and tm % ISSUE_UNROLL == 0
    depth = w_in.shape[0]
    for l in range(depth):
        w = _prep_layer(l, g_mix, w_in, b_fgate, b_gate, g_q, g_k, w_pool, s_pool, g_sgu, w_sgu, b_sgu,
                        w_branch, w_out, g_ffn, w_rg, b_rg, w_re, b_re)
        qT, k, vT, yb, yc = _proj_call(x, w, T)
        ya = _attention(qT, k, vT, g_q[l], g_k[l], tq, tk)
        x1, h2, route, route_t, cnt = _merge_call(
            x.reshape(N, D), ya.reshape(N, -1), yb.reshape(N, -1), yc.reshape(N, -1), w, T)
        counts = cnt[0, :N_EXPERTS].astype(I32)
        starts = jnp.cumsum(counts) - counts
        experts = route_t[0:TOP_K].astype(I32)
        expert_ids = jnp.arange(N_EXPERTS, dtype=I32)[:, None, None]
        start_of = jnp.sum(jnp.where(experts[None] == expert_ids, starts[:, None, None], 0), axis=0)
        dest = start_of + route_t[4:4 + TOP_K].astype(I32)
        dest3 = dest.reshape(TOP_K, N // tm, tm).transpose(1, 0, 2).reshape(N // tm, 1, TOP_K * tm)
        xs = _dispatch_call(dest3, h2, tm)
        y = _gmm_call(_gmm_plan(counts, TOP_K * N, bm), xs, w1, w3, w2, l, bm)
        x = _combine_call(dest3, x1, route, y, tm).reshape(B, S, D)
    return x
```

```python
import functools

import jax
import jax.numpy as jnp
from jax import lax
from jax.experimental import pallas as pl
from jax.experimental.pallas import tpu as pltpu

F32 = jnp.float32
BF16 = jnp.bfloat16
I32 = jnp.int32
U32 = jnp.uint32

D_MODEL = 1024
A_HEADS = 8
A_HEAD_DIM = 64
BRANCH_WIDTH = 512
N_BRANCH = 3
POOL_WINDOWS = (2, 4, 8, 16)
GROUP_DIM = 128
N_GROUPS = 4
C_CHUNK = 128
N_EXPERT_GROUPS = 4
EXPERTS_PER_GROUP = 8
N_EXPERTS = N_EXPERT_GROUPS * EXPERTS_PER_GROUP
TOP_K = 2
D_EXPERT = 256
RMS_EPS = 1e-6
OFF_F = 3 * BRANCH_WIDTH
OFF_P = OFF_F + A_HEADS
OFF_U = OFF_P + BRANCH_WIDTH
OFF_SV = OFF_U + BRANCH_WIDTH
OFF_G = OFF_SV + BRANCH_WIDTH

LANES = 128
SUBLANES = 8
V7X_VMEM_BYTES = 64 * 1024 * 1024
VMEM_LIMIT = 56 * 1024 * 1024
ROW_TILE = D_MODEL // (2 * LANES)
HALF_D = D_MODEL // 2
ROUTE_ROWS = SUBLANES
ISSUE_UNROLL = 8
N_DMA_PRIORITIES = 2
ATTN_BLOCKS_PER_TRIP = 4

PK_Q, PK_K, PK_V = 0, 512, 1024
PK_F = 1536
PK_P = PK_F + LANES
PK_U = PK_P + BRANCH_WIDTH
PK_SV = PK_U + BRANCH_WIDTH
PK_W = PK_SV + BRANCH_WIDTH
F_REP = 6

LOG2E = 1.4426950408889634
SCORE_CAP = 96.0
NEG = -1e30
HALO = 16


def _tiles(seq_len):
    t_proj = min(512, seq_len)
    t_q = min(512, seq_len)
    t_k = min(512, seq_len)
    moe_block = 512
    t_move = min(1024, seq_len)
    return t_proj, t_q, t_k, moe_block, t_move


def _const_spec(*shape):
    zeros = (0,) * len(shape)
    return pl.BlockSpec(shape, lambda *_: zeros, pipeline_mode=pl.Buffered(1))


def _rms(x, g):
    return x * lax.rsqrt(jnp.mean(x * x, axis=-1, keepdims=True) + RMS_EPS) * g


def _gelu_tanh(x):
    cdf = 0.5 * (1.0 + jnp.tanh(0.7978845608028654 * (x + 0.044715 * (x * x * x))))
    return x * cdf


def _log_sigmoid(x):
    return jnp.minimum(x, 0.0) - jnp.log1p(jnp.exp(-jnp.abs(x)))


def _split3(c):
    hi = c.astype(BF16).astype(F32)
    r = c - hi
    lo = r.astype(BF16).astype(F32)
    lolo = (r - lo).astype(BF16).astype(F32)
    return hi, lo, lolo


def _proj_kernel(x_ref, gmix_ref, w_ref, bf_ref, gq_ref, gk_ref, bd_ref, sel_ref, wpool_ref, spool_ref,
                 gsgu_ref, wsgu_ref, bsgu_ref,
                 qT_ref, k_ref, vT_ref, yb_ref, yc_ref,
                 carry_ref, halo_ref, *, T):
    i = pl.program_id(1)

    @pl.when(i == 0)
    def _():
        carry_ref[...] = jnp.zeros_like(carry_ref)
        halo_ref[...] = jnp.zeros_like(halo_ref)

    hb = _rms(x_ref[...], gmix_ref[...]).astype(BF16)

    def proj(lo, width):
        return jnp.dot(hb, w_ref[:, lo:lo + width], preferred_element_type=F32)

    bd = bd_ref[...]

    def head_norm(z, g):
        sq = z * z
        hi = sq.astype(BF16)
        lo = (sq - hi.astype(F32)).astype(BF16)
        ss = (jnp.dot(hi, bd, preferred_element_type=F32)
              + jnp.dot(lo, bd, preferred_element_type=F32))
        return z * lax.rsqrt(ss * (1.0 / A_HEAD_DIM) + RMS_EPS) * g

    qn = head_norm(proj(PK_Q, BRANCH_WIDTH), gq_ref[...]) * (A_HEAD_DIM ** -0.5 * LOG2E)
    kn = head_norm(proj(PK_K, BRANCH_WIDTH), gk_ref[...])
    zv = proj(PK_V, BRANCH_WIDTH)
    qk = qn * kn
    qk_hi = qk.astype(BF16)
    qk_lo = (qk - qk_hi.astype(F32)).astype(BF16)
    diag = (jnp.dot(qk_hi, sel_ref[...], preferred_element_type=F32)
            + jnp.dot(qk_lo, sel_ref[...], preferred_element_type=F32))

    logf = _log_sigmoid(proj(PK_F, LANES) + bf_ref[...])
    r_i = lax.broadcasted_iota(I32, (T, T), 0)
    c_i = lax.broadcasted_iota(I32, (T, T), 1)
    tri = (r_i >= c_i).astype(F32)
    c = jnp.dot(tri, logf, preferred_element_type=F32,
                precision=lax.Precision.HIGHEST) + carry_ref[...]
    carry_ref[...] = c[T - 1:T, :]

    lane = lax.broadcasted_iota(I32, (T, LANES), 1)
    lm = lane & 63
    grp = lm >> 3
    hsel = lm & 7
    first3 = grp < 3
    second3 = (grp >= 3) & (grp < F_REP)

    def parts(v):
        hi, lo, lolo = _split3(v)
        return jnp.where((grp == 0) | (grp == 3), hi, jnp.where((grp == 1) | (grp == 4), lo, lolo))

    k_aug = jnp.where(first3, -parts(c * LOG2E), 0.0)
    q_aug = jnp.where(second3, parts(c * LOG2E - diag), 0.0)

    for g in range(A_HEADS // 2):
        sl = slice(g * LANES, (g + 1) * LANES)
        zq2, zk2, zv2 = qn[:, sl], kn[:, sl], zv[:, sl]
        for par in range(2):
            h = 2 * g + par
            keep = (lane < 64) if par == 0 else (lane >= 64)
            oh_k = (second3 & (hsel == h)).astype(F32)
            oh_q = (first3 & (hsel == h)).astype(F32)
            k_ref[h] = jnp.where(keep, zk2, k_aug + oh_k).astype(BF16)
            qT_ref[h] = jnp.where(keep, zq2, q_aug + oh_q).T.astype(BF16)
            ones_lane = 64 if par == 0 else 0
            vT_ref[h] = jnp.where(keep, zv2, (lane == ones_lane).astype(F32)).T.astype(BF16)

    p = proj(PK_P, BRANCH_WIDTH)
    row8 = lax.broadcasted_iota(I32, (8, GROUP_DIM), 0)
    pos = i * T + lax.broadcasted_iota(I32, (T, GROUP_DIM), 0)

    def shift_down(v, tail, d):
        r = pltpu.roll(v, d, 0)
        rt = pltpu.roll(tail, d, 0)
        top = jnp.where(row8 < d, rt[0:8], r[0:8])
        return jnp.concatenate([top, r[8:]], axis=0)

    pooled = []
    for gi, w in enumerate(POOL_WINDOWS):
        sl = slice(gi * GROUP_DIM, (gi + 1) * GROUP_DIM)
        s = p[:, sl]
        for lv in range(gi + 1):
            tail = halo_ref[lv, :, sl]
            halo_ref[lv, :, sl] = s[T - HALO:T, :]
            s = s + shift_down(s, tail, 1 << lv)
        cnt = jnp.minimum(pos + 1, w).astype(F32)
        pooled.append(s / cnt - p[:, sl])
    pooled = jnp.concatenate(pooled, axis=1).astype(BF16)
    yb = jnp.dot(pooled, wpool_ref[...], preferred_element_type=F32) * spool_ref[...]
    yb_ref[...] = yb.astype(yb_ref.dtype)

    gu = _gelu_tanh(proj(PK_U, BRANCH_WIDTH))
    gv = _gelu_tanh(proj(PK_SV, BRANCH_WIDTH))
    t_r = lax.broadcasted_iota(I32, (C_CHUNK, C_CHUNK), 0)
    t_c = lax.broadcasted_iota(I32, (C_CHUNK, C_CHUNK), 1)
    causal = t_r >= t_c
    for g in range(N_GROUPS):
        sl = slice(g * GROUP_DIM, (g + 1) * GROUP_DIM)
        vn = _rms(gv[:, sl], gsgu_ref[:, sl]).astype(BF16)
        wc = jnp.where(causal, wsgu_ref[g], 0.0).astype(BF16)
        for ch in range(T // C_CHUNK):
            rows = slice(ch * C_CHUNK, (ch + 1) * C_CHUNK)
            mixed = jnp.dot(wc, vn[rows], preferred_element_type=F32) + bsgu_ref[g]
            yc_ref[rows, sl] = (gu[rows, sl] * mixed).astype(yc_ref.dtype)


def _proj_call(x, w, T):
    B, S, D = x.shape
    H = A_HEADS
    const = _const_spec
    return pl.pallas_call(
        functools.partial(_proj_kernel, T=T),
        grid=(B, S // T),
        in_specs=[
            pl.BlockSpec((None, T, D), lambda b, i: (b, i, 0)),
            const(1, D), const(D, PK_W), const(1, LANES), const(1, BRANCH_WIDTH), const(1, BRANCH_WIDTH),
            const(BRANCH_WIDTH, BRANCH_WIDTH), const(BRANCH_WIDTH, LANES), const(BRANCH_WIDTH, BRANCH_WIDTH),
            const(1, BRANCH_WIDTH), const(1, BRANCH_WIDTH), const(N_GROUPS, C_CHUNK, C_CHUNK), const(N_GROUPS, C_CHUNK, GROUP_DIM),
        ],
        out_specs=[
            pl.BlockSpec((None, H, LANES, T), lambda b, i: (b, 0, 0, i)),
            pl.BlockSpec((None, H, T, LANES), lambda b, i: (b, 0, i, 0)),
            pl.BlockSpec((None, H, LANES, T), lambda b, i: (b, 0, 0, i)),
            pl.BlockSpec((None, T, BRANCH_WIDTH), lambda b, i: (b, i, 0)),
            pl.BlockSpec((None, T, BRANCH_WIDTH), lambda b, i: (b, i, 0)),
        ],
        out_shape=[
            jax.ShapeDtypeStruct((B, H, LANES, S), BF16),
            jax.ShapeDtypeStruct((B, H, S, LANES), BF16),
            jax.ShapeDtypeStruct((B, H, LANES, S), BF16),
            jax.ShapeDtypeStruct((B, S, BRANCH_WIDTH), BF16),
            jax.ShapeDtypeStruct((B, S, BRANCH_WIDTH), BF16),
        ],
        scratch_shapes=[pltpu.VMEM((1, LANES), F32), pltpu.VMEM((4, HALO, BRANCH_WIDTH), F32)],
        compiler_params=pltpu.CompilerParams(
            dimension_semantics=("arbitrary", "arbitrary"), vmem_limit_bytes=VMEM_LIMIT),
        name="proj",
    )(x, w["g_mix"], w["w_pack"], w["b_f"], w["g_q"], w["g_k"], w["bd"], w["sel"], w["w_pool"], w["s_pool"],
      w["g_sgu"], w["w_sgu"], w["b_sgu"])


def _attn_kernel(qT_ref, k_ref, vT_ref, o_ref, m_ref, acc_ref, s_ref, *, tq, tk, n_q, online_max):
    assert tq == tk
    FIRST = 2

    def reset():
        acc_ref[...] = jnp.zeros_like(acc_ref)
        if online_max:
            m_ref[...] = jnp.full(m_ref.shape, NEG, F32)

    def scores(qi, j, slot):
        k0 = pl.multiple_of(j * tk, tk)
        q0 = pl.multiple_of(qi * tq, tq)
        for par in range(2):
            s_ref[slot, par] = jnp.dot(k_ref[par, pl.ds(k0, tk), :], qT_ref[par, :, pl.ds(q0, tq)],
                                       preferred_element_type=F32)

    def consume(qi, j, slot, masked):
        k0 = pl.multiple_of(j * tk, tk)
        q0 = qi * tq
        for par in range(2):
            s = s_ref[slot, par]
            if masked:
                kpos = k0 + lax.broadcasted_iota(I32, (tk, tq), 0)
                qpos = q0 + lax.broadcasted_iota(I32, (tk, tq), 1)
                s = jnp.where(kpos <= qpos, s, NEG)
            v_blk = vT_ref[par, :, pl.ds(k0, tk)]
            if online_max:
                m_old = m_ref[par]
                m_new = jnp.maximum(m_old, jnp.max(s, axis=0, keepdims=True))
                p = jnp.exp2(s - m_new).astype(BF16)
                pv = jnp.dot(v_blk, p, preferred_element_type=F32)
                acc_ref[par] = acc_ref[par] * jnp.exp2(m_old - m_new) + pv
                m_ref[par] = m_new
            else:
                p = jnp.exp2(s).astype(BF16)
                acc_ref[par] += jnp.dot(v_blk, p, preferred_element_type=F32)

    def finish(qi):
        lane = lax.broadcasted_iota(I32, (tq, LANES), 1)
        outs = []
        for par in range(2):
            acc = acc_ref[par]
            l = acc[64:65, :] if par == 0 else acc[0:1, :]
            outs.append((acc * (1.0 / l)).T)
        q0 = pl.multiple_of(qi * tq, tq)
        o_ref[pl.ds(q0, tq), :] = jnp.where(lane < 64, outs[0], outs[1]).astype(o_ref.dtype)
        reset()

    reset()
    scores(0, 0, FIRST)
    consume(0, 0, FIRST, True)
    finish(0)
    if n_q > 1:
        scores(1, 0, FIRST)

    def query_block(qi, carry):
        n_mid = qi - 1
        nxt = jnp.minimum(qi + 1, n_q - 1)
        scores(qi, 1, 1)
        consume(qi, 0, FIRST, False)

        def trip(t, c):
            j0 = 1 + ATTN_BLOCKS_PER_TRIP * t
            for u in range(ATTN_BLOCKS_PER_TRIP):
                scores(qi, j0 + u + 1, u % 2)
                consume(qi, j0 + u, (u + 1) % 2, False)
            return c

        lax.fori_loop(0, n_mid // ATTN_BLOCKS_PER_TRIP, trip, 0)
        j0 = 1 + (n_mid // ATTN_BLOCKS_PER_TRIP) * ATTN_BLOCKS_PER_TRIP
        for rem in range(ATTN_BLOCKS_PER_TRIP):
            @pl.when(n_mid % ATTN_BLOCKS_PER_TRIP == rem)
            def _(rem=rem):
                for u in range(rem):
                    scores(qi, j0 + u + 1, u % 2)
                    consume(qi, j0 + u, (u + 1) % 2, False)
                scores(nxt, 0, FIRST)
                consume(qi, j0 + rem, (rem + 1) % 2, True)
                finish(qi)
        return carry

    lax.fori_loop(1, n_q, query_block, 0)


def _attn_call(qT, k, vT, tq, tk, online_max):
    B, H, _, S = qT.shape
    return pl.pallas_call(
        functools.partial(_attn_kernel, tq=tq, tk=tk, n_q=S // tq, online_max=online_max),
        grid=(B, H // 2),
        in_specs=[
            pl.BlockSpec((None, 2, LANES, S), lambda b, g: (b, g, 0, 0)),
            pl.BlockSpec((None, 2, S, LANES), lambda b, g: (b, g, 0, 0)),
            pl.BlockSpec((None, 2, LANES, S), lambda b, g: (b, g, 0, 0)),
        ],
        out_specs=pl.BlockSpec((None, S, LANES), lambda b, g: (b, 0, g)),
        out_shape=jax.ShapeDtypeStruct((B, S, BRANCH_WIDTH), BF16),
        scratch_shapes=[pltpu.VMEM((2, 1, tq), F32), pltpu.VMEM((2, LANES, tq), F32),
                        pltpu.VMEM((3, 2, tk, tq), F32)],
        compiler_params=pltpu.CompilerParams(
            dimension_semantics=("parallel", "parallel"), vmem_limit_bytes=VMEM_LIMIT),
        name="attn_online" if online_max else "attn",
    )(qT, k, vT)


def _attention(qT, k, vT, g_q, g_k, tq, tk):
    bound = 16.0 * LOG2E * jnp.max(jnp.abs(g_q)) * jnp.max(jnp.abs(g_k))
    return lax.cond(bound <= SCORE_CAP,
                    lambda: _attn_call(qT, k, vT, tq, tk, False),
                    lambda: _attn_call(qT, k, vT, tq, tk, True))


def _merge_kernel(x_ref, ya_ref, yb_ref, yc_ref, gmix_ref, wg_ref, bg_ref, wb_ref, wo_ref,
                  gffn_ref, wr_ref, br_ref,
                  x1_ref, h2_ref, route_ref, route_t_ref, cnt_ref, carry_ref, *, T):
    i = pl.program_id(0)

    @pl.when(i == 0)
    def _():
        carry_ref[...] = jnp.zeros_like(carry_ref)

    x = x_ref[...]
    hb = _rms(x, gmix_ref[...]).astype(BF16)
    merged = None
    for bi, y_ref in enumerate((ya_ref, yb_ref, yc_ref)):
        sl = slice(bi * D_MODEL, (bi + 1) * D_MODEL)
        gate = jax.nn.sigmoid(jnp.dot(hb, wg_ref[:, sl], preferred_element_type=F32) + bg_ref[:, sl])
        term = gate * jnp.dot(y_ref[...], wb_ref[bi], preferred_element_type=F32)
        merged = term if merged is None else merged + term
    x1 = x + jnp.dot(merged.astype(BF16), wo_ref[...], preferred_element_type=F32)
    x1_ref[...] = x1
    h2 = _rms(x1, gffn_ref[...])
    _store_row_tiles(h2_ref, _pack_pairs(h2))

    h_hi = h2.astype(BF16)
    h_lo = (h2 - h_hi.astype(F32)).astype(BF16)
    hw = jnp.dot(h_hi, wr_ref[...], preferred_element_type=F32)
    lw = jnp.dot(h_lo, wr_ref[:, 0:LANES], preferred_element_type=F32)
    logits = hw[:, 0:LANES] + (hw[:, LANES:] + lw) + br_ref[...]
    lane = lax.broadcasted_iota(I32, (T, LANES), 1).astype(F32)
    big = float(LANES)

    def first_argmax(v):
        m = jnp.max(v, axis=-1, keepdims=True)
        return m, jnp.min(jnp.where(v == m, lane, big), axis=-1, keepdims=True)

    lg = jnp.where(lane < N_EXPERT_GROUPS, logits, NEG)
    mg, grp = first_argmax(lg)
    p_grp = 1.0 / jnp.sum(jnp.exp(lg - mg), axis=-1, keepdims=True)
    lo_lane = N_EXPERT_GROUPS + grp * EXPERTS_PER_GROUP
    le = jnp.where((lane >= lo_lane) & (lane < lo_lane + EXPERTS_PER_GROUP), logits, NEG)
    m1, i1 = first_argmax(le)
    m2, i2 = first_argmax(jnp.where(lane == i1, NEG, le))
    e21 = jnp.exp(m2 - m1)
    g1 = p_grp / (1.0 + e21)
    g2 = p_grp * e21 / (1.0 + e21)
    e1 = i1 - N_EXPERT_GROUPS
    e2 = i2 - N_EXPERT_GROUPS

    oh1 = lane == e1
    oh2 = lane == e2
    sel = (oh1 | oh2).astype(F32)
    r_i = lax.broadcasted_iota(I32, (T, T), 0)
    c_i = lax.broadcasted_iota(I32, (T, T), 1)
    before = (r_i > c_i).astype(BF16)
    seen = jnp.dot(before, sel.astype(BF16), preferred_element_type=F32) + carry_ref[...]
    r1 = jnp.sum(jnp.where(oh1, seen, 0.0), axis=-1, keepdims=True)
    r2 = jnp.sum(jnp.where(oh2, seen, 0.0), axis=-1, keepdims=True)
    carry_ref[...] = carry_ref[...] + jnp.sum(sel, axis=0, keepdims=True)
    cnt_ref[...] = carry_ref[...]

    route = jnp.zeros((T, LANES), F32)
    for idx, val in enumerate((e1, e2, g1, g2, r1, r2)):
        route = jnp.where(lane == idx, val, route)
    route_ref[...] = route
    route_t_ref[...] = route.T[0:ROUTE_ROWS]


def _row_tile_rows(s, n_rows, row0=0):
    return pl.ds(row0 * ROW_TILE + s, n_rows, stride=ROW_TILE)


def _pack_pairs(v):
    hi = lax.bitcast_convert_type(v[:, :HALF_D].astype(BF16).astype(F32), U32)
    lo = lax.bitcast_convert_type(v[:, HALF_D:].astype(BF16).astype(F32), U32)
    return hi | (lo >> 16)


def _unpack_pairs(w):
    hi = lax.bitcast_convert_type(w & jnp.uint32(0xFFFF0000), F32)
    lo = lax.bitcast_convert_type(w << 16, F32)
    return jnp.concatenate([hi, lo], axis=-1)


def _store_row_tiles(ref, words, row0=0):
    for s in range(ROW_TILE):
        ref[_row_tile_rows(s, words.shape[0], row0), :] = words[:, s * LANES:(s + 1) * LANES]


def _load_row_tiles(ref, n_rows, row0=0):
    return jnp.concatenate([ref[_row_tile_rows(s, n_rows, row0), :] for s in range(ROW_TILE)], axis=-1)


def _merge_call(x2d, ya, yb, yc, w, T):
    N, D = x2d.shape
    const = _const_spec
    tile = lambda width: pl.BlockSpec((T, width), lambda i: (i, 0))
    return pl.pallas_call(
        functools.partial(_merge_kernel, T=T),
        grid=(N // T,),
        in_specs=[
            tile(D), tile(BRANCH_WIDTH), tile(BRANCH_WIDTH), tile(BRANCH_WIDTH),
            const(1, D), const(D, N_BRANCH * D), const(1, N_BRANCH * D),
            const(N_BRANCH, BRANCH_WIDTH, D), const(D, D), const(1, D), const(D, 2 * LANES), const(1, LANES),
        ],
        out_specs=[tile(D), pl.BlockSpec((T * ROW_TILE, LANES), lambda i: (i, 0)), tile(LANES),
                   pl.BlockSpec((ROUTE_ROWS, T), lambda i: (0, i)),
                   pl.BlockSpec((1, LANES), lambda i: (0, 0))],
        out_shape=[
            jax.ShapeDtypeStruct((N, D), F32),
            jax.ShapeDtypeStruct((N * ROW_TILE, LANES), U32),
            jax.ShapeDtypeStruct((N, LANES), F32),
            jax.ShapeDtypeStruct((ROUTE_ROWS, N), F32),
            jax.ShapeDtypeStruct((1, LANES), F32),
        ],
        scratch_shapes=[pltpu.VMEM((1, LANES), F32)],
        compiler_params=pltpu.CompilerParams(
            dimension_semantics=("arbitrary",), vmem_limit_bytes=VMEM_LIMIT),
        name="merge",
    )(x2d, ya, yb, yc, w["g_mix"], w["w_gate"], w["b_gate"], w["w_branch"], w["w_out"],
      w["g_ffn"], w["w_router"], w["b_router"])


def _dispatch_kernel(dest_ref, h2_ref, xs_ref, sem, *, T):
    def row_copy(t, d):
        return pltpu.make_async_copy(h2_ref.at[_row_tile(t)], xs_ref.at[_row_tile(d)], sem)

    def body(tb, carry):
        t0 = tb * ISSUE_UNROLL
        for u in range(ISSUE_UNROLL):
            for kk in range(TOP_K):
                row_copy(t0 + u, dest_ref[0, kk * T + t0 + u]).start(priority=kk % N_DMA_PRIORITIES)
        return carry

    lax.fori_loop(0, T // ISSUE_UNROLL, body, 0)
    for kk in range(TOP_K):
        pltpu.make_async_copy(h2_ref, xs_ref.at[pl.ds(0, T * ROW_TILE)], sem).wait()


def _row_tile(r):
    return pl.ds(pl.multiple_of(r * ROW_TILE, ROW_TILE), ROW_TILE)


def _dispatch_call(dest3, h2, T):
    N = h2.shape[0] // ROW_TILE
    return pl.pallas_call(
        functools.partial(_dispatch_kernel, T=T),
        grid=(N // T,),
        in_specs=[
            pl.BlockSpec((None, 1, TOP_K * T), lambda i: (i, 0, 0), memory_space=pltpu.SMEM),
            pl.BlockSpec((T * ROW_TILE, LANES), lambda i: (i, 0)),
        ],
        out_specs=pl.BlockSpec(memory_space=pl.ANY),
        out_shape=jax.ShapeDtypeStruct((TOP_K * N * ROW_TILE, LANES), U32),
        scratch_shapes=[pltpu.SemaphoreType.DMA],
        compiler_params=pltpu.CompilerParams(
            dimension_semantics=("arbitrary",), vmem_limit_bytes=VMEM_LIMIT),
        name="dispatch",
    )(dest3, h2)


def _gmm_kernel(blk_ref, exp_ref, lo_ref, hi_ref, xs_ref, w1_ref, w3_ref, w2_ref, y_ref,
                w13_bf, w2_bf, *, bm):
    i = pl.program_id(0)
    lo = lo_ref[i]
    hi = hi_ref[i]
    prev = jnp.maximum(i - 1, 0)
    first = jnp.logical_or(i == 0, blk_ref[i] != blk_ref[prev])
    new_expert = jnp.logical_or(i == 0, exp_ref[i] != exp_ref[prev])

    @pl.when(new_expert)
    def _():
        w13_bf[:, 0:D_EXPERT] = w1_ref[...].astype(BF16)
        w13_bf[:, D_EXPERT:] = w3_ref[...].astype(BF16)
        w2_bf[...] = w2_ref[...].astype(BF16)

    @pl.when(hi > lo)
    def _():
        n_part = 2
        part = bm // n_part
        outs = []
        for h in range(n_part):
            x = _unpack_pairs(_load_row_tiles(xs_ref, part, h * part)).astype(BF16)
            ab = jnp.dot(x, w13_bf[...], preferred_element_type=F32)
            a = ab[:, 0:D_EXPERT]
            b = ab[:, D_EXPERT:]
            mid = (a * jax.nn.sigmoid(a) * b).astype(BF16)
            outs.append(_pack_pairs(jnp.dot(mid, w2_bf[...], preferred_element_type=F32)))

        @pl.when(first)
        def _():
            for h in range(n_part):
                _store_row_tiles(y_ref, outs[h], h * part)

        @pl.when(jnp.logical_not(first))
        def _():
            for h in range(n_part):
                row = h * part + lax.broadcasted_iota(I32, (part, HALF_D), 0)
                mine = (row >= lo) & (row < hi)
                old = _load_row_tiles(y_ref, part, h * part)
                _store_row_tiles(y_ref, jnp.where(mine, outs[h], old), h * part)


def _gmm_call(plan, xs, w1, w3, w2, layer, bm):
    A = xs.shape[0] // ROW_TILE
    D = D_MODEL
    n_items = plan[0].shape[0]
    grid_spec = pltpu.PrefetchScalarGridSpec(
        num_scalar_prefetch=4,
        grid=(n_items,),
        in_specs=[
            pl.BlockSpec((bm * ROW_TILE, LANES), lambda i, blk, ex, lo, hi: (blk[i], 0)),
            pl.BlockSpec((None, None, D, D_EXPERT), lambda i, blk, ex, lo, hi: (layer, ex[i], 0, 0)),
            pl.BlockSpec((None, None, D, D_EXPERT), lambda i, blk, ex, lo, hi: (layer, ex[i], 0, 0)),
            pl.BlockSpec((None, None, D_EXPERT, D), lambda i, blk, ex, lo, hi: (layer, ex[i], 0, 0)),
        ],
        out_specs=pl.BlockSpec((bm * ROW_TILE, LANES), lambda i, blk, ex, lo, hi: (blk[i], 0)),
        scratch_shapes=[pltpu.VMEM((D, 2 * D_EXPERT), BF16), pltpu.VMEM((D_EXPERT, D), BF16)],
    )
    return pl.pallas_call(
        functools.partial(_gmm_kernel, bm=bm),
        grid_spec=grid_spec,
        out_shape=jax.ShapeDtypeStruct((A * ROW_TILE, LANES), U32),
        compiler_params=pltpu.CompilerParams(
            dimension_semantics=("arbitrary",), vmem_limit_bytes=VMEM_LIMIT),
        name="gmm",
    )(*plan, xs, w1, w3, w2)


def _gmm_plan(counts, n_rows, bm):
    n_blk = n_rows // bm
    n_items = n_blk + N_EXPERTS - 1
    ends = jnp.cumsum(counts)
    starts = ends - counts
    first_blk = starts // bm
    n_it = jnp.where(counts > 0, (ends - 1) // bm - first_blk + 1, 0)
    item_end = jnp.cumsum(n_it)
    item_start = item_end - n_it
    total = item_end[-1]
    ids = jnp.arange(n_items, dtype=I32)
    valid = ids < total
    ex = jnp.minimum(jnp.sum(item_end[None, :] <= ids[:, None], axis=1).astype(I32), N_EXPERTS - 1)
    ex = jnp.where(valid, ex, ex[jnp.maximum(total - 1, 0)])
    blk = jnp.where(valid, first_blk[ex] + ids - item_start[ex], n_blk - 1)
    lo = jnp.where(valid, jnp.maximum(starts[ex], blk * bm) - blk * bm, 0)
    hi = jnp.where(valid, jnp.minimum(ends[ex], (blk + 1) * bm) - blk * bm, 0)
    return blk.astype(I32), ex.astype(I32), lo.astype(I32), hi.astype(I32)


def _combine_kernel(dest_ref, dest_next_ref, x1_ref, route_ref, y_hbm, o_ref, buf, sem, *, T):
    i = pl.program_id(0)
    n_steps = pl.num_programs(0)

    def issue(d_ref, slot):
        def body(tb, carry):
            t0 = tb * ISSUE_UNROLL
            for u in range(ISSUE_UNROLL):
                for kk in range(TOP_K):
                    d = d_ref[0, kk * T + t0 + u]
                    pltpu.make_async_copy(y_hbm.at[_row_tile(d)], buf.at[slot * TOP_K + kk, _row_tile(t0 + u)],
                                          sem.at[slot]).start(priority=kk % N_DMA_PRIORITIES)
            return carry

        lax.fori_loop(0, T // ISSUE_UNROLL, body, 0)

    def finish(slot):
        for kk in range(TOP_K):
            pltpu.make_async_copy(y_hbm.at[pl.ds(0, T * ROW_TILE)], buf.at[slot * TOP_K + kk],
                                  sem.at[slot]).wait()
        g1 = route_ref[:, 2:3]
        g2 = route_ref[:, 3:4]
        for s in range(ROW_TILE):
            rows = _row_tile_rows(s, T)
            y1 = _unpack_pairs(buf[slot * TOP_K, rows, :])
            y2 = _unpack_pairs(buf[slot * TOP_K + 1, rows, :])
            for half in range(2):
                src = slice(half * LANES, (half + 1) * LANES)
                dst = slice(half * HALF_D + s * LANES, half * HALF_D + (s + 1) * LANES)
                o_ref[:, dst] = x1_ref[:, dst] + (g1 * y1[:, src] + g2 * y2[:, src])

    @pl.when(i == 0)
    def _():
        issue(dest_ref, 0)

    for slot in range(2):
        @pl.when(i % 2 == slot)
        def _(slot=slot):
            @pl.when(i + 1 < n_steps)
            def _():
                issue(dest_next_ref, 1 - slot)

            finish(slot)


def _combine_call(dest3, x1, route, y, T):
    N, D = x1.shape
    n_steps = N // T
    dest_spec = lambda index: pl.BlockSpec((None, 1, TOP_K * T), index, memory_space=pltpu.SMEM)
    return pl.pallas_call(
        functools.partial(_combine_kernel, T=T),
        grid=(n_steps,),
        in_specs=[
            dest_spec(lambda i: (i, 0, 0)),
            dest_spec(lambda i: (jnp.minimum(i + 1, n_steps - 1), 0, 0)),
            pl.BlockSpec((T, D), lambda i: (i, 0)),
            pl.BlockSpec((T, LANES), lambda i: (i, 0)),
            pl.BlockSpec(memory_space=pl.ANY),
        ],
        out_specs=pl.BlockSpec((T, D), lambda i: (i, 0)),
        out_shape=jax.ShapeDtypeStruct((N, D), F32),
        scratch_shapes=[pltpu.VMEM((2 * TOP_K, T * ROW_TILE, LANES), U32), pltpu.SemaphoreType.DMA((2,))],
        compiler_params=pltpu.CompilerParams(
            dimension_semantics=("arbitrary",), vmem_limit_bytes=VMEM_LIMIT),
        name="combine",
    )(dest3, dest3, x1, route, y)


def _block_diag(blocks):
    n = len(blocks)
    rows = []
    for i, b in enumerate(blocks):
        rows.append(jnp.concatenate(
            [b if j == i else jnp.zeros((b.shape[0], blocks[j].shape[1]), b.dtype) for j in range(n)], axis=1))
    return jnp.concatenate(rows, axis=0)


def _rep_forget(cols):
    half = jnp.concatenate(
        [jnp.tile(cols, (1, F_REP)), jnp.zeros((cols.shape[0], 64 - F_REP * A_HEADS), cols.dtype)], axis=1)
    return jnp.concatenate([half, half], axis=1)


def _head_to_gate_lanes():
    head = jnp.arange(BRANCH_WIDTH, dtype=I32)[:, None] // A_HEAD_DIM
    lane = jnp.arange(LANES, dtype=I32)[None, :]
    hit = ((lane & 7) == head) & (((lane & 63) >> 3) < F_REP)
    return hit.astype(BF16)


def _prep_layer(l, g_mix, w_in, b_fgate, b_gate, g_q, g_k, w_pool, s_pool, g_sgu, w_sgu, b_sgu,
                w_branch, w_out, g_ffn, w_rg, b_rg, w_re, b_re):
    wi = w_in[l]
    w_pack = jnp.concatenate(
        [wi[:, 0:OFF_F], _rep_forget(wi[:, OFF_F:OFF_P]), wi[:, OFF_P:OFF_G]], axis=1).astype(BF16)
    pad_r = LANES - N_EXPERT_GROUPS - N_EXPERTS
    w_r = jnp.concatenate([w_rg[l], w_re[l], jnp.zeros((D_MODEL, pad_r), F32)], axis=1)
    w_r_hi = w_r.astype(BF16)
    w_r_lo = (w_r - w_r_hi.astype(F32)).astype(BF16)
    return dict(
        g_mix=g_mix[l][None, :],
        w_pack=w_pack,
        b_f=_rep_forget(b_fgate[l][None, :]),
        g_q=jnp.tile(g_q[l], A_HEADS)[None, :],
        g_k=jnp.tile(g_k[l], A_HEADS)[None, :],
        bd=_block_diag([jnp.ones((A_HEAD_DIM, A_HEAD_DIM), BF16)] * A_HEADS),
        sel=_head_to_gate_lanes(),
        w_pool=_block_diag([w_pool[l][g] for g in range(N_GROUPS)]).astype(BF16),
        s_pool=s_pool[l][None, :],
        g_sgu=g_sgu[l][None, :],
        w_sgu=w_sgu[l],
        b_sgu=jnp.broadcast_to(b_sgu[l][:, :, None], (N_GROUPS, C_CHUNK, GROUP_DIM)),
        w_gate=wi[:, OFF_G:].astype(BF16),
        b_gate=b_gate[l].reshape(1, N_BRANCH * D_MODEL),
        w_branch=w_branch[l].astype(BF16),
        w_out=w_out[l].astype(BF16),
        g_ffn=g_ffn[l][None, :],
        w_router=jnp.concatenate([w_r_hi, w_r_lo], axis=1),
        b_router=jnp.concatenate([b_rg[l], b_re[l], jnp.zeros((pad_r,), F32)])[None, :],
    )


def kernel(x, g_mix, w_in, b_fgate, b_gate, g_q, g_k, w_pool, s_pool, g_sgu, w_sgu, b_sgu, w_branch, w_out,
           g_ffn, w_rg, b_rg, w_re, b_re, w1, w3, w2):
    B, S, D = x.shape
    assert D == D_MODEL and x.dtype == F32
    N = B * S
    T, tq, tk, bm, tm = _tiles(S)
    assert S % T == 0 and S % tq == 0 and tq == tk and T % C_CHUNK == 0
    assert (TOP_K * N) % bm == 0 and N % tm == 0 and tm % ISSUE_UNROLL == 0
    depth = w_in.shape[0]
    for l in range(depth):
        w = _prep_layer(l, g_mix, w_in, b_fgate, b_gate, g_q, g_k, w_pool, s_pool, g_sgu, w_sgu, b_sgu,
                        w_branch, w_out, g_ffn, w_rg, b_rg, w_re, b_re)
        qT, k, vT, yb, yc = _proj_call(x, w, T)
        ya = _attention(qT, k, vT, g_q[l], g_k[l], tq, tk)
        x1, h2, route, route_t, cnt = _merge_call(
            x.reshape(N, D), ya.reshape(N, -1), yb.reshape(N, -1), yc.reshape(N, -1), w, T)
        counts = cnt[0, :N_EXPERTS].astype(I32)
        starts = jnp.cumsum(counts) - counts
        experts = route_t[0:TOP_K].astype(I32)
        expert_ids = jnp.arange(N_EXPERTS, dtype=I32)[:, None, None]
        start_of = jnp.sum(jnp.where(experts[None] == expert_ids, starts[:, None, None], 0), axis=0)
        dest = start_of + route_t[4:4 + TOP_K].astype(I32)
        dest3 = dest.reshape(TOP_K, N // tm, tm).transpose(1, 0, 2).reshape(N // tm, 1, TOP_K * tm)
        xs = _dispatch_call(dest3, h2, tm)
        y = _gmm_call(_gmm_plan(counts, TOP_K * N, bm), xs, w1, w3, w2, l, bm)
        x = _combine_call(dest3, x1, route, y, tm).reshape(B, S, D)
    return x
```

```python
import functools

import jax
import jax.numpy as jnp
from jax import lax
from jax.experimental import pallas as pl
from jax.experimental.pallas import tpu as pltpu

F32 = jnp.float32
BF16 = jnp.bfloat16
I32 = jnp.int32
U32 = jnp.uint32

D_MODEL = 1024
A_HEADS = 8
A_HEAD_DIM = 64
BRANCH_WIDTH = 512
N_BRANCH = 3
POOL_WINDOWS = (2, 4, 8, 16)
GROUP_DIM = 128
N_GROUPS = 4
C_CHUNK = 128
N_EXPERT_GROUPS = 4
EXPERTS_PER_GROUP = 8
N_EXPERTS = N_EXPERT_GROUPS * EXPERTS_PER_GROUP
TOP_K = 2
D_EXPERT = 256
RMS_EPS = 1e-6
OFF_F = 3 * BRANCH_WIDTH
OFF_P = OFF_F + A_HEADS
OFF_U = OFF_P + BRANCH_WIDTH
OFF_SV = OFF_U + BRANCH_WIDTH
OFF_G = OFF_SV + BRANCH_WIDTH

LANES = 128
SUBLANES = 8
V7X_VMEM_BYTES = 64 * 1024 * 1024
VMEM_LIMIT = 56 * 1024 * 1024
ROW_TILE = D_MODEL // (2 * LANES)
HALF_D = D_MODEL // 2
ROUTE_ROWS = SUBLANES
ISSUE_UNROLL = 8
N_DMA_PRIORITIES = 2
ATTN_BLOCKS_PER_TRIP = 4

PK_Q, PK_K, PK_V = 0, 512, 1024
PK_F = 1536
PK_P = PK_F + LANES
PK_U = PK_P + BRANCH_WIDTH
PK_SV = PK_U + BRANCH_WIDTH
PK_W = PK_SV + BRANCH_WIDTH
F_REP = 6

LOG2E = 1.4426950408889634
SCORE_CAP = 96.0
NEG = -1e30
HALO = 16


def _tiles(seq_len):
    t_proj = min(512, seq_len)
    t_q = min(512, seq_len)
    t_k = min(512, seq_len)
    moe_block = 512
    t_move = min(1024, seq_len)
    return t_proj, t_q, t_k, moe_block, t_move


def _const_spec(*shape):
    zeros = (0,) * len(shape)
    return pl.BlockSpec(shape, lambda *_: zeros, pipeline_mode=pl.Buffered(1))


def _rms(x, g):
    return x * lax.rsqrt(jnp.mean(x * x, axis=-1, keepdims=True) + RMS_EPS) * g


def _gelu_tanh(x):
    cdf = 0.5 * (1.0 + jnp.tanh(0.7978845608028654 * (x + 0.044715 * (x * x * x))))
    return x * cdf


def _log_sigmoid(x):
    return jnp.minimum(x, 0.0) - jnp.log1p(jnp.exp(-jnp.abs(x)))


def _split3(c):
    hi = c.astype(BF16).astype(F32)
    r = c - hi
    lo = r.astype(BF16).astype(F32)
    lolo = (r - lo).astype(BF16).astype(F32)
    return hi, lo, lolo


def _proj_kernel(x_ref, gmix_ref, w_ref, bf_ref, gq_ref, gk_ref, bd_ref, sel_ref, wpool_ref, spool_ref,
                 gsgu_ref, wsgu_ref, bsgu_ref,
                 qT_ref, k_ref, vT_ref, yb_ref, yc_ref,
                 carry_ref, halo_ref, *, T):
    i = pl.program_id(1)

    @pl.when(i == 0)
    def _():
        carry_ref[...] = jnp.zeros_like(carry_ref)
        halo_ref[...] = jnp.zeros_like(halo_ref)

    hb = _rms(x_ref[...], gmix_ref[...]).astype(BF16)

    def proj(lo, width):
        return jnp.dot(hb, w_ref[:, lo:lo + width], preferred_element_type=F32)

    bd = bd_ref[...]

    def head_norm(z, g):
        sq = z * z
        hi = sq.astype(BF16)
        lo = (sq - hi.astype(F32)).astype(BF16)
        ss = (jnp.dot(hi, bd, preferred_element_type=F32)
              + jnp.dot(lo, bd, preferred_element_type=F32))
        return z * lax.rsqrt(ss * (1.0 / A_HEAD_DIM) + RMS_EPS) * g

    qn = head_norm(proj(PK_Q, BRANCH_WIDTH), gq_ref[...]) * (A_HEAD_DIM ** -0.5 * LOG2E)
    kn = head_norm(proj(PK_K, BRANCH_WIDTH), gk_ref[...])
    zv = proj(PK_V, BRANCH_WIDTH)
    qk = qn * kn
    qk_hi = qk.astype(BF16)
    qk_lo = (qk - qk_hi.astype(F32)).astype(BF16)
    diag = (jnp.dot(qk_hi, sel_ref[...], preferred_element_type=F32)
            + jnp.dot(qk_lo, sel_ref[...], preferred_element_type=F32))

    logf = _log_sigmoid(proj(PK_F, LANES) + bf_ref[...])
    r_i = lax.broadcasted_iota(I32, (T, T), 0)
    c_i = lax.broadcasted_iota(I32, (T, T), 1)
    tri = (r_i >= c_i).astype(F32)
    c = jnp.dot(tri, logf, preferred_element_type=F32,
                precision=lax.Precision.HIGHEST) + carry_ref[...]
    carry_ref[...] = c[T - 1:T, :]

    lane = lax.broadcasted_iota(I32, (T, LANES), 1)
    lm = lane & 63
    grp = lm >> 3
    hsel = lm & 7
    first3 = grp < 3
    second3 = (grp >= 3) & (grp < F_REP)

    def parts(v):
        hi, lo, lolo = _split3(v)
        return jnp.where((grp == 0) | (grp == 3), hi, jnp.where((grp == 1) | (grp == 4), lo, lolo))

    k_aug = jnp.where(first3, -parts(c * LOG2E), 0.0)
    q_aug = jnp.where(second3, parts(c * LOG2E - diag), 0.0)

    for g in range(A_HEADS // 2):
        sl = slice(g * LANES, (g + 1) * LANES)
        zq2, zk2, zv2 = qn[:, sl], kn[:, sl], zv[:, sl]
        for par in range(2):
            h = 2 * g + par
            keep = (lane < 64) if par == 0 else (lane >= 64)
            oh_k = (second3 & (hsel == h)).astype(F32)
            oh_q = (first3 & (hsel == h)).astype(F32)
            k_ref[h] = jnp.where(keep, zk2, k_aug + oh_k).astype(BF16)
            qT_ref[h] = jnp.where(keep, zq2, q_aug + oh_q).T.astype(BF16)
            ones_lane = 64 if par == 0 else 0
            vT_ref[h] = jnp.where(keep, zv2, (lane == ones_lane).astype(F32)).T.astype(BF16)

    p = proj(PK_P, BRANCH_WIDTH)
    row8 = lax.broadcasted_iota(I32, (8, GROUP_DIM), 0)
    pos = i * T + lax.broadcasted_iota(I32, (T, GROUP_DIM), 0)

    def shift_down(v, tail, d):
        r = pltpu.roll(v, d, 0)
        rt = pltpu.roll(tail, d, 0)
        top = jnp.where(row8 < d, rt[0:8], r[0:8])
        return jnp.concatenate([top, r[8:]], axis=0)

    pooled = []
    for gi, w in enumerate(POOL_WINDOWS):
        sl = slice(gi * GROUP_DIM, (gi + 1) * GROUP_DIM)
        s = p[:, sl]
        for lv in range(gi + 1):
            tail = halo_ref[lv, :, sl]
            halo_ref[lv, :, sl] = s[T - HALO:T, :]
            s = s + shift_down(s, tail, 1 << lv)
        cnt = jnp.minimum(pos + 1, w).astype(F32)
        pooled.append(s / cnt - p[:, sl])
    pooled = jnp.concatenate(pooled, axis=1).astype(BF16)
    yb = jnp.dot(pooled, wpool_ref[...], preferred_element_type=F32) * spool_ref[...]
    yb_ref[...] = yb.astype(yb_ref.dtype)

    gu = _gelu_tanh(proj(PK_U, BRANCH_WIDTH))
    gv = _gelu_tanh(proj(PK_SV, BRANCH_WIDTH))
    t_r = lax.broadcasted_iota(I32, (C_CHUNK, C_CHUNK), 0)
    t_c = lax.broadcasted_iota(I32, (C_CHUNK, C_CHUNK), 1)
    causal = t_r >= t_c
    for g in range(N_GROUPS):
        sl = slice(g * GROUP_DIM, (g + 1) * GROUP_DIM)
        vn = _rms(gv[:, sl], gsgu_ref[:, sl]).astype(BF16)
        wc = jnp.where(causal, wsgu_ref[g], 0.0).astype(BF16)
        for ch in range(T // C_CHUNK):
            rows = slice(ch * C_CHUNK, (ch + 1) * C_CHUNK)
            mixed = jnp.dot(wc, vn[rows], preferred_element_type=F32) + bsgu_ref[g]
            yc_ref[rows, sl] = (gu[rows, sl] * mixed).astype(yc_ref.dtype)


def _proj_call(x, w, T):
    B, S, D = x.shape
    H = A_HEADS
    const = _const_spec
    return pl.pallas_call(
        functools.partial(_proj_kernel, T=T),
        grid=(B, S // T),
        in_specs=[
            pl.BlockSpec((None, T, D), lambda b, i: (b, i, 0)),
            const(1, D), const(D, PK_W), const(1, LANES), const(1, BRANCH_WIDTH), const(1, BRANCH_WIDTH),
            const(BRANCH_WIDTH, BRANCH_WIDTH), const(BRANCH_WIDTH, LANES), const(BRANCH_WIDTH, BRANCH_WIDTH),
            const(1, BRANCH_WIDTH), const(1, BRANCH_WIDTH), const(N_GROUPS, C_CHUNK, C_CHUNK), const(N_GROUPS, C_CHUNK, GROUP_DIM),
        ],
        out_specs=[
            pl.BlockSpec((None, H, LANES, T), lambda b, i: (b, 0, 0, i)),
            pl.BlockSpec((None, H, T, LANES), lambda b, i: (b, 0, i, 0)),
            pl.BlockSpec((None, H, LANES, T), lambda b, i: (b, 0, 0, i)),
            pl.BlockSpec((None, T, BRANCH_WIDTH), lambda b, i: (b, i, 0)),
            pl.BlockSpec((None, T, BRANCH_WIDTH), lambda b, i: (b, i, 0)),
        ],
        out_shape=[
            jax.ShapeDtypeStruct((B, H, LANES, S), BF16),
            jax.ShapeDtypeStruct((B, H, S, LANES), BF16),
            jax.ShapeDtypeStruct((B, H, LANES, S), BF16),
            jax.ShapeDtypeStruct((B, S, BRANCH_WIDTH), BF16),
            jax.ShapeDtypeStruct((B, S, BRANCH_WIDTH), BF16),
        ],
        scratch_shapes=[pltpu.VMEM((1, LANES), F32), pltpu.VMEM((4, HALO, BRANCH_WIDTH), F32)],
        compiler_params=pltpu.CompilerParams(
            dimension_semantics=("arbitrary", "arbitrary"), vmem_limit_bytes=VMEM_LIMIT),
        name="proj",
    )(x, w["g_mix"], w["w_pack"], w["b_f"], w["g_q"], w["g_k"], w["bd"], w["sel"], w["w_pool"], w["s_pool"],
      w["g_sgu"], w["w_sgu"], w["b_sgu"])


def _attn_kernel(qT_ref, k_ref, vT_ref, o_ref, m_ref, acc_ref, s_ref, *, tq, tk, n_q, online_max):
    assert tq == tk
    FIRST = 2

    def reset():
        acc_ref[...] = jnp.zeros_like(acc_ref)
        if online_max:
            m_ref[...] = jnp.full(m_ref.shape, NEG, F32)

    def scores(qi, j, slot):
        k0 = pl.multiple_of(j * tk, tk)
        q0 = pl.multiple_of(qi * tq, tq)
        for par in range(2):
            s_ref[slot, par, :, 0:tq] = jnp.dot(k_ref[par, pl.ds(k0, tk), :], qT_ref[par, :, pl.ds(q0, tq)],
                                                preferred_element_type=F32)

    def consume(qi, j, slot, masked):
        k0 = pl.multiple_of(j * tk, tk)
        q0 = qi * tq
        for par in range(2):
            s = s_ref[slot, par, :, 0:tq]
            if masked:
                kpos = k0 + lax.broadcasted_iota(I32, (tk, tq), 0)
                qpos = q0 + lax.broadcasted_iota(I32, (tk, tq), 1)
                s = jnp.where(kpos <= qpos, s, NEG)
            v_blk = vT_ref[par, :, pl.ds(k0, tk)]
            if online_max:
                m_old = m_ref[par]
                m_new = jnp.maximum(m_old, jnp.max(s, axis=0, keepdims=True))
                p = jnp.exp2(s - m_new).astype(BF16)
                pv = jnp.dot(v_blk, p, preferred_element_type=F32)
                acc_ref[par] = acc_ref[par] * jnp.exp2(m_old - m_new) + pv
                m_ref[par] = m_new
            else:
                p = jnp.exp2(s).astype(BF16)
                acc_ref[par] += jnp.dot(v_blk, p, preferred_element_type=F32)

    def finish(qi):
        lane = lax.broadcasted_iota(I32, (tq, LANES), 1)
        outs = []
        for par in range(2):
            acc = acc_ref[par]
            l = acc[64:65, :] if par == 0 else acc[0:1, :]
            outs.append((acc * (1.0 / l)).T)
        q0 = pl.multiple_of(qi * tq, tq)
        o_ref[pl.ds(q0, tq), :] = jnp.where(lane < 64, outs[0], outs[1]).astype(o_ref.dtype)
        reset()

    reset()
    scores(0, 0, FIRST)
    consume(0, 0, FIRST, True)
    finish(0)
    if n_q > 1:
        scores(1, 0, FIRST)

    def query_block(qi, carry):
        n_mid = qi - 1
        nxt = jnp.minimum(qi + 1, n_q - 1)
        scores(qi, 1, 1)
        consume(qi, 0, FIRST, False)

        def trip(t, c):
            j0 = 1 + ATTN_BLOCKS_PER_TRIP * t
            for u in range(ATTN_BLOCKS_PER_TRIP):
                scores(qi, j0 + u + 1, u % 2)
                consume(qi, j0 + u, (u + 1) % 2, False)
            return c

        lax.fori_loop(0, n_mid // ATTN_BLOCKS_PER_TRIP, trip, 0)
        j0 = 1 + (n_mid // ATTN_BLOCKS_PER_TRIP) * ATTN_BLOCKS_PER_TRIP
        for rem in range(ATTN_BLOCKS_PER_TRIP):
            @pl.when(n_mid % ATTN_BLOCKS_PER_TRIP == rem)
            def _(rem=rem):
                for u in range(rem):
                    scores(qi, j0 + u + 1, u % 2)
                    consume(qi, j0 + u, (u + 1) % 2, False)
                scores(nxt, 0, FIRST)
                consume(qi, j0 + rem, (rem + 1) % 2, True)
                finish(qi)
        return carry

    lax.fori_loop(1, n_q, query_block, 0)


def _attn_call(qT, k, vT, tq, tk, online_max):
    B, H, _, S = qT.shape
    return pl.pallas_call(
        functools.partial(_attn_kernel, tq=tq, tk=tk, n_q=S // tq, online_max=online_max),
        grid=(B, H // 2),
        in_specs=[
            pl.BlockSpec((None, 2, LANES, S), lambda b, g: (b, g, 0, 0)),
            pl.BlockSpec((None, 2, S, LANES), lambda b, g: (b, g, 0, 0)),
            pl.BlockSpec((None, 2, LANES, S), lambda b, g: (b, g, 0, 0)),
        ],
        out_specs=pl.BlockSpec((None, S, LANES), lambda b, g: (b, 0, g)),
        out_shape=jax.ShapeDtypeStruct((B, S, BRANCH_WIDTH), BF16),
        scratch_shapes=[pltpu.VMEM((2, 1, tq), F32), pltpu.VMEM((2, LANES, tq), F32),
                        pltpu.VMEM((3, 2, tk, tq + LANES), F32)],
        compiler_params=pltpu.CompilerParams(
            dimension_semantics=("parallel", "parallel"), vmem_limit_bytes=VMEM_LIMIT),
        name="attn_online" if online_max else "attn",
    )(qT, k, vT)


def _attention(qT, k, vT, g_q, g_k, tq, tk):
    bound = 16.0 * LOG2E * jnp.max(jnp.abs(g_q)) * jnp.max(jnp.abs(g_k))
    return lax.cond(bound <= SCORE_CAP,
                    lambda: _attn_call(qT, k, vT, tq, tk, False),
                    lambda: _attn_call(qT, k, vT, tq, tk, True))


def _merge_kernel(x_ref, ya_ref, yb_ref, yc_ref, gmix_ref, wg_ref, bg_ref, wb_ref, wo_ref,
                  gffn_ref, wr_ref, br_ref,
                  x1_ref, h2_ref, route_ref, route_t_ref, cnt_ref, carry_ref, *, T):
    i = pl.program_id(0)

    @pl.when(i == 0)
    def _():
        carry_ref[...] = jnp.zeros_like(carry_ref)

    x = x_ref[...]
    hb = _rms(x, gmix_ref[...]).astype(BF16)
    merged = None
    for bi, y_ref in enumerate((ya_ref, yb_ref, yc_ref)):
        sl = slice(bi * D_MODEL, (bi + 1) * D_MODEL)
        gate = jax.nn.sigmoid(jnp.dot(hb, wg_ref[:, sl], preferred_element_type=F32) + bg_ref[:, sl])
        term = gate * jnp.dot(y_ref[...], wb_ref[bi], preferred_element_type=F32)
        merged = term if merged is None else merged + term
    x1 = x + jnp.dot(merged.astype(BF16), wo_ref[...], preferred_element_type=F32)
    x1_ref[...] = x1
    h2 = _rms(x1, gffn_ref[...])
    _store_row_tiles(h2_ref, _pack_pairs(h2))

    h_hi = h2.astype(BF16)
    h_lo = (h2 - h_hi.astype(F32)).astype(BF16)
    hw = jnp.dot(h_hi, wr_ref[...], preferred_element_type=F32)
    lw = jnp.dot(h_lo, wr_ref[:, 0:LANES], preferred_element_type=F32)
    logits = hw[:, 0:LANES] + (hw[:, LANES:] + lw) + br_ref[...]
    lane = lax.broadcasted_iota(I32, (T, LANES), 1).astype(F32)
    big = float(LANES)

    def first_argmax(v):
        m = jnp.max(v, axis=-1, keepdims=True)
        return m, jnp.min(jnp.where(v == m, lane, big), axis=-1, keepdims=True)

    lg = jnp.where(lane < N_EXPERT_GROUPS, logits, NEG)
    mg, grp = first_argmax(lg)
    p_grp = 1.0 / jnp.sum(jnp.exp(lg - mg), axis=-1, keepdims=True)
    lo_lane = N_EXPERT_GROUPS + grp * EXPERTS_PER_GROUP
    le = jnp.where((lane >= lo_lane) & (lane < lo_lane + EXPERTS_PER_GROUP), logits, NEG)
    m1, i1 = first_argmax(le)
    m2, i2 = first_argmax(jnp.where(lane == i1, NEG, le))
    e21 = jnp.exp(m2 - m1)
    g1 = p_grp / (1.0 + e21)
    g2 = p_grp * e21 / (1.0 + e21)
    e1 = i1 - N_EXPERT_GROUPS
    e2 = i2 - N_EXPERT_GROUPS

    oh1 = lane == e1
    oh2 = lane == e2
    sel = (oh1 | oh2).astype(F32)
    r_i = lax.broadcasted_iota(I32, (T, T), 0)
    c_i = lax.broadcasted_iota(I32, (T, T), 1)
    before = (r_i > c_i).astype(BF16)
    seen = jnp.dot(before, sel.astype(BF16), preferred_element_type=F32) + carry_ref[...]
    r1 = jnp.sum(jnp.where(oh1, seen, 0.0), axis=-1, keepdims=True)
    r2 = jnp.sum(jnp.where(oh2, seen, 0.0), axis=-1, keepdims=True)
    carry_ref[...] = carry_ref[...] + jnp.sum(sel, axis=0, keepdims=True)
    cnt_ref[...] = carry_ref[...]

    route = jnp.zeros((T, LANES), F32)
    for idx, val in enumerate((e1, e2, g1, g2, r1, r2)):
        route = jnp.where(lane == idx, val, route)
    route_ref[...] = route
    route_t_ref[...] = route.T[0:ROUTE_ROWS]


def _row_tile_rows(s, n_rows, row0=0):
    return pl.ds(row0 * ROW_TILE + s, n_rows, stride=ROW_TILE)


def _pack_pairs(v):
    hi = lax.bitcast_convert_type(v[:, :HALF_D].astype(BF16).astype(F32), U32)
    lo = lax.bitcast_convert_type(v[:, HALF_D:].astype(BF16).astype(F32), U32)
    return hi | (lo >> 16)


def _unpack_pairs(w):
    hi = lax.bitcast_convert_type(w & jnp.uint32(0xFFFF0000), F32)
    lo = lax.bitcast_convert_type(w << 16, F32)
    return jnp.concatenate([hi, lo], axis=-1)


def _store_row_tiles(ref, words, row0=0):
    for s in range(ROW_TILE):
        ref[_row_tile_rows(s, words.shape[0], row0), :] = words[:, s * LANES:(s + 1) * LANES]


def _load_row_tiles(ref, n_rows, row0=0):
    return jnp.concatenate([ref[_row_tile_rows(s, n_rows, row0), :] for s in range(ROW_TILE)], axis=-1)


def _merge_call(x2d, ya, yb, yc, w, T):
    N, D = x2d.shape
    const = _const_spec
    tile = lambda width: pl.BlockSpec((T, width), lambda i: (i, 0))
    return pl.pallas_call(
        functools.partial(_merge_kernel, T=T),
        grid=(N // T,),
        in_specs=[
            tile(D), tile(BRANCH_WIDTH), tile(BRANCH_WIDTH), tile(BRANCH_WIDTH),
            const(1, D), const(D, N_BRANCH * D), const(1, N_BRANCH * D),
            const(N_BRANCH, BRANCH_WIDTH, D), const(D, D), const(1, D), const(D, 2 * LANES), const(1, LANES),
        ],
        out_specs=[tile(D), pl.BlockSpec((T * ROW_TILE, LANES), lambda i: (i, 0)), tile(LANES),
                   pl.BlockSpec((ROUTE_ROWS, T), lambda i: (0, i)),
                   pl.BlockSpec((1, LANES), lambda i: (0, 0))],
        out_shape=[
            jax.ShapeDtypeStruct((N, D), F32),
            jax.ShapeDtypeStruct((N * ROW_TILE, LANES), U32),
            jax.ShapeDtypeStruct((N, LANES), F32),
            jax.ShapeDtypeStruct((ROUTE_ROWS, N), F32),
            jax.ShapeDtypeStruct((1, LANES), F32),
        ],
        scratch_shapes=[pltpu.VMEM((1, LANES), F32)],
        compiler_params=pltpu.CompilerParams(
            dimension_semantics=("arbitrary",), vmem_limit_bytes=VMEM_LIMIT),
        name="merge",
    )(x2d, ya, yb, yc, w["g_mix"], w["w_gate"], w["b_gate"], w["w_branch"], w["w_out"],
      w["g_ffn"], w["w_router"], w["b_router"])


def _dispatch_kernel(dest_ref, h2_ref, xs_ref, sem, *, T):
    def row_copy(t, d):
        return pltpu.make_async_copy(h2_ref.at[_row_tile(t)], xs_ref.at[_row_tile(d)], sem)

    def body(tb, carry):
        t0 = tb * ISSUE_UNROLL
        for u in range(ISSUE_UNROLL):
            for kk in range(TOP_K):
                row_copy(t0 + u, dest_ref[0, kk * T + t0 + u]).start(priority=kk % N_DMA_PRIORITIES)
        return carry

    lax.fori_loop(0, T // ISSUE_UNROLL, body, 0)
    for kk in range(TOP_K):
        pltpu.make_async_copy(h2_ref, xs_ref.at[pl.ds(0, T * ROW_TILE)], sem).wait()


def _row_tile(r):
    return pl.ds(pl.multiple_of(r * ROW_TILE, ROW_TILE), ROW_TILE)


def _dispatch_call(dest3, h2, T):
    N = h2.shape[0] // ROW_TILE
    return pl.pallas_call(
        functools.partial(_dispatch_kernel, T=T),
        grid=(N // T,),
        in_specs=[
            pl.BlockSpec((None, 1, TOP_K * T), lambda i: (i, 0, 0), memory_space=pltpu.SMEM),
            pl.BlockSpec((T * ROW_TILE, LANES), lambda i: (i, 0)),
        ],
        out_specs=pl.BlockSpec(memory_space=pl.ANY),
        out_shape=jax.ShapeDtypeStruct((TOP_K * N * ROW_TILE, LANES), U32),
        scratch_shapes=[pltpu.SemaphoreType.DMA],
        compiler_params=pltpu.CompilerParams(
            dimension_semantics=("arbitrary",), vmem_limit_bytes=VMEM_LIMIT),
        name="dispatch",
    )(dest3, h2)


def _gmm_kernel(blk_ref, exp_ref, lo_ref, hi_ref, xs_ref, w1_ref, w3_ref, w2_ref, y_ref,
                w13_bf, w2_bf, *, bm):
    i = pl.program_id(0)
    lo = lo_ref[i]
    hi = hi_ref[i]
    prev = jnp.maximum(i - 1, 0)
    first = jnp.logical_or(i == 0, blk_ref[i] != blk_ref[prev])
    new_expert = jnp.logical_or(i == 0, exp_ref[i] != exp_ref[prev])

    @pl.when(new_expert)
    def _():
        w13_bf[:, 0:D_EXPERT] = w1_ref[...].astype(BF16)
        w13_bf[:, D_EXPERT:] = w3_ref[...].astype(BF16)
        w2_bf[...] = w2_ref[...].astype(BF16)

    @pl.when(hi > lo)
    def _():
        n_part = 2
        part = bm // n_part
        outs = []
        for h in range(n_part):
            x = _unpack_pairs(_load_row_tiles(xs_ref, part, h * part)).astype(BF16)
            ab = jnp.dot(x, w13_bf[...], preferred_element_type=F32)
            a = ab[:, 0:D_EXPERT]
            b = ab[:, D_EXPERT:]
            mid = (a * jax.nn.sigmoid(a) * b).astype(BF16)
            outs.append(_pack_pairs(jnp.dot(mid, w2_bf[...], preferred_element_type=F32)))

        @pl.when(first)
        def _():
            for h in range(n_part):
                _store_row_tiles(y_ref, outs[h], h * part)

        @pl.when(jnp.logical_not(first))
        def _():
            for h in range(n_part):
                row = h * part + lax.broadcasted_iota(I32, (part, HALF_D), 0)
                mine = (row >= lo) & (row < hi)
                old = _load_row_tiles(y_ref, part, h * part)
                _store_row_tiles(y_ref, jnp.where(mine, outs[h], old), h * part)


def _gmm_call(plan, xs, w1, w3, w2, layer, bm):
    A = xs.shape[0] // ROW_TILE
    D = D_MODEL
    n_items = plan[0].shape[0]
    grid_spec = pltpu.PrefetchScalarGridSpec(
        num_scalar_prefetch=4,
        grid=(n_items,),
        in_specs=[
            pl.BlockSpec((bm * ROW_TILE, LANES), lambda i, blk, ex, lo, hi: (blk[i], 0)),
            pl.BlockSpec((None, None, D, D_EXPERT), lambda i, blk, ex, lo, hi: (layer, ex[i], 0, 0)),
            pl.BlockSpec((None, None, D, D_EXPERT), lambda i, blk, ex, lo, hi: (layer, ex[i], 0, 0)),
            pl.BlockSpec((None, None, D_EXPERT, D), lambda i, blk, ex, lo, hi: (layer, ex[i], 0, 0)),
        ],
        out_specs=pl.BlockSpec((bm * ROW_TILE, LANES), lambda i, blk, ex, lo, hi: (blk[i], 0)),
        scratch_shapes=[pltpu.VMEM((D, 2 * D_EXPERT), BF16), pltpu.VMEM((D_EXPERT, D), BF16)],
    )
    return pl.pallas_call(
        functools.partial(_gmm_kernel, bm=bm),
        grid_spec=grid_spec,
        out_shape=jax.ShapeDtypeStruct((A * ROW_TILE, LANES), U32),
        compiler_params=pltpu.CompilerParams(
            dimension_semantics=("arbitrary",), vmem_limit_bytes=VMEM_LIMIT),
        name="gmm",
    )(*plan, xs, w1, w3, w2)


def _gmm_plan(counts, n_rows, bm):
    n_blk = n_rows // bm
    n_items = n_blk + N_EXPERTS - 1
    ends = jnp.cumsum(counts)
    starts = ends - counts
    first_blk = starts // bm
    n_it = jnp.where(counts > 0, (ends - 1) // bm - first_blk + 1, 0)
    item_end = jnp.cumsum(n_it)
    item_start = item_end - n_it
    total = item_end[-1]
    ids = jnp.arange(n_items, dtype=I32)
    valid = ids < total
    ex = jnp.minimum(jnp.sum(item_end[None, :] <= ids[:, None], axis=1).astype(I32), N_EXPERTS - 1)
    ex = jnp.where(valid, ex, ex[jnp.maximum(total - 1, 0)])
    blk = jnp.where(valid, first_blk[ex] + ids - item_start[ex], n_blk - 1)
    lo = jnp.where(valid, jnp.maximum(starts[ex], blk * bm) - blk * bm, 0)
    hi = jnp.where(valid, jnp.minimum(ends[ex], (blk + 1) * bm) - blk * bm, 0)
    return blk.astype(I32), ex.astype(I32), lo.astype(I32), hi.astype(I32)


def _combine_kernel(dest_ref, dest_next_ref, x1_ref, route_ref, y_hbm, o_ref, buf, sem, *, T):
    i = pl.program_id(0)
    n_steps = pl.num_programs(0)

    def issue(d_ref, slot):
        def body(tb, carry):
            t0 = tb * ISSUE_UNROLL
            for u in range(ISSUE_UNROLL):
                for kk in range(TOP_K):
                    d = d_ref[0, kk * T + t0 + u]
                    pltpu.make_async_copy(y_hbm.at[_row_tile(d)], buf.at[slot * TOP_K + kk, _row_tile(t0 + u)],
                                          sem.at[slot]).start(priority=kk % N_DMA_PRIORITIES)
            return carry

        lax.fori_loop(0, T // ISSUE_UNROLL, body, 0)

    def finish(slot):
        for kk in range(TOP_K):
            pltpu.make_async_copy(y_hbm.at[pl.ds(0, T * ROW_TILE)], buf.at[slot * TOP_K + kk],
                                  sem.at[slot]).wait()
        g1 = route_ref[:, 2:3]
        g2 = route_ref[:, 3:4]
        for s in range(ROW_TILE):
            rows = _row_tile_rows(s, T)
            y1 = _unpack_pairs(buf[slot * TOP_K, rows, :])
            y2 = _unpack_pairs(buf[slot * TOP_K + 1, rows, :])
            for half in range(2):
                src = slice(half * LANES, (half + 1) * LANES)
                dst = slice(half * HALF_D + s * LANES, half * HALF_D + (s + 1) * LANES)
                o_ref[:, dst] = x1_ref[:, dst] + (g1 * y1[:, src] + g2 * y2[:, src])

    @pl.when(i == 0)
    def _():
        issue(dest_ref, 0)

    for slot in range(2):
        @pl.when(i % 2 == slot)
        def _(slot=slot):
            @pl.when(i + 1 < n_steps)
            def _():
                issue(dest_next_ref, 1 - slot)

            finish(slot)


def _combine_call(dest3, x1, route, y, T):
    N, D = x1.shape
    n_steps = N // T
    dest_spec = lambda index: pl.BlockSpec((None, 1, TOP_K * T), index, memory_space=pltpu.SMEM)
    return pl.pallas_call(
        functools.partial(_combine_kernel, T=T),
        grid=(n_steps,),
        in_specs=[
            dest_spec(lambda i: (i, 0, 0)),
            dest_spec(lambda i: (jnp.minimum(i + 1, n_steps - 1), 0, 0)),
            pl.BlockSpec((T, D), lambda i: (i, 0)),
            pl.BlockSpec((T, LANES), lambda i: (i, 0)),
            pl.BlockSpec(memory_space=pl.ANY),
        ],
        out_specs=pl.BlockSpec((T, D), lambda i: (i, 0)),
        out_shape=jax.ShapeDtypeStruct((N, D), F32),
        scratch_shapes=[pltpu.VMEM((2 * TOP_K, T * ROW_TILE, LANES), U32), pltpu.SemaphoreType.DMA((2,))],
        compiler_params=pltpu.CompilerParams(
            dimension_semantics=("arbitrary",), vmem_limit_bytes=VMEM_LIMIT),
        name="combine",
    )(dest3, dest3, x1, route, y)


def _block_diag(blocks):
    n = len(blocks)
    rows = []
    for i, b in enumerate(blocks):
        rows.append(jnp.concatenate(
            [b if j == i else jnp.zeros((b.shape[0], blocks[j].shape[1]), b.dtype) for j in range(n)], axis=1))
    return jnp.concatenate(rows, axis=0)


def _rep_forget(cols):
    half = jnp.concatenate(
        [jnp.tile(cols, (1, F_REP)), jnp.zeros((cols.shape[0], 64 - F_REP * A_HEADS), cols.dtype)], axis=1)
    return jnp.concatenate([half, half], axis=1)


def _head_to_gate_lanes():
    head = jnp.arange(BRANCH_WIDTH, dtype=I32)[:, None] // A_HEAD_DIM
    lane = jnp.arange(LANES, dtype=I32)[None, :]
    hit = ((lane & 7) == head) & (((lane & 63) >> 3) < F_REP)
    return hit.astype(BF16)


def _prep_layer(l, g_mix, w_in, b_fgate, b_gate, g_q, g_k, w_pool, s_pool, g_sgu, w_sgu, b_sgu,
                w_branch, w_out, g_ffn, w_rg, b_rg, w_re, b_re):
    wi = w_in[l]
    w_pack = jnp.concatenate(
        [wi[:, 0:OFF_F], _rep_forget(wi[:, OFF_F:OFF_P]), wi[:, OFF_P:OFF_G]], axis=1).astype(BF16)
    pad_r = LANES - N_EXPERT_GROUPS - N_EXPERTS
    w_r = jnp.concatenate([w_rg[l], w_re[l], jnp.zeros((D_MODEL, pad_r), F32)], axis=1)
    w_r_hi = w_r.astype(BF16)
    w_r_lo = (w_r - w_r_hi.astype(F32)).astype(BF16)
    return dict(
        g_mix=g_mix[l][None, :],
        w_pack=w_pack,
        b_f=_rep_forget(b_fgate[l][None, :]),
        g_q=jnp.tile(g_q[l], A_HEADS)[None, :],
        g_k=jnp.tile(g_k[l], A_HEADS)[None, :],
        bd=_block_diag([jnp.ones((A_HEAD_DIM, A_HEAD_DIM), BF16)] * A_HEADS),
        sel=_head_to_gate_lanes(),
        w_pool=_block_diag([w_pool[l][g] for g in range(N_GROUPS)]).astype(BF16),
        s_pool=s_pool[l][None, :],
        g_sgu=g_sgu[l][None, :],
        w_sgu=w_sgu[l],
        b_sgu=jnp.broadcast_to(b_sgu[l][:, :, None], (N_GROUPS, C_CHUNK, GROUP_DIM)),
        w_gate=wi[:, OFF_G:].astype(BF16),
        b_gate=b_gate[l].reshape(1, N_BRANCH * D_MODEL),
        w_branch=w_branch[l].astype(BF16),
        w_out=w_out[l].astype(BF16),
        g_ffn=g_ffn[l][None, :],
        w_router=jnp.concatenate([w_r_hi, w_r_lo], axis=1),
        b_router=jnp.concatenate([b_rg[l], b_re[l], jnp.zeros((pad_r,), F32)])[None, :],
    )


def kernel(x, g_mix, w_in, b_fgate, b_gate, g_q, g_k, w_pool, s_pool, g_sgu, w_sgu, b_sgu, w_branch, w_out,
           g_ffn, w_rg, b_rg, w_re, b_re, w1, w3, w2):
    B, S, D = x.shape
    assert D == D_MODEL and x.dtype == F32
    N = B * S
    T, tq, tk, bm, tm = _tiles(S)
    assert S % T == 0 and S % tq == 0 and tq == tk and T % C_CHUNK == 0
    assert (TOP_K * N) % bm == 0 and N % tm == 0 and tm % ISSUE_UNROLL == 0
    depth = w_in.shape[0]
    for l in range(depth):
        w = _prep_layer(l, g_mix, w_in, b_fgate, b_gate, g_q, g_k, w_pool, s_pool, g_sgu, w_sgu, b_sgu,
                        w_branch, w_out, g_ffn, w_rg, b_rg, w_re, b_re)
        qT, k, vT, yb, yc = _proj_call(x, w, T)
        ya = _attention(qT, k, vT, g_q[l], g_k[l], tq, tk)
        x1, h2, route, route_t, cnt = _merge_call(
            x.reshape(N, D), ya.reshape(N, -1), yb.reshape(N, -1), yc.reshape(N, -1), w, T)
        counts = cnt[0, :N_EXPERTS].astype(I32)
        starts = jnp.cumsum(counts) - counts
        experts = route_t[0:TOP_K].astype(I32)
        expert_ids = jnp.arange(N_EXPERTS, dtype=I32)[:, None, None]
        start_of = jnp.sum(jnp.where(experts[None] == expert_ids, starts[:, None, None], 0), axis=0)
        dest = start_of + route_t[4:4 + TOP_K].astype(I32)
        dest3 = dest.reshape(TOP_K, N // tm, tm).transpose(1, 0, 2).reshape(N // tm, 1, TOP_K * tm)
        xs = _dispatch_call(dest3, h2, tm)
        y = _gmm_call(_gmm_plan(counts, TOP_K * N, bm), xs, w1, w3, w2, l, bm)
        x = _combine_call(dest3, x1, route, y, tm).reshape(B, S, D)
    return x
```

```python
import functools

import jax
import jax.numpy as jnp
from jax import lax
from jax.experimental import pallas as pl
from jax.experimental.pallas import tpu as pltpu

F32 = jnp.float32
BF16 = jnp.bfloat16
I32 = jnp.int32
U32 = jnp.uint32

D_MODEL = 1024
A_HEADS = 8
A_HEAD_DIM = 64
BRANCH_WIDTH = 512
N_BRANCH = 3
POOL_WINDOWS = (2, 4, 8, 16)
GROUP_DIM = 128
N_GROUPS = 4
C_CHUNK = 128
N_EXPERT_GROUPS = 4
EXPERTS_PER_GROUP = 8
N_EXPERTS = N_EXPERT_GROUPS * EXPERTS_PER_GROUP
TOP_K = 2
D_EXPERT = 256
RMS_EPS = 1e-6
OFF_F = 3 * BRANCH_WIDTH
OFF_P = OFF_F + A_HEADS
OFF_U = OFF_P + BRANCH_WIDTH
OFF_SV = OFF_U + BRANCH_WIDTH
OFF_G = OFF_SV + BRANCH_WIDTH

LANES = 128
SUBLANES = 8
V7X_VMEM_BYTES = 64 * 1024 * 1024
VMEM_LIMIT = 56 * 1024 * 1024
ROW_TILE = D_MODEL // (2 * LANES)
HALF_D = D_MODEL // 2
ROUTE_ROWS = SUBLANES
ISSUE_UNROLL = 8
N_DMA_PRIORITIES = 2
ATTN_BLOCKS_PER_TRIP = 4

PK_Q, PK_K, PK_V = 0, 512, 1024
PK_F = 1536
PK_P = PK_F + LANES
PK_U = PK_P + BRANCH_WIDTH
PK_SV = PK_U + BRANCH_WIDTH
PK_W = PK_SV + BRANCH_WIDTH
F_REP = 6

LOG2E = 1.4426950408889634
SCORE_CAP = 96.0
DEAD_SCORE = -160.0
NEG = -1e30
HALO = 16


def _tiles(seq_len):
    t_proj = min(512, seq_len)
    t_q = min(512, seq_len)
    t_k = min(512, seq_len)
    moe_block = 512
    t_move = min(1024, seq_len)
    return t_proj, t_q, t_k, moe_block, t_move


def _const_spec(*shape):
    zeros = (0,) * len(shape)
    return pl.BlockSpec(shape, lambda *_: zeros, pipeline_mode=pl.Buffered(1))


def _rms(x, g):
    return x * lax.rsqrt(jnp.mean(x * x, axis=-1, keepdims=True) + RMS_EPS) * g


def _gelu_tanh(x):
    cdf = 0.5 * (1.0 + jnp.tanh(0.7978845608028654 * (x + 0.044715 * (x * x * x))))
    return x * cdf


def _log_sigmoid(x):
    return jnp.minimum(x, 0.0) - jnp.log1p(jnp.exp(-jnp.abs(x)))


def _split3(c):
    hi = c.astype(BF16).astype(F32)
    r = c - hi
    lo = r.astype(BF16).astype(F32)
    lolo = (r - lo).astype(BF16).astype(F32)
    return hi, lo, lolo


def _proj_kernel(x_ref, gmix_ref, w_ref, bf_ref, gq_ref, gk_ref, bd_ref, sel_ref, wpool_ref, spool_ref,
                 gsgu_ref, wsgu_ref, bsgu_ref,
                 qT_ref, k_ref, vT_ref, yb_ref, yc_ref, c_ref,
                 carry_ref, halo_ref, *, T):
    i = pl.program_id(1)

    @pl.when(i == 0)
    def _():
        carry_ref[...] = jnp.zeros_like(carry_ref)
        halo_ref[...] = jnp.zeros_like(halo_ref)

    hb = _rms(x_ref[...], gmix_ref[...]).astype(BF16)

    def proj(lo, width):
        return jnp.dot(hb, w_ref[:, lo:lo + width], preferred_element_type=F32)

    bd = bd_ref[...]

    def head_norm(z, g):
        sq = z * z
        hi = sq.astype(BF16)
        lo = (sq - hi.astype(F32)).astype(BF16)
        ss = (jnp.dot(hi, bd, preferred_element_type=F32)
              + jnp.dot(lo, bd, preferred_element_type=F32))
        return z * lax.rsqrt(ss * (1.0 / A_HEAD_DIM) + RMS_EPS) * g

    qn = head_norm(proj(PK_Q, BRANCH_WIDTH), gq_ref[...]) * (A_HEAD_DIM ** -0.5 * LOG2E)
    kn = head_norm(proj(PK_K, BRANCH_WIDTH), gk_ref[...])
    zv = proj(PK_V, BRANCH_WIDTH)
    qk = qn * kn
    qk_hi = qk.astype(BF16)
    qk_lo = (qk - qk_hi.astype(F32)).astype(BF16)
    diag = (jnp.dot(qk_hi, sel_ref[...], preferred_element_type=F32)
            + jnp.dot(qk_lo, sel_ref[...], preferred_element_type=F32))

    logf = _log_sigmoid(proj(PK_F, LANES) + bf_ref[...])
    r_i = lax.broadcasted_iota(I32, (T, T), 0)
    c_i = lax.broadcasted_iota(I32, (T, T), 1)
    tri = (r_i >= c_i).astype(F32)
    c = jnp.dot(tri, logf, preferred_element_type=F32,
                precision=lax.Precision.HIGHEST) + carry_ref[...]
    carry_ref[...] = c[T - 1:T, :]
    c_ref[...] = c

    lane = lax.broadcasted_iota(I32, (T, LANES), 1)
    lm = lane & 63
    grp = lm >> 3
    hsel = lm & 7
    first3 = grp < 3
    second3 = (grp >= 3) & (grp < F_REP)

    def parts(v):
        hi, lo, lolo = _split3(v)
        return jnp.where((grp == 0) | (grp == 3), hi, jnp.where((grp == 1) | (grp == 4), lo, lolo))

    k_aug = jnp.where(first3, -parts(c * LOG2E), 0.0)
    q_aug = jnp.where(second3, parts(c * LOG2E - diag), 0.0)

    for g in range(A_HEADS // 2):
        sl = slice(g * LANES, (g + 1) * LANES)
        zq2, zk2, zv2 = qn[:, sl], kn[:, sl], zv[:, sl]
        for par in range(2):
            h = 2 * g + par
            keep = (lane < 64) if par == 0 else (lane >= 64)
            oh_k = (second3 & (hsel == h)).astype(F32)
            oh_q = (first3 & (hsel == h)).astype(F32)
            k_ref[h] = jnp.where(keep, zk2, k_aug + oh_k).astype(BF16)
            qT_ref[h] = jnp.where(keep, zq2, q_aug + oh_q).T.astype(BF16)
            ones_lane = 64 if par == 0 else 0
            vT_ref[h] = jnp.where(keep, zv2, (lane == ones_lane).astype(F32)).T.astype(BF16)

    p = proj(PK_P, BRANCH_WIDTH)
    row8 = lax.broadcasted_iota(I32, (8, GROUP_DIM), 0)
    pos = i * T + lax.broadcasted_iota(I32, (T, GROUP_DIM), 0)

    def shift_down(v, tail, d):
        r = pltpu.roll(v, d, 0)
        rt = pltpu.roll(tail, d, 0)
        top = jnp.where(row8 < d, rt[0:8], r[0:8])
        return jnp.concatenate([top, r[8:]], axis=0)

    pooled = []
    for gi, w in enumerate(POOL_WINDOWS):
        sl = slice(gi * GROUP_DIM, (gi + 1) * GROUP_DIM)
        s = p[:, sl]
        for lv in range(gi + 1):
            tail = halo_ref[lv, :, sl]
            halo_ref[lv, :, sl] = s[T - HALO:T, :]
            s = s + shift_down(s, tail, 1 << lv)
        cnt = jnp.minimum(pos + 1, w).astype(F32)
        pooled.append(s / cnt - p[:, sl])
    pooled = jnp.concatenate(pooled, axis=1).astype(BF16)
    yb = jnp.dot(pooled, wpool_ref[...], preferred_element_type=F32) * spool_ref[...]
    yb_ref[...] = yb.astype(yb_ref.dtype)

    gu = _gelu_tanh(proj(PK_U, BRANCH_WIDTH))
    gv = _gelu_tanh(proj(PK_SV, BRANCH_WIDTH))
    t_r = lax.broadcasted_iota(I32, (C_CHUNK, C_CHUNK), 0)
    t_c = lax.broadcasted_iota(I32, (C_CHUNK, C_CHUNK), 1)
    causal = t_r >= t_c
    for g in range(N_GROUPS):
        sl = slice(g * GROUP_DIM, (g + 1) * GROUP_DIM)
        vn = _rms(gv[:, sl], gsgu_ref[:, sl]).astype(BF16)
        wc = jnp.where(causal, wsgu_ref[g], 0.0).astype(BF16)
        for ch in range(T // C_CHUNK):
            rows = slice(ch * C_CHUNK, (ch + 1) * C_CHUNK)
            mixed = jnp.dot(wc, vn[rows], preferred_element_type=F32) + bsgu_ref[g]
            yc_ref[rows, sl] = (gu[rows, sl] * mixed).astype(yc_ref.dtype)


def _proj_call(x, w, T):
    B, S, D = x.shape
    H = A_HEADS
    const = _const_spec
    return pl.pallas_call(
        functools.partial(_proj_kernel, T=T),
        grid=(B, S // T),
        in_specs=[
            pl.BlockSpec((None, T, D), lambda b, i: (b, i, 0)),
            const(1, D), const(D, PK_W), const(1, LANES), const(1, BRANCH_WIDTH), const(1, BRANCH_WIDTH),
            const(BRANCH_WIDTH, BRANCH_WIDTH), const(BRANCH_WIDTH, LANES), const(BRANCH_WIDTH, BRANCH_WIDTH),
            const(1, BRANCH_WIDTH), const(1, BRANCH_WIDTH), const(N_GROUPS, C_CHUNK, C_CHUNK), const(N_GROUPS, C_CHUNK, GROUP_DIM),
        ],
        out_specs=[
            pl.BlockSpec((None, H, LANES, T), lambda b, i: (b, 0, 0, i)),
            pl.BlockSpec((None, H, T, LANES), lambda b, i: (b, 0, i, 0)),
            pl.BlockSpec((None, H, LANES, T), lambda b, i: (b, 0, 0, i)),
            pl.BlockSpec((None, T, BRANCH_WIDTH), lambda b, i: (b, i, 0)),
            pl.BlockSpec((None, T, BRANCH_WIDTH), lambda b, i: (b, i, 0)),
            pl.BlockSpec((None, T, LANES), lambda b, i: (b, i, 0)),
        ],
        out_shape=[
            jax.ShapeDtypeStruct((B, H, LANES, S), BF16),
            jax.ShapeDtypeStruct((B, H, S, LANES), BF16),
            jax.ShapeDtypeStruct((B, H, LANES, S), BF16),
            jax.ShapeDtypeStruct((B, S, BRANCH_WIDTH), BF16),
            jax.ShapeDtypeStruct((B, S, BRANCH_WIDTH), BF16),
            jax.ShapeDtypeStruct((B, S, LANES), F32),
        ],
        scratch_shapes=[pltpu.VMEM((1, LANES), F32), pltpu.VMEM((4, HALO, BRANCH_WIDTH), F32)],
        compiler_params=pltpu.CompilerParams(
            dimension_semantics=("arbitrary", "arbitrary"), vmem_limit_bytes=VMEM_LIMIT),
        name="proj",
    )(x, w["g_mix"], w["w_pack"], w["b_f"], w["g_q"], w["g_k"], w["bd"], w["sel"], w["w_pool"], w["s_pool"],
      w["g_sgu"], w["w_sgu"], w["b_sgu"])


def _attn_kernel(first_ref, qT_ref, k_ref, vT_ref, o_ref, m_ref, acc_ref, s_ref, *, tq, tk, n_q, online_max):
    assert tq == tk
    FIRST = 2

    def reset():
        acc_ref[...] = jnp.zeros_like(acc_ref)
        if online_max:
            m_ref[...] = jnp.full(m_ref.shape, NEG, F32)

    def scores(qi, j, slot):
        k0 = pl.multiple_of(j * tk, tk)
        q0 = pl.multiple_of(qi * tq, tq)
        for par in range(2):
            s_ref[slot, par] = jnp.dot(k_ref[par, pl.ds(k0, tk), :], qT_ref[par, :, pl.ds(q0, tq)],
                                       preferred_element_type=F32)

    def consume(qi, j, slot, masked):
        k0 = pl.multiple_of(j * tk, tk)
        q0 = qi * tq
        for par in range(2):
            s = s_ref[slot, par]
            if masked:
                kpos = k0 + lax.broadcasted_iota(I32, (tk, tq), 0)
                qpos = q0 + lax.broadcasted_iota(I32, (tk, tq), 1)
                s = jnp.where(kpos <= qpos, s, NEG)
            v_blk = vT_ref[par, :, pl.ds(k0, tk)]
            if online_max:
                m_old = m_ref[par]
                m_new = jnp.maximum(m_old, jnp.max(s, axis=0, keepdims=True))
                p = jnp.exp2(s - m_new).astype(BF16)
                pv = jnp.dot(v_blk, p, preferred_element_type=F32)
                acc_ref[par] = acc_ref[par] * jnp.exp2(m_old - m_new) + pv
                m_ref[par] = m_new
            else:
                p = jnp.exp2(s).astype(BF16)
                acc_ref[par] += jnp.dot(v_blk, p, preferred_element_type=F32)

    def finish(qi):
        lane = lax.broadcasted_iota(I32, (tq, LANES), 1)
        outs = []
        for par in range(2):
            acc = acc_ref[par]
            l = acc[64:65, :] if par == 0 else acc[0:1, :]
            outs.append((acc * (1.0 / l)).T)
        q0 = pl.multiple_of(qi * tq, tq)
        o_ref[pl.ds(q0, tq), :] = jnp.where(lane < 64, outs[0], outs[1]).astype(o_ref.dtype)
        reset()

    b_id = pl.program_id(0)
    g_id = pl.program_id(1)

    def first_block(qi):
        return first_ref[(b_id * pl.num_programs(1) + g_id) * n_q + qi]

    reset()
    scores(0, 0, FIRST)
    consume(0, 0, FIRST, True)
    finish(0)
    if n_q > 1:
        scores(1, first_block(1), FIRST)

    def query_block(qi, carry):
        base = first_block(qi)
        n_mid = qi - 1 - base
        nxt = jnp.minimum(qi + 1, n_q - 1)
        scores(qi, base + 1, 1)
        consume(qi, base, FIRST, False)

        def trip(t, c):
            j0 = base + 1 + ATTN_BLOCKS_PER_TRIP * t
            for u in range(ATTN_BLOCKS_PER_TRIP):
                scores(qi, j0 + u + 1, u % 2)
                consume(qi, j0 + u, (u + 1) % 2, False)
            return c

        lax.fori_loop(0, n_mid // ATTN_BLOCKS_PER_TRIP, trip, 0)
        j0 = base + 1 + (n_mid // ATTN_BLOCKS_PER_TRIP) * ATTN_BLOCKS_PER_TRIP
        for rem in range(ATTN_BLOCKS_PER_TRIP):
            @pl.when(n_mid % ATTN_BLOCKS_PER_TRIP == rem)
            def _(rem=rem):
                for u in range(rem):
                    scores(qi, j0 + u + 1, u % 2)
                    consume(qi, j0 + u, (u + 1) % 2, False)
                scores(nxt, first_block(nxt), FIRST)
                consume(qi, j0 + rem, (rem + 1) % 2, True)
                finish(qi)
        return carry

    lax.fori_loop(1, n_q, query_block, 0)


def _attn_call(first, qT, k, vT, tq, tk, online_max):
    B, H, _, S = qT.shape
    grid_spec = pltpu.PrefetchScalarGridSpec(
        num_scalar_prefetch=1,
        grid=(B, H // 2),
        in_specs=[
            pl.BlockSpec((None, 2, LANES, S), lambda b, g, first: (b, g, 0, 0)),
            pl.BlockSpec((None, 2, S, LANES), lambda b, g, first: (b, g, 0, 0)),
            pl.BlockSpec((None, 2, LANES, S), lambda b, g, first: (b, g, 0, 0)),
        ],
        out_specs=pl.BlockSpec((None, S, LANES), lambda b, g, first: (b, 0, g)),
        scratch_shapes=[pltpu.VMEM((2, 1, tq), F32), pltpu.VMEM((2, LANES, tq), F32),
                        pltpu.VMEM((3, 2, tk, tq), F32)],
    )
    return pl.pallas_call(
        functools.partial(_attn_kernel, tq=tq, tk=tk, n_q=S // tq, online_max=online_max),
        grid_spec=grid_spec,
        out_shape=jax.ShapeDtypeStruct((B, S, BRANCH_WIDTH), BF16),
        compiler_params=pltpu.CompilerParams(
            dimension_semantics=("parallel", "parallel"), vmem_limit_bytes=VMEM_LIMIT),
        name="attn_online" if online_max else "attn",
    )(first, qT, k, vT)


def _first_live_block(c, bound, tq, tk):
    c_q = c[:, 0::tq, 0:A_HEADS]
    c_k = c[:, tk - 1::tk, 0:A_HEADS]
    top = bound + LOG2E * (c_q[:, :, None, :] - c_k[:, None, :, :])
    dead = top < DEAD_SCORE
    dead_pair = dead[..., 0::2] & dead[..., 1::2]
    n_q = c_q.shape[1]
    n_dead = jnp.sum(jnp.cumprod(dead_pair.astype(I32), axis=2), axis=2)
    limit = jnp.maximum(jnp.arange(n_q, dtype=I32) - 1, 0)[None, :, None]
    first = jnp.minimum(n_dead, limit)
    return first.transpose(0, 2, 1).reshape(-1).astype(I32)


def _attention(qT, k, vT, c, g_q, g_k, tq, tk):
    bound = 16.0 * LOG2E * jnp.max(jnp.abs(g_q)) * jnp.max(jnp.abs(g_k))
    first = _first_live_block(c, bound, tq, tk)
    return lax.cond(bound <= SCORE_CAP,
                    lambda: _attn_call(first, qT, k, vT, tq, tk, False),
                    lambda: _attn_call(first, qT, k, vT, tq, tk, True))


def _merge_kernel(x_ref, ya_ref, yb_ref, yc_ref, gmix_ref, wg_ref, bg_ref, wb_ref, wo_ref,
                  gffn_ref, wr_ref, br_ref,
                  x1_ref, h2_ref, route_ref, route_t_ref, cnt_ref, carry_ref, *, T):
    i = pl.program_id(0)

    @pl.when(i == 0)
    def _():
        carry_ref[...] = jnp.zeros_like(carry_ref)

    x = x_ref[...]
    hb = _rms(x, gmix_ref[...]).astype(BF16)
    merged = None
    for bi, y_ref in enumerate((ya_ref, yb_ref, yc_ref)):
        sl = slice(bi * D_MODEL, (bi + 1) * D_MODEL)
        gate = jax.nn.sigmoid(jnp.dot(hb, wg_ref[:, sl], preferred_element_type=F32) + bg_ref[:, sl])
        term = gate * jnp.dot(y_ref[...], wb_ref[bi], preferred_element_type=F32)
        merged = term if merged is None else merged + term
    x1 = x + jnp.dot(merged.astype(BF16), wo_ref[...], preferred_element_type=F32)
    x1_ref[...] = x1
    h2 = _rms(x1, gffn_ref[...])
    _store_row_tiles(h2_ref, _pack_pairs(h2))

    h_hi = h2.astype(BF16)
    h_lo = (h2 - h_hi.astype(F32)).astype(BF16)
    hw = jnp.dot(h_hi, wr_ref[...], preferred_element_type=F32)
    lw = jnp.dot(h_lo, wr_ref[:, 0:LANES], preferred_element_type=F32)
    logits = hw[:, 0:LANES] + (hw[:, LANES:] + lw) + br_ref[...]
    lane = lax.broadcasted_iota(I32, (T, LANES), 1).astype(F32)
    big = float(LANES)

    def first_argmax(v):
        m = jnp.max(v, axis=-1, keepdims=True)
        return m, jnp.min(jnp.where(v == m, lane, big), axis=-1, keepdims=True)

    lg = jnp.where(lane < N_EXPERT_GROUPS, logits, NEG)
    mg, grp = first_argmax(lg)
    p_grp = 1.0 / jnp.sum(jnp.exp(lg - mg), axis=-1, keepdims=True)
    lo_lane = N_EXPERT_GROUPS + grp * EXPERTS_PER_GROUP
    le = jnp.where((lane >= lo_lane) & (lane < lo_lane + EXPERTS_PER_GROUP), logits, NEG)
    m1, i1 = first_argmax(le)
    m2, i2 = first_argmax(jnp.where(lane == i1, NEG, le))
    e21 = jnp.exp(m2 - m1)
    g1 = p_grp / (1.0 + e21)
    g2 = p_grp * e21 / (1.0 + e21)
    e1 = i1 - N_EXPERT_GROUPS
    e2 = i2 - N_EXPERT_GROUPS

    oh1 = lane == e1
    oh2 = lane == e2
    sel = (oh1 | oh2).astype(F32)
    r_i = lax.broadcasted_iota(I32, (T, T), 0)
    c_i = lax.broadcasted_iota(I32, (T, T), 1)
    before = (r_i > c_i).astype(BF16)
    seen = jnp.dot(before, sel.astype(BF16), preferred_element_type=F32) + carry_ref[...]
    r1 = jnp.sum(jnp.where(oh1, seen, 0.0), axis=-1, keepdims=True)
    r2 = jnp.sum(jnp.where(oh2, seen, 0.0), axis=-1, keepdims=True)
    carry_ref[...] = carry_ref[...] + jnp.sum(sel, axis=0, keepdims=True)
    cnt_ref[...] = carry_ref[...]

    route = jnp.zeros((T, LANES), F32)
    for idx, val in enumerate((e1, e2, g1, g2, r1, r2)):
        route = jnp.where(lane == idx, val, route)
    route_ref[...] = route
    route_t_ref[...] = route.T[0:ROUTE_ROWS]


def _row_tile_rows(s, n_rows, row0=0):
    return pl.ds(row0 * ROW_TILE + s, n_rows, stride=ROW_TILE)


def _pack_pairs(v):
    hi = lax.bitcast_convert_type(v[:, :HALF_D].astype(BF16).astype(F32), U32)
    lo = lax.bitcast_convert_type(v[:, HALF_D:].astype(BF16).astype(F32), U32)
    return hi | (lo >> 16)


def _unpack_pairs(w):
    hi = lax.bitcast_convert_type(w & jnp.uint32(0xFFFF0000), F32)
    lo = lax.bitcast_convert_type(w << 16, F32)
    return jnp.concatenate([hi, lo], axis=-1)


def _store_row_tiles(ref, words, row0=0):
    for s in range(ROW_TILE):
        ref[_row_tile_rows(s, words.shape[0], row0), :] = words[:, s * LANES:(s + 1) * LANES]


def _load_row_tiles(ref, n_rows, row0=0):
    return jnp.concatenate([ref[_row_tile_rows(s, n_rows, row0), :] for s in range(ROW_TILE)], axis=-1)


def _merge_call(x2d, ya, yb, yc, w, T):
    N, D = x2d.shape
    const = _const_spec
    tile = lambda width: pl.BlockSpec((T, width), lambda i: (i, 0))
    return pl.pallas_call(
        functools.partial(_merge_kernel, T=T),
        grid=(N // T,),
        in_specs=[
            tile(D), tile(BRANCH_WIDTH), tile(BRANCH_WIDTH), tile(BRANCH_WIDTH),
            const(1, D), const(D, N_BRANCH * D), const(1, N_BRANCH * D),
            const(N_BRANCH, BRANCH_WIDTH, D), const(D, D), const(1, D), const(D, 2 * LANES), const(1, LANES),
        ],
        out_specs=[tile(D), pl.BlockSpec((T * ROW_TILE, LANES), lambda i: (i, 0)), tile(LANES),
                   pl.BlockSpec((ROUTE_ROWS, T), lambda i: (0, i)),
                   pl.BlockSpec((1, LANES), lambda i: (0, 0))],
        out_shape=[
            jax.ShapeDtypeStruct((N, D), F32),
            jax.ShapeDtypeStruct((N * ROW_TILE, LANES), U32),
            jax.ShapeDtypeStruct((N, LANES), F32),
            jax.ShapeDtypeStruct((ROUTE_ROWS, N), F32),
            jax.ShapeDtypeStruct((1, LANES), F32),
        ],
        scratch_shapes=[pltpu.VMEM((1, LANES), F32)],
        compiler_params=pltpu.CompilerParams(
            dimension_semantics=("arbitrary",), vmem_limit_bytes=VMEM_LIMIT),
        name="merge",
    )(x2d, ya, yb, yc, w["g_mix"], w["w_gate"], w["b_gate"], w["w_branch"], w["w_out"],
      w["g_ffn"], w["w_router"], w["b_router"])


def _dispatch_kernel(dest_ref, h2_ref, xs_ref, sem, *, T):
    def row_copy(t, d):
        return pltpu.make_async_copy(h2_ref.at[_row_tile(t)], xs_ref.at[_row_tile(d)], sem)

    def body(tb, carry):
        t0 = tb * ISSUE_UNROLL
        for u in range(ISSUE_UNROLL):
            for kk in range(TOP_K):
                row_copy(t0 + u, dest_ref[0, kk * T + t0 + u]).start(priority=kk % N_DMA_PRIORITIES)
        return carry

    lax.fori_loop(0, T // ISSUE_UNROLL, body, 0)
    for kk in range(TOP_K):
        pltpu.make_async_copy(h2_ref, xs_ref.at[pl.ds(0, T * ROW_TILE)], sem).wait()


def _row_tile(r):
    return pl.ds(pl.multiple_of(r * ROW_TILE, ROW_TILE), ROW_TILE)


def _dispatch_call(dest3, h2, T):
    N = h2.shape[0] // ROW_TILE
    return pl.pallas_call(
        functools.partial(_dispatch_kernel, T=T),
        grid=(N // T,),
        in_specs=[
            pl.BlockSpec((None, 1, TOP_K * T), lambda i: (i, 0, 0), memory_space=pltpu.SMEM),
            pl.BlockSpec((T * ROW_TILE, LANES), lambda i: (i, 0)),
        ],
        out_specs=pl.BlockSpec(memory_space=pl.ANY),
        out_shape=jax.ShapeDtypeStruct((TOP_K * N * ROW_TILE, LANES), U32),
        scratch_shapes=[pltpu.SemaphoreType.DMA],
        compiler_params=pltpu.CompilerParams(
            dimension_semantics=("arbitrary",), vmem_limit_bytes=VMEM_LIMIT),
        name="dispatch",
    )(dest3, h2)


def _gmm_kernel(blk_ref, exp_ref, lo_ref, hi_ref, xs_ref, w1_ref, w3_ref, w2_ref, y_ref,
                w13_bf, w2_bf, *, bm):
    i = pl.program_id(0)
    lo = lo_ref[i]
    hi = hi_ref[i]
    prev = jnp.maximum(i - 1, 0)
    first = jnp.logical_or(i == 0, blk_ref[i] != blk_ref[prev])
    new_expert = jnp.logical_or(i == 0, exp_ref[i] != exp_ref[prev])

    @pl.when(new_expert)
    def _():
        w13_bf[:, 0:D_EXPERT] = w1_ref[...].astype(BF16)
        w13_bf[:, D_EXPERT:] = w3_ref[...].astype(BF16)
        w2_bf[...] = w2_ref[...].astype(BF16)

    @pl.when(hi > lo)
    def _():
        n_part = 2
        part = bm // n_part
        outs = []
        for h in range(n_part):
            x = _unpack_pairs(_load_row_tiles(xs_ref, part, h * part)).astype(BF16)
            ab = jnp.dot(x, w13_bf[...], preferred_element_type=F32)
            a = ab[:, 0:D_EXPERT]
            b = ab[:, D_EXPERT:]
            mid = (a * jax.nn.sigmoid(a) * b).astype(BF16)
            outs.append(_pack_pairs(jnp.dot(mid, w2_bf[...], preferred_element_type=F32)))

        @pl.when(first)
        def _():
            for h in range(n_part):
                _store_row_tiles(y_ref, outs[h], h * part)

        @pl.when(jnp.logical_not(first))
        def _():
            for h in range(n_part):
                row = h * part + lax.broadcasted_iota(I32, (part, HALF_D), 0)
                mine = (row >= lo) & (row < hi)
                old = _load_row_tiles(y_ref, part, h * part)
                _store_row_tiles(y_ref, jnp.where(mine, outs[h], old), h * part)


def _gmm_call(plan, xs, w1, w3, w2, layer, bm):
    A = xs.shape[0] // ROW_TILE
    D = D_MODEL
    n_items = plan[0].shape[0]
    grid_spec = pltpu.PrefetchScalarGridSpec(
        num_scalar_prefetch=4,
        grid=(n_items,),
        in_specs=[
            pl.BlockSpec((bm * ROW_TILE, LANES), lambda i, blk, ex, lo, hi: (blk[i], 0)),
            pl.BlockSpec((None, None, D, D_EXPERT), lambda i, blk, ex, lo, hi: (layer, ex[i], 0, 0)),
            pl.BlockSpec((None, None, D, D_EXPERT), lambda i, blk, ex, lo, hi: (layer, ex[i], 0, 0)),
            pl.BlockSpec((None, None, D_EXPERT, D), lambda i, blk, ex, lo, hi: (layer, ex[i], 0, 0)),
        ],
        out_specs=pl.BlockSpec((bm * ROW_TILE, LANES), lambda i, blk, ex, lo, hi: (blk[i], 0)),
        scratch_shapes=[pltpu.VMEM((D, 2 * D_EXPERT), BF16), pltpu.VMEM((D_EXPERT, D), BF16)],
    )
    return pl.pallas_call(
        functools.partial(_gmm_kernel, bm=bm),
        grid_spec=grid_spec,
        out_shape=jax.ShapeDtypeStruct((A * ROW_TILE, LANES), U32),
        compiler_params=pltpu.CompilerParams(
            dimension_semantics=("arbitrary",), vmem_limit_bytes=VMEM_LIMIT),
        name="gmm",
    )(*plan, xs, w1, w3, w2)


def _gmm_plan(counts, n_rows, bm):
    n_blk = n_rows // bm
    n_items = n_blk + N_EXPERTS - 1
    ends = jnp.cumsum(counts)
    starts = ends - counts
    first_blk = starts // bm
    n_it = jnp.where(counts > 0, (ends - 1) // bm - first_blk + 1, 0)
    item_end = jnp.cumsum(n_it)
    item_start = item_end - n_it
    total = item_end[-1]
    ids = jnp.arange(n_items, dtype=I32)
    valid = ids < total
    ex = jnp.minimum(jnp.sum(item_end[None, :] <= ids[:, None], axis=1).astype(I32), N_EXPERTS - 1)
    ex = jnp.where(valid, ex, ex[jnp.maximum(total - 1, 0)])
    blk = jnp.where(valid, first_blk[ex] + ids - item_start[ex], n_blk - 1)
    lo = jnp.where(valid, jnp.maximum(starts[ex], blk * bm) - blk * bm, 0)
    hi = jnp.where(valid, jnp.minimum(ends[ex], (blk + 1) * bm) - blk * bm, 0)
    return blk.astype(I32), ex.astype(I32), lo.astype(I32), hi.astype(I32)


def _combine_kernel(dest_ref, dest_next_ref, x1_ref, route_ref, y_hbm, o_ref, buf, sem, *, T):
    i = pl.program_id(0)
    n_steps = pl.num_programs(0)

    def issue(d_ref, slot):
        def body(tb, carry):
            t0 = tb * ISSUE_UNROLL
            for u in range(ISSUE_UNROLL):
                for kk in range(TOP_K):
                    d = d_ref[0, kk * T + t0 + u]
                    pltpu.make_async_copy(y_hbm.at[_row_tile(d)], buf.at[slot * TOP_K + kk, _row_tile(t0 + u)],
                                          sem.at[slot]).start(priority=kk % N_DMA_PRIORITIES)
            return carry

        lax.fori_loop(0, T // ISSUE_UNROLL, body, 0)

    def finish(slot):
        for kk in range(TOP_K):
            pltpu.make_async_copy(y_hbm.at[pl.ds(0, T * ROW_TILE)], buf.at[slot * TOP_K + kk],
                                  sem.at[slot]).wait()
        g1 = route_ref[:, 2:3]
        g2 = route_ref[:, 3:4]
        for s in range(ROW_TILE):
            rows = _row_tile_rows(s, T)
            y1 = _unpack_pairs(buf[slot * TOP_K, rows, :])
            y2 = _unpack_pairs(buf[slot * TOP_K + 1, rows, :])
            for half in range(2):
                src = slice(half * LANES, (half + 1) * LANES)
                dst = slice(half * HALF_D + s * LANES, half * HALF_D + (s + 1) * LANES)
                o_ref[:, dst] = x1_ref[:, dst] + (g1 * y1[:, src] + g2 * y2[:, src])

    @pl.when(i == 0)
    def _():
        issue(dest_ref, 0)

    for slot in range(2):
        @pl.when(i % 2 == slot)
        def _(slot=slot):
            @pl.when(i + 1 < n_steps)
            def _():
                issue(dest_next_ref, 1 - slot)

            finish(slot)


def _combine_call(dest3, x1, route, y, T):
    N, D = x1.shape
    n_steps = N // T
    dest_spec = lambda index: pl.BlockSpec((None, 1, TOP_K * T), index, memory_space=pltpu.SMEM)
    return pl.pallas_call(
        functools.partial(_combine_kernel, T=T),
        grid=(n_steps,),
        in_specs=[
            dest_spec(lambda i: (i, 0, 0)),
            dest_spec(lambda i: (jnp.minimum(i + 1, n_steps - 1), 0, 0)),
            pl.BlockSpec((T, D), lambda i: (i, 0)),
            pl.BlockSpec((T, LANES), lambda i: (i, 0)),
            pl.BlockSpec(memory_space=pl.ANY),
        ],
        out_specs=pl.BlockSpec((T, D), lambda i: (i, 0)),
        out_shape=jax.ShapeDtypeStruct((N, D), F32),
        scratch_shapes=[pltpu.VMEM((2 * TOP_K, T * ROW_TILE, LANES), U32), pltpu.SemaphoreType.DMA((2,))],
        compiler_params=pltpu.CompilerParams(
            dimension_semantics=("arbitrary",), vmem_limit_bytes=VMEM_LIMIT),
        name="combine",
    )(dest3, dest3, x1, route, y)


def _block_diag(blocks):
    n = len(blocks)
    rows = []
    for i, b in enumerate(blocks):
        rows.append(jnp.concatenate(
            [b if j == i else jnp.zeros((b.shape[0], blocks[j].shape[1]), b.dtype) for j in range(n)], axis=1))
    return jnp.concatenate(rows, axis=0)


def _rep_forget(cols):
    half = jnp.concatenate(
        [jnp.tile(cols, (1, F_REP)), jnp.zeros((cols.shape[0], 64 - F_REP * A_HEADS), cols.dtype)], axis=1)
    return jnp.concatenate([half, half], axis=1)


def _head_to_gate_lanes():
    head = jnp.arange(BRANCH_WIDTH, dtype=I32)[:, None] // A_HEAD_DIM
    lane = jnp.arange(LANES, dtype=I32)[None, :]
    hit = ((lane & 7) == head) & (((lane & 63) >> 3) < F_REP)
    return hit.astype(BF16)


def _prep_layer(l, g_mix, w_in, b_fgate, b_gate, g_q, g_k, w_pool, s_pool, g_sgu, w_sgu, b_sgu,
                w_branch, w_out, g_ffn, w_rg, b_rg, w_re, b_re):
    wi = w_in[l]
    w_pack = jnp.concatenate(
        [wi[:, 0:OFF_F], _rep_forget(wi[:, OFF_F:OFF_P]), wi[:, OFF_P:OFF_G]], axis=1).astype(BF16)
    pad_r = LANES - N_EXPERT_GROUPS - N_EXPERTS
    w_r = jnp.concatenate([w_rg[l], w_re[l], jnp.zeros((D_MODEL, pad_r), F32)], axis=1)
    w_r_hi = w_r.astype(BF16)
    w_r_lo = (w_r - w_r_hi.astype(F32)).astype(BF16)
    return dict(
        g_mix=g_mix[l][None, :],
        w_pack=w_pack,
        b_f=_rep_forget(b_fgate[l][None, :]),
        g_q=jnp.tile(g_q[l], A_HEADS)[None, :],
        g_k=jnp.tile(g_k[l], A_HEADS)[None, :],
        bd=_block_diag([jnp.ones((A_HEAD_DIM, A_HEAD_DIM), BF16)] * A_HEADS),
        sel=_head_to_gate_lanes(),
        w_pool=_block_diag([w_pool[l][g] for g in range(N_GROUPS)]).astype(BF16),
        s_pool=s_pool[l][None, :],
        g_sgu=g_sgu[l][None, :],
        w_sgu=w_sgu[l],
        b_sgu=jnp.broadcast_to(b_sgu[l][:, :, None], (N_GROUPS, C_CHUNK, GROUP_DIM)),
        w_gate=wi[:, OFF_G:].astype(BF16),
        b_gate=b_gate[l].reshape(1, N_BRANCH * D_MODEL),
        w_branch=w_branch[l].astype(BF16),
        w_out=w_out[l].astype(BF16),
        g_ffn=g_ffn[l][None, :],
        w_router=jnp.concatenate([w_r_hi, w_r_lo], axis=1),
        b_router=jnp.concatenate([b_rg[l], b_re[l], jnp.zeros((pad_r,), F32)])[None, :],
    )


def kernel(x, g_mix, w_in, b_fgate, b_gate, g_q, g_k, w_pool, s_pool, g_sgu, w_sgu, b_sgu, w_branch, w_out,
           g_ffn, w_rg, b_rg, w_re, b_re, w1, w3, w2):
    B, S, D = x.shape
    assert D == D_MODEL and x.dtype == F32
    N = B * S
    T, tq, tk, bm, tm = _tiles(S)
    assert S % T == 0 and S % tq == 0 and tq == tk and T % C_CHUNK == 0
    assert (TOP_K * N) % bm == 0 and N % tm == 0 and tm % ISSUE_UNROLL == 0
    depth = w_in.shape[0]
    for l in range(depth):
        w = _prep_layer(l, g_mix, w_in, b_fgate, b_gate, g_q, g_k, w_pool, s_pool, g_sgu, w_sgu, b_sgu,
                        w_branch, w_out, g_ffn, w_rg, b_rg, w_re, b_re)
        qT, k, vT, yb, yc, c = _proj_call(x, w, T)
        ya = _attention(qT, k, vT, c, g_q[l], g_k[l], tq, tk)
        x1, h2, route, route_t, cnt = _merge_call(
            x.reshape(N, D), ya.reshape(N, -1), yb.reshape(N, -1), yc.reshape(N, -1), w, T)
        counts = cnt[0, :N_EXPERTS].astype(I32)
        starts = jnp.cumsum(counts) - counts
        experts = route_t[0:TOP_K].astype(I32)
        expert_ids = jnp.arange(N_EXPERTS, dtype=I32)[:, None, None]
        start_of = jnp.sum(jnp.where(experts[None] == expert_ids, starts[:, None, None], 0), axis=0)
        dest = start_of + route_t[4:4 + TOP_K].astype(I32)
        dest3 = dest.reshape(TOP_K, N // tm, tm).transpose(1, 0, 2).reshape(N // tm, 1, TOP_K * tm)
        xs = _dispatch_call(dest3, h2, tm)
        y = _gmm_call(_gmm_plan(counts, TOP_K * N, bm), xs, w1, w3, w2, l, bm)
        x = _combine_call(dest3, x1, route, y, tm).reshape(B, S, D)
    return x
```

```python
import functools

import jax
import jax.numpy as jnp
from jax import lax
from jax.experimental import pallas as pl
from jax.experimental.pallas import tpu as pltpu

F32 = jnp.float32
BF16 = jnp.bfloat16
I32 = jnp.int32
U32 = jnp.uint32

D_MODEL = 1024
A_HEADS = 8
A_HEAD_DIM = 64
BRANCH_WIDTH = 512
N_BRANCH = 3
POOL_WINDOWS = (2, 4, 8, 16)
GROUP_DIM = 128
N_GROUPS = 4
C_CHUNK = 128
N_EXPERT_GROUPS = 4
EXPERTS_PER_GROUP = 8
N_EXPERTS = N_EXPERT_GROUPS * EXPERTS_PER_GROUP
TOP_K = 2
D_EXPERT = 256
RMS_EPS = 1e-6
OFF_F = 3 * BRANCH_WIDTH
OFF_P = OFF_F + A_HEADS
OFF_U = OFF_P + BRANCH_WIDTH
OFF_SV = OFF_U + BRANCH_WIDTH
OFF_G = OFF_SV + BRANCH_WIDTH

LANES = 128
SUBLANES = 8
V7X_VMEM_BYTES = 64 * 1024 * 1024
VMEM_LIMIT = 56 * 1024 * 1024
ROW_TILE = D_MODEL // (2 * LANES)
HALF_D = D_MODEL // 2
ROUTE_ROWS = SUBLANES
ISSUE_UNROLL = 8
N_DMA_PRIORITIES = 2
ATTN_BLOCKS_PER_TRIP = 4

PK_Q, PK_K, PK_V = 0, 512, 1024
PK_F = 1536
PK_P = PK_F + LANES
PK_U = PK_P + BRANCH_WIDTH
PK_SV = PK_U + BRANCH_WIDTH
PK_W = PK_SV + BRANCH_WIDTH
F_REP = 6

LOG2E = 1.4426950408889634
SCORE_CAP = 96.0
DEAD_SCORE = -160.0
NEG = -1e30
HALO = 16


def _tiles(seq_len):
    t_proj = min(512, seq_len)
    t_q = min(512, seq_len)
    t_k = min(512, seq_len)
    moe_block = 512
    t_move = min(1024, seq_len)
    return t_proj, t_q, t_k, moe_block, t_move


def _const_spec(*shape):
    zeros = (0,) * len(shape)
    return pl.BlockSpec(shape, lambda *_: zeros, pipeline_mode=pl.Buffered(1))


def _rms(x, g):
    return x * lax.rsqrt(jnp.mean(x * x, axis=-1, keepdims=True) + RMS_EPS) * g


def _gelu_tanh(x):
    cdf = 0.5 * (1.0 + jnp.tanh(0.7978845608028654 * (x + 0.044715 * (x * x * x))))
    return x * cdf


def _log_sigmoid(x):
    return jnp.minimum(x, 0.0) - jnp.log1p(jnp.exp(-jnp.abs(x)))


def _split3(c):
    hi = c.astype(BF16).astype(F32)
    r = c - hi
    lo = r.astype(BF16).astype(F32)
    lolo = (r - lo).astype(BF16).astype(F32)
    return hi, lo, lolo


def _proj_kernel(x_ref, gmix_ref, w_ref, bf_ref, gq_ref, gk_ref, bd_ref, sel_ref, wpool_ref, spool_ref,
                 gsgu_ref, wsgu_ref, bsgu_ref,
                 qT_ref, k_ref, vT_ref, yb_ref, yc_ref, c_ref,
                 carry_ref, halo_ref, *, T):
    i = pl.program_id(1)

    @pl.when(i == 0)
    def _():
        carry_ref[...] = jnp.zeros_like(carry_ref)
        halo_ref[...] = jnp.zeros_like(halo_ref)

    hb = _rms(x_ref[...], gmix_ref[...]).astype(BF16)

    def proj(lo, width):
        return jnp.dot(hb, w_ref[:, lo:lo + width], preferred_element_type=F32)

    bd = bd_ref[...]

    def head_norm(z, g):
        sq = z * z
        hi = sq.astype(BF16)
        lo = (sq - hi.astype(F32)).astype(BF16)
        ss = (jnp.dot(hi, bd, preferred_element_type=F32)
              + jnp.dot(lo, bd, preferred_element_type=F32))
        return z * lax.rsqrt(ss * (1.0 / A_HEAD_DIM) + RMS_EPS) * g

    qn = head_norm(proj(PK_Q, BRANCH_WIDTH), gq_ref[...]) * (A_HEAD_DIM ** -0.5 * LOG2E)
    kn = head_norm(proj(PK_K, BRANCH_WIDTH), gk_ref[...])
    zv = proj(PK_V, BRANCH_WIDTH)
    qk = qn * kn
    qk_hi = qk.astype(BF16)
    qk_lo = (qk - qk_hi.astype(F32)).astype(BF16)
    diag = (jnp.dot(qk_hi, sel_ref[...], preferred_element_type=F32)
            + jnp.dot(qk_lo, sel_ref[...], preferred_element_type=F32))

    logf = _log_sigmoid(proj(PK_F, LANES) + bf_ref[...])
    r_i = lax.broadcasted_iota(I32, (T, T), 0)
    c_i = lax.broadcasted_iota(I32, (T, T), 1)
    tri = (r_i >= c_i).astype(F32)
    c = jnp.dot(tri, logf, preferred_element_type=F32,
                precision=lax.Precision.HIGHEST) + carry_ref[...]
    carry_ref[...] = c[T - 1:T, :]
    c_ref[...] = c

    lane = lax.broadcasted_iota(I32, (T, LANES), 1)
    lm = lane & 63
    grp = lm >> 3
    hsel = lm & 7
    first3 = grp < 3
    second3 = (grp >= 3) & (grp < F_REP)

    def parts(v):
        hi, lo, lolo = _split3(v)
        return jnp.where((grp == 0) | (grp == 3), hi, jnp.where((grp == 1) | (grp == 4), lo, lolo))

    k_aug = jnp.where(first3, -parts(c * LOG2E), 0.0)
    q_aug = jnp.where(second3, parts(c * LOG2E - diag), 0.0)

    for g in range(A_HEADS // 2):
        sl = slice(g * LANES, (g + 1) * LANES)
        zq2, zk2, zv2 = qn[:, sl], kn[:, sl], zv[:, sl]
        for par in range(2):
            h = 2 * g + par
            keep = (lane < 64) if par == 0 else (lane >= 64)
            oh_k = (second3 & (hsel == h)).astype(F32)
            oh_q = (first3 & (hsel == h)).astype(F32)
            k_ref[h] = jnp.where(keep, zk2, k_aug + oh_k).astype(BF16)
            qT_ref[h] = jnp.where(keep, zq2, q_aug + oh_q).T.astype(BF16)
            ones_lane = 64 if par == 0 else 0
            vT_ref[h] = jnp.where(keep, zv2, (lane == ones_lane).astype(F32)).T.astype(BF16)

    p = proj(PK_P, BRANCH_WIDTH)
    row8 = lax.broadcasted_iota(I32, (8, GROUP_DIM), 0)
    pos = i * T + lax.broadcasted_iota(I32, (T, GROUP_DIM), 0)

    def shift_down(v, tail, d):
        r = pltpu.roll(v, d, 0)
        rt = pltpu.roll(tail, d, 0)
        top = jnp.where(row8 < d, rt[0:8], r[0:8])
        return jnp.concatenate([top, r[8:]], axis=0)

    pooled = []
    for gi, w in enumerate(POOL_WINDOWS):
        sl = slice(gi * GROUP_DIM, (gi + 1) * GROUP_DIM)
        s = p[:, sl]
        for lv in range(gi + 1):
            tail = halo_ref[lv, :, sl]
            halo_ref[lv, :, sl] = s[T - HALO:T, :]
            s = s + shift_down(s, tail, 1 << lv)
        cnt = jnp.minimum(pos + 1, w).astype(F32)
        pooled.append(s / cnt - p[:, sl])
    pooled = jnp.concatenate(pooled, axis=1).astype(BF16)
    yb = jnp.dot(pooled, wpool_ref[...], preferred_element_type=F32) * spool_ref[...]
    yb_ref[...] = yb.astype(yb_ref.dtype)

    gu = _gelu_tanh(proj(PK_U, BRANCH_WIDTH))
    gv = _gelu_tanh(proj(PK_SV, BRANCH_WIDTH))
    t_r = lax.broadcasted_iota(I32, (C_CHUNK, C_CHUNK), 0)
    t_c = lax.broadcasted_iota(I32, (C_CHUNK, C_CHUNK), 1)
    causal = t_r >= t_c
    for g in range(N_GROUPS):
        sl = slice(g * GROUP_DIM, (g + 1) * GROUP_DIM)
        vn = _rms(gv[:, sl], gsgu_ref[:, sl]).astype(BF16)
        wc = jnp.where(causal, wsgu_ref[g], 0.0).astype(BF16)
        for ch in range(T // C_CHUNK):
            rows = slice(ch * C_CHUNK, (ch + 1) * C_CHUNK)
            mixed = jnp.dot(wc, vn[rows], preferred_element_type=F32) + bsgu_ref[g]
            yc_ref[rows, sl] = (gu[rows, sl] * mixed).astype(yc_ref.dtype)


def _proj_call(x, w, T):
    B, S, D = x.shape
    H = A_HEADS
    const = _const_spec
    return pl.pallas_call(
        functools.partial(_proj_kernel, T=T),
        grid=(B, S // T),
        in_specs=[
            pl.BlockSpec((None, T, D), lambda b, i: (b, i, 0)),
            const(1, D), const(D, PK_W), const(1, LANES), const(1, BRANCH_WIDTH), const(1, BRANCH_WIDTH),
            const(BRANCH_WIDTH, BRANCH_WIDTH), const(BRANCH_WIDTH, LANES), const(BRANCH_WIDTH, BRANCH_WIDTH),
            const(1, BRANCH_WIDTH), const(1, BRANCH_WIDTH), const(N_GROUPS, C_CHUNK, C_CHUNK), const(N_GROUPS, C_CHUNK, GROUP_DIM),
        ],
        out_specs=[
            pl.BlockSpec((None, H, LANES, T), lambda b, i: (b, 0, 0, i)),
            pl.BlockSpec((None, H, T, LANES), lambda b, i: (b, 0, i, 0)),
            pl.BlockSpec((None, H, LANES, T), lambda b, i: (b, 0, 0, i)),
            pl.BlockSpec((None, T, BRANCH_WIDTH), lambda b, i: (b, i, 0)),
            pl.BlockSpec((None, T, BRANCH_WIDTH), lambda b, i: (b, i, 0)),
            pl.BlockSpec((None, T, LANES), lambda b, i: (b, i, 0)),
        ],
        out_shape=[
            jax.ShapeDtypeStruct((B, H, LANES, S), BF16),
            jax.ShapeDtypeStruct((B, H, S, LANES), BF16),
            jax.ShapeDtypeStruct((B, H, LANES, S), BF16),
            jax.ShapeDtypeStruct((B, S, BRANCH_WIDTH), BF16),
            jax.ShapeDtypeStruct((B, S, BRANCH_WIDTH), BF16),
            jax.ShapeDtypeStruct((B, S, LANES), F32),
        ],
        scratch_shapes=[pltpu.VMEM((1, LANES), F32), pltpu.VMEM((4, HALO, BRANCH_WIDTH), F32)],
        compiler_params=pltpu.CompilerParams(
            dimension_semantics=("arbitrary", "arbitrary"), vmem_limit_bytes=VMEM_LIMIT),
        name="proj",
    )(x, w["g_mix"], w["w_pack"], w["b_f"], w["g_q"], w["g_k"], w["bd"], w["sel"], w["w_pool"], w["s_pool"],
      w["g_sgu"], w["w_sgu"], w["b_sgu"])


def _attn_kernel(first_ref, qT_ref, k_ref, vT_ref, o_ref, m_ref, acc_ref, s_ref, *, tq, tk, n_q, online_max):
    assert tq == tk
    FIRST = 2

    def reset():
        acc_ref[...] = jnp.zeros_like(acc_ref)
        if online_max:
            m_ref[...] = jnp.full(m_ref.shape, NEG, F32)

    def scores(qi, j, slot):
        k0 = pl.multiple_of(j * tk, tk)
        q0 = pl.multiple_of(qi * tq, tq)
        for par in range(2):
            s_ref[slot, par] = jnp.dot(k_ref[par, pl.ds(k0, tk), :], qT_ref[par, :, pl.ds(q0, tq)],
                                       preferred_element_type=F32)

    def consume(qi, j, slot, masked):
        k0 = pl.multiple_of(j * tk, tk)
        q0 = qi * tq
        for par in range(2):
            s = s_ref[slot, par]
            if masked:
                kpos = k0 + lax.broadcasted_iota(I32, (tk, tq), 0)
                qpos = q0 + lax.broadcasted_iota(I32, (tk, tq), 1)
                s = jnp.where(kpos <= qpos, s, NEG)
            v_blk = vT_ref[par, :, pl.ds(k0, tk)]
            if online_max:
                m_old = m_ref[par]
                m_new = jnp.maximum(m_old, jnp.max(s, axis=0, keepdims=True))
                p = jnp.exp2(s - m_new).astype(BF16)
                pv = jnp.dot(v_blk, p, preferred_element_type=F32)
                acc_ref[par] = acc_ref[par] * jnp.exp2(m_old - m_new) + pv
                m_ref[par] = m_new
            else:
                p = jnp.exp2(s).astype(BF16)
                acc_ref[par] += jnp.dot(v_blk, p, preferred_element_type=F32)

    def finish(qi):
        lane = lax.broadcasted_iota(I32, (tq, LANES), 1)
        outs = []
        for par in range(2):
            acc = acc_ref[par]
            l = acc[64:65, :] if par == 0 else acc[0:1, :]
            outs.append((acc * (1.0 / l)).T)
        q0 = pl.multiple_of(qi * tq, tq)
        o_ref[pl.ds(q0, tq), :] = jnp.where(lane < 64, outs[0], outs[1]).astype(o_ref.dtype)
        reset()

    b_id = pl.program_id(0)
    g_id = pl.program_id(1)

    def first_block(qi):
        return first_ref[(b_id * pl.num_programs(1) + g_id) * n_q + qi]

    reset()
    scores(0, 0, FIRST)
    consume(0, 0, FIRST, True)
    finish(0)
    if n_q > 1:
        scores(1, first_block(1), FIRST)

    def query_block(qi, carry):
        base = first_block(qi)
        n_mid = qi - 1 - base
        nxt = jnp.minimum(qi + 1, n_q - 1)
        scores(qi, base + 1, 1)
        consume(qi, base, FIRST, False)

        def trip(t, c):
            j0 = base + 1 + ATTN_BLOCKS_PER_TRIP * t
            for u in range(ATTN_BLOCKS_PER_TRIP):
                scores(qi, j0 + u + 1, u % 2)
                consume(qi, j0 + u, (u + 1) % 2, False)
            return c

        lax.fori_loop(0, n_mid // ATTN_BLOCKS_PER_TRIP, trip, 0)
        j0 = base + 1 + (n_mid // ATTN_BLOCKS_PER_TRIP) * ATTN_BLOCKS_PER_TRIP
        for rem in range(ATTN_BLOCKS_PER_TRIP):
            @pl.when(n_mid % ATTN_BLOCKS_PER_TRIP == rem)
            def _(rem=rem):
                for u in range(rem):
                    scores(qi, j0 + u + 1, u % 2)
                    consume(qi, j0 + u, (u + 1) % 2, False)
                scores(nxt, first_block(nxt), FIRST)
                consume(qi, j0 + rem, (rem + 1) % 2, True)
                finish(qi)
        return carry

    lax.fori_loop(1, n_q, query_block, 0)


def _attn_call(first, qT, k, vT, tq, tk, online_max):
    B, H, _, S = qT.shape
    grid_spec = pltpu.PrefetchScalarGridSpec(
        num_scalar_prefetch=1,
        grid=(B, H // 2),
        in_specs=[
            pl.BlockSpec((None, 2, LANES, S), lambda b, g, first: (b, g, 0, 0)),
            pl.BlockSpec((None, 2, S, LANES), lambda b, g, first: (b, g, 0, 0)),
            pl.BlockSpec((None, 2, LANES, S), lambda b, g, first: (b, g, 0, 0)),
        ],
        out_specs=pl.BlockSpec((None, S, LANES), lambda b, g, first: (b, 0, g)),
        scratch_shapes=[pltpu.VMEM((2, 1, tq), F32), pltpu.VMEM((2, LANES, tq), F32),
                        pltpu.VMEM((3, 2, tk, tq), F32)],
    )
    return pl.pallas_call(
        functools.partial(_attn_kernel, tq=tq, tk=tk, n_q=S // tq, online_max=online_max),
        grid_spec=grid_spec,
        out_shape=jax.ShapeDtypeStruct((B, S, BRANCH_WIDTH), BF16),
        compiler_params=pltpu.CompilerParams(
            dimension_semantics=("parallel", "parallel"), vmem_limit_bytes=VMEM_LIMIT),
        name="attn_online" if online_max else "attn",
    )(first, qT, k, vT)


def _first_live_block(c, bound, tq, tk):
    c_q = c[:, 0::tq, 0:A_HEADS]
    c_k = c[:, tk - 1::tk, 0:A_HEADS]
    top = bound + LOG2E * (c_q[:, :, None, :] - c_k[:, None, :, :])
    dead = top < DEAD_SCORE
    dead_pair = dead[..., 0::2] & dead[..., 1::2]
    n_q = c_q.shape[1]
    n_dead = jnp.sum(jnp.cumprod(dead_pair.astype(I32), axis=2), axis=2)
    limit = jnp.maximum(jnp.arange(n_q, dtype=I32) - 1, 0)[None, :, None]
    first = jnp.minimum(n_dead, limit)
    return first.transpose(0, 2, 1).reshape(-1).astype(I32)


def _attention(qT, k, vT, c, g_q, g_k, tq, tk):
    bound = 16.0 * LOG2E * jnp.max(jnp.abs(g_q)) * jnp.max(jnp.abs(g_k))
    first = _first_live_block(c, bound, tq, tk)
    return lax.cond(bound <= SCORE_CAP,
                    lambda: _attn_call(first, qT, k, vT, tq, tk, False),
                    lambda: _attn_call(first, qT, k, vT, tq, tk, True))


def _merge_kernel(x_ref, ya_ref, yb_ref, yc_ref, gmix_ref, wg_ref, bg_ref, wb_ref, wo_ref,
                  gffn_ref, wr_ref, br_ref,
                  x1_ref, h2_ref, route_ref, route_t_ref, cnt_ref, carry_ref, *, T):
    i = pl.program_id(0)

    @pl.when(i == 0)
    def _():
        carry_ref[...] = jnp.zeros_like(carry_ref)

    x = x_ref[...]
    hb = _rms(x, gmix_ref[...]).astype(BF16)
    merged = None
    for bi, y_ref in enumerate((ya_ref, yb_ref, yc_ref)):
        sl = slice(bi * D_MODEL, (bi + 1) * D_MODEL)
        gate = jax.nn.sigmoid(jnp.dot(hb, wg_ref[:, sl], preferred_element_type=F32) + bg_ref[:, sl])
        term = gate * jnp.dot(y_ref[...], wb_ref[bi], preferred_element_type=F32)
        merged = term if merged is None else merged + term
    x1 = x + jnp.dot(merged.astype(BF16), wo_ref[...], preferred_element_type=F32)
    x1_ref[...] = x1
    h2 = _rms(x1, gffn_ref[...])
    _store_row_tiles(h2_ref, _pack_pairs(h2))

    h_hi = h2.astype(BF16)
    h_lo = (h2 - h_hi.astype(F32)).astype(BF16)
    hw = jnp.dot(h_hi, wr_ref[...], preferred_element_type=F32)
    lw = jnp.dot(h_lo, wr_ref[:, 0:LANES], preferred_element_type=F32)
    logits = hw[:, 0:LANES] + (hw[:, LANES:] + lw) + br_ref[...]
    lane = lax.broadcasted_iota(I32, (T, LANES), 1).astype(F32)
    big = float(LANES)

    def first_argmax(v):
        m = jnp.max(v, axis=-1, keepdims=True)
        return m, jnp.min(jnp.where(v == m, lane, big), axis=-1, keepdims=True)

    lg = jnp.where(lane < N_EXPERT_GROUPS, logits, NEG)
    mg, grp = first_argmax(lg)
    p_grp = 1.0 / jnp.sum(jnp.exp(lg - mg), axis=-1, keepdims=True)
    lo_lane = N_EXPERT_GROUPS + grp * EXPERTS_PER_GROUP
    le = jnp.where((lane >= lo_lane) & (lane < lo_lane + EXPERTS_PER_GROUP), logits, NEG)
    m1, i1 = first_argmax(le)
    m2, i2 = first_argmax(jnp.where(lane == i1, NEG, le))
    e21 = jnp.exp(m2 - m1)
    g1 = p_grp / (1.0 + e21)
    g2 = p_grp * e21 / (1.0 + e21)
    e1 = i1 - N_EXPERT_GROUPS
    e2 = i2 - N_EXPERT_GROUPS

    oh1 = lane == e1
    oh2 = lane == e2
    sel = (oh1 | oh2).astype(F32)
    r_i = lax.broadcasted_iota(I32, (T, T), 0)
    c_i = lax.broadcasted_iota(I32, (T, T), 1)
    before = (r_i > c_i).astype(BF16)
    seen = jnp.dot(before, sel.astype(BF16), preferred_element_type=F32) + carry_ref[...]
    r1 = jnp.sum(jnp.where(oh1, seen, 0.0), axis=-1, keepdims=True)
    r2 = jnp.sum(jnp.where(oh2, seen, 0.0), axis=-1, keepdims=True)
    carry_ref[...] = carry_ref[...] + jnp.sum(sel, axis=0, keepdims=True)
    cnt_ref[...] = carry_ref[...]

    route = jnp.zeros((T, LANES), F32)
    for idx, val in enumerate((e1, e2, g1, g2, r1, r2)):
        route = jnp.where(lane == idx, val, route)
    route_ref[...] = route
    route_t_ref[...] = route.T[0:ROUTE_ROWS]


def _row_tile_rows(s, n_rows, row0=0):
    return pl.ds(row0 * ROW_TILE + s, n_rows, stride=ROW_TILE)


def _pack_pairs(v):
    hi = lax.bitcast_convert_type(v[:, :HALF_D].astype(BF16).astype(F32), U32)
    lo = lax.bitcast_convert_type(v[:, HALF_D:].astype(BF16).astype(F32), U32)
    return hi | (lo >> 16)


def _unpack_pairs(w):
    hi = lax.bitcast_convert_type(w & jnp.uint32(0xFFFF0000), F32)
    lo = lax.bitcast_convert_type(w << 16, F32)
    return jnp.concatenate([hi, lo], axis=-1)


def _store_row_tiles(ref, words, row0=0):
    for s in range(ROW_TILE):
        ref[_row_tile_rows(s, words.shape[0], row0), :] = words[:, s * LANES:(s + 1) * LANES]


def _load_row_tiles(ref, n_rows, row0=0):
    return jnp.concatenate([ref[_row_tile_rows(s, n_rows, row0), :] for s in range(ROW_TILE)], axis=-1)


def _merge_call(x2d, ya, yb, yc, w, T):
    N, D = x2d.shape
    const = _const_spec
    tile = lambda width: pl.BlockSpec((T, width), lambda i: (i, 0))
    return pl.pallas_call(
        functools.partial(_merge_kernel, T=T),
        grid=(N // T,),
        in_specs=[
            tile(D), tile(BRANCH_WIDTH), tile(BRANCH_WIDTH), tile(BRANCH_WIDTH),
            const(1, D), const(D, N_BRANCH * D), const(1, N_BRANCH * D),
            const(N_BRANCH, BRANCH_WIDTH, D), const(D, D), const(1, D), const(D, 2 * LANES), const(1, LANES),
        ],
        out_specs=[tile(D), pl.BlockSpec((T * ROW_TILE, LANES), lambda i: (i, 0)), tile(LANES),
                   pl.BlockSpec((ROUTE_ROWS, T), lambda i: (0, i)),
                   pl.BlockSpec((1, LANES), lambda i: (0, 0))],
        out_shape=[
            jax.ShapeDtypeStruct((N, D), F32),
            jax.ShapeDtypeStruct((N * ROW_TILE, LANES), U32),
            jax.ShapeDtypeStruct((N, LANES), F32),
            jax.ShapeDtypeStruct((ROUTE_ROWS, N), F32),
            jax.ShapeDtypeStruct((1, LANES), F32),
        ],
        scratch_shapes=[pltpu.VMEM((1, LANES), F32)],
        compiler_params=pltpu.CompilerParams(
            dimension_semantics=("arbitrary",), vmem_limit_bytes=VMEM_LIMIT),
        name="merge",
    )(x2d, ya, yb, yc, w["g_mix"], w["w_gate"], w["b_gate"], w["w_branch"], w["w_out"],
      w["g_ffn"], w["w_router"], w["b_router"])


def _dispatch_kernel(dest_ref, h2_ref, xs_ref, sem, *, T):
    def row_copy(t, d):
        return pltpu.make_async_copy(h2_ref.at[_row_tile(t)], xs_ref.at[_row_tile(d)], sem)

    def body(tb, carry):
        t0 = tb * ISSUE_UNROLL
        for u in range(ISSUE_UNROLL):
            for kk in range(TOP_K):
                row_copy(t0 + u, dest_ref[0, kk * T + t0 + u]).start(priority=kk % N_DMA_PRIORITIES)
        return carry

    lax.fori_loop(0, T // ISSUE_UNROLL, body, 0)
    for kk in range(TOP_K):
        pltpu.make_async_copy(h2_ref, xs_ref.at[pl.ds(0, T * ROW_TILE)], sem).wait()


def _row_tile(r):
    return pl.ds(pl.multiple_of(r * ROW_TILE, ROW_TILE), ROW_TILE)


def _dispatch_call(dest3, h2, T):
    N = h2.shape[0] // ROW_TILE
    return pl.pallas_call(
        functools.partial(_dispatch_kernel, T=T),
        grid=(N // T,),
        in_specs=[
            pl.BlockSpec((None, 1, TOP_K * T), lambda i: (i, 0, 0), memory_space=pltpu.SMEM),
            pl.BlockSpec((T * ROW_TILE, LANES), lambda i: (i, 0)),
        ],
        out_specs=pl.BlockSpec(memory_space=pl.ANY),
        out_shape=jax.ShapeDtypeStruct((TOP_K * N * ROW_TILE, LANES), U32),
        scratch_shapes=[pltpu.SemaphoreType.DMA],
        compiler_params=pltpu.CompilerParams(
            dimension_semantics=("arbitrary",), vmem_limit_bytes=VMEM_LIMIT),
        name="dispatch",
    )(dest3, h2)


def _gmm_kernel(blk_ref, exp_ref, lo_ref, hi_ref, xs_ref, w1_ref, w3_ref, w2_ref, y_ref,
                w13_bf, w2_bf, *, bm):
    i = pl.program_id(0)
    lo = lo_ref[i]
    hi = hi_ref[i]
    prev = jnp.maximum(i - 1, 0)
    first = jnp.logical_or(i == 0, blk_ref[i] != blk_ref[prev])
    new_expert = jnp.logical_or(i == 0, exp_ref[i] != exp_ref[prev])

    @pl.when(new_expert)
    def _():
        w13_bf[:, 0:D_EXPERT] = w1_ref[...].astype(BF16)
        w13_bf[:, D_EXPERT:] = w3_ref[...].astype(BF16)
        w2_bf[...] = w2_ref[...].astype(BF16)

    @pl.when(hi > lo)
    def _():
        n_part = 2
        part = bm // n_part
        outs = []
        for h in range(n_part):
            x = _unpack_pairs(_load_row_tiles(xs_ref, part, h * part)).astype(BF16)
            ab = jnp.dot(x, w13_bf[...], preferred_element_type=F32)
            a = ab[:, 0:D_EXPERT]
            b = ab[:, D_EXPERT:]
            mid = (a * jax.nn.sigmoid(a) * b).astype(BF16)
            outs.append(_pack_pairs(jnp.dot(mid, w2_bf[...], preferred_element_type=F32)))

        @pl.when(first)
        def _():
            for h in range(n_part):
                _store_row_tiles(y_ref, outs[h], h * part)

        @pl.when(jnp.logical_not(first))
        def _():
            for h in range(n_part):
                row = h * part + lax.broadcasted_iota(I32, (part, HALF_D), 0)
                mine = (row >= lo) & (row < hi)
                old = _load_row_tiles(y_ref, part, h * part)
                _store_row_tiles(y_ref, jnp.where(mine, outs[h], old), h * part)


def _gmm_call(plan, xs, w1, w3, w2, layer, bm):
    A = xs.shape[0] // ROW_TILE
    D = D_MODEL
    n_items = plan[0].shape[0]
    grid_spec = pltpu.PrefetchScalarGridSpec(
        num_scalar_prefetch=4,
        grid=(n_items,),
        in_specs=[
            pl.BlockSpec((bm * ROW_TILE, LANES), lambda i, blk, ex, lo, hi: (blk[i], 0)),
            pl.BlockSpec((None, None, D, D_EXPERT), lambda i, blk, ex, lo, hi: (layer, ex[i], 0, 0)),
            pl.BlockSpec((None, None, D, D_EXPERT), lambda i, blk, ex, lo, hi: (layer, ex[i], 0, 0)),
            pl.BlockSpec((None, None, D_EXPERT, D), lambda i, blk, ex, lo, hi: (layer, ex[i], 0, 0)),
        ],
        out_specs=pl.BlockSpec((bm * ROW_TILE, LANES), lambda i, blk, ex, lo, hi: (blk[i], 0)),
        scratch_shapes=[pltpu.VMEM((D, 2 * D_EXPERT), BF16), pltpu.VMEM((D_EXPERT, D), BF16)],
    )
    return pl.pallas_call(
        functools.partial(_gmm_kernel, bm=bm),
        grid_spec=grid_spec,
        out_shape=jax.ShapeDtypeStruct((A * ROW_TILE, LANES), U32),
        compiler_params=pltpu.CompilerParams(
            dimension_semantics=("arbitrary",), vmem_limit_bytes=VMEM_LIMIT),
        name="gmm",
    )(*plan, xs, w1, w3, w2)


def _gmm_plan(counts, n_rows, bm):
    n_blk = n_rows // bm
    n_items = n_blk + N_EXPERTS - 1
    ends = jnp.cumsum(counts)
    starts = ends - counts
    first_blk = starts // bm
    n_it = jnp.where(counts > 0, (ends - 1) // bm - first_blk + 1, 0)
    item_end = jnp.cumsum(n_it)
    item_start = item_end - n_it
    total = item_end[-1]
    ids = jnp.arange(n_items, dtype=I32)
    valid = ids < total
    ex = jnp.minimum(jnp.sum(item_end[None, :] <= ids[:, None], axis=1).astype(I32), N_EXPERTS - 1)
    ex = jnp.where(valid, ex, ex[jnp.maximum(total - 1, 0)])
    blk = jnp.where(valid, first_blk[ex] + ids - item_start[ex], n_blk - 1)
    lo = jnp.where(valid, jnp.maximum(starts[ex], blk * bm) - blk * bm, 0)
    hi = jnp.where(valid, jnp.minimum(ends[ex], (blk + 1) * bm) - blk * bm, 0)
    return blk.astype(I32), ex.astype(I32), lo.astype(I32), hi.astype(I32)


def _combine_kernel(dest_ref, dest_next_ref, x1_ref, route_ref, y_hbm, o_ref, buf, sem, *, T):
    i = pl.program_id(0)
    n_steps = pl.num_programs(0)

    def issue(d_ref, slot):
        def body(tb, carry):
            t0 = tb * ISSUE_UNROLL
            for u in range(ISSUE_UNROLL):
                for kk in range(TOP_K):
                    d = d_ref[0, kk * T + t0 + u]
                    pltpu.make_async_copy(y_hbm.at[_row_tile(d)], buf.at[slot * TOP_K + kk, _row_tile(t0 + u)],
                                          sem.at[slot]).start(priority=kk % N_DMA_PRIORITIES)
            return carry

        lax.fori_loop(0, T // ISSUE_UNROLL, body, 0)

    def finish(slot):
        for kk in range(TOP_K):
            pltpu.make_async_copy(y_hbm.at[pl.ds(0, T * ROW_TILE)], buf.at[slot * TOP_K + kk],
                                  sem.at[slot]).wait()
        g1 = route_ref[:, 2:3]
        g2 = route_ref[:, 3:4]
        for s in range(ROW_TILE):
            rows = _row_tile_rows(s, T)
            y1 = _unpack_pairs(buf[slot * TOP_K, rows, :])
            y2 = _unpack_pairs(buf[slot * TOP_K + 1, rows, :])
            for half in range(2):
                src = slice(half * LANES, (half + 1) * LANES)
                dst = slice(half * HALF_D + s * LANES, half * HALF_D + (s + 1) * LANES)
                o_ref[:, dst] = x1_ref[:, dst] + (g1 * y1[:, src] + g2 * y2[:, src])

    @pl.when(i == 0)
    def _():
        issue(dest_ref, 0)

    for slot in range(2):
        @pl.when(i % 2 == slot)
        def _(slot=slot):
            @pl.when(i + 1 < n_steps)
            def _():
                issue(dest_next_ref, 1 - slot)

            finish(slot)


def _combine_call(dest3, x1, route, y, T):
    N, D = x1.shape
    n_steps = N // T
    dest_spec = lambda index: pl.BlockSpec((None, 1, TOP_K * T), index, memory_space=pltpu.SMEM)
    return pl.pallas_call(
        functools.partial(_combine_kernel, T=T),
        grid=(n_steps,),
        in_specs=[
            dest_spec(lambda i: (i, 0, 0)),
            dest_spec(lambda i: (jnp.minimum(i + 1, n_steps - 1), 0, 0)),
            pl.BlockSpec((T, D), lambda i: (i, 0)),
            pl.BlockSpec((T, LANES), lambda i: (i, 0)),
            pl.BlockSpec(memory_space=pl.ANY),
        ],
        out_specs=pl.BlockSpec((T, D), lambda i: (i, 0)),
        out_shape=jax.ShapeDtypeStruct((N, D), F32),
        scratch_shapes=[pltpu.VMEM((2 * TOP_K, T * ROW_TILE, LANES), U32), pltpu.SemaphoreType.DMA((2,))],
        compiler_params=pltpu.CompilerParams(
            dimension_semantics=("arbitrary",), vmem_limit_bytes=VMEM_LIMIT),
        name="combine",
    )(dest3, dest3, x1, route, y)


def _block_diag(blocks):
    n = len(blocks)
    rows = []
    for i, b in enumerate(blocks):
        rows.append(jnp.concatenate(
            [b if j == i else jnp.zeros((b.shape[0], blocks[j].shape[1]), b.dtype) for j in range(n)], axis=1))
    return jnp.concatenate(rows, axis=0)


def _rep_forget(cols):
    half = jnp.concatenate(
        [jnp.tile(cols, (1, F_REP)), jnp.zeros((cols.shape[0], 64 - F_REP * A_HEADS), cols.dtype)], axis=1)
    return jnp.concatenate([half, half], axis=1)


def _head_to_gate_lanes():
    head = jnp.arange(BRANCH_WIDTH, dtype=I32)[:, None] // A_HEAD_DIM
    lane = jnp.arange(LANES, dtype=I32)[None, :]
    hit = ((lane & 7) == head) & (((lane & 63) >> 3) < F_REP)
    return hit.astype(BF16)


def _prep_layer(l, g_mix, w_in, b_fgate, b_gate, g_q, g_k, w_pool, s_pool, g_sgu, w_sgu, b_sgu,
                w_branch, w_out, g_ffn, w_rg, b_rg, w_re, b_re):
    wi = w_in[l]
    order = jnp.argsort(b_fgate[l])

    def by_head(cols):
        r = cols.shape[0]
        return jnp.take(cols.reshape(r, A_HEADS, A_HEAD_DIM), order, axis=1).reshape(r, BRANCH_WIDTH)

    qkv = [by_head(wi[:, i * BRANCH_WIDTH:(i + 1) * BRANCH_WIDTH]) for i in range(3)]
    w_f = jnp.take(wi[:, OFF_F:OFF_P], order, axis=1)
    b_f = jnp.take(b_fgate[l], order)
    w_branch_l = w_branch[l].at[0].set(
        jnp.take(w_branch[l][0].reshape(A_HEADS, A_HEAD_DIM, D_MODEL), order, axis=0).reshape(BRANCH_WIDTH, D_MODEL))
    w_pack = jnp.concatenate(qkv + [_rep_forget(w_f), wi[:, OFF_P:OFF_G]], axis=1).astype(BF16)
    pad_r = LANES - N_EXPERT_GROUPS - N_EXPERTS
    w_r = jnp.concatenate([w_rg[l], w_re[l], jnp.zeros((D_MODEL, pad_r), F32)], axis=1)
    w_r_hi = w_r.astype(BF16)
    w_r_lo = (w_r - w_r_hi.astype(F32)).astype(BF16)
    return dict(
        g_mix=g_mix[l][None, :],
        w_pack=w_pack,
        b_f=_rep_forget(b_f[None, :]),
        g_q=jnp.tile(g_q[l], A_HEADS)[None, :],
        g_k=jnp.tile(g_k[l], A_HEADS)[None, :],
        bd=_block_diag([jnp.ones((A_HEAD_DIM, A_HEAD_DIM), BF16)] * A_HEADS),
        sel=_head_to_gate_lanes(),
        w_pool=_block_diag([w_pool[l][g] for g in range(N_GROUPS)]).astype(BF16),
        s_pool=s_pool[l][None, :],
        g_sgu=g_sgu[l][None, :],
        w_sgu=w_sgu[l],
        b_sgu=jnp.broadcast_to(b_sgu[l][:, :, None], (N_GROUPS, C_CHUNK, GROUP_DIM)),
        w_gate=wi[:, OFF_G:].astype(BF16),
        b_gate=b_gate[l].reshape(1, N_BRANCH * D_MODEL),
        w_branch=w_branch_l.astype(BF16),
        w_out=w_out[l].astype(BF16),
        g_ffn=g_ffn[l][None, :],
        w_router=jnp.concatenate([w_r_hi, w_r_lo], axis=1),
        b_router=jnp.concatenate([b_rg[l], b_re[l], jnp.zeros((pad_r,), F32)])[None, :],
    )


def kernel(x, g_mix, w_in, b_fgate, b_gate, g_q, g_k, w_pool, s_pool, g_sgu, w_sgu, b_sgu, w_branch, w_out,
           g_ffn, w_rg, b_rg, w_re, b_re, w1, w3, w2):
    B, S, D = x.shape
    assert D == D_MODEL and x.dtype == F32
    N = B * S
    T, tq, tk, bm, tm = _tiles(S)
    assert S % T == 0 and S % tq == 0 and tq == tk and T % C_CHUNK == 0
    assert (TOP_K * N) % bm == 0 and N % tm == 0 and tm % ISSUE_UNROLL == 0
    depth = w_in.shape[0]
    for l in range(depth):
        w = _prep_layer(l, g_mix, w_in, b_fgate, b_gate, g_q, g_k, w_pool, s_pool, g_sgu, w_sgu, b_sgu,
                        w_branch, w_out, g_ffn, w_rg, b_rg, w_re, b_re)
        qT, k, vT, yb, yc, c = _proj_call(x, w, T)
        ya = _attention(qT, k, vT, c, g_q[l], g_k[l], tq, tk)
        x1, h2, route, route_t, cnt = _merge_call(
            x.reshape(N, D), ya.reshape(N, -1), yb.reshape(N, -1), yc.reshape(N, -1), w, T)
        counts = cnt[0, :N_EXPERTS].astype(I32)
        starts = jnp.cumsum(counts) - counts
        experts = route_t[0:TOP_K].astype(I32)
        expert_ids = jnp.arange(N_EXPERTS, dtype=I32)[:, None, None]
        start_of = jnp.sum(jnp.where(experts[None] == expert_ids, starts[:, None, None], 0), axis=0)
        dest = start_of + route_t[4:4 + TOP_K].astype(I32)
        dest3 = dest.reshape(TOP_K, N // tm, tm).transpose(1, 0, 2).reshape(N // tm, 1, TOP_K * tm)
        xs = _dispatch_call(dest3, h2, tm)
        y = _gmm_call(_gmm_plan(counts, TOP_K * N, bm), xs, w1, w3, w2, l, bm)
        x = _combine_call(dest3, x1, route, y, tm).reshape(B, S, D)
    return x
```

```python
import functools

import jax
import jax.numpy as jnp
from jax import lax
from jax.experimental import pallas as pl
from jax.experimental.pallas import tpu as pltpu

F32 = jnp.float32
BF16 = jnp.bfloat16
I32 = jnp.int32
U32 = jnp.uint32

D_MODEL = 1024
A_HEADS = 8
A_HEAD_DIM = 64
BRANCH_WIDTH = 512
N_BRANCH = 3
POOL_WINDOWS = (2, 4, 8, 16)
GROUP_DIM = 128
N_GROUPS = 4
C_CHUNK = 128
N_EXPERT_GROUPS = 4
EXPERTS_PER_GROUP = 8
N_EXPERTS = N_EXPERT_GROUPS * EXPERTS_PER_GROUP
TOP_K = 2
D_EXPERT = 256
RMS_EPS = 1e-6
OFF_F = 3 * BRANCH_WIDTH
OFF_P = OFF_F + A_HEADS
OFF_U = OFF_P + BRANCH_WIDTH
OFF_SV = OFF_U + BRANCH_WIDTH
OFF_G = OFF_SV + BRANCH_WIDTH

LANES = 128
SUBLANES = 8
V7X_VMEM_BYTES = 64 * 1024 * 1024
VMEM_LIMIT = 56 * 1024 * 1024
ROW_TILE = D_MODEL // (2 * LANES)
HALF_D = D_MODEL // 2
ROUTE_ROWS = SUBLANES
ISSUE_UNROLL = 8
N_DMA_PRIORITIES = 2
ATTN_BLOCKS_PER_TRIP = 4

PK_Q, PK_K, PK_V = 0, 512, 1024
PK_F = 1536
PK_P = PK_F + LANES
PK_U = PK_P + BRANCH_WIDTH
PK_SV = PK_U + BRANCH_WIDTH
PK_W = PK_SV + BRANCH_WIDTH
F_REP = 6

LOG2E = 1.4426950408889634
SCORE_CAP = 96.0
DEAD_SCORE = -160.0
NEG = -1e30
HALO = 16


def _tiles(seq_len):
    t_proj = min(512, seq_len)
    t_q = min(512, seq_len)
    t_k = min(512, seq_len)
    moe_block = 512
    t_move = min(1024, seq_len)
    return t_proj, t_q, t_k, moe_block, t_move


def _const_spec(*shape):
    zeros = (0,) * len(shape)
    return pl.BlockSpec(shape, lambda *_: zeros, pipeline_mode=pl.Buffered(1))


def _rms(x, g):
    return x * lax.rsqrt(jnp.mean(x * x, axis=-1, keepdims=True) + RMS_EPS) * g


def _gelu_tanh(x):
    cdf = 0.5 * (1.0 + jnp.tanh(0.7978845608028654 * (x + 0.044715 * (x * x * x))))
    return x * cdf


def _log_sigmoid(x):
    return jnp.minimum(x, 0.0) - jnp.log1p(jnp.exp(-jnp.abs(x)))


def _split3(c):
    hi = c.astype(BF16).astype(F32)
    r = c - hi
    lo = r.astype(BF16).astype(F32)
    lolo = (r - lo).astype(BF16).astype(F32)
    return hi, lo, lolo


def _proj_kernel(x_ref, gmix_ref, w_ref, bf_ref, gq_ref, gk_ref, bd_ref, sel_ref, wpool_ref, spool_ref,
                 gsgu_ref, wsgu_ref, bsgu_ref,
                 qT_ref, k_ref, vT_ref, yb_ref, yc_ref, c_ref,
                 carry_ref, halo_ref, *, T):
    i = pl.program_id(1)

    @pl.when(i == 0)
    def _():
        carry_ref[...] = jnp.zeros_like(carry_ref)
        halo_ref[...] = jnp.zeros_like(halo_ref)

    hb = _rms(x_ref[...], gmix_ref[...]).astype(BF16)

    def proj(lo, width):
        return jnp.dot(hb, w_ref[:, lo:lo + width], preferred_element_type=F32)

    bd = bd_ref[...]

    def head_norm(z, g):
        sq = z * z
        hi = sq.astype(BF16)
        lo = (sq - hi.astype(F32)).astype(BF16)
        ss = (jnp.dot(hi, bd, preferred_element_type=F32)
              + jnp.dot(lo, bd, preferred_element_type=F32))
        return z * lax.rsqrt(ss * (1.0 / A_HEAD_DIM) + RMS_EPS) * g

    qn = head_norm(proj(PK_Q, BRANCH_WIDTH), gq_ref[...]) * (A_HEAD_DIM ** -0.5 * LOG2E)
    kn = head_norm(proj(PK_K, BRANCH_WIDTH), gk_ref[...])
    zv = proj(PK_V, BRANCH_WIDTH)
    qk = qn * kn
    qk_hi = qk.astype(BF16)
    qk_lo = (qk - qk_hi.astype(F32)).astype(BF16)
    diag = (jnp.dot(qk_hi, sel_ref[...], preferred_element_type=F32)
            + jnp.dot(qk_lo, sel_ref[...], preferred_element_type=F32))

    logf = _log_sigmoid(proj(PK_F, LANES) + bf_ref[...])
    r_i = lax.broadcasted_iota(I32, (T, T), 0)
    c_i = lax.broadcasted_iota(I32, (T, T), 1)
    tri = (r_i >= c_i).astype(F32)
    c = jnp.dot(tri, logf, preferred_element_type=F32,
                precision=lax.Precision.HIGHEST) + carry_ref[...]
    carry_ref[...] = c[T - 1:T, :]
    c_ref[...] = c

    lane = lax.broadcasted_iota(I32, (T, LANES), 1)
    lm = lane & 63
    grp = lm >> 3
    hsel = lm & 7
    first3 = grp < 3
    second3 = (grp >= 3) & (grp < F_REP)

    def parts(v):
        hi, lo, lolo = _split3(v)
        return jnp.where((grp == 0) | (grp == 3), hi, jnp.where((grp == 1) | (grp == 4), lo, lolo))

    k_aug = jnp.where(first3, -parts(c * LOG2E), 0.0)
    q_aug = jnp.where(second3, parts(c * LOG2E - diag), 0.0)

    for g in range(A_HEADS // 2):
        sl = slice(g * LANES, (g + 1) * LANES)
        zq2, zk2, zv2 = qn[:, sl], kn[:, sl], zv[:, sl]
        for par in range(2):
            h = 2 * g + par
            keep = (lane < 64) if par == 0 else (lane >= 64)
            oh_k = (second3 & (hsel == h)).astype(F32)
            oh_q = (first3 & (hsel == h)).astype(F32)
            k_ref[h] = jnp.where(keep, zk2, k_aug + oh_k).astype(BF16)
            qT_ref[h] = jnp.where(keep, zq2, q_aug + oh_q).T.astype(BF16)
            ones_lane = 64 if par == 0 else 0
            vT_ref[h] = jnp.where(keep, zv2, (lane == ones_lane).astype(F32)).T.astype(BF16)

    p = proj(PK_P, BRANCH_WIDTH)
    row8 = lax.broadcasted_iota(I32, (8, GROUP_DIM), 0)
    pos = i * T + lax.broadcasted_iota(I32, (T, GROUP_DIM), 0)

    def shift_down(v, tail, d):
        r = pltpu.roll(v, d, 0)
        rt = pltpu.roll(tail, d, 0)
        top = jnp.where(row8 < d, rt[0:8], r[0:8])
        return jnp.concatenate([top, r[8:]], axis=0)

    pooled = []
    for gi, w in enumerate(POOL_WINDOWS):
        sl = slice(gi * GROUP_DIM, (gi + 1) * GROUP_DIM)
        s = p[:, sl]
        for lv in range(gi + 1):
            tail = halo_ref[lv, :, sl]
            halo_ref[lv, :, sl] = s[T - HALO:T, :]
            s = s + shift_down(s, tail, 1 << lv)
        cnt = jnp.minimum(pos + 1, w).astype(F32)
        pooled.append(s / cnt - p[:, sl])
    pooled = jnp.concatenate(pooled, axis=1).astype(BF16)
    yb = jnp.dot(pooled, wpool_ref[...], preferred_element_type=F32) * spool_ref[...]
    yb_ref[...] = yb.astype(yb_ref.dtype)

    gu = _gelu_tanh(proj(PK_U, BRANCH_WIDTH))
    gv = _gelu_tanh(proj(PK_SV, BRANCH_WIDTH))
    t_r = lax.broadcasted_iota(I32, (C_CHUNK, C_CHUNK), 0)
    t_c = lax.broadcasted_iota(I32, (C_CHUNK, C_CHUNK), 1)
    causal = t_r >= t_c
    for g in range(N_GROUPS):
        sl = slice(g * GROUP_DIM, (g + 1) * GROUP_DIM)
        vn = _rms(gv[:, sl], gsgu_ref[:, sl]).astype(BF16)
        wc = jnp.where(causal, wsgu_ref[g], 0.0).astype(BF16)
        for ch in range(T // C_CHUNK):
            rows = slice(ch * C_CHUNK, (ch + 1) * C_CHUNK)
            mixed = jnp.dot(wc, vn[rows], preferred_element_type=F32) + bsgu_ref[g]
            yc_ref[rows, sl] = (gu[rows, sl] * mixed).astype(yc_ref.dtype)


def _proj_call(x, w, T):
    B, S, D = x.shape
    H = A_HEADS
    const = _const_spec
    return pl.pallas_call(
        functools.partial(_proj_kernel, T=T),
        grid=(B, S // T),
        in_specs=[
            pl.BlockSpec((None, T, D), lambda b, i: (b, i, 0)),
            const(1, D), const(D, PK_W), const(1, LANES), const(1, BRANCH_WIDTH), const(1, BRANCH_WIDTH),
            const(BRANCH_WIDTH, BRANCH_WIDTH), const(BRANCH_WIDTH, LANES), const(BRANCH_WIDTH, BRANCH_WIDTH),
            const(1, BRANCH_WIDTH), const(1, BRANCH_WIDTH), const(N_GROUPS, C_CHUNK, C_CHUNK), const(N_GROUPS, C_CHUNK, GROUP_DIM),
        ],
        out_specs=[
            pl.BlockSpec((None, H, LANES, T), lambda b, i: (b, 0, 0, i)),
            pl.BlockSpec((None, H, T, LANES), lambda b, i: (b, 0, i, 0)),
            pl.BlockSpec((None, H, LANES, T), lambda b, i: (b, 0, 0, i)),
            pl.BlockSpec((None, T, BRANCH_WIDTH), lambda b, i: (b, i, 0)),
            pl.BlockSpec((None, T, BRANCH_WIDTH), lambda b, i: (b, i, 0)),
            pl.BlockSpec((None, T, LANES), lambda b, i: (b, i, 0)),
        ],
        out_shape=[
            jax.ShapeDtypeStruct((B, H, LANES, S), BF16),
            jax.ShapeDtypeStruct((B, H, S, LANES), BF16),
            jax.ShapeDtypeStruct((B, H, LANES, S), BF16),
            jax.ShapeDtypeStruct((B, S, BRANCH_WIDTH), BF16),
            jax.ShapeDtypeStruct((B, S, BRANCH_WIDTH), BF16),
            jax.ShapeDtypeStruct((B, S, LANES), F32),
        ],
        scratch_shapes=[pltpu.VMEM((1, LANES), F32), pltpu.VMEM((4, HALO, BRANCH_WIDTH), F32)],
        compiler_params=pltpu.CompilerParams(
            dimension_semantics=("arbitrary", "arbitrary"), vmem_limit_bytes=VMEM_LIMIT),
        name="proj",
    )(x, w["g_mix"], w["w_pack"], w["b_f"], w["g_q"], w["g_k"], w["bd"], w["sel"], w["w_pool"], w["s_pool"],
      w["g_sgu"], w["w_sgu"], w["b_sgu"])


def _attn_kernel(first_ref, qT_ref, k_ref, vT_ref, o_ref, m_ref, acc_ref, s_ref, *, tq, tk, n_q, online_max):
    assert tq == tk
    FIRST = 2

    def reset():
        acc_ref[...] = jnp.zeros_like(acc_ref)
        if online_max:
            m_ref[...] = jnp.full(m_ref.shape, NEG, F32)

    def scores(qi, j, slot):
        k0 = pl.multiple_of(j * tk, tk)
        q0 = pl.multiple_of(qi * tq, tq)
        for par in range(2):
            s_ref[slot, par] = jnp.dot(k_ref[par, pl.ds(k0, tk), :], qT_ref[par, :, pl.ds(q0, tq)],
                                       preferred_element_type=F32)

    def consume(qi, j, slot, masked):
        k0 = pl.multiple_of(j * tk, tk)
        q0 = qi * tq
        for par in range(2):
            s = s_ref[slot, par]
            if masked:
                kpos = k0 + lax.broadcasted_iota(I32, (tk, tq), 0)
                qpos = q0 + lax.broadcasted_iota(I32, (tk, tq), 1)
                s = jnp.where(kpos <= qpos, s, NEG)
            v_blk = vT_ref[par, :, pl.ds(k0, tk)]
            if online_max:
                m_old = m_ref[par]
                m_new = jnp.maximum(m_old, jnp.max(s, axis=0, keepdims=True))
                p = jnp.exp2(s - m_new).astype(BF16)
                pv = jnp.dot(v_blk, p, preferred_element_type=F32)
                acc_ref[par] = acc_ref[par] * jnp.exp2(m_old - m_new) + pv
                m_ref[par] = m_new
            else:
                p = jnp.exp2(s).astype(BF16)
                acc_ref[par] += jnp.dot(v_blk, p, preferred_element_type=F32)

    def finish(qi):
        lane = lax.broadcasted_iota(I32, (tq, LANES), 1)
        outs = []
        for par in range(2):
            acc = acc_ref[par]
            l = acc[64:65, :] if par == 0 else acc[0:1, :]
            outs.append((acc * (1.0 / l)).T)
        q0 = pl.multiple_of(qi * tq, tq)
        o_ref[pl.ds(q0, tq), :] = jnp.where(lane < 64, outs[0], outs[1]).astype(o_ref.dtype)
        reset()

    b_id = pl.program_id(0)
    g_id = pl.program_id(1)

    def first_block(qi):
        return first_ref[(b_id * pl.num_programs(1) + g_id) * n_q + qi]

    reset()
    scores(0, 0, FIRST)
    consume(0, 0, FIRST, True)
    finish(0)
    if n_q > 1:
        scores(1, first_block(1), FIRST)

    def query_block(qi, carry):
        base = first_block(qi)
        n_mid = qi - 1 - base
        nxt = jnp.minimum(qi + 1, n_q - 1)
        scores(qi, base + 1, 1)
        consume(qi, base, FIRST, False)

        def trip(t, c):
            j0 = base + 1 + ATTN_BLOCKS_PER_TRIP * t
            for u in range(ATTN_BLOCKS_PER_TRIP):
                scores(qi, j0 + u + 1, u % 2)
                consume(qi, j0 + u, (u + 1) % 2, False)
            return c

        lax.fori_loop(0, n_mid // ATTN_BLOCKS_PER_TRIP, trip, 0)
        j0 = base + 1 + (n_mid // ATTN_BLOCKS_PER_TRIP) * ATTN_BLOCKS_PER_TRIP
        for rem in range(ATTN_BLOCKS_PER_TRIP):
            @pl.when(n_mid % ATTN_BLOCKS_PER_TRIP == rem)
            def _(rem=rem):
                for u in range(rem):
                    scores(qi, j0 + u + 1, u % 2)
                    consume(qi, j0 + u, (u + 1) % 2, False)
                scores(nxt, first_block(nxt), FIRST)
                consume(qi, j0 + rem, (rem + 1) % 2, True)
                finish(qi)
        return carry

    lax.fori_loop(1, n_q, query_block, 0)


def _attn_call(first, qT, k, vT, tq, tk, online_max):
    B, H, _, S = qT.shape
    grid_spec = pltpu.PrefetchScalarGridSpec(
        num_scalar_prefetch=1,
        grid=(B, H // 2),
        in_specs=[
            pl.BlockSpec((None, 2, LANES, S), lambda b, g, first: (b, g, 0, 0)),
            pl.BlockSpec((None, 2, S, LANES), lambda b, g, first: (b, g, 0, 0)),
            pl.BlockSpec((None, 2, LANES, S), lambda b, g, first: (b, g, 0, 0)),
        ],
        out_specs=pl.BlockSpec((None, S, LANES), lambda b, g, first: (b, 0, g)),
        scratch_shapes=[pltpu.VMEM((2, 1, tq), F32), pltpu.VMEM((2, LANES, tq), F32),
                        pltpu.VMEM((3, 2, tk, tq), F32)],
    )
    return pl.pallas_call(
        functools.partial(_attn_kernel, tq=tq, tk=tk, n_q=S // tq, online_max=online_max),
        grid_spec=grid_spec,
        out_shape=jax.ShapeDtypeStruct((B, S, BRANCH_WIDTH), BF16),
        compiler_params=pltpu.CompilerParams(
            dimension_semantics=("parallel", "parallel"), vmem_limit_bytes=VMEM_LIMIT),
        name="attn_online" if online_max else "attn",
    )(first, qT, k, vT)


def _first_live_block(c, bound, tq, tk):
    c_q = c[:, 0::tq, 0:A_HEADS]
    c_k = c[:, tk - 1::tk, 0:A_HEADS]
    top = bound + LOG2E * (c_q[:, :, None, :] - c_k[:, None, :, :])
    dead = top < DEAD_SCORE
    dead_pair = dead[..., 0::2] & dead[..., 1::2]
    n_q = c_q.shape[1]
    n_dead = jnp.sum(jnp.cumprod(dead_pair.astype(I32), axis=2), axis=2)
    limit = jnp.maximum(jnp.arange(n_q, dtype=I32) - 1, 0)[None, :, None]
    first = jnp.minimum(n_dead, limit)
    return first.transpose(0, 2, 1).reshape(-1).astype(I32)


def _attention(qT, k, vT, c, g_q, g_k, tq, tk):
    bound = 16.0 * LOG2E * jnp.max(jnp.abs(g_q)) * jnp.max(jnp.abs(g_k))
    first = _first_live_block(c, bound, tq, tk)
    return lax.cond(bound <= SCORE_CAP,
                    lambda: _attn_call(first, qT, k, vT, tq, tk, False),
                    lambda: _attn_call(first, qT, k, vT, tq, tk, True))


def _merge_kernel(x_ref, ya_ref, yb_ref, yc_ref, gmix_ref, wg_ref, bg_ref, wb_ref, wo_ref,
                  gffn_ref, wr_ref, br_ref,
                  x1_ref, h2_ref, route_ref, route_t_ref, cnt_ref, carry_ref, logit_ref, *, T):
    i = pl.program_id(0)
    n_tiles = pl.num_programs(0) - 1

    @pl.when(i == 0)
    def _():
        carry_ref[...] = jnp.zeros_like(carry_ref)
        logit_ref[...] = jnp.zeros_like(logit_ref)

    def route_previous():
        live = (i > 0).astype(F32)
        logits = logit_ref[...]
        lane = lax.broadcasted_iota(I32, (T, LANES), 1).astype(F32)
        big = float(LANES)

        def first_argmax(v):
            m = jnp.max(v, axis=-1, keepdims=True)
            return m, jnp.min(jnp.where(v == m, lane, big), axis=-1, keepdims=True)

        lg = jnp.where(lane < N_EXPERT_GROUPS, logits, NEG)
        mg, grp = first_argmax(lg)
        p_grp = 1.0 / jnp.sum(jnp.exp(lg - mg), axis=-1, keepdims=True)
        lo_lane = N_EXPERT_GROUPS + grp * EXPERTS_PER_GROUP
        le = jnp.where((lane >= lo_lane) & (lane < lo_lane + EXPERTS_PER_GROUP), logits, NEG)
        m1, i1 = first_argmax(le)
        m2, i2 = first_argmax(jnp.where(lane == i1, NEG, le))
        e21 = jnp.exp(m2 - m1)
        g1 = p_grp / (1.0 + e21)
        g2 = p_grp * e21 / (1.0 + e21)
        e1 = i1 - N_EXPERT_GROUPS
        e2 = i2 - N_EXPERT_GROUPS

        oh1 = lane == e1
        oh2 = lane == e2
        sel = (oh1 | oh2).astype(F32)
        r_i = lax.broadcasted_iota(I32, (T, T), 0)
        c_i = lax.broadcasted_iota(I32, (T, T), 1)
        before = (r_i > c_i).astype(BF16)
        seen = jnp.dot(before, sel.astype(BF16), preferred_element_type=F32) + carry_ref[...]
        r1 = jnp.sum(jnp.where(oh1, seen, 0.0), axis=-1, keepdims=True)
        r2 = jnp.sum(jnp.where(oh2, seen, 0.0), axis=-1, keepdims=True)
        carry_ref[...] = carry_ref[...] + live * jnp.sum(sel, axis=0, keepdims=True)
        cnt_ref[...] = carry_ref[...]

        route = jnp.zeros((T, LANES), F32)
        for idx, val in enumerate((e1, e2, g1, g2, r1, r2)):
            route = jnp.where(lane == idx, val, route)
        route_ref[...] = route
        route_t_ref[...] = route.T[0:ROUTE_ROWS]

    @pl.when(i < n_tiles)
    def _():
        route_previous()
        x = x_ref[...]
        hb = _rms(x, gmix_ref[...]).astype(BF16)
        merged = None
        for bi, y_ref in enumerate((ya_ref, yb_ref, yc_ref)):
            sl = slice(bi * D_MODEL, (bi + 1) * D_MODEL)
            gate = jax.nn.sigmoid(jnp.dot(hb, wg_ref[:, sl], preferred_element_type=F32) + bg_ref[:, sl])
            term = gate * jnp.dot(y_ref[...], wb_ref[bi], preferred_element_type=F32)
            merged = term if merged is None else merged + term
        x1 = x + jnp.dot(merged.astype(BF16), wo_ref[...], preferred_element_type=F32)
        x1_ref[...] = x1
        h2 = _rms(x1, gffn_ref[...])
        _store_row_tiles(h2_ref, _pack_pairs(h2))

        h_hi = h2.astype(BF16)
        h_lo = (h2 - h_hi.astype(F32)).astype(BF16)
        hw = jnp.dot(h_hi, wr_ref[...], preferred_element_type=F32)
        lw = jnp.dot(h_lo, wr_ref[:, 0:LANES], preferred_element_type=F32)
        logit_ref[...] = hw[:, 0:LANES] + (hw[:, LANES:] + lw) + br_ref[...]

    @pl.when(i == n_tiles)
    def _():
        route_previous()


def _row_tile_rows(s, n_rows, row0=0):
    return pl.ds(row0 * ROW_TILE + s, n_rows, stride=ROW_TILE)


def _pack_pairs(v):
    hi = lax.bitcast_convert_type(v[:, :HALF_D].astype(BF16).astype(F32), U32)
    lo = lax.bitcast_convert_type(v[:, HALF_D:].astype(BF16).astype(F32), U32)
    return hi | (lo >> 16)


def _unpack_pairs(w):
    hi = lax.bitcast_convert_type(w & jnp.uint32(0xFFFF0000), F32)
    lo = lax.bitcast_convert_type(w << 16, F32)
    return jnp.concatenate([hi, lo], axis=-1)


def _store_row_tiles(ref, words, row0=0):
    for s in range(ROW_TILE):
        ref[_row_tile_rows(s, words.shape[0], row0), :] = words[:, s * LANES:(s + 1) * LANES]


def _load_row_tiles(ref, n_rows, row0=0):
    return jnp.concatenate([ref[_row_tile_rows(s, n_rows, row0), :] for s in range(ROW_TILE)], axis=-1)


def _merge_call(x2d, ya, yb, yc, w, T):
    N, D = x2d.shape
    n_tiles = N // T
    const = _const_spec
    cur = lambda i: jnp.minimum(i, n_tiles - 1)
    prev = lambda i: jnp.maximum(i - 1, 0)
    tile = lambda width: pl.BlockSpec((T, width), lambda i: (cur(i), 0))
    return pl.pallas_call(
        functools.partial(_merge_kernel, T=T),
        grid=(n_tiles + 1,),
        in_specs=[
            tile(D), tile(BRANCH_WIDTH), tile(BRANCH_WIDTH), tile(BRANCH_WIDTH),
            const(1, D), const(D, N_BRANCH * D), const(1, N_BRANCH * D),
            const(N_BRANCH, BRANCH_WIDTH, D), const(D, D), const(1, D), const(D, 2 * LANES), const(1, LANES),
        ],
        out_specs=[tile(D), pl.BlockSpec((T * ROW_TILE, LANES), lambda i: (cur(i), 0)),
                   pl.BlockSpec((T, LANES), lambda i: (prev(i), 0)),
                   pl.BlockSpec((ROUTE_ROWS, T), lambda i: (0, prev(i))),
                   pl.BlockSpec((1, LANES), lambda i: (0, 0))],
        out_shape=[
            jax.ShapeDtypeStruct((N, D), F32),
            jax.ShapeDtypeStruct((N * ROW_TILE, LANES), U32),
            jax.ShapeDtypeStruct((N, LANES), F32),
            jax.ShapeDtypeStruct((ROUTE_ROWS, N), F32),
            jax.ShapeDtypeStruct((1, LANES), F32),
        ],
        scratch_shapes=[pltpu.VMEM((1, LANES), F32), pltpu.VMEM((T, LANES), F32)],
        compiler_params=pltpu.CompilerParams(
            dimension_semantics=("arbitrary",), vmem_limit_bytes=VMEM_LIMIT),
        name="merge",
    )(x2d, ya, yb, yc, w["g_mix"], w["w_gate"], w["b_gate"], w["w_branch"], w["w_out"],
      w["g_ffn"], w["w_router"], w["b_router"])


def _dispatch_kernel(dest_ref, h2_ref, xs_ref, sem, *, T):
    def row_copy(t, d):
        return pltpu.make_async_copy(h2_ref.at[_row_tile(t)], xs_ref.at[_row_tile(d)], sem)

    def body(tb, carry):
        t0 = tb * ISSUE_UNROLL
        for u in range(ISSUE_UNROLL):
            for kk in range(TOP_K):
                row_copy(t0 + u, dest_ref[0, kk * T + t0 + u]).start(priority=kk % N_DMA_PRIORITIES)
        return carry

    lax.fori_loop(0, T // ISSUE_UNROLL, body, 0)
    for kk in range(TOP_K):
        pltpu.make_async_copy(h2_ref, xs_ref.at[pl.ds(0, T * ROW_TILE)], sem).wait()


def _row_tile(r):
    return pl.ds(pl.multiple_of(r * ROW_TILE, ROW_TILE), ROW_TILE)


def _dispatch_call(dest3, h2, T):
    N = h2.shape[0] // ROW_TILE
    return pl.pallas_call(
        functools.partial(_dispatch_kernel, T=T),
        grid=(N // T,),
        in_specs=[
            pl.BlockSpec((None, 1, TOP_K * T), lambda i: (i, 0, 0), memory_space=pltpu.SMEM),
            pl.BlockSpec((T * ROW_TILE, LANES), lambda i: (i, 0)),
        ],
        out_specs=pl.BlockSpec(memory_space=pl.ANY),
        out_shape=jax.ShapeDtypeStruct((TOP_K * N * ROW_TILE, LANES), U32),
        scratch_shapes=[pltpu.SemaphoreType.DMA],
        compiler_params=pltpu.CompilerParams(
            dimension_semantics=("arbitrary",), vmem_limit_bytes=VMEM_LIMIT),
        name="dispatch",
    )(dest3, h2)


def _gmm_kernel(blk_ref, exp_ref, lo_ref, hi_ref, xs_ref, w1_ref, w3_ref, w2_ref, y_ref,
                w13_bf, w2_bf, *, bm):
    i = pl.program_id(0)
    lo = lo_ref[i]
    hi = hi_ref[i]
    prev = jnp.maximum(i - 1, 0)
    first = jnp.logical_or(i == 0, blk_ref[i] != blk_ref[prev])
    new_expert = jnp.logical_or(i == 0, exp_ref[i] != exp_ref[prev])

    @pl.when(new_expert)
    def _():
        w13_bf[:, 0:D_EXPERT] = w1_ref[...].astype(BF16)
        w13_bf[:, D_EXPERT:] = w3_ref[...].astype(BF16)
        w2_bf[...] = w2_ref[...].astype(BF16)

    @pl.when(hi > lo)
    def _():
        n_part = 2
        part = bm // n_part
        outs = []
        for h in range(n_part):
            x = _unpack_pairs(_load_row_tiles(xs_ref, part, h * part)).astype(BF16)
            ab = jnp.dot(x, w13_bf[...], preferred_element_type=F32)
            a = ab[:, 0:D_EXPERT]
            b = ab[:, D_EXPERT:]
            mid = (a * jax.nn.sigmoid(a) * b).astype(BF16)
            outs.append(_pack_pairs(jnp.dot(mid, w2_bf[...], preferred_element_type=F32)))

        @pl.when(first)
        def _():
            for h in range(n_part):
                _store_row_tiles(y_ref, outs[h], h * part)

        @pl.when(jnp.logical_not(first))
        def _():
            for h in range(n_part):
                row = h * part + lax.broadcasted_iota(I32, (part, HALF_D), 0)
                mine = (row >= lo) & (row < hi)
                old = _load_row_tiles(y_ref, part, h * part)
                _store_row_tiles(y_ref, jnp.where(mine, outs[h], old), h * part)


def _gmm_call(plan, xs, w1, w3, w2, layer, bm):
    A = xs.shape[0] // ROW_TILE
    D = D_MODEL
    n_items = plan[0].shape[0]
    grid_spec = pltpu.PrefetchScalarGridSpec(
        num_scalar_prefetch=4,
        grid=(n_items,),
        in_specs=[
            pl.BlockSpec((bm * ROW_TILE, LANES), lambda i, blk, ex, lo, hi: (blk[i], 0)),
            pl.BlockSpec((None, None, D, D_EXPERT), lambda i, blk, ex, lo, hi: (layer, ex[i], 0, 0)),
            pl.BlockSpec((None, None, D, D_EXPERT), lambda i, blk, ex, lo, hi: (layer, ex[i], 0, 0)),
            pl.BlockSpec((None, None, D_EXPERT, D), lambda i, blk, ex, lo, hi: (layer, ex[i], 0, 0)),
        ],
        out_specs=pl.BlockSpec((bm * ROW_TILE, LANES), lambda i, blk, ex, lo, hi: (blk[i], 0)),
        scratch_shapes=[pltpu.VMEM((D, 2 * D_EXPERT), BF16), pltpu.VMEM((D_EXPERT, D), BF16)],
    )
    return pl.pallas_call(
        functools.partial(_gmm_kernel, bm=bm),
        grid_spec=grid_spec,
        out_shape=jax.ShapeDtypeStruct((A * ROW_TILE, LANES), U32),
        compiler_params=pltpu.CompilerParams(
            dimension_semantics=("arbitrary",), vmem_limit_bytes=VMEM_LIMIT),
        name="gmm",
    )(*plan, xs, w1, w3, w2)


def _gmm_plan(counts, n_rows, bm):
    n_blk = n_rows // bm
    n_items = n_blk + N_EXPERTS - 1
    ends = jnp.cumsum(counts)
    starts = ends - counts
    first_blk = starts // bm
    n_it = jnp.where(counts > 0, (ends - 1) // bm - first_blk + 1, 0)
    item_end = jnp.cumsum(n_it)
    item_start = item_end - n_it
    total = item_end[-1]
    ids = jnp.arange(n_items, dtype=I32)
    valid = ids < total
    ex = jnp.minimum(jnp.sum(item_end[None, :] <= ids[:, None], axis=1).astype(I32), N_EXPERTS - 1)
    ex = jnp.where(valid, ex, ex[jnp.maximum(total - 1, 0)])
    blk = jnp.where(valid, first_blk[ex] + ids - item_start[ex], n_blk - 1)
    lo = jnp.where(valid, jnp.maximum(starts[ex], blk * bm) - blk * bm, 0)
    hi = jnp.where(valid, jnp.minimum(ends[ex], (blk + 1) * bm) - blk * bm, 0)
    return blk.astype(I32), ex.astype(I32), lo.astype(I32), hi.astype(I32)


def _combine_kernel(dest_ref, dest_next_ref, x1_ref, route_ref, y_hbm, o_ref, buf, sem, *, T):
    i = pl.program_id(0)
    n_steps = pl.num_programs(0)

    def issue(d_ref, slot):
        def body(tb, carry):
            t0 = tb * ISSUE_UNROLL
            for u in range(ISSUE_UNROLL):
                for kk in range(TOP_K):
                    d = d_ref[0, kk * T + t0 + u]
                    pltpu.make_async_copy(y_hbm.at[_row_tile(d)], buf.at[slot * TOP_K + kk, _row_tile(t0 + u)],
                                          sem.at[slot]).start(priority=kk % N_DMA_PRIORITIES)
            return carry

        lax.fori_loop(0, T // ISSUE_UNROLL, body, 0)

    def finish(slot):
        for kk in range(TOP_K):
            pltpu.make_async_copy(y_hbm.at[pl.ds(0, T * ROW_TILE)], buf.at[slot * TOP_K + kk],
                                  sem.at[slot]).wait()
        g1 = route_ref[:, 2:3]
        g2 = route_ref[:, 3:4]
        for s in range(ROW_TILE):
            rows = _row_tile_rows(s, T)
            y1 = _unpack_pairs(buf[slot * TOP_K, rows, :])
            y2 = _unpack_pairs(buf[slot * TOP_K + 1, rows, :])
            for half in range(2):
                src = slice(half * LANES, (half + 1) * LANES)
                dst = slice(half * HALF_D + s * LANES, half * HALF_D + (s + 1) * LANES)
                o_ref[:, dst] = x1_ref[:, dst] + (g1 * y1[:, src] + g2 * y2[:, src])

    @pl.when(i == 0)
    def _():
        issue(dest_ref, 0)

    for slot in range(2):
        @pl.when(i % 2 == slot)
        def _(slot=slot):
            @pl.when(i + 1 < n_steps)
            def _():
                issue(dest_next_ref, 1 - slot)

            finish(slot)


def _combine_call(dest3, x1, route, y, T):
    N, D = x1.shape
    n_steps = N // T
    dest_spec = lambda index: pl.BlockSpec((None, 1, TOP_K * T), index, memory_space=pltpu.SMEM)
    return pl.pallas_call(
        functools.partial(_combine_kernel, T=T),
        grid=(n_steps,),
        in_specs=[
            dest_spec(lambda i: (i, 0, 0)),
            dest_spec(lambda i: (jnp.minimum(i + 1, n_steps - 1), 0, 0)),
            pl.BlockSpec((T, D), lambda i: (i, 0)),
            pl.BlockSpec((T, LANES), lambda i: (i, 0)),
            pl.BlockSpec(memory_space=pl.ANY),
        ],
        out_specs=pl.BlockSpec((T, D), lambda i: (i, 0)),
        out_shape=jax.ShapeDtypeStruct((N, D), F32),
        scratch_shapes=[pltpu.VMEM((2 * TOP_K, T * ROW_TILE, LANES), U32), pltpu.SemaphoreType.DMA((2,))],
        compiler_params=pltpu.CompilerParams(
            dimension_semantics=("arbitrary",), vmem_limit_bytes=VMEM_LIMIT),
        name="combine",
    )(dest3, dest3, x1, route, y)


def _block_diag(blocks):
    n = len(blocks)
    rows = []
    for i, b in enumerate(blocks):
        rows.append(jnp.concatenate(
            [b if j == i else jnp.zeros((b.shape[0], blocks[j].shape[1]), b.dtype) for j in range(n)], axis=1))
    return jnp.concatenate(rows, axis=0)


def _rep_forget(cols):
    half = jnp.concatenate(
        [jnp.tile(cols, (1, F_REP)), jnp.zeros((cols.shape[0], 64 - F_REP * A_HEADS), cols.dtype)], axis=1)
    return jnp.concatenate([half, half], axis=1)


def _head_to_gate_lanes():
    head = jnp.arange(BRANCH_WIDTH, dtype=I32)[:, None] // A_HEAD_DIM
    lane = jnp.arange(LANES, dtype=I32)[None, :]
    hit = ((lane & 7) == head) & (((lane & 63) >> 3) < F_REP)
    return hit.astype(BF16)


def _prep_layer(l, g_mix, w_in, b_fgate, b_gate, g_q, g_k, w_pool, s_pool, g_sgu, w_sgu, b_sgu,
                w_branch, w_out, g_ffn, w_rg, b_rg, w_re, b_re):
    wi = w_in[l]
    order = jnp.argsort(b_fgate[l])

    def by_head(cols):
        r = cols.shape[0]
        return jnp.take(cols.reshape(r, A_HEADS, A_HEAD_DIM), order, axis=1).reshape(r, BRANCH_WIDTH)

    qkv = [by_head(wi[:, i * BRANCH_WIDTH:(i + 1) * BRANCH_WIDTH]) for i in range(3)]
    w_f = jnp.take(wi[:, OFF_F:OFF_P], order, axis=1)
    b_f = jnp.take(b_fgate[l], order)
    w_branch_l = w_branch[l].at[0].set(
        jnp.take(w_branch[l][0].reshape(A_HEADS, A_HEAD_DIM, D_MODEL), order, axis=0).reshape(BRANCH_WIDTH, D_MODEL))
    w_pack = jnp.concatenate(qkv + [_rep_forget(w_f), wi[:, OFF_P:OFF_G]], axis=1).astype(BF16)
    pad_r = LANES - N_EXPERT_GROUPS - N_EXPERTS
    w_r = jnp.concatenate([w_rg[l], w_re[l], jnp.zeros((D_MODEL, pad_r), F32)], axis=1)
    w_r_hi = w_r.astype(BF16)
    w_r_lo = (w_r - w_r_hi.astype(F32)).astype(BF16)
    return dict(
        g_mix=g_mix[l][None, :],
        w_pack=w_pack,
        b_f=_rep_forget(b_f[None, :]),
        g_q=jnp.tile(g_q[l], A_HEADS)[None, :],
        g_k=jnp.tile(g_k[l], A_HEADS)[None, :],
        bd=_block_diag([jnp.ones((A_HEAD_DIM, A_HEAD_DIM), BF16)] * A_HEADS),
        sel=_head_to_gate_lanes(),
        w_pool=_block_diag([w_pool[l][g] for g in range(N_GROUPS)]).astype(BF16),
        s_pool=s_pool[l][None, :],
        g_sgu=g_sgu[l][None, :],
        w_sgu=w_sgu[l],
        b_sgu=jnp.broadcast_to(b_sgu[l][:, :, None], (N_GROUPS, C_CHUNK, GROUP_DIM)),
        w_gate=wi[:, OFF_G:].astype(BF16),
        b_gate=b_gate[l].reshape(1, N_BRANCH * D_MODEL),
        w_branch=w_branch_l.astype(BF16),
        w_out=w_out[l].astype(BF16),
        g_ffn=g_ffn[l][None, :],
        w_router=jnp.concatenate([w_r_hi, w_r_lo], axis=1),
        b_router=jnp.concatenate([b_rg[l], b_re[l], jnp.zeros((pad_r,), F32)])[None, :],
    )


def kernel(x, g_mix, w_in, b_fgate, b_gate, g_q, g_k, w_pool, s_pool, g_sgu, w_sgu, b_sgu, w_branch, w_out,
           g_ffn, w_rg, b_rg, w_re, b_re, w1, w3, w2):
    B, S, D = x.shape
    assert D == D_MODEL and x.dtype == F32
    N = B * S
    T, tq, tk, bm, tm = _tiles(S)
    assert S % T == 0 and S % tq == 0 and tq == tk and T % C_CHUNK == 0
    assert (TOP_K * N) % bm == 0 and N % tm == 0 and tm % ISSUE_UNROLL == 0
    depth = w_in.shape[0]
    for l in range(depth):
        w = _prep_layer(l, g_mix, w_in, b_fgate, b_gate, g_q, g_k, w_pool, s_pool, g_sgu, w_sgu, b_sgu,
                        w_branch, w_out, g_ffn, w_rg, b_rg, w_re, b_re)
        qT, k, vT, yb, yc, c = _proj_call(x, w, T)
        ya = _attention(qT, k, vT, c, g_q[l], g_k[l], tq, tk)
        x1, h2, route, route_t, cnt = _merge_call(
            x.reshape(N, D), ya.reshape(N, -1), yb.reshape(N, -1), yc.reshape(N, -1), w, T)
        counts = cnt[0, :N_EXPERTS].astype(I32)
        starts = jnp.cumsum(counts) - counts
        experts = route_t[0:TOP_K].astype(I32)
        expert_ids = jnp.arange(N_EXPERTS, dtype=I32)[:, None, None]
        start_of = jnp.sum(jnp.where(experts[None] == expert_ids, starts[:, None, None], 0), axis=0)
        dest = start_of + route_t[4:4 + TOP_K].astype(I32)
        dest3 = dest.reshape(TOP_K, N // tm, tm).transpose(1, 0, 2).reshape(N // tm, 1, TOP_K * tm)
        xs = _dispatch_call(dest3, h2, tm)
        y = _gmm_call(_gmm_plan(counts, TOP_K * N, bm), xs, w1, w3, w2, l, bm)
        x = _combine_call(dest3, x1, route, y, tm).reshape(B, S, D)
    return x
```

```python
import functools

import jax
import jax.numpy as jnp
from jax import lax
from jax.experimental import pallas as pl
from jax.experimental.pallas import tpu as pltpu

F32 = jnp.float32
BF16 = jnp.bfloat16
I32 = jnp.int32
U32 = jnp.uint32

D_MODEL = 1024
A_HEADS = 8
A_HEAD_DIM = 64
BRANCH_WIDTH = 512
N_BRANCH = 3
POOL_WINDOWS = (2, 4, 8, 16)
GROUP_DIM = 128
N_GROUPS = 4
C_CHUNK = 128
N_EXPERT_GROUPS = 4
EXPERTS_PER_GROUP = 8
N_EXPERTS = N_EXPERT_GROUPS * EXPERTS_PER_GROUP
TOP_K = 2
D_EXPERT = 256
RMS_EPS = 1e-6
OFF_F = 3 * BRANCH_WIDTH
OFF_P = OFF_F + A_HEADS
OFF_U = OFF_P + BRANCH_WIDTH
OFF_SV = OFF_U + BRANCH_WIDTH
OFF_G = OFF_SV + BRANCH_WIDTH

LANES = 128
SUBLANES = 8
V7X_VMEM_BYTES = 64 * 1024 * 1024
VMEM_LIMIT = 56 * 1024 * 1024
ROW_TILE = D_MODEL // (2 * LANES)
HALF_D = D_MODEL // 2
ROUTE_ROWS = SUBLANES
ISSUE_UNROLL = 8
N_DMA_PRIORITIES = 2
ATTN_BLOCKS_PER_TRIP = 4

PK_Q, PK_K, PK_V = 0, 512, 1024
PK_F = 1536
PK_P = PK_F + LANES
PK_U = PK_P + BRANCH_WIDTH
PK_SV = PK_U + BRANCH_WIDTH
PK_W = PK_SV + BRANCH_WIDTH
F_REP = 6

LOG2E = 1.4426950408889634
SCORE_CAP = 96.0
DEAD_SCORE = -160.0
NEG = -1e30
HALO = 16


def _tiles(seq_len):
    t_proj = min(512, seq_len)
    t_q = min(512, seq_len)
    t_k = min(512, seq_len)
    moe_block = 512
    t_move = min(1024, seq_len)
    return t_proj, t_q, t_k, moe_block, t_move


def _const_spec(*shape):
    zeros = (0,) * len(shape)
    return pl.BlockSpec(shape, lambda *_: zeros, pipeline_mode=pl.Buffered(1))


def _layer_spec(layer, *shape):
    index = (layer,) + (0,) * len(shape)
    return pl.BlockSpec((None,) + shape, lambda *_: index, pipeline_mode=pl.Buffered(1))


def _rms(x, g):
    return x * lax.rsqrt(jnp.mean(x * x, axis=-1, keepdims=True) + RMS_EPS) * g


def _gelu_tanh(x):
    cdf = 0.5 * (1.0 + jnp.tanh(0.7978845608028654 * (x + 0.044715 * (x * x * x))))
    return x * cdf


def _log_sigmoid(x):
    return jnp.minimum(x, 0.0) - jnp.log1p(jnp.exp(-jnp.abs(x)))


def _split3(c):
    hi = c.astype(BF16).astype(F32)
    r = c - hi
    lo = r.astype(BF16).astype(F32)
    lolo = (r - lo).astype(BF16).astype(F32)
    return hi, lo, lolo


def _proj_kernel(x_ref, gmix_ref, w_ref, bf_ref, gq_ref, gk_ref, bd_ref, sel_ref, wpool_ref, spool_ref,
                 gsgu_ref, wsgu_ref, bsgu_ref,
                 qT_ref, k_ref, vT_ref, yb_ref, yc_ref, c_ref,
                 carry_ref, halo_ref, *, T):
    i = pl.program_id(1)

    @pl.when(i == 0)
    def _():
        carry_ref[...] = jnp.zeros_like(carry_ref)
        halo_ref[...] = jnp.zeros_like(halo_ref)

    hb = _rms(x_ref[...], gmix_ref[...]).astype(BF16)

    def proj(lo, width):
        return jnp.dot(hb, w_ref[:, lo:lo + width], preferred_element_type=F32)

    bd = bd_ref[...]

    def head_norm(z, g):
        sq = z * z
        hi = sq.astype(BF16)
        lo = (sq - hi.astype(F32)).astype(BF16)
        ss = (jnp.dot(hi, bd, preferred_element_type=F32)
              + jnp.dot(lo, bd, preferred_element_type=F32))
        return z * lax.rsqrt(ss * (1.0 / A_HEAD_DIM) + RMS_EPS) * g

    qn = head_norm(proj(PK_Q, BRANCH_WIDTH), gq_ref[...]) * (A_HEAD_DIM ** -0.5 * LOG2E)
    kn = head_norm(proj(PK_K, BRANCH_WIDTH), gk_ref[...])
    zv = proj(PK_V, BRANCH_WIDTH)
    qk = qn * kn
    qk_hi = qk.astype(BF16)
    qk_lo = (qk - qk_hi.astype(F32)).astype(BF16)
    diag = (jnp.dot(qk_hi, sel_ref[...], preferred_element_type=F32)
            + jnp.dot(qk_lo, sel_ref[...], preferred_element_type=F32))

    logf = _log_sigmoid(proj(PK_F, LANES) + bf_ref[...])
    r_i = lax.broadcasted_iota(I32, (T, T), 0)
    c_i = lax.broadcasted_iota(I32, (T, T), 1)
    tri = (r_i >= c_i).astype(F32)
    c = jnp.dot(tri, logf, preferred_element_type=F32,
                precision=lax.Precision.HIGHEST) + carry_ref[...]
    carry_ref[...] = c[T - 1:T, :]
    c_ref[...] = c

    lane = lax.broadcasted_iota(I32, (T, LANES), 1)
    lm = lane & 63
    grp = lm >> 3
    hsel = lm & 7
    first3 = grp < 3
    second3 = (grp >= 3) & (grp < F_REP)

    def parts(v):
        hi, lo, lolo = _split3(v)
        return jnp.where((grp == 0) | (grp == 3), hi, jnp.where((grp == 1) | (grp == 4), lo, lolo))

    k_aug = jnp.where(first3, -parts(c * LOG2E), 0.0)
    q_aug = jnp.where(second3, parts(c * LOG2E - diag), 0.0)

    for g in range(A_HEADS // 2):
        sl = slice(g * LANES, (g + 1) * LANES)
        zq2, zk2, zv2 = qn[:, sl], kn[:, sl], zv[:, sl]
        for par in range(2):
            h = 2 * g + par
            keep = (lane < 64) if par == 0 else (lane >= 64)
            oh_k = (second3 & (hsel == h)).astype(F32)
            oh_q = (first3 & (hsel == h)).astype(F32)
            k_ref[h] = jnp.where(keep, zk2, k_aug + oh_k).astype(BF16)
            qT_ref[h] = jnp.where(keep, zq2, q_aug + oh_q).T.astype(BF16)
            ones_lane = 64 if par == 0 else 0
            vT_ref[h] = jnp.where(keep, zv2, (lane == ones_lane).astype(F32)).T.astype(BF16)

    p = proj(PK_P, BRANCH_WIDTH)
    row8 = lax.broadcasted_iota(I32, (8, GROUP_DIM), 0)
    pos = i * T + lax.broadcasted_iota(I32, (T, GROUP_DIM), 0)

    def shift_down(v, tail, d):
        r = pltpu.roll(v, d, 0)
        rt = pltpu.roll(tail, d, 0)
        top = jnp.where(row8 < d, rt[0:8], r[0:8])
        return jnp.concatenate([top, r[8:]], axis=0)

    pooled = []
    for gi, w in enumerate(POOL_WINDOWS):
        sl = slice(gi * GROUP_DIM, (gi + 1) * GROUP_DIM)
        s = p[:, sl]
        for lv in range(gi + 1):
            tail = halo_ref[lv, :, sl]
            halo_ref[lv, :, sl] = s[T - HALO:T, :]
            s = s + shift_down(s, tail, 1 << lv)
        cnt = jnp.minimum(pos + 1, w).astype(F32)
        pooled.append(s / cnt - p[:, sl])
    pooled = jnp.concatenate(pooled, axis=1).astype(BF16)
    yb = jnp.dot(pooled, wpool_ref[...], preferred_element_type=F32) * spool_ref[...]
    yb_ref[...] = yb.astype(yb_ref.dtype)

    gu = _gelu_tanh(proj(PK_U, BRANCH_WIDTH))
    gv = _gelu_tanh(proj(PK_SV, BRANCH_WIDTH))
    t_r = lax.broadcasted_iota(I32, (C_CHUNK, C_CHUNK), 0)
    t_c = lax.broadcasted_iota(I32, (C_CHUNK, C_CHUNK), 1)
    causal = t_r >= t_c
    for g in range(N_GROUPS):
        sl = slice(g * GROUP_DIM, (g + 1) * GROUP_DIM)
        vn = _rms(gv[:, sl], gsgu_ref[:, sl]).astype(BF16)
        wc = jnp.where(causal, wsgu_ref[g], 0.0).astype(BF16)
        for ch in range(T // C_CHUNK):
            rows = slice(ch * C_CHUNK, (ch + 1) * C_CHUNK)
            mixed = jnp.dot(wc, vn[rows], preferred_element_type=F32) + bsgu_ref[g]
            yc_ref[rows, sl] = (gu[rows, sl] * mixed).astype(yc_ref.dtype)


def _proj_call(x, w, layer, T):
    B, S, D = x.shape
    H = A_HEADS
    const = _const_spec
    per_layer = functools.partial(_layer_spec, layer)
    return pl.pallas_call(
        functools.partial(_proj_kernel, T=T),
        grid=(B, S // T),
        in_specs=[
            pl.BlockSpec((None, T, D), lambda b, i: (b, i, 0)),
            per_layer(1, D), per_layer(D, PK_W), per_layer(1, LANES), per_layer(1, BRANCH_WIDTH),
            per_layer(1, BRANCH_WIDTH), const(BRANCH_WIDTH, BRANCH_WIDTH), const(BRANCH_WIDTH, LANES),
            per_layer(BRANCH_WIDTH, BRANCH_WIDTH), per_layer(1, BRANCH_WIDTH), per_layer(1, BRANCH_WIDTH),
            per_layer(N_GROUPS, C_CHUNK, C_CHUNK), per_layer(N_GROUPS, C_CHUNK, GROUP_DIM),
        ],
        out_specs=[
            pl.BlockSpec((None, H, LANES, T), lambda b, i: (b, 0, 0, i)),
            pl.BlockSpec((None, H, T, LANES), lambda b, i: (b, 0, i, 0)),
            pl.BlockSpec((None, H, LANES, T), lambda b, i: (b, 0, 0, i)),
            pl.BlockSpec((None, T, BRANCH_WIDTH), lambda b, i: (b, i, 0)),
            pl.BlockSpec((None, T, BRANCH_WIDTH), lambda b, i: (b, i, 0)),
            pl.BlockSpec((None, T, LANES), lambda b, i: (b, i, 0)),
        ],
        out_shape=[
            jax.ShapeDtypeStruct((B, H, LANES, S), BF16),
            jax.ShapeDtypeStruct((B, H, S, LANES), BF16),
            jax.ShapeDtypeStruct((B, H, LANES, S), BF16),
            jax.ShapeDtypeStruct((B, S, BRANCH_WIDTH), BF16),
            jax.ShapeDtypeStruct((B, S, BRANCH_WIDTH), BF16),
            jax.ShapeDtypeStruct((B, S, LANES), F32),
        ],
        scratch_shapes=[pltpu.VMEM((1, LANES), F32), pltpu.VMEM((4, HALO, BRANCH_WIDTH), F32)],
        compiler_params=pltpu.CompilerParams(
            dimension_semantics=("arbitrary", "arbitrary"), vmem_limit_bytes=VMEM_LIMIT),
        name="proj",
    )(x, w["g_mix"], w["w_pack"], w["b_f"], w["g_q"], w["g_k"], w["bd"], w["sel"], w["w_pool"], w["s_pool"],
      w["g_sgu"], w["w_sgu"], w["b_sgu"])


def _attn_kernel(first_ref, qT_ref, k_ref, vT_ref, o_ref, m_ref, acc_ref, s_ref, *, tq, tk, n_q, online_max):
    assert tq == tk
    FIRST = 2

    def reset():
        acc_ref[...] = jnp.zeros_like(acc_ref)
        if online_max:
            m_ref[...] = jnp.full(m_ref.shape, NEG, F32)

    def scores(qi, j, slot):
        k0 = pl.multiple_of(j * tk, tk)
        q0 = pl.multiple_of(qi * tq, tq)
        for par in range(2):
            s_ref[slot, par] = jnp.dot(k_ref[par, pl.ds(k0, tk), :], qT_ref[par, :, pl.ds(q0, tq)],
                                       preferred_element_type=F32)

    def consume(qi, j, slot, masked):
        k0 = pl.multiple_of(j * tk, tk)
        q0 = qi * tq
        for par in range(2):
            s = s_ref[slot, par]
            if masked:
                kpos = k0 + lax.broadcasted_iota(I32, (tk, tq), 0)
                qpos = q0 + lax.broadcasted_iota(I32, (tk, tq), 1)
                s = jnp.where(kpos <= qpos, s, NEG)
            v_blk = vT_ref[par, :, pl.ds(k0, tk)]
            if online_max:
                m_old = m_ref[par]
                m_new = jnp.maximum(m_old, jnp.max(s, axis=0, keepdims=True))
                p = jnp.exp2(s - m_new).astype(BF16)
                pv = jnp.dot(v_blk, p, preferred_element_type=F32)
                acc_ref[par] = acc_ref[par] * jnp.exp2(m_old - m_new) + pv
                m_ref[par] = m_new
            else:
                p = jnp.exp2(s).astype(BF16)
                acc_ref[par] += jnp.dot(v_blk, p, preferred_element_type=F32)

    def finish(qi):
        lane = lax.broadcasted_iota(I32, (tq, LANES), 1)
        outs = []
        for par in range(2):
            acc = acc_ref[par]
            l = acc[64:65, :] if par == 0 else acc[0:1, :]
            outs.append((acc * (1.0 / l)).T)
        q0 = pl.multiple_of(qi * tq, tq)
        o_ref[pl.ds(q0, tq), :] = jnp.where(lane < 64, outs[0], outs[1]).astype(o_ref.dtype)
        reset()

    b_id = pl.program_id(0)
    g_id = pl.program_id(1)

    def first_block(qi):
        return first_ref[(b_id * pl.num_programs(1) + g_id) * n_q + qi]

    reset()
    scores(0, 0, FIRST)
    consume(0, 0, FIRST, True)
    finish(0)
    if n_q > 1:
        scores(1, first_block(1), FIRST)

    def query_block(qi, carry):
        base = first_block(qi)
        n_mid = qi - 1 - base
        nxt = jnp.minimum(qi + 1, n_q - 1)
        scores(qi, base + 1, 1)
        consume(qi, base, FIRST, False)

        def trip(t, c):
            j0 = base + 1 + ATTN_BLOCKS_PER_TRIP * t
            for u in range(ATTN_BLOCKS_PER_TRIP):
                scores(qi, j0 + u + 1, u % 2)
                consume(qi, j0 + u, (u + 1) % 2, False)
            return c

        lax.fori_loop(0, n_mid // ATTN_BLOCKS_PER_TRIP, trip, 0)
        j0 = base + 1 + (n_mid // ATTN_BLOCKS_PER_TRIP) * ATTN_BLOCKS_PER_TRIP
        for rem in range(ATTN_BLOCKS_PER_TRIP):
            @pl.when(n_mid % ATTN_BLOCKS_PER_TRIP == rem)
            def _(rem=rem):
                for u in range(rem):
                    scores(qi, j0 + u + 1, u % 2)
                    consume(qi, j0 + u, (u + 1) % 2, False)
                scores(nxt, first_block(nxt), FIRST)
                consume(qi, j0 + rem, (rem + 1) % 2, True)
                finish(qi)
        return carry

    lax.fori_loop(1, n_q, query_block, 0)


def _attn_call(first, qT, k, vT, tq, tk, online_max):
    B, H, _, S = qT.shape
    grid_spec = pltpu.PrefetchScalarGridSpec(
        num_scalar_prefetch=1,
        grid=(B, H // 2),
        in_specs=[
            pl.BlockSpec((None, 2, LANES, S), lambda b, g, first: (b, g, 0, 0)),
            pl.BlockSpec((None, 2, S, LANES), lambda b, g, first: (b, g, 0, 0)),
            pl.BlockSpec((None, 2, LANES, S), lambda b, g, first: (b, g, 0, 0)),
        ],
        out_specs=pl.BlockSpec((None, S, LANES), lambda b, g, first: (b, 0, g)),
        scratch_shapes=[pltpu.VMEM((2, 1, tq), F32), pltpu.VMEM((2, LANES, tq), F32),
                        pltpu.VMEM((3, 2, tk, tq), F32)],
    )
    return pl.pallas_call(
        functools.partial(_attn_kernel, tq=tq, tk=tk, n_q=S // tq, online_max=online_max),
        grid_spec=grid_spec,
        out_shape=jax.ShapeDtypeStruct((B, S, BRANCH_WIDTH), BF16),
        compiler_params=pltpu.CompilerParams(
            dimension_semantics=("parallel", "parallel"), vmem_limit_bytes=VMEM_LIMIT),
        name="attn_online" if online_max else "attn",
    )(first, qT, k, vT)


def _first_live_block(c, bound, tq, tk):
    c_q = c[:, 0::tq, 0:A_HEADS]
    c_k = c[:, tk - 1::tk, 0:A_HEADS]
    top = bound + LOG2E * (c_q[:, :, None, :] - c_k[:, None, :, :])
    dead = top < DEAD_SCORE
    dead_pair = dead[..., 0::2] & dead[..., 1::2]
    n_q = c_q.shape[1]
    n_dead = jnp.sum(jnp.cumprod(dead_pair.astype(I32), axis=2), axis=2)
    limit = jnp.maximum(jnp.arange(n_q, dtype=I32) - 1, 0)[None, :, None]
    first = jnp.minimum(n_dead, limit)
    return first.transpose(0, 2, 1).reshape(-1).astype(I32)


def _attention(qT, k, vT, c, g_q, g_k, tq, tk):
    bound = 16.0 * LOG2E * jnp.max(jnp.abs(g_q)) * jnp.max(jnp.abs(g_k))
    first = _first_live_block(c, bound, tq, tk)
    return lax.cond(bound <= SCORE_CAP,
                    lambda: _attn_call(first, qT, k, vT, tq, tk, False),
                    lambda: _attn_call(first, qT, k, vT, tq, tk, True))


def _merge_kernel(x_ref, ya_ref, yb_ref, yc_ref, gmix_ref, wg_ref, bg_ref, wb_ref, wo_ref,
                  gffn_ref, wr_ref, br_ref,
                  x1_ref, h2_ref, route_ref, route_t_ref, cnt_ref, carry_ref, logit_ref, *, T):
    i = pl.program_id(0)
    n_tiles = pl.num_programs(0) - 1

    @pl.when(i == 0)
    def _():
        carry_ref[...] = jnp.zeros_like(carry_ref)
        logit_ref[...] = jnp.zeros_like(logit_ref)

    def route_previous():
        live = (i > 0).astype(F32)
        logits = logit_ref[...]
        lane = lax.broadcasted_iota(I32, (T, LANES), 1).astype(F32)
        big = float(LANES)

        def first_argmax(v):
            m = jnp.max(v, axis=-1, keepdims=True)
            return m, jnp.min(jnp.where(v == m, lane, big), axis=-1, keepdims=True)

        lg = jnp.where(lane < N_EXPERT_GROUPS, logits, NEG)
        mg, grp = first_argmax(lg)
        p_grp = 1.0 / jnp.sum(jnp.exp(lg - mg), axis=-1, keepdims=True)
        lo_lane = N_EXPERT_GROUPS + grp * EXPERTS_PER_GROUP
        le = jnp.where((lane >= lo_lane) & (lane < lo_lane + EXPERTS_PER_GROUP), logits, NEG)
        m1, i1 = first_argmax(le)
        m2, i2 = first_argmax(jnp.where(lane == i1, NEG, le))
        e21 = jnp.exp(m2 - m1)
        g1 = p_grp / (1.0 + e21)
        g2 = p_grp * e21 / (1.0 + e21)
        e1 = i1 - N_EXPERT_GROUPS
        e2 = i2 - N_EXPERT_GROUPS

        oh1 = lane == e1
        oh2 = lane == e2
        sel = (oh1 | oh2).astype(F32)
        r_i = lax.broadcasted_iota(I32, (T, T), 0)
        c_i = lax.broadcasted_iota(I32, (T, T), 1)
        before = (r_i > c_i).astype(BF16)
        seen = jnp.dot(before, sel.astype(BF16), preferred_element_type=F32) + carry_ref[...]
        r1 = jnp.sum(jnp.where(oh1, seen, 0.0), axis=-1, keepdims=True)
        r2 = jnp.sum(jnp.where(oh2, seen, 0.0), axis=-1, keepdims=True)
        carry_ref[...] = carry_ref[...] + live * jnp.sum(sel, axis=0, keepdims=True)
        cnt_ref[...] = carry_ref[...]

        route = jnp.zeros((T, LANES), F32)
        for idx, val in enumerate((e1, e2, g1, g2, r1, r2)):
            route = jnp.where(lane == idx, val, route)
        route_ref[...] = route
        route_t_ref[...] = route.T[0:ROUTE_ROWS]

    @pl.when(i < n_tiles)
    def _():
        route_previous()
        x = x_ref[...]
        hb = _rms(x, gmix_ref[...]).astype(BF16)
        merged = None
        for bi, y_ref in enumerate((ya_ref, yb_ref, yc_ref)):
            sl = slice(bi * D_MODEL, (bi + 1) * D_MODEL)
            gate = jax.nn.sigmoid(jnp.dot(hb, wg_ref[:, sl], preferred_element_type=F32) + bg_ref[:, sl])
            term = gate * jnp.dot(y_ref[...], wb_ref[bi], preferred_element_type=F32)
            merged = term if merged is None else merged + term
        x1 = x + jnp.dot(merged.astype(BF16), wo_ref[...], preferred_element_type=F32)
        x1_ref[...] = x1
        h2 = _rms(x1, gffn_ref[...])
        _store_row_tiles(h2_ref, _pack_pairs(h2))

        h_hi = h2.astype(BF16)
        h_lo = (h2 - h_hi.astype(F32)).astype(BF16)
        hw = jnp.dot(h_hi, wr_ref[...], preferred_element_type=F32)
        lw = jnp.dot(h_lo, wr_ref[:, 0:LANES], preferred_element_type=F32)
        logit_ref[...] = hw[:, 0:LANES] + (hw[:, LANES:] + lw) + br_ref[...]

    @pl.when(i == n_tiles)
    def _():
        route_previous()


def _row_tile_rows(s, n_rows, row0=0):
    return pl.ds(row0 * ROW_TILE + s, n_rows, stride=ROW_TILE)


def _pack_pairs(v):
    hi = lax.bitcast_convert_type(v[:, :HALF_D].astype(BF16).astype(F32), U32)
    lo = lax.bitcast_convert_type(v[:, HALF_D:].astype(BF16).astype(F32), U32)
    return hi | (lo >> 16)


def _unpack_pairs(w):
    hi = lax.bitcast_convert_type(w & jnp.uint32(0xFFFF0000), F32)
    lo = lax.bitcast_convert_type(w << 16, F32)
    return jnp.concatenate([hi, lo], axis=-1)


def _store_row_tiles(ref, words, row0=0):
    for s in range(ROW_TILE):
        ref[_row_tile_rows(s, words.shape[0], row0), :] = words[:, s * LANES:(s + 1) * LANES]


def _load_row_tiles(ref, n_rows, row0=0):
    return jnp.concatenate([ref[_row_tile_rows(s, n_rows, row0), :] for s in range(ROW_TILE)], axis=-1)


def _merge_call(x2d, ya, yb, yc, w, layer, T):
    N, D = x2d.shape
    n_tiles = N // T
    const = functools.partial(_layer_spec, layer)
    cur = lambda i: jnp.minimum(i, n_tiles - 1)
    prev = lambda i: jnp.maximum(i - 1, 0)
    tile = lambda width: pl.BlockSpec((T, width), lambda i: (cur(i), 0))
    return pl.pallas_call(
        functools.partial(_merge_kernel, T=T),
        grid=(n_tiles + 1,),
        in_specs=[
            tile(D), tile(BRANCH_WIDTH), tile(BRANCH_WIDTH), tile(BRANCH_WIDTH),
            const(1, D), const(D, N_BRANCH * D), const(1, N_BRANCH * D),
            const(N_BRANCH, BRANCH_WIDTH, D), const(D, D), const(1, D), const(D, 2 * LANES), const(1, LANES),
        ],
        out_specs=[tile(D), pl.BlockSpec((T * ROW_TILE, LANES), lambda i: (cur(i), 0)),
                   pl.BlockSpec((T, LANES), lambda i: (prev(i), 0)),
                   pl.BlockSpec((ROUTE_ROWS, T), lambda i: (0, prev(i))),
                   pl.BlockSpec((1, LANES), lambda i: (0, 0))],
        out_shape=[
            jax.ShapeDtypeStruct((N, D), F32),
            jax.ShapeDtypeStruct((N * ROW_TILE, LANES), U32),
            jax.ShapeDtypeStruct((N, LANES), F32),
            jax.ShapeDtypeStruct((ROUTE_ROWS, N), F32),
            jax.ShapeDtypeStruct((1, LANES), F32),
        ],
        scratch_shapes=[pltpu.VMEM((1, LANES), F32), pltpu.VMEM((T, LANES), F32)],
        compiler_params=pltpu.CompilerParams(
            dimension_semantics=("arbitrary",), vmem_limit_bytes=VMEM_LIMIT),
        name="merge",
    )(x2d, ya, yb, yc, w["g_mix"], w["w_gate"], w["b_gate"], w["w_branch"], w["w_out"],
      w["g_ffn"], w["w_router"], w["b_router"])


def _dispatch_kernel(dest_ref, h2_ref, xs_ref, sem, *, T):
    def row_copy(t, d):
        return pltpu.make_async_copy(h2_ref.at[_row_tile(t)], xs_ref.at[_row_tile(d)], sem)

    def body(tb, carry):
        t0 = tb * ISSUE_UNROLL
        for u in range(ISSUE_UNROLL):
            for kk in range(TOP_K):
                row_copy(t0 + u, dest_ref[0, kk * T + t0 + u]).start(priority=kk % N_DMA_PRIORITIES)
        return carry

    lax.fori_loop(0, T // ISSUE_UNROLL, body, 0)
    for kk in range(TOP_K):
        pltpu.make_async_copy(h2_ref, xs_ref.at[pl.ds(0, T * ROW_TILE)], sem).wait()


def _row_tile(r):
    return pl.ds(pl.multiple_of(r * ROW_TILE, ROW_TILE), ROW_TILE)


def _dispatch_call(dest3, h2, T):
    N = h2.shape[0] // ROW_TILE
    return pl.pallas_call(
        functools.partial(_dispatch_kernel, T=T),
        grid=(N // T,),
        in_specs=[
            pl.BlockSpec((None, 1, TOP_K * T), lambda i: (i, 0, 0), memory_space=pltpu.SMEM),
            pl.BlockSpec((T * ROW_TILE, LANES), lambda i: (i, 0)),
        ],
        out_specs=pl.BlockSpec(memory_space=pl.ANY),
        out_shape=jax.ShapeDtypeStruct((TOP_K * N * ROW_TILE, LANES), U32),
        scratch_shapes=[pltpu.SemaphoreType.DMA],
        compiler_params=pltpu.CompilerParams(
            dimension_semantics=("arbitrary",), vmem_limit_bytes=VMEM_LIMIT),
        name="dispatch",
    )(dest3, h2)


def _gmm_kernel(blk_ref, exp_ref, lo_ref, hi_ref, xs_ref, w1_ref, w3_ref, w2_ref, y_ref,
                w13_bf, w2_bf, *, bm):
    i = pl.program_id(0)
    lo = lo_ref[i]
    hi = hi_ref[i]
    prev = jnp.maximum(i - 1, 0)
    first = jnp.logical_or(i == 0, blk_ref[i] != blk_ref[prev])
    new_expert = jnp.logical_or(i == 0, exp_ref[i] != exp_ref[prev])

    @pl.when(new_expert)
    def _():
        w13_bf[:, 0:D_EXPERT] = w1_ref[...].astype(BF16)
        w13_bf[:, D_EXPERT:] = w3_ref[...].astype(BF16)
        w2_bf[...] = w2_ref[...].astype(BF16)

    @pl.when(hi > lo)
    def _():
        n_part = 2
        part = bm // n_part
        outs = []
        for h in range(n_part):
            x = _unpack_pairs(_load_row_tiles(xs_ref, part, h * part)).astype(BF16)
            ab = jnp.dot(x, w13_bf[...], preferred_element_type=F32)
            a = ab[:, 0:D_EXPERT]
            b = ab[:, D_EXPERT:]
            mid = (a * jax.nn.sigmoid(a) * b).astype(BF16)
            outs.append(_pack_pairs(jnp.dot(mid, w2_bf[...], preferred_element_type=F32)))

        @pl.when(first)
        def _():
            for h in range(n_part):
                _store_row_tiles(y_ref, outs[h], h * part)

        @pl.when(jnp.logical_not(first))
        def _():
            for h in range(n_part):
                row = h * part + lax.broadcasted_iota(I32, (part, HALF_D), 0)
                mine = (row >= lo) & (row < hi)
                old = _load_row_tiles(y_ref, part, h * part)
                _store_row_tiles(y_ref, jnp.where(mine, outs[h], old), h * part)


def _gmm_call(plan, xs, w1, w3, w2, layer, bm):
    A = xs.shape[0] // ROW_TILE
    D = D_MODEL
    n_items = plan[0].shape[0]
    grid_spec = pltpu.PrefetchScalarGridSpec(
        num_scalar_prefetch=4,
        grid=(n_items,),
        in_specs=[
            pl.BlockSpec((bm * ROW_TILE, LANES), lambda i, blk, ex, lo, hi: (blk[i], 0)),
            pl.BlockSpec((None, None, D, D_EXPERT), lambda i, blk, ex, lo, hi: (layer, ex[i], 0, 0)),
            pl.BlockSpec((None, None, D, D_EXPERT), lambda i, blk, ex, lo, hi: (layer, ex[i], 0, 0)),
            pl.BlockSpec((None, None, D_EXPERT, D), lambda i, blk, ex, lo, hi: (layer, ex[i], 0, 0)),
        ],
        out_specs=pl.BlockSpec((bm * ROW_TILE, LANES), lambda i, blk, ex, lo, hi: (blk[i], 0)),
        scratch_shapes=[pltpu.VMEM((D, 2 * D_EXPERT), BF16), pltpu.VMEM((D_EXPERT, D), BF16)],
    )
    return pl.pallas_call(
        functools.partial(_gmm_kernel, bm=bm),
        grid_spec=grid_spec,
        out_shape=jax.ShapeDtypeStruct((A * ROW_TILE, LANES), U32),
        compiler_params=pltpu.CompilerParams(
            dimension_semantics=("arbitrary",), vmem_limit_bytes=VMEM_LIMIT),
        name="gmm",
    )(*plan, xs, w1, w3, w2)


def _gmm_plan(counts, n_rows, bm):
    n_blk = n_rows // bm
    n_items = n_blk + N_EXPERTS - 1
    ends = jnp.cumsum(counts)
    starts = ends - counts
    first_blk = starts // bm
    n_it = jnp.where(counts > 0, (ends - 1) // bm - first_blk + 1, 0)
    item_end = jnp.cumsum(n_it)
    item_start = item_end - n_it
    total = item_end[-1]
    ids = jnp.arange(n_items, dtype=I32)
    valid = ids < total
    ex = jnp.minimum(jnp.sum(item_end[None, :] <= ids[:, None], axis=1).astype(I32), N_EXPERTS - 1)
    ex = jnp.where(valid, ex, ex[jnp.maximum(total - 1, 0)])
    blk = jnp.where(valid, first_blk[ex] + ids - item_start[ex], n_blk - 1)
    lo = jnp.where(valid, jnp.maximum(starts[ex], blk * bm) - blk * bm, 0)
    hi = jnp.where(valid, jnp.minimum(ends[ex], (blk + 1) * bm) - blk * bm, 0)
    return blk.astype(I32), ex.astype(I32), lo.astype(I32), hi.astype(I32)


def _combine_kernel(dest_ref, dest_next_ref, x1_ref, route_ref, y_hbm, o_ref, buf, sem, *, T):
    i = pl.program_id(0)
    n_steps = pl.num_programs(0)

    def issue(d_ref, slot):
        def body(tb, carry):
            t0 = tb * ISSUE_UNROLL
            for u in range(ISSUE_UNROLL):
                for kk in range(TOP_K):
                    d = d_ref[0, kk * T + t0 + u]
                    pltpu.make_async_copy(y_hbm.at[_row_tile(d)], buf.at[slot * TOP_K + kk, _row_tile(t0 + u)],
                                          sem.at[slot]).start(priority=kk % N_DMA_PRIORITIES)
            return carry

        lax.fori_loop(0, T // ISSUE_UNROLL, body, 0)

    def finish(slot):
        for kk in range(TOP_K):
            pltpu.make_async_copy(y_hbm.at[pl.ds(0, T * ROW_TILE)], buf.at[slot * TOP_K + kk],
                                  sem.at[slot]).wait()
        g1 = route_ref[:, 2:3]
        g2 = route_ref[:, 3:4]
        for s in range(ROW_TILE):
            rows = _row_tile_rows(s, T)
            y1 = _unpack_pairs(buf[slot * TOP_K, rows, :])
            y2 = _unpack_pairs(buf[slot * TOP_K + 1, rows, :])
            for half in range(2):
                src = slice(half * LANES, (half + 1) * LANES)
                dst = slice(half * HALF_D + s * LANES, half * HALF_D + (s + 1) * LANES)
                o_ref[:, dst] = x1_ref[:, dst] + (g1 * y1[:, src] + g2 * y2[:, src])

    @pl.when(i == 0)
    def _():
        issue(dest_ref, 0)

    for slot in range(2):
        @pl.when(i % 2 == slot)
        def _(slot=slot):
            @pl.when(i + 1 < n_steps)
            def _():
                issue(dest_next_ref, 1 - slot)

            finish(slot)


def _combine_call(dest3, x1, route, y, T):
    N, D = x1.shape
    n_steps = N // T
    dest_spec = lambda index: pl.BlockSpec((None, 1, TOP_K * T), index, memory_space=pltpu.SMEM)
    return pl.pallas_call(
        functools.partial(_combine_kernel, T=T),
        grid=(n_steps,),
        in_specs=[
            dest_spec(lambda i: (i, 0, 0)),
            dest_spec(lambda i: (jnp.minimum(i + 1, n_steps - 1), 0, 0)),
            pl.BlockSpec((T, D), lambda i: (i, 0)),
            pl.BlockSpec((T, LANES), lambda i: (i, 0)),
            pl.BlockSpec(memory_space=pl.ANY),
        ],
        out_specs=pl.BlockSpec((T, D), lambda i: (i, 0)),
        out_shape=jax.ShapeDtypeStruct((N, D), F32),
        scratch_shapes=[pltpu.VMEM((2 * TOP_K, T * ROW_TILE, LANES), U32), pltpu.SemaphoreType.DMA((2,))],
        compiler_params=pltpu.CompilerParams(
            dimension_semantics=("arbitrary",), vmem_limit_bytes=VMEM_LIMIT),
        name="combine",
    )(dest3, dest3, x1, route, y)


def _block_diag(blocks):
    n = blocks.shape[1]
    eye = jnp.eye(n, dtype=blocks.dtype)
    out = blocks[:, :, :, None, :] * eye[None, :, None, :, None]
    return out.reshape(blocks.shape[0], n * blocks.shape[2], n * blocks.shape[3])


def _rep_forget(cols):
    lead = cols.shape[:-1]
    half = jnp.concatenate(
        [jnp.tile(cols, (1,) * len(lead) + (F_REP,)), jnp.zeros(lead + (64 - F_REP * A_HEADS,), cols.dtype)],
        axis=-1)
    return jnp.concatenate([half, half], axis=-1)


def _head_to_gate_lanes():
    head = jnp.arange(BRANCH_WIDTH, dtype=I32)[:, None] // A_HEAD_DIM
    lane = jnp.arange(LANES, dtype=I32)[None, :]
    hit = ((lane & 7) == head) & (((lane & 63) >> 3) < F_REP)
    return hit.astype(BF16)


def _prep_all(g_mix, w_in, b_fgate, b_gate, g_q, g_k, w_pool, s_pool, g_sgu, w_sgu, b_sgu,
              w_branch, w_out, g_ffn, w_rg, b_rg, w_re, b_re):
    L = w_in.shape[0]
    order = jnp.argsort(b_fgate, axis=1)
    qkv = w_in[:, :, 0:OFF_F].reshape(L, D_MODEL, 3, A_HEADS, A_HEAD_DIM)
    qkv = jnp.take_along_axis(qkv, order[:, None, None, :, None], axis=3).reshape(L, D_MODEL, OFF_F)
    w_f = jnp.take_along_axis(w_in[:, :, OFF_F:OFF_P], order[:, None, :], axis=2)
    b_f = jnp.take_along_axis(b_fgate, order, axis=1)
    wb0 = w_branch[:, 0].reshape(L, A_HEADS, A_HEAD_DIM, D_MODEL)
    wb0 = jnp.take_along_axis(wb0, order[:, :, None, None], axis=1).reshape(L, 1, BRANCH_WIDTH, D_MODEL)
    pad_r = LANES - N_EXPERT_GROUPS - N_EXPERTS
    w_r = jnp.concatenate([w_rg, w_re, jnp.zeros((L, D_MODEL, pad_r), F32)], axis=2)
    w_r_hi = w_r.astype(BF16)
    w_r_lo = (w_r - w_r_hi.astype(F32)).astype(BF16)
    row = lambda v: v.reshape(L, 1, v.shape[-1])
    return dict(
        g_mix=row(g_mix),
        w_pack=jnp.concatenate([qkv, _rep_forget(w_f), w_in[:, :, OFF_P:OFF_G]], axis=2).astype(BF16),
        b_f=_rep_forget(row(b_f)),
        g_q=row(jnp.tile(g_q, (1, A_HEADS))),
        g_k=row(jnp.tile(g_k, (1, A_HEADS))),
        bd=_block_diag(jnp.ones((1, A_HEADS, A_HEAD_DIM, A_HEAD_DIM), BF16))[0],
        sel=_head_to_gate_lanes(),
        w_pool=_block_diag(w_pool).astype(BF16),
        s_pool=row(s_pool),
        g_sgu=row(g_sgu),
        w_sgu=w_sgu,
        b_sgu=jnp.broadcast_to(b_sgu[:, :, :, None], (L, N_GROUPS, C_CHUNK, GROUP_DIM)),
        w_gate=w_in[:, :, OFF_G:].astype(BF16),
        b_gate=b_gate.reshape(L, 1, N_BRANCH * D_MODEL),
        w_branch=jnp.concatenate([wb0, w_branch[:, 1:]], axis=1).astype(BF16),
        w_out=w_out.astype(BF16),
        g_ffn=row(g_ffn),
        w_router=jnp.concatenate([w_r_hi, w_r_lo], axis=2),
        b_router=row(jnp.concatenate([b_rg, b_re, jnp.zeros((L, pad_r), F32)], axis=1)),
    )


def kernel(x, g_mix, w_in, b_fgate, b_gate, g_q, g_k, w_pool, s_pool, g_sgu, w_sgu, b_sgu, w_branch, w_out,
           g_ffn, w_rg, b_rg, w_re, b_re, w1, w3, w2):
    B, S, D = x.shape
    assert D == D_MODEL and x.dtype == F32
    N = B * S
    T, tq, tk, bm, tm = _tiles(S)
    assert S % T == 0 and S % tq == 0 and tq == tk and T % C_CHUNK == 0
    assert (TOP_K * N) % bm == 0 and N % tm == 0 and tm % ISSUE_UNROLL == 0
    depth = w_in.shape[0]
    w = _prep_all(g_mix, w_in, b_fgate, b_gate, g_q, g_k, w_pool, s_pool, g_sgu, w_sgu, b_sgu,
                  w_branch, w_out, g_ffn, w_rg, b_rg, w_re, b_re)
    for l in range(depth):
        qT, k, vT, yb, yc, c = _proj_call(x, w, l, T)
        ya = _attention(qT, k, vT, c, g_q[l], g_k[l], tq, tk)
        x1, h2, route, route_t, cnt = _merge_call(
            x.reshape(N, D), ya.reshape(N, -1), yb.reshape(N, -1), yc.reshape(N, -1), w, l, T)
        counts = cnt[0, :N_EXPERTS].astype(I32)
        starts = jnp.cumsum(counts) - counts
        experts = route_t[0:TOP_K].astype(I32)
        expert_ids = jnp.arange(N_EXPERTS, dtype=I32)[:, None, None]
        start_of = jnp.sum(jnp.where(experts[None] == expert_ids, starts[:, None, None], 0), axis=0)
        dest = start_of + route_t[4:4 + TOP_K].astype(I32)
        dest3 = dest.reshape(TOP_K, N // tm, tm).transpose(1, 0, 2).reshape(N // tm, 1, TOP_K * tm)
        xs = _dispatch_call(dest3, h2, tm)
        y = _gmm_call(_gmm_plan(counts, TOP_K * N, bm), xs, w1, w3, w2, l, bm)
        x = _combine_call(dest3, x1, route, y, tm).reshape(B, S, D)
    return x
```

```python
import functools

import jax
import jax.numpy as jnp
from jax import lax
from jax.experimental import pallas as pl
from jax.experimental.pallas import tpu as pltpu

F32 = jnp.float32
BF16 = jnp.bfloat16
I32 = jnp.int32
U32 = jnp.uint32

D_MODEL = 1024
A_HEADS = 8
A_HEAD_DIM = 64
BRANCH_WIDTH = 512
N_BRANCH = 3
POOL_WINDOWS = (2, 4, 8, 16)
GROUP_DIM = 128
N_GROUPS = 4
C_CHUNK = 128
N_EXPERT_GROUPS = 4
EXPERTS_PER_GROUP = 8
N_EXPERTS = N_EXPERT_GROUPS * EXPERTS_PER_GROUP
TOP_K = 2
D_EXPERT = 256
RMS_EPS = 1e-6
OFF_F = 3 * BRANCH_WIDTH
OFF_P = OFF_F + A_HEADS
OFF_U = OFF_P + BRANCH_WIDTH
OFF_SV = OFF_U + BRANCH_WIDTH
OFF_G = OFF_SV + BRANCH_WIDTH

LANES = 128
SUBLANES = 8
V7X_VMEM_BYTES = 64 * 1024 * 1024
VMEM_LIMIT = 56 * 1024 * 1024
ROW_TILE = D_MODEL // (2 * LANES)
HALF_D = D_MODEL // 2
ROUTE_ROWS = SUBLANES
ISSUE_UNROLL = 8
N_DMA_PRIORITIES = 2
ATTN_BLOCKS_PER_TRIP = 4

PK_Q, PK_K, PK_V = 0, 512, 1024
PK_F = 1536
PK_P = PK_F + LANES
PK_U = PK_P + BRANCH_WIDTH
PK_SV = PK_U + BRANCH_WIDTH
PK_W = PK_SV + BRANCH_WIDTH
F_REP = 6

LOG2E = 1.4426950408889634
SCORE_CAP = 96.0
DEAD_SCORE = -160.0
NEG = -1e30
HALO = 16


def _tiles(seq_len):
    t_proj = min(512, seq_len)
    t_q = min(512, seq_len)
    t_k = min(512, seq_len)
    moe_block = 512
    t_move = min(1024, seq_len)
    return t_proj, t_q, t_k, moe_block, t_move


def _const_spec(*shape):
    zeros = (0,) * len(shape)
    return pl.BlockSpec(shape, lambda *_: zeros, pipeline_mode=pl.Buffered(1))


def _layer_spec(layer, *shape):
    index = (layer,) + (0,) * len(shape)
    return pl.BlockSpec((None,) + shape, lambda *_: index, pipeline_mode=pl.Buffered(1))


def _rms(x, g):
    return x * lax.rsqrt(jnp.mean(x * x, axis=-1, keepdims=True) + RMS_EPS) * g


def _gelu_tanh(x):
    cdf = 0.5 * (1.0 + jnp.tanh(0.7978845608028654 * (x + 0.044715 * (x * x * x))))
    return x * cdf


def _log_sigmoid(x):
    return jnp.minimum(x, 0.0) - jnp.log1p(jnp.exp(-jnp.abs(x)))


def _split3(c):
    hi = c.astype(BF16).astype(F32)
    r = c - hi
    lo = r.astype(BF16).astype(F32)
    lolo = (r - lo).astype(BF16).astype(F32)
    return hi, lo, lolo


def _proj_kernel(x_ref, gmix_ref, w_ref, bf_ref, gq_ref, gk_ref, wpool_ref, spool_ref,
                 gsgu_ref, wsgu_ref, bsgu_ref,
                 qT_ref, k_ref, vT_ref, yb_ref, yc_ref, c_ref,
                 carry_ref, halo_ref, *, T):
    i = pl.program_id(1)

    @pl.when(i == 0)
    def _():
        carry_ref[...] = jnp.zeros_like(carry_ref)
        halo_ref[...] = jnp.zeros_like(halo_ref)

    hb = _rms(x_ref[...], gmix_ref[...]).astype(BF16)

    def proj(lo, width):
        return jnp.dot(hb, w_ref[:, lo:lo + width], preferred_element_type=F32)

    lane = lax.broadcasted_iota(I32, (T, LANES), 1)
    left = lane < A_HEAD_DIM

    def head_sums(v):
        cols = []
        for g in range(A_HEADS // 2):
            pair = v[:, g * LANES:(g + 1) * LANES]
            cols.append(jnp.sum(jnp.where(left, pair, 0.0), axis=-1, keepdims=True))
            cols.append(jnp.sum(jnp.where(left, 0.0, pair), axis=-1, keepdims=True))
        return cols

    def head_norm(z, g):
        cols = head_sums(z * z)
        ss = jnp.concatenate([jnp.where(left, cols[2 * g], cols[2 * g + 1]) for g in range(A_HEADS // 2)],
                             axis=-1)
        return z * lax.rsqrt(ss * (1.0 / A_HEAD_DIM) + RMS_EPS) * g

    qn = head_norm(proj(PK_Q, BRANCH_WIDTH), gq_ref[...]) * (A_HEAD_DIM ** -0.5 * LOG2E)
    kn = head_norm(proj(PK_K, BRANCH_WIDTH), gk_ref[...])
    zv = proj(PK_V, BRANCH_WIDTH)
    d_cols = head_sums(qn * kn)
    gate_head = lane & 7
    diag = jnp.zeros((T, LANES), F32)
    for h in range(A_HEADS):
        diag = jnp.where(gate_head == h, d_cols[h], diag)

    logf = _log_sigmoid(proj(PK_F, LANES) + bf_ref[...])
    r_i = lax.broadcasted_iota(I32, (C_CHUNK, C_CHUNK), 0)
    c_i = lax.broadcasted_iota(I32, (C_CHUNK, C_CHUNK), 1)
    tri = (r_i >= c_i).astype(BF16)
    offset = carry_ref[...]
    c_chunks = []
    for ch in range(T // C_CHUNK):
        rows = logf[ch * C_CHUNK:(ch + 1) * C_CHUNK]
        local = sum(jnp.dot(tri, part.astype(BF16), preferred_element_type=F32) for part in _split3(rows))
        c_chunks.append(local + offset)
        offset = c_chunks[-1][C_CHUNK - 1:C_CHUNK, :]
    c = jnp.concatenate(c_chunks, axis=0)
    carry_ref[...] = offset
    c_ref[...] = c

    lm = lane & 63
    grp = lm >> 3
    hsel = lm & 7
    first3 = grp < 3
    second3 = (grp >= 3) & (grp < F_REP)

    def parts(v):
        hi, lo, lolo = _split3(v)
        return jnp.where((grp == 0) | (grp == 3), hi, jnp.where((grp == 1) | (grp == 4), lo, lolo))

    k_aug = jnp.where(first3, -parts(c * LOG2E), 0.0)
    q_aug = jnp.where(second3, parts(c * LOG2E - diag), 0.0)

    for g in range(A_HEADS // 2):
        sl = slice(g * LANES, (g + 1) * LANES)
        zq2, zk2, zv2 = qn[:, sl], kn[:, sl], zv[:, sl]
        for par in range(2):
            h = 2 * g + par
            keep = (lane < 64) if par == 0 else (lane >= 64)
            oh_k = (second3 & (hsel == h)).astype(F32)
            oh_q = (first3 & (hsel == h)).astype(F32)
            k_ref[h] = jnp.where(keep, zk2, k_aug + oh_k).astype(BF16)
            qT_ref[h] = jnp.where(keep, zq2, q_aug + oh_q).T.astype(BF16)
            ones_lane = 64 if par == 0 else 0
            vT_ref[h] = jnp.where(keep, zv2, (lane == ones_lane).astype(F32)).T.astype(BF16)

    p = proj(PK_P, BRANCH_WIDTH)
    row8 = lax.broadcasted_iota(I32, (8, GROUP_DIM), 0)
    pos = i * T + lax.broadcasted_iota(I32, (T, GROUP_DIM), 0)

    def shift_down(v, tail, d):
        r = pltpu.roll(v, d, 0)
        rt = pltpu.roll(tail, d, 0)
        top = jnp.where(row8 < d, rt[0:8], r[0:8])
        return jnp.concatenate([top, r[8:]], axis=0)

    pooled = []
    for gi, w in enumerate(POOL_WINDOWS):
        sl = slice(gi * GROUP_DIM, (gi + 1) * GROUP_DIM)
        s = p[:, sl]
        for lv in range(gi + 1):
            tail = halo_ref[lv, :, sl]
            halo_ref[lv, :, sl] = s[T - HALO:T, :]
            s = s + shift_down(s, tail, 1 << lv)
        cnt = jnp.minimum(pos + 1, w).astype(F32)
        pooled.append(s / cnt - p[:, sl])
    pooled = jnp.concatenate(pooled, axis=1).astype(BF16)
    yb = jnp.dot(pooled, wpool_ref[...], preferred_element_type=F32) * spool_ref[...]
    yb_ref[...] = yb.astype(yb_ref.dtype)

    gu = _gelu_tanh(proj(PK_U, BRANCH_WIDTH))
    gv = _gelu_tanh(proj(PK_SV, BRANCH_WIDTH))
    t_r = lax.broadcasted_iota(I32, (C_CHUNK, C_CHUNK), 0)
    t_c = lax.broadcasted_iota(I32, (C_CHUNK, C_CHUNK), 1)
    causal = t_r >= t_c
    for g in range(N_GROUPS):
        sl = slice(g * GROUP_DIM, (g + 1) * GROUP_DIM)
        vn = _rms(gv[:, sl], gsgu_ref[:, sl]).astype(BF16)
        wc = jnp.where(causal, wsgu_ref[g], 0.0).astype(BF16)
        for ch in range(T // C_CHUNK):
            rows = slice(ch * C_CHUNK, (ch + 1) * C_CHUNK)
            mixed = jnp.dot(wc, vn[rows], preferred_element_type=F32) + bsgu_ref[g]
            yc_ref[rows, sl] = (gu[rows, sl] * mixed).astype(yc_ref.dtype)


def _proj_call(x, w, layer, T):
    B, S, D = x.shape
    H = A_HEADS
    const = _const_spec
    per_layer = functools.partial(_layer_spec, layer)
    return pl.pallas_call(
        functools.partial(_proj_kernel, T=T),
        grid=(B, S // T),
        in_specs=[
            pl.BlockSpec((None, T, D), lambda b, i: (b, i, 0)),
            per_layer(1, D), per_layer(D, PK_W), per_layer(1, LANES), per_layer(1, BRANCH_WIDTH),
            per_layer(1, BRANCH_WIDTH),
            per_layer(BRANCH_WIDTH, BRANCH_WIDTH), per_layer(1, BRANCH_WIDTH), per_layer(1, BRANCH_WIDTH),
            per_layer(N_GROUPS, C_CHUNK, C_CHUNK), per_layer(N_GROUPS, C_CHUNK, GROUP_DIM),
        ],
        out_specs=[
            pl.BlockSpec((None, H, LANES, T), lambda b, i: (b, 0, 0, i)),
            pl.BlockSpec((None, H, T, LANES), lambda b, i: (b, 0, i, 0)),
            pl.BlockSpec((None, H, LANES, T), lambda b, i: (b, 0, 0, i)),
            pl.BlockSpec((None, T, BRANCH_WIDTH), lambda b, i: (b, i, 0)),
            pl.BlockSpec((None, T, BRANCH_WIDTH), lambda b, i: (b, i, 0)),
            pl.BlockSpec((None, T, LANES), lambda b, i: (b, i, 0)),
        ],
        out_shape=[
            jax.ShapeDtypeStruct((B, H, LANES, S), BF16),
            jax.ShapeDtypeStruct((B, H, S, LANES), BF16),
            jax.ShapeDtypeStruct((B, H, LANES, S), BF16),
            jax.ShapeDtypeStruct((B, S, BRANCH_WIDTH), BF16),
            jax.ShapeDtypeStruct((B, S, BRANCH_WIDTH), BF16),
            jax.ShapeDtypeStruct((B, S, LANES), F32),
        ],
        scratch_shapes=[pltpu.VMEM((1, LANES), F32), pltpu.VMEM((4, HALO, BRANCH_WIDTH), F32)],
        compiler_params=pltpu.CompilerParams(
            dimension_semantics=("arbitrary", "arbitrary"), vmem_limit_bytes=VMEM_LIMIT),
        name="proj",
    )(x, w["g_mix"], w["w_pack"], w["b_f"], w["g_q"], w["g_k"], w["w_pool"], w["s_pool"],
      w["g_sgu"], w["w_sgu"], w["b_sgu"])


def _attn_kernel(first_ref, qT_ref, k_ref, vT_ref, o_ref, m_ref, acc_ref, s_ref, *, tq, tk, n_q, online_max):
    assert tq == tk
    FIRST = 2

    def reset():
        acc_ref[...] = jnp.zeros_like(acc_ref)
        if online_max:
            m_ref[...] = jnp.full(m_ref.shape, NEG, F32)

    def scores(qi, j, slot):
        k0 = pl.multiple_of(j * tk, tk)
        q0 = pl.multiple_of(qi * tq, tq)
        for par in range(2):
            s_ref[slot, par] = jnp.dot(k_ref[par, pl.ds(k0, tk), :], qT_ref[par, :, pl.ds(q0, tq)],
                                       preferred_element_type=F32)

    def consume(qi, j, slot, masked):
        k0 = pl.multiple_of(j * tk, tk)
        q0 = qi * tq
        for par in range(2):
            s = s_ref[slot, par]
            if masked:
                kpos = k0 + lax.broadcasted_iota(I32, (tk, tq), 0)
                qpos = q0 + lax.broadcasted_iota(I32, (tk, tq), 1)
                s = jnp.where(kpos <= qpos, s, NEG)
            v_blk = vT_ref[par, :, pl.ds(k0, tk)]
            if online_max:
                m_old = m_ref[par]
                m_new = jnp.maximum(m_old, jnp.max(s, axis=0, keepdims=True))
                p = jnp.exp2(s - m_new).astype(BF16)
                pv = jnp.dot(v_blk, p, preferred_element_type=F32)
                acc_ref[par] = acc_ref[par] * jnp.exp2(m_old - m_new) + pv
                m_ref[par] = m_new
            else:
                p = jnp.exp2(s).astype(BF16)
                acc_ref[par] += jnp.dot(v_blk, p, preferred_element_type=F32)

    def finish(qi):
        lane = lax.broadcasted_iota(I32, (tq, LANES), 1)
        outs = []
        for par in range(2):
            acc = acc_ref[par]
            l = acc[64:65, :] if par == 0 else acc[0:1, :]
            outs.append((acc * (1.0 / l)).T)
        q0 = pl.multiple_of(qi * tq, tq)
        o_ref[pl.ds(q0, tq), :] = jnp.where(lane < 64, outs[0], outs[1]).astype(o_ref.dtype)
        reset()

    b_id = pl.program_id(0)
    g_id = pl.program_id(1)

    def first_block(qi):
        return first_ref[(b_id * pl.num_programs(1) + g_id) * n_q + qi]

    reset()
    scores(0, 0, FIRST)
    consume(0, 0, FIRST, True)
    finish(0)
    if n_q > 1:
        scores(1, first_block(1), FIRST)

    def query_block(qi, carry):
        base = first_block(qi)
        n_mid = qi - 1 - base
        nxt = jnp.minimum(qi + 1, n_q - 1)
        scores(qi, base + 1, 1)
        consume(qi, base, FIRST, False)

        def trip(t, c):
            j0 = base + 1 + ATTN_BLOCKS_PER_TRIP * t
            for u in range(ATTN_BLOCKS_PER_TRIP):
                scores(qi, j0 + u + 1, u % 2)
                consume(qi, j0 + u, (u + 1) % 2, False)
            return c

        lax.fori_loop(0, n_mid // ATTN_BLOCKS_PER_TRIP, trip, 0)
        j0 = base + 1 + (n_mid // ATTN_BLOCKS_PER_TRIP) * ATTN_BLOCKS_PER_TRIP
        for rem in range(ATTN_BLOCKS_PER_TRIP):
            @pl.when(n_mid % ATTN_BLOCKS_PER_TRIP == rem)
            def _(rem=rem):
                for u in range(rem):
                    scores(qi, j0 + u + 1, u % 2)
                    consume(qi, j0 + u, (u + 1) % 2, False)
                scores(nxt, first_block(nxt), FIRST)
                consume(qi, j0 + rem, (rem + 1) % 2, True)
                finish(qi)
        return carry

    lax.fori_loop(1, n_q, query_block, 0)


def _attn_call(first, qT, k, vT, tq, tk, online_max):
    B, H, _, S = qT.shape
    grid_spec = pltpu.PrefetchScalarGridSpec(
        num_scalar_prefetch=1,
        grid=(B, H // 2),
        in_specs=[
            pl.BlockSpec((None, 2, LANES, S), lambda b, g, first: (b, g, 0, 0)),
            pl.BlockSpec((None, 2, S, LANES), lambda b, g, first: (b, g, 0, 0)),
            pl.BlockSpec((None, 2, LANES, S), lambda b, g, first: (b, g, 0, 0)),
        ],
        out_specs=pl.BlockSpec((None, S, LANES), lambda b, g, first: (b, 0, g)),
        scratch_shapes=[pltpu.VMEM((2, 1, tq), F32), pltpu.VMEM((2, LANES, tq), F32),
                        pltpu.VMEM((3, 2, tk, tq), F32)],
    )
    return pl.pallas_call(
        functools.partial(_attn_kernel, tq=tq, tk=tk, n_q=S // tq, online_max=online_max),
        grid_spec=grid_spec,
        out_shape=jax.ShapeDtypeStruct((B, S, BRANCH_WIDTH), BF16),
        compiler_params=pltpu.CompilerParams(
            dimension_semantics=("parallel", "parallel"), vmem_limit_bytes=VMEM_LIMIT),
        name="attn_online" if online_max else "attn",
    )(first, qT, k, vT)


def _first_live_block(c, bound, tq, tk):
    c_q = c[:, 0::tq, 0:A_HEADS]
    c_k = c[:, tk - 1::tk, 0:A_HEADS]
    top = bound + LOG2E * (c_q[:, :, None, :] - c_k[:, None, :, :])
    dead = top < DEAD_SCORE
    dead_pair = dead[..., 0::2] & dead[..., 1::2]
    n_q = c_q.shape[1]
    n_dead = jnp.sum(jnp.cumprod(dead_pair.astype(I32), axis=2), axis=2)
    limit = jnp.maximum(jnp.arange(n_q, dtype=I32) - 1, 0)[None, :, None]
    first = jnp.minimum(n_dead, limit)
    return first.transpose(0, 2, 1).reshape(-1).astype(I32)


def _attention(qT, k, vT, c, g_q, g_k, tq, tk):
    bound = 16.0 * LOG2E * jnp.max(jnp.abs(g_q)) * jnp.max(jnp.abs(g_k))
    first = _first_live_block(c, bound, tq, tk)
    return lax.cond(bound <= SCORE_CAP,
                    lambda: _attn_call(first, qT, k, vT, tq, tk, False),
                    lambda: _attn_call(first, qT, k, vT, tq, tk, True))


def _merge_kernel(x_ref, ya_ref, yb_ref, yc_ref, gmix_ref, wg_ref, bg_ref, wb_ref, wo_ref,
                  gffn_ref, wr_ref, br_ref,
                  x1_ref, h2_ref, route_ref, route_t_ref, cnt_ref, carry_ref, logit_ref, *, T):
    i = pl.program_id(0)
    n_tiles = pl.num_programs(0) - 1

    @pl.when(i == 0)
    def _():
        carry_ref[...] = jnp.zeros_like(carry_ref)
        logit_ref[...] = jnp.zeros_like(logit_ref)

    def route_previous():
        live = (i > 0).astype(F32)
        logits = logit_ref[...]
        lane = lax.broadcasted_iota(I32, (T, LANES), 1).astype(F32)
        big = float(LANES)

        def first_argmax(v):
            m = jnp.max(v, axis=-1, keepdims=True)
            return m, jnp.min(jnp.where(v == m, lane, big), axis=-1, keepdims=True)

        lg = jnp.where(lane < N_EXPERT_GROUPS, logits, NEG)
        mg, grp = first_argmax(lg)
        p_grp = 1.0 / jnp.sum(jnp.exp(lg - mg), axis=-1, keepdims=True)
        lo_lane = N_EXPERT_GROUPS + grp * EXPERTS_PER_GROUP
        le = jnp.where((lane >= lo_lane) & (lane < lo_lane + EXPERTS_PER_GROUP), logits, NEG)
        m1, i1 = first_argmax(le)
        m2, i2 = first_argmax(jnp.where(lane == i1, NEG, le))
        e21 = jnp.exp(m2 - m1)
        g1 = p_grp / (1.0 + e21)
        g2 = p_grp * e21 / (1.0 + e21)
        e1 = i1 - N_EXPERT_GROUPS
        e2 = i2 - N_EXPERT_GROUPS

        oh1 = lane == e1
        oh2 = lane == e2
        sel = (oh1 | oh2).astype(F32)
        r_i = lax.broadcasted_iota(I32, (T, T), 0)
        c_i = lax.broadcasted_iota(I32, (T, T), 1)
        before = (r_i > c_i).astype(BF16)
        seen = jnp.dot(before, sel.astype(BF16), preferred_element_type=F32) + carry_ref[...]
        r1 = jnp.sum(jnp.where(oh1, seen, 0.0), axis=-1, keepdims=True)
        r2 = jnp.sum(jnp.where(oh2, seen, 0.0), axis=-1, keepdims=True)
        carry_ref[...] = carry_ref[...] + live * jnp.sum(sel, axis=0, keepdims=True)
        cnt_ref[...] = carry_ref[...]

        route = jnp.zeros((T, LANES), F32)
        for idx, val in enumerate((e1, e2, g1, g2, r1, r2)):
            route = jnp.where(lane == idx, val, route)
        route_ref[...] = route
        route_t_ref[...] = route.T[0:ROUTE_ROWS]

    @pl.when(i < n_tiles)
    def _():
        route_previous()
        x = x_ref[...]
        hb = _rms(x, gmix_ref[...]).astype(BF16)
        merged = None
        for bi, y_ref in enumerate((ya_ref, yb_ref, yc_ref)):
            sl = slice(bi * D_MODEL, (bi + 1) * D_MODEL)
            gate = jax.nn.sigmoid(jnp.dot(hb, wg_ref[:, sl], preferred_element_type=F32) + bg_ref[:, sl])
            term = gate * jnp.dot(y_ref[...], wb_ref[bi], preferred_element_type=F32)
            merged = term if merged is None else merged + term
        x1 = x + jnp.dot(merged.astype(BF16), wo_ref[...], preferred_element_type=F32)
        x1_ref[...] = x1
        h2 = _rms(x1, gffn_ref[...])
        _store_row_tiles(h2_ref, _pack_pairs(h2))

        h_hi = h2.astype(BF16)
        h_lo = (h2 - h_hi.astype(F32)).astype(BF16)
        hw = jnp.dot(h_hi, wr_ref[...], preferred_element_type=F32)
        lw = jnp.dot(h_lo, wr_ref[:, 0:LANES], preferred_element_type=F32)
        logit_ref[...] = hw[:, 0:LANES] + (hw[:, LANES:] + lw) + br_ref[...]

    @pl.when(i == n_tiles)
    def _():
        route_previous()


def _row_tile_rows(s, n_rows, row0=0):
    return pl.ds(row0 * ROW_TILE + s, n_rows, stride=ROW_TILE)


def _pack_pairs(v):
    hi = lax.bitcast_convert_type(v[:, :HALF_D].astype(BF16).astype(F32), U32)
    lo = lax.bitcast_convert_type(v[:, HALF_D:].astype(BF16).astype(F32), U32)
    return hi | (lo >> 16)


def _unpack_pairs(w):
    hi = lax.bitcast_convert_type(w & jnp.uint32(0xFFFF0000), F32)
    lo = lax.bitcast_convert_type(w << 16, F32)
    return jnp.concatenate([hi, lo], axis=-1)


def _store_row_tiles(ref, words, row0=0):
    for s in range(ROW_TILE):
        ref[_row_tile_rows(s, words.shape[0], row0), :] = words[:, s * LANES:(s + 1) * LANES]


def _load_row_tiles(ref, n_rows, row0=0):
    return jnp.concatenate([ref[_row_tile_rows(s, n_rows, row0), :] for s in range(ROW_TILE)], axis=-1)


def _merge_call(x2d, ya, yb, yc, w, layer, T):
    N, D = x2d.shape
    n_tiles = N // T
    const = functools.partial(_layer_spec, layer)
    cur = lambda i: jnp.minimum(i, n_tiles - 1)
    prev = lambda i: jnp.maximum(i - 1, 0)
    tile = lambda width: pl.BlockSpec((T, width), lambda i: (cur(i), 0))
    return pl.pallas_call(
        functools.partial(_merge_kernel, T=T),
        grid=(n_tiles + 1,),
        in_specs=[
            tile(D), tile(BRANCH_WIDTH), tile(BRANCH_WIDTH), tile(BRANCH_WIDTH),
            const(1, D), const(D, N_BRANCH * D), const(1, N_BRANCH * D),
            const(N_BRANCH, BRANCH_WIDTH, D), const(D, D), const(1, D), const(D, 2 * LANES), const(1, LANES),
        ],
        out_specs=[tile(D), pl.BlockSpec((T * ROW_TILE, LANES), lambda i: (cur(i), 0)),
                   pl.BlockSpec((T, LANES), lambda i: (prev(i), 0)),
                   pl.BlockSpec((ROUTE_ROWS, T), lambda i: (0, prev(i))),
                   pl.BlockSpec((1, LANES), lambda i: (0, 0))],
        out_shape=[
            jax.ShapeDtypeStruct((N, D), F32),
            jax.ShapeDtypeStruct((N * ROW_TILE, LANES), U32),
            jax.ShapeDtypeStruct((N, LANES), F32),
            jax.ShapeDtypeStruct((ROUTE_ROWS, N), F32),
            jax.ShapeDtypeStruct((1, LANES), F32),
        ],
        scratch_shapes=[pltpu.VMEM((1, LANES), F32), pltpu.VMEM((T, LANES), F32)],
        compiler_params=pltpu.CompilerParams(
            dimension_semantics=("arbitrary",), vmem_limit_bytes=VMEM_LIMIT),
        name="merge",
    )(x2d, ya, yb, yc, w["g_mix"], w["w_gate"], w["b_gate"], w["w_branch"], w["w_out"],
      w["g_ffn"], w["w_router"], w["b_router"])


def _dispatch_kernel(dest_ref, h2_ref, xs_ref, sem, *, T):
    def row_copy(t, d):
        return pltpu.make_async_copy(h2_ref.at[_row_tile(t)], xs_ref.at[_row_tile(d)], sem)

    def body(tb, carry):
        t0 = tb * ISSUE_UNROLL
        for u in range(ISSUE_UNROLL):
            for kk in range(TOP_K):
                row_copy(t0 + u, dest_ref[0, kk * T + t0 + u]).start(priority=kk % N_DMA_PRIORITIES)
        return carry

    lax.fori_loop(0, T // ISSUE_UNROLL, body, 0)
    for kk in range(TOP_K):
        pltpu.make_async_copy(h2_ref, xs_ref.at[pl.ds(0, T * ROW_TILE)], sem).wait()


def _row_tile(r):
    return pl.ds(pl.multiple_of(r * ROW_TILE, ROW_TILE), ROW_TILE)


def _dispatch_call(dest3, h2, T):
    N = h2.shape[0] // ROW_TILE
    return pl.pallas_call(
        functools.partial(_dispatch_kernel, T=T),
        grid=(N // T,),
        in_specs=[
            pl.BlockSpec((None, 1, TOP_K * T), lambda i: (i, 0, 0), memory_space=pltpu.SMEM),
            pl.BlockSpec((T * ROW_TILE, LANES), lambda i: (i, 0)),
        ],
        out_specs=pl.BlockSpec(memory_space=pl.ANY),
        out_shape=jax.ShapeDtypeStruct((TOP_K * N * ROW_TILE, LANES), U32),
        scratch_shapes=[pltpu.SemaphoreType.DMA],
        compiler_params=pltpu.CompilerParams(
            dimension_semantics=("arbitrary",), vmem_limit_bytes=VMEM_LIMIT),
        name="dispatch",
    )(dest3, h2)


def _gmm_kernel(blk_ref, exp_ref, lo_ref, hi_ref, xs_ref, w1_ref, w3_ref, w2_ref, y_ref,
                w13_bf, w2_bf, *, bm):
    i = pl.program_id(0)
    lo = lo_ref[i]
    hi = hi_ref[i]
    prev = jnp.maximum(i - 1, 0)
    first = jnp.logical_or(i == 0, blk_ref[i] != blk_ref[prev])
    new_expert = jnp.logical_or(i == 0, exp_ref[i] != exp_ref[prev])

    @pl.when(new_expert)
    def _():
        w13_bf[:, 0:D_EXPERT] = w1_ref[...].astype(BF16)
        w13_bf[:, D_EXPERT:] = w3_ref[...].astype(BF16)
        w2_bf[...] = w2_ref[...].astype(BF16)

    @pl.when(hi > lo)
    def _():
        n_part = 2
        part = bm // n_part
        outs = []
        for h in range(n_part):
            x = _unpack_pairs(_load_row_tiles(xs_ref, part, h * part)).astype(BF16)
            ab = jnp.dot(x, w13_bf[...], preferred_element_type=F32)
            a = ab[:, 0:D_EXPERT]
            b = ab[:, D_EXPERT:]
            mid = (a * jax.nn.sigmoid(a) * b).astype(BF16)
            outs.append(_pack_pairs(jnp.dot(mid, w2_bf[...], preferred_element_type=F32)))

        @pl.when(first)
        def _():
            for h in range(n_part):
                _store_row_tiles(y_ref, outs[h], h * part)

        @pl.when(jnp.logical_not(first))
        def _():
            for h in range(n_part):
                row = h * part + lax.broadcasted_iota(I32, (part, HALF_D), 0)
                mine = (row >= lo) & (row < hi)
                old = _load_row_tiles(y_ref, part, h * part)
                _store_row_tiles(y_ref, jnp.where(mine, outs[h], old), h * part)


def _gmm_call(plan, xs, w1, w3, w2, layer, bm):
    A = xs.shape[0] // ROW_TILE
    D = D_MODEL
    n_items = plan[0].shape[0]
    grid_spec = pltpu.PrefetchScalarGridSpec(
        num_scalar_prefetch=4,
        grid=(n_items,),
        in_specs=[
            pl.BlockSpec((bm * ROW_TILE, LANES), lambda i, blk, ex, lo, hi: (blk[i], 0)),
            pl.BlockSpec((None, None, D, D_EXPERT), lambda i, blk, ex, lo, hi: (layer, ex[i], 0, 0)),
            pl.BlockSpec((None, None, D, D_EXPERT), lambda i, blk, ex, lo, hi: (layer, ex[i], 0, 0)),
            pl.BlockSpec((None, None, D_EXPERT, D), lambda i, blk, ex, lo, hi: (layer, ex[i], 0, 0)),
        ],
        out_specs=pl.BlockSpec((bm * ROW_TILE, LANES), lambda i, blk, ex, lo, hi: (blk[i], 0)),
        scratch_shapes=[pltpu.VMEM((D, 2 * D_EXPERT), BF16), pltpu.VMEM((D_EXPERT, D), BF16)],
    )
    return pl.pallas_call(
        functools.partial(_gmm_kernel, bm=bm),
        grid_spec=grid_spec,
        out_shape=jax.ShapeDtypeStruct((A * ROW_TILE, LANES), U32),
        compiler_params=pltpu.CompilerParams(
            dimension_semantics=("arbitrary",), vmem_limit_bytes=VMEM_LIMIT),
        name="gmm",
    )(*plan, xs, w1, w3, w2)


def _gmm_plan(counts, n_rows, bm):
    n_blk = n_rows // bm
    n_items = n_blk + N_EXPERTS - 1
    ends = jnp.cumsum(counts)
    starts = ends - counts
    first_blk = starts // bm
    n_it = jnp.where(counts > 0, (ends - 1) // bm - first_blk + 1, 0)
    item_end = jnp.cumsum(n_it)
    item_start = item_end - n_it
    total = item_end[-1]
    ids = jnp.arange(n_items, dtype=I32)
    valid = ids < total
    ex = jnp.minimum(jnp.sum(item_end[None, :] <= ids[:, None], axis=1).astype(I32), N_EXPERTS - 1)
    ex = jnp.where(valid, ex, ex[jnp.maximum(total - 1, 0)])
    blk = jnp.where(valid, first_blk[ex] + ids - item_start[ex], n_blk - 1)
    lo = jnp.where(valid, jnp.maximum(starts[ex], blk * bm) - blk * bm, 0)
    hi = jnp.where(valid, jnp.minimum(ends[ex], (blk + 1) * bm) - blk * bm, 0)
    return blk.astype(I32), ex.astype(I32), lo.astype(I32), hi.astype(I32)


def _combine_kernel(dest_ref, dest_next_ref, x1_ref, route_ref, y_hbm, o_ref, buf, sem, *, T):
    i = pl.program_id(0)
    n_steps = pl.num_programs(0)

    def issue(d_ref, slot):
        def body(tb, carry):
            t0 = tb * ISSUE_UNROLL
            for u in range(ISSUE_UNROLL):
                for kk in range(TOP_K):
                    d = d_ref[0, kk * T + t0 + u]
                    pltpu.make_async_copy(y_hbm.at[_row_tile(d)], buf.at[slot * TOP_K + kk, _row_tile(t0 + u)],
                                          sem.at[slot]).start(priority=kk % N_DMA_PRIORITIES)
            return carry

        lax.fori_loop(0, T // ISSUE_UNROLL, body, 0)

    def finish(slot):
        for kk in range(TOP_K):
            pltpu.make_async_copy(y_hbm.at[pl.ds(0, T * ROW_TILE)], buf.at[slot * TOP_K + kk],
                                  sem.at[slot]).wait()
        g1 = route_ref[:, 2:3]
        g2 = route_ref[:, 3:4]
        for s in range(ROW_TILE):
            rows = _row_tile_rows(s, T)
            y1 = _unpack_pairs(buf[slot * TOP_K, rows, :])
            y2 = _unpack_pairs(buf[slot * TOP_K + 1, rows, :])
            for half in range(2):
                src = slice(half * LANES, (half + 1) * LANES)
                dst = slice(half * HALF_D + s * LANES, half * HALF_D + (s + 1) * LANES)
                o_ref[:, dst] = x1_ref[:, dst] + (g1 * y1[:, src] + g2 * y2[:, src])

    @pl.when(i == 0)
    def _():
        issue(dest_ref, 0)

    for slot in range(2):
        @pl.when(i % 2 == slot)
        def _(slot=slot):
            @pl.when(i + 1 < n_steps)
            def _():
                issue(dest_next_ref, 1 - slot)

            finish(slot)


def _combine_call(dest3, x1, route, y, T):
    N, D = x1.shape
    n_steps = N // T
    dest_spec = lambda index: pl.BlockSpec((None, 1, TOP_K * T), index, memory_space=pltpu.SMEM)
    return pl.pallas_call(
        functools.partial(_combine_kernel, T=T),
        grid=(n_steps,),
        in_specs=[
            dest_spec(lambda i: (i, 0, 0)),
            dest_spec(lambda i: (jnp.minimum(i + 1, n_steps - 1), 0, 0)),
            pl.BlockSpec((T, D), lambda i: (i, 0)),
            pl.BlockSpec((T, LANES), lambda i: (i, 0)),
            pl.BlockSpec(memory_space=pl.ANY),
        ],
        out_specs=pl.BlockSpec((T, D), lambda i: (i, 0)),
        out_shape=jax.ShapeDtypeStruct((N, D), F32),
        scratch_shapes=[pltpu.VMEM((2 * TOP_K, T * ROW_TILE, LANES), U32), pltpu.SemaphoreType.DMA((2,))],
        compiler_params=pltpu.CompilerParams(
            dimension_semantics=("arbitrary",), vmem_limit_bytes=VMEM_LIMIT),
        name="combine",
    )(dest3, dest3, x1, route, y)


def _block_diag(blocks):
    n = blocks.shape[1]
    eye = jnp.eye(n, dtype=blocks.dtype)
    out = blocks[:, :, :, None, :] * eye[None, :, None, :, None]
    return out.reshape(blocks.shape[0], n * blocks.shape[2], n * blocks.shape[3])


def _rep_forget(cols):
    lead = cols.shape[:-1]
    half = jnp.concatenate(
        [jnp.tile(cols, (1,) * len(lead) + (F_REP,)), jnp.zeros(lead + (64 - F_REP * A_HEADS,), cols.dtype)],
        axis=-1)
    return jnp.concatenate([half, half], axis=-1)


def _prep_all(g_mix, w_in, b_fgate, b_gate, g_q, g_k, w_pool, s_pool, g_sgu, w_sgu, b_sgu,
              w_branch, w_out, g_ffn, w_rg, b_rg, w_re, b_re):
    L = w_in.shape[0]
    order = jnp.argsort(b_fgate, axis=1)
    qkv = w_in[:, :, 0:OFF_F].reshape(L, D_MODEL, 3, A_HEADS, A_HEAD_DIM)
    qkv = jnp.take_along_axis(qkv, order[:, None, None, :, None], axis=3).reshape(L, D_MODEL, OFF_F)
    w_f = jnp.take_along_axis(w_in[:, :, OFF_F:OFF_P], order[:, None, :], axis=2)
    b_f = jnp.take_along_axis(b_fgate, order, axis=1)
    wb0 = w_branch[:, 0].reshape(L, A_HEADS, A_HEAD_DIM, D_MODEL)
    wb0 = jnp.take_along_axis(wb0, order[:, :, None, None], axis=1).reshape(L, 1, BRANCH_WIDTH, D_MODEL)
    pad_r = LANES - N_EXPERT_GROUPS - N_EXPERTS
    w_r = jnp.concatenate([w_rg, w_re, jnp.zeros((L, D_MODEL, pad_r), F32)], axis=2)
    w_r_hi = w_r.astype(BF16)
    w_r_lo = (w_r - w_r_hi.astype(F32)).astype(BF16)
    row = lambda v: v.reshape(L, 1, v.shape[-1])
    return dict(
        g_mix=row(g_mix),
        w_pack=jnp.concatenate([qkv, _rep_forget(w_f), w_in[:, :, OFF_P:OFF_G]], axis=2).astype(BF16),
        b_f=_rep_forget(row(b_f)),
        g_q=row(jnp.tile(g_q, (1, A_HEADS))),
        g_k=row(jnp.tile(g_k, (1, A_HEADS))),
        w_pool=_block_diag(w_pool).astype(BF16),
        s_pool=row(s_pool),
        g_sgu=row(g_sgu),
        w_sgu=w_sgu,
        b_sgu=jnp.broadcast_to(b_sgu[:, :, :, None], (L, N_GROUPS, C_CHUNK, GROUP_DIM)),
        w_gate=w_in[:, :, OFF_G:].astype(BF16),
        b_gate=b_gate.reshape(L, 1, N_BRANCH * D_MODEL),
        w_branch=jnp.concatenate([wb0, w_branch[:, 1:]], axis=1).astype(BF16),
        w_out=w_out.astype(BF16),
        g_ffn=row(g_ffn),
        w_router=jnp.concatenate([w_r_hi, w_r_lo], axis=2),
        b_router=row(jnp.concatenate([b_rg, b_re, jnp.zeros((L, pad_r), F32)], axis=1)),
    )


def kernel(x, g_mix, w_in, b_fgate, b_gate, g_q, g_k, w_pool, s_pool, g_sgu, w_sgu, b_sgu, w_branch, w_out,
           g_ffn, w_rg, b_rg, w_re, b_re, w1, w3, w2):
    B, S, D = x.shape
    assert D == D_MODEL and x.dtype == F32
    N = B * S
    T, tq, tk, bm, tm = _tiles(S)
    assert S % T == 0 and S % tq == 0 and tq == tk and T % C_CHUNK == 0
    assert (TOP_K * N) % bm == 0 and N % tm == 0 and tm % ISSUE_UNROLL == 0
    depth = w_in.shape[0]
    w = _prep_all(g_mix, w_in, b_fgate, b_gate, g_q, g_k, w_pool, s_pool, g_sgu, w_sgu, b_sgu,
                  w_branch, w_out, g_ffn, w_rg, b_rg, w_re, b_re)
    for l in range(depth):
        qT, k, vT, yb, yc, c = _proj_call(x, w, l, T)
        ya = _attention(qT, k, vT, c, g_q[l], g_k[l], tq, tk)
        x1, h2, route, route_t, cnt = _merge_call(
            x.reshape(N, D), ya.reshape(N, -1), yb.reshape(N, -1), yc.reshape(N, -1), w, l, T)
        counts = cnt[0, :N_EXPERTS].astype(I32)
        starts = jnp.cumsum(counts) - counts
        experts = route_t[0:TOP_K].astype(I32)
        expert_ids = jnp.arange(N_EXPERTS, dtype=I32)[:, None, None]
        start_of = jnp.sum(jnp.where(experts[None] == expert_ids, starts[:, None, None], 0), axis=0)
        dest = start_of + route_t[4:4 + TOP_K].astype(I32)
        dest3 = dest.reshape(TOP_K, N // tm, tm).transpose(1, 0, 2).reshape(N // tm, 1, TOP_K * tm)
        xs = _dispatch_call(dest3, h2, tm)
        y = _gmm_call(_gmm_plan(counts, TOP_K * N, bm), xs, w1, w3, w2, l, bm)
        x = _combine_call(dest3, x1, route, y, tm).reshape(B, S, D)
    return x
```

```python
import functools

import jax
import jax.numpy as jnp
from jax import lax
from jax.experimental import pallas as pl
from jax.experimental.pallas import tpu as pltpu

F32 = jnp.float32
BF16 = jnp.bfloat16
I32 = jnp.int32
U32 = jnp.uint32

D_MODEL = 1024
A_HEADS = 8
A_HEAD_DIM = 64
BRANCH_WIDTH = 512
N_BRANCH = 3
POOL_WINDOWS = (2, 4, 8, 16)
GROUP_DIM = 128
N_GROUPS = 4
C_CHUNK = 128
N_EXPERT_GROUPS = 4
EXPERTS_PER_GROUP = 8
N_EXPERTS = N_EXPERT_GROUPS * EXPERTS_PER_GROUP
TOP_K = 2
D_EXPERT = 256
RMS_EPS = 1e-6
OFF_F = 3 * BRANCH_WIDTH
OFF_P = OFF_F + A_HEADS
OFF_U = OFF_P + BRANCH_WIDTH
OFF_SV = OFF_U + BRANCH_WIDTH
OFF_G = OFF_SV + BRANCH_WIDTH

LANES = 128
SUBLANES = 8
V7X_VMEM_BYTES = 64 * 1024 * 1024
VMEM_LIMIT = 56 * 1024 * 1024
ROW_TILE = D_MODEL // (2 * LANES)
HALF_D = D_MODEL // 2
ROUTE_ROWS = SUBLANES
ISSUE_UNROLL = 8
N_DMA_PRIORITIES = 2
ATTN_BLOCKS_PER_TRIP = 4

PK_Q, PK_K, PK_V = 0, 512, 1024
PK_F = 1536
PK_P = PK_F + LANES
PK_U = PK_P + BRANCH_WIDTH
PK_SV = PK_U + BRANCH_WIDTH
PK_W = PK_SV + BRANCH_WIDTH
F_REP = 6

LOG2E = 1.4426950408889634
SCORE_CAP = 96.0
DEAD_SCORE = -160.0
NEG = -1e30
HALO = 16


def _tiles(seq_len):
    t_proj = min(512, seq_len)
    t_q = min(512, seq_len)
    t_k = min(512, seq_len)
    moe_block = 512
    t_move = min(1024, seq_len)
    return t_proj, t_q, t_k, moe_block, t_move


def _const_spec(*shape):
    zeros = (0,) * len(shape)
    return pl.BlockSpec(shape, lambda *_: zeros, pipeline_mode=pl.Buffered(1))


def _layer_spec(layer, *shape):
    index = (layer,) + (0,) * len(shape)
    return pl.BlockSpec((None,) + shape, lambda *_: index, pipeline_mode=pl.Buffered(1))


def _rms(x, g):
    return x * lax.rsqrt(jnp.mean(x * x, axis=-1, keepdims=True) + RMS_EPS) * g


def _gelu_tanh(x):
    cdf = 0.5 * (1.0 + jnp.tanh(0.7978845608028654 * (x + 0.044715 * (x * x * x))))
    return x * cdf


def _log_sigmoid(x):
    return jnp.minimum(x, 0.0) - jnp.log1p(jnp.exp(-jnp.abs(x)))


def _split3(c):
    hi = c.astype(BF16).astype(F32)
    r = c - hi
    lo = r.astype(BF16).astype(F32)
    lolo = (r - lo).astype(BF16).astype(F32)
    return hi, lo, lolo


def _proj_kernel(x_ref, gmix_ref, w_ref, bf_ref, gq_ref, gk_ref, wpool_ref, spool_ref,
                 gsgu_ref, wsgu_ref, bsgu_ref,
                 qT_ref, k_ref, vT_ref, yb_ref, yc_ref, c_ref,
                 carry_ref, halo_ref, *, T):
    i = pl.program_id(1)

    @pl.when(i == 0)
    def _():
        carry_ref[...] = jnp.zeros_like(carry_ref)
        halo_ref[...] = jnp.zeros_like(halo_ref)

    hb = _rms(x_ref[...], gmix_ref[...]).astype(BF16)

    def proj(lo, width):
        return jnp.dot(hb, w_ref[:, lo:lo + width], preferred_element_type=F32)

    lane = lax.broadcasted_iota(I32, (T, LANES), 1)
    left = lane < A_HEAD_DIM

    def head_sums(v):
        cols = []
        for g in range(A_HEADS // 2):
            pair = v[:, g * LANES:(g + 1) * LANES]
            cols.append(jnp.sum(jnp.where(left, pair, 0.0), axis=-1, keepdims=True))
            cols.append(jnp.sum(jnp.where(left, 0.0, pair), axis=-1, keepdims=True))
        return cols

    def head_norm(z, g):
        cols = head_sums(z * z)
        ss = jnp.concatenate([jnp.where(left, cols[2 * g], cols[2 * g + 1]) for g in range(A_HEADS // 2)],
                             axis=-1)
        return z * lax.rsqrt(ss * (1.0 / A_HEAD_DIM) + RMS_EPS) * g

    qn = head_norm(proj(PK_Q, BRANCH_WIDTH), gq_ref[...]) * (A_HEAD_DIM ** -0.5 * LOG2E)
    kn = head_norm(proj(PK_K, BRANCH_WIDTH), gk_ref[...])
    zv = proj(PK_V, BRANCH_WIDTH)
    d_cols = head_sums(qn * kn)
    gate_head = lane & 7
    diag = jnp.zeros((T, LANES), F32)
    for h in range(A_HEADS):
        diag = jnp.where(gate_head == h, d_cols[h], diag)

    logf = _log_sigmoid(proj(PK_F, LANES) + bf_ref[...])
    r_i = lax.broadcasted_iota(I32, (C_CHUNK, C_CHUNK), 0)
    c_i = lax.broadcasted_iota(I32, (C_CHUNK, C_CHUNK), 1)
    tri = (r_i >= c_i).astype(BF16)
    offset = carry_ref[...]
    c_chunks = []
    for ch in range(T // C_CHUNK):
        rows = logf[ch * C_CHUNK:(ch + 1) * C_CHUNK]
        local = sum(jnp.dot(tri, part.astype(BF16), preferred_element_type=F32) for part in _split3(rows))
        c_chunks.append(local + offset)
        offset = c_chunks[-1][C_CHUNK - 1:C_CHUNK, :]
    c = jnp.concatenate(c_chunks, axis=0)
    carry_ref[...] = offset
    c_ref[...] = c

    lm = lane & 63
    grp = lm >> 3
    hsel = lm & 7
    first3 = grp < 3
    second3 = (grp >= 3) & (grp < F_REP)

    def parts(v):
        hi, lo, lolo = _split3(v)
        return jnp.where((grp == 0) | (grp == 3), hi, jnp.where((grp == 1) | (grp == 4), lo, lolo))

    k_aug = jnp.where(first3, -parts(c * LOG2E), 0.0)
    q_aug = jnp.where(second3, parts(c * LOG2E - diag), 0.0)

    for g in range(A_HEADS // 2):
        sl = slice(g * LANES, (g + 1) * LANES)
        zq2, zk2, zv2 = qn[:, sl], kn[:, sl], zv[:, sl]
        for par in range(2):
            h = 2 * g + par
            keep = (lane < 64) if par == 0 else (lane >= 64)
            oh_k = (second3 & (hsel == h)).astype(F32)
            oh_q = (first3 & (hsel == h)).astype(F32)
            k_ref[h] = jnp.where(keep, zk2, k_aug + oh_k).astype(BF16)
            qT_ref[h] = jnp.where(keep, zq2, q_aug + oh_q).T.astype(BF16)
            ones_lane = 64 if par == 0 else 0
            vT_ref[h] = jnp.where(keep, zv2, (lane == ones_lane).astype(F32)).T.astype(BF16)

    p = proj(PK_P, BRANCH_WIDTH)
    row8 = lax.broadcasted_iota(I32, (8, GROUP_DIM), 0)
    pos = i * T + lax.broadcasted_iota(I32, (T, GROUP_DIM), 0)

    def shift_down(v, tail, d):
        r = pltpu.roll(v, d, 0)
        rt = pltpu.roll(tail, d, 0)
        top = jnp.where(row8 < d, rt[0:8], r[0:8])
        return jnp.concatenate([top, r[8:]], axis=0)

    pooled = []
    for gi, w in enumerate(POOL_WINDOWS):
        sl = slice(gi * GROUP_DIM, (gi + 1) * GROUP_DIM)
        s = p[:, sl]
        for lv in range(gi + 1):
            tail = halo_ref[lv, :, sl]
            halo_ref[lv, :, sl] = s[T - HALO:T, :]
            s = s + shift_down(s, tail, 1 << lv)
        cnt = jnp.minimum(pos + 1, w).astype(F32)
        pooled.append(s / cnt - p[:, sl])
    pooled = jnp.concatenate(pooled, axis=1).astype(BF16)
    yb = jnp.dot(pooled, wpool_ref[...], preferred_element_type=F32) * spool_ref[...]
    yb_ref[...] = yb.astype(yb_ref.dtype)

    gu = _gelu_tanh(proj(PK_U, BRANCH_WIDTH))
    gv = _gelu_tanh(proj(PK_SV, BRANCH_WIDTH))
    t_r = lax.broadcasted_iota(I32, (C_CHUNK, C_CHUNK), 0)
    t_c = lax.broadcasted_iota(I32, (C_CHUNK, C_CHUNK), 1)
    causal = t_r >= t_c
    for g in range(N_GROUPS):
        sl = slice(g * GROUP_DIM, (g + 1) * GROUP_DIM)
        vn = _rms(gv[:, sl], gsgu_ref[:, sl]).astype(BF16)
        wc = jnp.where(causal, wsgu_ref[g], 0.0).astype(BF16)
        for ch in range(T // C_CHUNK):
            rows = slice(ch * C_CHUNK, (ch + 1) * C_CHUNK)
            mixed = jnp.dot(wc, vn[rows], preferred_element_type=F32) + bsgu_ref[g]
            yc_ref[rows, sl] = (gu[rows, sl] * mixed).astype(yc_ref.dtype)


def _proj_call(x, w, layer, T):
    B, S, D = x.shape
    H = A_HEADS
    const = _const_spec
    per_layer = functools.partial(_layer_spec, layer)
    return pl.pallas_call(
        functools.partial(_proj_kernel, T=T),
        grid=(B, S // T),
        in_specs=[
            pl.BlockSpec((None, T, D), lambda b, i: (b, i, 0)),
            per_layer(1, D), per_layer(D, PK_W), per_layer(1, LANES), per_layer(1, BRANCH_WIDTH),
            per_layer(1, BRANCH_WIDTH),
            per_layer(BRANCH_WIDTH, BRANCH_WIDTH), per_layer(1, BRANCH_WIDTH), per_layer(1, BRANCH_WIDTH),
            per_layer(N_GROUPS, C_CHUNK, C_CHUNK), per_layer(N_GROUPS, C_CHUNK, GROUP_DIM),
        ],
        out_specs=[
            pl.BlockSpec((None, H, LANES, T), lambda b, i: (b, 0, 0, i)),
            pl.BlockSpec((None, H, T, LANES), lambda b, i: (b, 0, i, 0)),
            pl.BlockSpec((None, H, LANES, T), lambda b, i: (b, 0, 0, i)),
            pl.BlockSpec((None, T, BRANCH_WIDTH), lambda b, i: (b, i, 0)),
            pl.BlockSpec((None, T, BRANCH_WIDTH), lambda b, i: (b, i, 0)),
            pl.BlockSpec((None, T, LANES), lambda b, i: (b, i, 0)),
        ],
        out_shape=[
            jax.ShapeDtypeStruct((B, H, LANES, S), BF16),
            jax.ShapeDtypeStruct((B, H, S, LANES), BF16),
            jax.ShapeDtypeStruct((B, H, LANES, S), BF16),
            jax.ShapeDtypeStruct((B, S, BRANCH_WIDTH), BF16),
            jax.ShapeDtypeStruct((B, S, BRANCH_WIDTH), BF16),
            jax.ShapeDtypeStruct((B, S, LANES), F32),
        ],
        scratch_shapes=[pltpu.VMEM((1, LANES), F32), pltpu.VMEM((4, HALO, BRANCH_WIDTH), F32)],
        compiler_params=pltpu.CompilerParams(
            dimension_semantics=("arbitrary", "arbitrary"), vmem_limit_bytes=VMEM_LIMIT),
        name="proj",
    )(x, w["g_mix"], w["w_pack"], w["b_f"], w["g_q"], w["g_k"], w["w_pool"], w["s_pool"],
      w["g_sgu"], w["w_sgu"], w["b_sgu"])


def _attn_kernel(first_ref, qT_ref, k_ref, vT_ref, o_ref, m_ref, acc_ref, s_ref, *, tq, tk, n_q, online_max):
    assert tq == tk
    FIRST = 2

    def reset():
        acc_ref[...] = jnp.zeros_like(acc_ref)
        if online_max:
            m_ref[...] = jnp.full(m_ref.shape, NEG, F32)

    def scores(qi, j, slot):
        k0 = pl.multiple_of(j * tk, tk)
        q0 = pl.multiple_of(qi * tq, tq)
        for par in range(2):
            s_ref[slot, par] = jnp.dot(k_ref[par, pl.ds(k0, tk), :], qT_ref[par, :, pl.ds(q0, tq)],
                                       preferred_element_type=F32)

    def consume(qi, j, slot, masked):
        k0 = pl.multiple_of(j * tk, tk)
        q0 = qi * tq
        for par in range(2):
            s = s_ref[slot, par]
            if masked:
                kpos = k0 + lax.broadcasted_iota(I32, (tk, tq), 0)
                qpos = q0 + lax.broadcasted_iota(I32, (tk, tq), 1)
                s = jnp.where(kpos <= qpos, s, NEG)
            v_blk = vT_ref[par, :, pl.ds(k0, tk)]
            if online_max:
                m_old = m_ref[par]
                m_new = jnp.maximum(m_old, jnp.max(s, axis=0, keepdims=True))
                p = jnp.exp2(s - m_new).astype(BF16)
                pv = jnp.dot(v_blk, p, preferred_element_type=F32)
                acc_ref[par] = acc_ref[par] * jnp.exp2(m_old - m_new) + pv
                m_ref[par] = m_new
            else:
                p = jnp.exp2(s).astype(BF16)
                acc_ref[par] += jnp.dot(v_blk, p, preferred_element_type=F32)

    def finish(qi):
        lane = lax.broadcasted_iota(I32, (tq, LANES), 1)
        outs = []
        for par in range(2):
            acc = acc_ref[par]
            l = acc[64:65, :] if par == 0 else acc[0:1, :]
            outs.append((acc * (1.0 / l)).T)
        q0 = pl.multiple_of(qi * tq, tq)
        o_ref[pl.ds(q0, tq), :] = jnp.where(lane < 64, outs[0], outs[1]).astype(o_ref.dtype)
        reset()

    b_id = pl.program_id(0)
    g_id = pl.program_id(1)

    def first_block(qi):
        return first_ref[(b_id * pl.num_programs(1) + g_id) * n_q + qi]

    reset()
    scores(0, 0, FIRST)
    consume(0, 0, FIRST, True)
    finish(0)
    if n_q > 1:
        scores(1, first_block(1), FIRST)

    def query_block(qi, carry):
        base = first_block(qi)
        n_mid = qi - 1 - base
        nxt = jnp.minimum(qi + 1, n_q - 1)
        scores(qi, base + 1, 1)
        consume(qi, base, FIRST, False)

        def trip(t, c):
            j0 = base + 1 + ATTN_BLOCKS_PER_TRIP * t
            for u in range(ATTN_BLOCKS_PER_TRIP):
                scores(qi, j0 + u + 1, u % 2)
                consume(qi, j0 + u, (u + 1) % 2, False)
            return c

        lax.fori_loop(0, n_mid // ATTN_BLOCKS_PER_TRIP, trip, 0)
        j0 = base + 1 + (n_mid // ATTN_BLOCKS_PER_TRIP) * ATTN_BLOCKS_PER_TRIP
        for rem in range(ATTN_BLOCKS_PER_TRIP):
            @pl.when(n_mid % ATTN_BLOCKS_PER_TRIP == rem)
            def _(rem=rem):
                for u in range(rem):
                    scores(qi, j0 + u + 1, u % 2)
                    consume(qi, j0 + u, (u + 1) % 2, False)
                scores(nxt, first_block(nxt), FIRST)
                consume(qi, j0 + rem, (rem + 1) % 2, True)
                finish(qi)
        return carry

    lax.fori_loop(1, n_q, query_block, 0)


def _attn_call(first, qT, k, vT, tq, tk, online_max):
    B, H, _, S = qT.shape
    grid_spec = pltpu.PrefetchScalarGridSpec(
        num_scalar_prefetch=1,
        grid=(B, H // 2),
        in_specs=[
            pl.BlockSpec((None, 2, LANES, S), lambda b, g, first: (b, g, 0, 0)),
            pl.BlockSpec((None, 2, S, LANES), lambda b, g, first: (b, g, 0, 0)),
            pl.BlockSpec((None, 2, LANES, S), lambda b, g, first: (b, g, 0, 0)),
        ],
        out_specs=pl.BlockSpec((None, S, LANES), lambda b, g, first: (b, 0, g)),
        scratch_shapes=[pltpu.VMEM((2, 1, tq), F32), pltpu.VMEM((2, LANES, tq), F32),
                        pltpu.VMEM((3, 2, tk, tq), F32)],
    )
    return pl.pallas_call(
        functools.partial(_attn_kernel, tq=tq, tk=tk, n_q=S // tq, online_max=online_max),
        grid_spec=grid_spec,
        out_shape=jax.ShapeDtypeStruct((B, S, BRANCH_WIDTH), BF16),
        compiler_params=pltpu.CompilerParams(
            dimension_semantics=("parallel", "parallel"), vmem_limit_bytes=VMEM_LIMIT),
        name="attn_online" if online_max else "attn",
    )(first, qT, k, vT)


def _first_live_block(c, bound, tq, tk):
    c_q = c[:, 0::tq, 0:A_HEADS]
    c_k = c[:, tk - 1::tk, 0:A_HEADS]
    top = bound + LOG2E * (c_q[:, :, None, :] - c_k[:, None, :, :])
    dead = top < DEAD_SCORE
    dead_pair = dead[..., 0::2] & dead[..., 1::2]
    n_q = c_q.shape[1]
    n_dead = jnp.sum(jnp.cumprod(dead_pair.astype(I32), axis=2), axis=2)
    limit = jnp.maximum(jnp.arange(n_q, dtype=I32) - 1, 0)[None, :, None]
    first = jnp.minimum(n_dead, limit)
    return first.transpose(0, 2, 1).reshape(-1).astype(I32)


def _attention(qT, k, vT, c, g_q, g_k, tq, tk):
    bound = 16.0 * LOG2E * jnp.max(jnp.abs(g_q)) * jnp.max(jnp.abs(g_k))
    first = _first_live_block(c, bound, tq, tk)
    return lax.cond(bound <= SCORE_CAP,
                    lambda: _attn_call(first, qT, k, vT, tq, tk, False),
                    lambda: _attn_call(first, qT, k, vT, tq, tk, True))


def _merge_kernel(x_ref, ya_ref, yb_ref, yc_ref, gmix_ref, wg_ref, bg_ref, wb_ref, wo_ref,
                  gffn_ref, wr_ref, br_ref,
                  x1_ref, h2_ref, route_ref, route_t_ref, cnt_ref, carry_ref, logit_ref, *, T):
    i = pl.program_id(0)
    n_tiles = pl.num_programs(0) - 1

    @pl.when(i == 0)
    def _():
        carry_ref[...] = jnp.zeros_like(carry_ref)
        logit_ref[...] = jnp.zeros_like(logit_ref)

    def route_previous():
        live = (i > 0).astype(F32)
        logits = logit_ref[...]
        lane = lax.broadcasted_iota(I32, (T, LANES), 1).astype(F32)
        big = float(LANES)

        def first_argmax(v):
            m = jnp.max(v, axis=-1, keepdims=True)
            return m, jnp.min(jnp.where(v == m, lane, big), axis=-1, keepdims=True)

        lg = jnp.where(lane < N_EXPERT_GROUPS, logits, NEG)
        mg, grp = first_argmax(lg)
        p_grp = 1.0 / jnp.sum(jnp.exp(lg - mg), axis=-1, keepdims=True)
        lo_lane = N_EXPERT_GROUPS + grp * EXPERTS_PER_GROUP
        le = jnp.where((lane >= lo_lane) & (lane < lo_lane + EXPERTS_PER_GROUP), logits, NEG)
        m1, i1 = first_argmax(le)
        m2, i2 = first_argmax(jnp.where(lane == i1, NEG, le))
        e21 = jnp.exp(m2 - m1)
        g1 = p_grp / (1.0 + e21)
        g2 = p_grp * e21 / (1.0 + e21)
        e1 = i1 - N_EXPERT_GROUPS
        e2 = i2 - N_EXPERT_GROUPS

        oh1 = lane == e1
        oh2 = lane == e2
        sel = (oh1 | oh2).astype(F32)
        r_i = lax.broadcasted_iota(I32, (T, T), 0)
        c_i = lax.broadcasted_iota(I32, (T, T), 1)
        before = (r_i > c_i).astype(BF16)
        seen = jnp.dot(before, sel.astype(BF16), preferred_element_type=F32) + carry_ref[...]
        r1 = jnp.sum(jnp.where(oh1, seen, 0.0), axis=-1, keepdims=True)
        r2 = jnp.sum(jnp.where(oh2, seen, 0.0), axis=-1, keepdims=True)
        carry_ref[...] = carry_ref[...] + live * jnp.sum(sel, axis=0, keepdims=True)
        cnt_ref[...] = carry_ref[...]

        route = jnp.zeros((T, LANES), F32)
        for idx, val in enumerate((e1, e2, g1, g2, r1, r2)):
            route = jnp.where(lane == idx, val, route)
        route_ref[...] = route
        route_t_ref[...] = route.T[0:ROUTE_ROWS]

    @pl.when(i < n_tiles)
    def _():
        route_previous()
        x = x_ref[...]
        hb = _rms(x, gmix_ref[...]).astype(BF16)
        merged = None
        for bi, y_ref in enumerate((ya_ref, yb_ref, yc_ref)):
            sl = slice(bi * D_MODEL, (bi + 1) * D_MODEL)
            gate = jax.nn.sigmoid(jnp.dot(hb, wg_ref[:, sl], preferred_element_type=F32) + bg_ref[:, sl])
            term = gate * jnp.dot(y_ref[...], wb_ref[bi], preferred_element_type=F32)
            merged = term if merged is None else merged + term
        x1 = x + jnp.dot(merged.astype(BF16), wo_ref[...], preferred_element_type=F32)
        x1_ref[...] = x1
        h2 = _rms(x1, gffn_ref[...])
        _store_row_tiles(h2_ref, _pack_pairs(h2))

        h_hi = h2.astype(BF16)
        h_lo = (h2 - h_hi.astype(F32)).astype(BF16)
        hw = jnp.dot(h_hi, wr_ref[...], preferred_element_type=F32)
        lw = jnp.dot(h_lo, wr_ref[:, 0:LANES], preferred_element_type=F32)
        logit_ref[...] = hw[:, 0:LANES] + (hw[:, LANES:] + lw) + br_ref[...]

    @pl.when(i == n_tiles)
    def _():
        route_previous()


def _row_tile_rows(s, n_rows, row0=0):
    return pl.ds(row0 * ROW_TILE + s, n_rows, stride=ROW_TILE)


def _pack_pairs(v):
    hi = lax.bitcast_convert_type(v[:, :HALF_D].astype(BF16).astype(F32), U32)
    lo = lax.bitcast_convert_type(v[:, HALF_D:].astype(BF16).astype(F32), U32)
    return hi | (lo >> 16)


def _unpack_pairs(w):
    hi = lax.bitcast_convert_type(w & jnp.uint32(0xFFFF0000), F32)
    lo = lax.bitcast_convert_type(w << 16, F32)
    return jnp.concatenate([hi, lo], axis=-1)


def _store_row_tiles(ref, words, row0=0):
    for s in range(ROW_TILE):
        ref[_row_tile_rows(s, words.shape[0], row0), :] = words[:, s * LANES:(s + 1) * LANES]


def _load_row_tiles(ref, n_rows, row0=0):
    return jnp.concatenate([ref[_row_tile_rows(s, n_rows, row0), :] for s in range(ROW_TILE)], axis=-1)


def _merge_call(x2d, ya, yb, yc, w, layer, T):
    N, D = x2d.shape
    n_tiles = N // T
    const = functools.partial(_layer_spec, layer)
    cur = lambda i: jnp.minimum(i, n_tiles - 1)
    prev = lambda i: jnp.maximum(i - 1, 0)
    tile = lambda width: pl.BlockSpec((T, width), lambda i: (cur(i), 0))
    return pl.pallas_call(
        functools.partial(_merge_kernel, T=T),
        grid=(n_tiles + 1,),
        in_specs=[
            tile(D), tile(BRANCH_WIDTH), tile(BRANCH_WIDTH), tile(BRANCH_WIDTH),
            const(1, D), const(D, N_BRANCH * D), const(1, N_BRANCH * D),
            const(N_BRANCH, BRANCH_WIDTH, D), const(D, D), const(1, D), const(D, 2 * LANES), const(1, LANES),
        ],
        out_specs=[tile(D), pl.BlockSpec((T * ROW_TILE, LANES), lambda i: (cur(i), 0)),
                   pl.BlockSpec((T, LANES), lambda i: (prev(i), 0)),
                   pl.BlockSpec((ROUTE_ROWS, T), lambda i: (0, prev(i))),
                   pl.BlockSpec((1, LANES), lambda i: (0, 0))],
        out_shape=[
            jax.ShapeDtypeStruct((N, D), F32),
            jax.ShapeDtypeStruct((N * ROW_TILE, LANES), U32),
            jax.ShapeDtypeStruct((N, LANES), F32),
            jax.ShapeDtypeStruct((ROUTE_ROWS, N), F32),
            jax.ShapeDtypeStruct((1, LANES), F32),
        ],
        scratch_shapes=[pltpu.VMEM((1, LANES), F32), pltpu.VMEM((T, LANES), F32)],
        compiler_params=pltpu.CompilerParams(
            dimension_semantics=("arbitrary",), vmem_limit_bytes=VMEM_LIMIT),
        name="merge",
    )(x2d, ya, yb, yc, w["g_mix"], w["w_gate"], w["b_gate"], w["w_branch"], w["w_out"],
      w["g_ffn"], w["w_router"], w["b_router"])


def _dispatch_kernel(dest_ref, h2_ref, xs_ref, sem, *, T):
    def row_copy(t, d):
        return pltpu.make_async_copy(h2_ref.at[_row_tile(t)], xs_ref.at[_row_tile(d)], sem)

    def body(tb, carry):
        t0 = tb * ISSUE_UNROLL
        for u in range(ISSUE_UNROLL):
            for kk in range(TOP_K):
                row_copy(t0 + u, dest_ref[0, kk * T + t0 + u]).start(priority=kk % N_DMA_PRIORITIES)
        return carry

    lax.fori_loop(0, T // ISSUE_UNROLL, body, 0)
    for kk in range(TOP_K):
        pltpu.make_async_copy(h2_ref, xs_ref.at[pl.ds(0, T * ROW_TILE)], sem).wait()


def _row_tile(r):
    return pl.ds(pl.multiple_of(r * ROW_TILE, ROW_TILE), ROW_TILE)


def _dispatch_call(dest3, h2, T):
    N = h2.shape[0] // ROW_TILE
    return pl.pallas_call(
        functools.partial(_dispatch_kernel, T=T),
        grid=(N // T,),
        in_specs=[
            pl.BlockSpec((None, 1, TOP_K * T), lambda i: (i, 0, 0), memory_space=pltpu.SMEM),
            pl.BlockSpec((T * ROW_TILE, LANES), lambda i: (i, 0)),
        ],
        out_specs=pl.BlockSpec(memory_space=pl.ANY),
        out_shape=jax.ShapeDtypeStruct((TOP_K * N * ROW_TILE, LANES), U32),
        scratch_shapes=[pltpu.SemaphoreType.DMA],
        compiler_params=pltpu.CompilerParams(
            dimension_semantics=("arbitrary",), vmem_limit_bytes=VMEM_LIMIT),
        name="dispatch",
    )(dest3, h2)


def _gmm_kernel(blk_ref, exp_ref, lo_ref, hi_ref, xs_ref, w1_ref, w3_ref, w2_ref, y_ref,
                w13_bf, w2_bf, *, bm):
    i = pl.program_id(0)
    lo = lo_ref[i]
    hi = hi_ref[i]
    prev = jnp.maximum(i - 1, 0)
    first = jnp.logical_or(i == 0, blk_ref[i] != blk_ref[prev])
    new_expert = jnp.logical_or(i == 0, exp_ref[i] != exp_ref[prev])

    @pl.when(new_expert)
    def _():
        w13_bf[:, 0:D_EXPERT] = w1_ref[...].astype(BF16)
        w13_bf[:, D_EXPERT:] = w3_ref[...].astype(BF16)
        w2_bf[...] = w2_ref[...].astype(BF16)

    @pl.when(first)
    def _():
        y_ref[...] = jnp.zeros_like(y_ref)

    @pl.when(hi > lo)
    def _():
        n_part = 2
        part = bm // n_part
        for h in range(n_part):
            x = _unpack_pairs(_load_row_tiles(xs_ref, part, h * part)).astype(BF16)
            ab = jnp.dot(x, w13_bf[...], preferred_element_type=F32)
            a = ab[:, 0:D_EXPERT]
            b = ab[:, D_EXPERT:]
            mid = (a * jax.nn.sigmoid(a) * b).astype(BF16)
            out = _pack_pairs(jnp.dot(mid, w2_bf[...], preferred_element_type=F32))
            row = h * part + lax.broadcasted_iota(I32, (part, HALF_D), 0)
            mine = (row >= lo) & (row < hi)
            old = _load_row_tiles(y_ref, part, h * part)
            _store_row_tiles(y_ref, jnp.where(mine, out, old), h * part)


def _gmm_call(plan, xs, w1, w3, w2, layer, bm):
    A = xs.shape[0] // ROW_TILE
    D = D_MODEL
    n_items = plan[0].shape[0]
    grid_spec = pltpu.PrefetchScalarGridSpec(
        num_scalar_prefetch=4,
        grid=(n_items,),
        in_specs=[
            pl.BlockSpec((bm * ROW_TILE, LANES), lambda i, blk, ex, lo, hi: (blk[i], 0)),
            pl.BlockSpec((None, None, D, D_EXPERT), lambda i, blk, ex, lo, hi: (layer, ex[i], 0, 0)),
            pl.BlockSpec((None, None, D, D_EXPERT), lambda i, blk, ex, lo, hi: (layer, ex[i], 0, 0)),
            pl.BlockSpec((None, None, D_EXPERT, D), lambda i, blk, ex, lo, hi: (layer, ex[i], 0, 0)),
        ],
        out_specs=pl.BlockSpec((bm * ROW_TILE, LANES), lambda i, blk, ex, lo, hi: (blk[i], 0)),
        scratch_shapes=[pltpu.VMEM((D, 2 * D_EXPERT), BF16), pltpu.VMEM((D_EXPERT, D), BF16)],
    )
    return pl.pallas_call(
        functools.partial(_gmm_kernel, bm=bm),
        grid_spec=grid_spec,
        out_shape=jax.ShapeDtypeStruct((A * ROW_TILE, LANES), U32),
        compiler_params=pltpu.CompilerParams(
            dimension_semantics=("arbitrary",), vmem_limit_bytes=VMEM_LIMIT),
        name="gmm",
    )(*plan, xs, w1, w3, w2)


def _gmm_plan(counts, n_rows, bm):
    n_blk = n_rows // bm
    n_items = n_blk + N_EXPERTS - 1
    ends = jnp.cumsum(counts)
    starts = ends - counts
    first_blk = starts // bm
    n_it = jnp.where(counts > 0, (ends - 1) // bm - first_blk + 1, 0)
    item_end = jnp.cumsum(n_it)
    item_start = item_end - n_it
    total = item_end[-1]
    ids = jnp.arange(n_items, dtype=I32)
    valid = ids < total
    ex = jnp.minimum(jnp.sum(item_end[None, :] <= ids[:, None], axis=1).astype(I32), N_EXPERTS - 1)
    ex = jnp.where(valid, ex, ex[jnp.maximum(total - 1, 0)])
    blk = jnp.where(valid, first_blk[ex] + ids - item_start[ex], n_blk - 1)
    lo = jnp.where(valid, jnp.maximum(starts[ex], blk * bm) - blk * bm, 0)
    hi = jnp.where(valid, jnp.minimum(ends[ex], (blk + 1) * bm) - blk * bm, 0)
    return blk.astype(I32), ex.astype(I32), lo.astype(I32), hi.astype(I32)


def _combine_kernel(dest_ref, dest_next_ref, x1_ref, route_ref, y_hbm, o_ref, buf, sem, *, T):
    i = pl.program_id(0)
    n_steps = pl.num_programs(0)

    def issue(d_ref, slot):
        def body(tb, carry):
            t0 = tb * ISSUE_UNROLL
            for u in range(ISSUE_UNROLL):
                for kk in range(TOP_K):
                    d = d_ref[0, kk * T + t0 + u]
                    pltpu.make_async_copy(y_hbm.at[_row_tile(d)], buf.at[slot * TOP_K + kk, _row_tile(t0 + u)],
                                          sem.at[slot]).start(priority=kk % N_DMA_PRIORITIES)
            return carry

        lax.fori_loop(0, T // ISSUE_UNROLL, body, 0)

    def finish(slot):
        for kk in range(TOP_K):
            pltpu.make_async_copy(y_hbm.at[pl.ds(0, T * ROW_TILE)], buf.at[slot * TOP_K + kk],
                                  sem.at[slot]).wait()
        g1 = route_ref[:, 2:3]
        g2 = route_ref[:, 3:4]
        for s in range(ROW_TILE):
            rows = _row_tile_rows(s, T)
            y1 = _unpack_pairs(buf[slot * TOP_K, rows, :])
            y2 = _unpack_pairs(buf[slot * TOP_K + 1, rows, :])
            for half in range(2):
                src = slice(half * LANES, (half + 1) * LANES)
                dst = slice(half * HALF_D + s * LANES, half * HALF_D + (s + 1) * LANES)
                o_ref[:, dst] = x1_ref[:, dst] + (g1 * y1[:, src] + g2 * y2[:, src])

    @pl.when(i == 0)
    def _():
        issue(dest_ref, 0)

    for slot in range(2):
        @pl.when(i % 2 == slot)
        def _(slot=slot):
            @pl.when(i + 1 < n_steps)
            def _():
                issue(dest_next_ref, 1 - slot)

            finish(slot)


def _combine_call(dest3, x1, route, y, T):
    N, D = x1.shape
    n_steps = N // T
    dest_spec = lambda index: pl.BlockSpec((None, 1, TOP_K * T), index, memory_space=pltpu.SMEM)
    return pl.pallas_call(
        functools.partial(_combine_kernel, T=T),
        grid=(n_steps,),
        in_specs=[
            dest_spec(lambda i: (i, 0, 0)),
            dest_spec(lambda i: (jnp.minimum(i + 1, n_steps - 1), 0, 0)),
            pl.BlockSpec((T, D), lambda i: (i, 0)),
            pl.BlockSpec((T, LANES), lambda i: (i, 0)),
            pl.BlockSpec(memory_space=pl.ANY),
        ],
        out_specs=pl.BlockSpec((T, D), lambda i: (i, 0)),
        out_shape=jax.ShapeDtypeStruct((N, D), F32),
        scratch_shapes=[pltpu.VMEM((2 * TOP_K, T * ROW_TILE, LANES), U32), pltpu.SemaphoreType.DMA((2,))],
        compiler_params=pltpu.CompilerParams(
            dimension_semantics=("arbitrary",), vmem_limit_bytes=VMEM_LIMIT),
        name="combine",
    )(dest3, dest3, x1, route, y)


def _block_diag(blocks):
    n = blocks.shape[1]
    eye = jnp.eye(n, dtype=blocks.dtype)
    out = blocks[:, :, :, None, :] * eye[None, :, None, :, None]
    return out.reshape(blocks.shape[0], n * blocks.shape[2], n * blocks.shape[3])


def _rep_forget(cols):
    lead = cols.shape[:-1]
    half = jnp.concatenate(
        [jnp.tile(cols, (1,) * len(lead) + (F_REP,)), jnp.zeros(lead + (64 - F_REP * A_HEADS,), cols.dtype)],
        axis=-1)
    return jnp.concatenate([half, half], axis=-1)


def _prep_all(g_mix, w_in, b_fgate, b_gate, g_q, g_k, w_pool, s_pool, g_sgu, w_sgu, b_sgu,
              w_branch, w_out, g_ffn, w_rg, b_rg, w_re, b_re):
    L = w_in.shape[0]
    order = jnp.argsort(b_fgate, axis=1)
    qkv = w_in[:, :, 0:OFF_F].reshape(L, D_MODEL, 3, A_HEADS, A_HEAD_DIM)
    qkv = jnp.take_along_axis(qkv, order[:, None, None, :, None], axis=3).reshape(L, D_MODEL, OFF_F)
    w_f = jnp.take_along_axis(w_in[:, :, OFF_F:OFF_P], order[:, None, :], axis=2)
    b_f = jnp.take_along_axis(b_fgate, order, axis=1)
    wb0 = w_branch[:, 0].reshape(L, A_HEADS, A_HEAD_DIM, D_MODEL)
    wb0 = jnp.take_along_axis(wb0, order[:, :, None, None], axis=1).reshape(L, 1, BRANCH_WIDTH, D_MODEL)
    pad_r = LANES - N_EXPERT_GROUPS - N_EXPERTS
    w_r = jnp.concatenate([w_rg, w_re, jnp.zeros((L, D_MODEL, pad_r), F32)], axis=2)
    w_r_hi = w_r.astype(BF16)
    w_r_lo = (w_r - w_r_hi.astype(F32)).astype(BF16)
    row = lambda v: v.reshape(L, 1, v.shape[-1])
    return dict(
        g_mix=row(g_mix),
        w_pack=jnp.concatenate([qkv, _rep_forget(w_f), w_in[:, :, OFF_P:OFF_G]], axis=2).astype(BF16),
        b_f=_rep_forget(row(b_f)),
        g_q=row(jnp.tile(g_q, (1, A_HEADS))),
        g_k=row(jnp.tile(g_k, (1, A_HEADS))),
        w_pool=_block_diag(w_pool).astype(BF16),
        s_pool=row(s_pool),
        g_sgu=row(g_sgu),
        w_sgu=w_sgu,
        b_sgu=jnp.broadcast_to(b_sgu[:, :, :, None], (L, N_GROUPS, C_CHUNK, GROUP_DIM)),
        w_gate=w_in[:, :, OFF_G:].astype(BF16),
        b_gate=b_gate.reshape(L, 1, N_BRANCH * D_MODEL),
        w_branch=jnp.concatenate([wb0, w_branch[:, 1:]], axis=1).astype(BF16),
        w_out=w_out.astype(BF16),
        g_ffn=row(g_ffn),
        w_router=jnp.concatenate([w_r_hi, w_r_lo], axis=2),
        b_router=row(jnp.concatenate([b_rg, b_re, jnp.zeros((L, pad_r), F32)], axis=1)),
    )


def kernel(x, g_mix, w_in, b_fgate, b_gate, g_q, g_k, w_pool, s_pool, g_sgu, w_sgu, b_sgu, w_branch, w_out,
           g_ffn, w_rg, b_rg, w_re, b_re, w1, w3, w2):
    B, S, D = x.shape
    assert D == D_MODEL and x.dtype == F32
    N = B * S
    T, tq, tk, bm, tm = _tiles(S)
    assert S % T == 0 and S % tq == 0 and tq == tk and T % C_CHUNK == 0
    assert (TOP_K * N) % bm == 0 and N % tm == 0 and tm % ISSUE_UNROLL == 0
    depth = w_in.shape[0]
    w = _prep_all(g_mix, w_in, b_fgate, b_gate, g_q, g_k, w_pool, s_pool, g_sgu, w_sgu, b_sgu,
                  w_branch, w_out, g_ffn, w_rg, b_rg, w_re, b_re)
    for l in range(depth):
        qT, k, vT, yb, yc, c = _proj_call(x, w, l, T)
        ya = _attention(qT, k, vT, c, g_q[l], g_k[l], tq, tk)
        x1, h2, route, route_t, cnt = _merge_call(
            x.reshape(N, D), ya.reshape(N, -1), yb.reshape(N, -1), yc.reshape(N, -1), w, l, T)
        counts = cnt[0, :N_EXPERTS].astype(I32)
        starts = jnp.cumsum(counts) - counts
        experts = route_t[0:TOP_K].astype(I32)
        expert_ids = jnp.arange(N_EXPERTS, dtype=I32)[:, None, None]
        start_of = jnp.sum(jnp.where(experts[None] == expert_ids, starts[:, None, None], 0), axis=0)
        dest = start_of + route_t[4:4 + TOP_K].astype(I32)
        dest3 = dest.reshape(TOP_K, N // tm, tm).transpose(1, 0, 2).reshape(N // tm, 1, TOP_K * tm)
        xs = _dispatch_call(dest3, h2, tm)
        y = _gmm_call(_gmm_plan(counts, TOP_K * N, bm), xs, w1, w3, w2, l, bm)
        x = _combine_call(dest3, x1, route, y, tm).reshape(B, S, D)
    return x
```

```python
import functools

import jax
import jax.numpy as jnp
from jax import lax
from jax.experimental import pallas as pl
from jax.experimental.pallas import tpu as pltpu

F32 = jnp.float32
BF16 = jnp.bfloat16
I32 = jnp.int32
U32 = jnp.uint32

D_MODEL = 1024
A_HEADS = 8
A_HEAD_DIM = 64
BRANCH_WIDTH = 512
N_BRANCH = 3
POOL_WINDOWS = (2, 4, 8, 16)
GROUP_DIM = 128
N_GROUPS = 4
C_CHUNK = 128
N_EXPERT_GROUPS = 4
EXPERTS_PER_GROUP = 8
N_EXPERTS = N_EXPERT_GROUPS * EXPERTS_PER_GROUP
TOP_K = 2
D_EXPERT = 256
RMS_EPS = 1e-6
OFF_F = 3 * BRANCH_WIDTH
OFF_P = OFF_F + A_HEADS
OFF_U = OFF_P + BRANCH_WIDTH
OFF_SV = OFF_U + BRANCH_WIDTH
OFF_G = OFF_SV + BRANCH_WIDTH

LANES = 128
SUBLANES = 8
V7X_VMEM_BYTES = 64 * 1024 * 1024
VMEM_LIMIT = V7X_VMEM_BYTES * 7 // 8
ROW_TILE = D_MODEL // (2 * LANES)
HALF_D = D_MODEL // 2
ROUTE_ROWS = SUBLANES
ISSUE_UNROLL = 8
N_DMA_PRIORITIES = 2
ATTN_BLOCKS_PER_TRIP = 4

PK_Q, PK_K, PK_V = 0, BRANCH_WIDTH, 2 * BRANCH_WIDTH
PK_F = 3 * BRANCH_WIDTH
PK_P = PK_F + LANES
PK_U = PK_P + BRANCH_WIDTH
PK_SV = PK_U + BRANCH_WIDTH
PK_W = PK_SV + BRANCH_WIDTH
F_REP = 6

LOG2E = 1.4426950408889634
SCORE_CAP = 96.0
DEAD_SCORE = -160.0
NEG = -1e30
HALO = 16


def _tiles(seq_len):
    t_proj = min(512, seq_len)
    t_q = min(512, seq_len)
    t_k = min(512, seq_len)
    moe_block = 512
    t_move = min(1024, seq_len)
    return t_proj, t_q, t_k, moe_block, t_move


def _const_spec(*shape):
    zeros = (0,) * len(shape)
    return pl.BlockSpec(shape, lambda *_: zeros, pipeline_mode=pl.Buffered(1))


def _layer_spec(layer, *shape):
    index = (layer,) + (0,) * len(shape)
    return pl.BlockSpec((None,) + shape, lambda *_: index, pipeline_mode=pl.Buffered(1))


def _rms(x, g):
    return x * lax.rsqrt(jnp.mean(x * x, axis=-1, keepdims=True) + RMS_EPS) * g


def _gelu_tanh(x):
    cdf = 0.5 * (1.0 + jnp.tanh(0.7978845608028654 * (x + 0.044715 * (x * x * x))))
    return x * cdf


def _log_sigmoid(x):
    return jnp.minimum(x, 0.0) - jnp.log1p(jnp.exp(-jnp.abs(x)))


def _split3(c):
    hi = c.astype(BF16).astype(F32)
    r = c - hi
    lo = r.astype(BF16).astype(F32)
    lolo = (r - lo).astype(BF16).astype(F32)
    return hi, lo, lolo


def _proj_kernel(x_ref, gmix_ref, w_ref, bf_ref, gq_ref, gk_ref, wpool_ref, spool_ref,
                 gsgu_ref, wsgu_ref, bsgu_ref,
                 qT_ref, k_ref, vT_ref, yb_ref, yc_ref, c_ref,
                 carry_ref, halo_ref, *, T):
    i = pl.program_id(1)

    @pl.when(i == 0)
    def _():
        carry_ref[...] = jnp.zeros_like(carry_ref)
        halo_ref[...] = jnp.zeros_like(halo_ref)

    hb = _rms(x_ref[...], gmix_ref[...]).astype(BF16)

    def proj(lo, width):
        return jnp.dot(hb, w_ref[:, lo:lo + width], preferred_element_type=F32)

    lane = lax.broadcasted_iota(I32, (T, LANES), 1)
    left = lane < A_HEAD_DIM

    def head_sums(v):
        cols = []
        for g in range(A_HEADS // 2):
            pair = v[:, g * LANES:(g + 1) * LANES]
            cols.append(jnp.sum(jnp.where(left, pair, 0.0), axis=-1, keepdims=True))
            cols.append(jnp.sum(jnp.where(left, 0.0, pair), axis=-1, keepdims=True))
        return cols

    def head_norm(z, g):
        cols = head_sums(z * z)
        ss = jnp.concatenate([jnp.where(left, cols[2 * g], cols[2 * g + 1]) for g in range(A_HEADS // 2)],
                             axis=-1)
        return z * lax.rsqrt(ss * (1.0 / A_HEAD_DIM) + RMS_EPS) * g

    qn = head_norm(proj(PK_Q, BRANCH_WIDTH), gq_ref[...]) * (A_HEAD_DIM ** -0.5 * LOG2E)
    kn = head_norm(proj(PK_K, BRANCH_WIDTH), gk_ref[...])
    zv = proj(PK_V, BRANCH_WIDTH)
    d_cols = head_sums(qn * kn)
    gate_head = lane & 7
    diag = jnp.zeros((T, LANES), F32)
    for h in range(A_HEADS):
        diag = jnp.where(gate_head == h, d_cols[h], diag)

    logf = _log_sigmoid(proj(PK_F, LANES) + bf_ref[...])
    r_i = lax.broadcasted_iota(I32, (C_CHUNK, C_CHUNK), 0)
    c_i = lax.broadcasted_iota(I32, (C_CHUNK, C_CHUNK), 1)
    tri = (r_i >= c_i).astype(BF16)
    offset = carry_ref[...]
    c_chunks = []
    for ch in range(T // C_CHUNK):
        rows = logf[ch * C_CHUNK:(ch + 1) * C_CHUNK]
        local = sum(jnp.dot(tri, part.astype(BF16), preferred_element_type=F32) for part in _split3(rows))
        c_chunks.append(local + offset)
        offset = c_chunks[-1][C_CHUNK - 1:C_CHUNK, :]
    c = jnp.concatenate(c_chunks, axis=0)
    carry_ref[...] = offset
    c_ref[...] = c

    lm = lane & 63
    grp = lm >> 3
    hsel = lm & 7
    first3 = grp < 3
    second3 = (grp >= 3) & (grp < F_REP)

    def parts(v):
        hi, lo, lolo = _split3(v)
        return jnp.where((grp == 0) | (grp == 3), hi, jnp.where((grp == 1) | (grp == 4), lo, lolo))

    k_aug = jnp.where(first3, -parts(c * LOG2E), 0.0)
    q_aug = jnp.where(second3, parts(c * LOG2E - diag), 0.0)

    for g in range(A_HEADS // 2):
        sl = slice(g * LANES, (g + 1) * LANES)
        zq2, zk2, zv2 = qn[:, sl], kn[:, sl], zv[:, sl]
        for par in range(2):
            h = 2 * g + par
            keep = (lane < 64) if par == 0 else (lane >= 64)
            oh_k = (second3 & (hsel == h)).astype(F32)
            oh_q = (first3 & (hsel == h)).astype(F32)
            k_ref[h] = jnp.where(keep, zk2, k_aug + oh_k).astype(BF16)
            qT_ref[h] = jnp.where(keep, zq2, q_aug + oh_q).T.astype(BF16)
            ones_lane = 64 if par == 0 else 0
            vT_ref[h] = jnp.where(keep, zv2, (lane == ones_lane).astype(F32)).T.astype(BF16)

    p = proj(PK_P, BRANCH_WIDTH)
    row8 = lax.broadcasted_iota(I32, (8, GROUP_DIM), 0)
    pos = i * T + lax.broadcasted_iota(I32, (T, GROUP_DIM), 0)

    def shift_down(v, tail, d):
        r = pltpu.roll(v, d, 0)
        rt = pltpu.roll(tail, d, 0)
        top = jnp.where(row8 < d, rt[0:8], r[0:8])
        return jnp.concatenate([top, r[8:]], axis=0)

    pooled = []
    for gi, w in enumerate(POOL_WINDOWS):
        sl = slice(gi * GROUP_DIM, (gi + 1) * GROUP_DIM)
        s = p[:, sl]
        for lv in range(gi + 1):
            tail = halo_ref[lv, :, sl]
            halo_ref[lv, :, sl] = s[T - HALO:T, :]
            s = s + shift_down(s, tail, 1 << lv)
        cnt = jnp.minimum(pos + 1, w).astype(F32)
        pooled.append(s / cnt - p[:, sl])
    pooled = jnp.concatenate(pooled, axis=1).astype(BF16)
    yb = jnp.dot(pooled, wpool_ref[...], preferred_element_type=F32) * spool_ref[...]
    yb_ref[...] = yb.astype(yb_ref.dtype)

    gu = _gelu_tanh(proj(PK_U, BRANCH_WIDTH))
    gv = _gelu_tanh(proj(PK_SV, BRANCH_WIDTH))
    t_r = lax.broadcasted_iota(I32, (C_CHUNK, C_CHUNK), 0)
    t_c = lax.broadcasted_iota(I32, (C_CHUNK, C_CHUNK), 1)
    causal = t_r >= t_c
    for g in range(N_GROUPS):
        sl = slice(g * GROUP_DIM, (g + 1) * GROUP_DIM)
        vn = _rms(gv[:, sl], gsgu_ref[:, sl]).astype(BF16)
        wc = jnp.where(causal, wsgu_ref[g], 0.0).astype(BF16)
        for ch in range(T // C_CHUNK):
            rows = slice(ch * C_CHUNK, (ch + 1) * C_CHUNK)
            mixed = jnp.dot(wc, vn[rows], preferred_element_type=F32) + bsgu_ref[g]
            yc_ref[rows, sl] = (gu[rows, sl] * mixed).astype(yc_ref.dtype)


def _proj_call(x, w, layer, T):
    B, S, D = x.shape
    H = A_HEADS
    per_layer = functools.partial(_layer_spec, layer)
    return pl.pallas_call(
        functools.partial(_proj_kernel, T=T),
        grid=(B, S // T),
        in_specs=[
            pl.BlockSpec((None, T, D), lambda b, i: (b, i, 0)),
            per_layer(1, D), per_layer(D, PK_W), per_layer(1, LANES), per_layer(1, BRANCH_WIDTH),
            per_layer(1, BRANCH_WIDTH),
            per_layer(BRANCH_WIDTH, BRANCH_WIDTH), per_layer(1, BRANCH_WIDTH), per_layer(1, BRANCH_WIDTH),
            per_layer(N_GROUPS, C_CHUNK, C_CHUNK), per_layer(N_GROUPS, C_CHUNK, GROUP_DIM),
        ],
        out_specs=[
            pl.BlockSpec((None, H, LANES, T), lambda b, i: (b, 0, 0, i)),
            pl.BlockSpec((None, H, T, LANES), lambda b, i: (b, 0, i, 0)),
            pl.BlockSpec((None, H, LANES, T), lambda b, i: (b, 0, 0, i)),
            pl.BlockSpec((None, T, BRANCH_WIDTH), lambda b, i: (b, i, 0)),
            pl.BlockSpec((None, T, BRANCH_WIDTH), lambda b, i: (b, i, 0)),
            pl.BlockSpec((None, T, LANES), lambda b, i: (b, i, 0)),
        ],
        out_shape=[
            jax.ShapeDtypeStruct((B, H, LANES, S), BF16),
            jax.ShapeDtypeStruct((B, H, S, LANES), BF16),
            jax.ShapeDtypeStruct((B, H, LANES, S), BF16),
            jax.ShapeDtypeStruct((B, S, BRANCH_WIDTH), BF16),
            jax.ShapeDtypeStruct((B, S, BRANCH_WIDTH), BF16),
            jax.ShapeDtypeStruct((B, S, LANES), F32),
        ],
        scratch_shapes=[pltpu.VMEM((1, LANES), F32), pltpu.VMEM((4, HALO, BRANCH_WIDTH), F32)],
        compiler_params=pltpu.CompilerParams(
            dimension_semantics=("arbitrary", "arbitrary"), vmem_limit_bytes=VMEM_LIMIT),
        name="proj",
    )(x, w["g_mix"], w["w_pack"], w["b_f"], w["g_q"], w["g_k"], w["w_pool"], w["s_pool"],
      w["g_sgu"], w["w_sgu"], w["b_sgu"])


def _attn_kernel(first_ref, qT_ref, k_ref, vT_ref, o_ref, m_ref, acc_ref, s_ref, *, tq, tk, n_q, online_max):
    assert tq == tk
    FIRST = 2

    def reset():
        acc_ref[...] = jnp.zeros_like(acc_ref)
        if online_max:
            m_ref[...] = jnp.full(m_ref.shape, NEG, F32)

    def scores(qi, j, slot):
        k0 = pl.multiple_of(j * tk, tk)
        q0 = pl.multiple_of(qi * tq, tq)
        for par in range(2):
            s_ref[slot, par] = jnp.dot(k_ref[par, pl.ds(k0, tk), :], qT_ref[par, :, pl.ds(q0, tq)],
                                       preferred_element_type=F32)

    def consume(qi, j, slot, masked):
        k0 = pl.multiple_of(j * tk, tk)
        q0 = qi * tq
        for par in range(2):
            s = s_ref[slot, par]
            if masked:
                kpos = k0 + lax.broadcasted_iota(I32, (tk, tq), 0)
                qpos = q0 + lax.broadcasted_iota(I32, (tk, tq), 1)
                s = jnp.where(kpos <= qpos, s, NEG)
            v_blk = vT_ref[par, :, pl.ds(k0, tk)]
            if online_max:
                m_old = m_ref[par]
                m_new = jnp.maximum(m_old, jnp.max(s, axis=0, keepdims=True))
                p = jnp.exp2(s - m_new).astype(BF16)
                pv = jnp.dot(v_blk, p, preferred_element_type=F32)
                acc_ref[par] = acc_ref[par] * jnp.exp2(m_old - m_new) + pv
                m_ref[par] = m_new
            else:
                p = jnp.exp2(s).astype(BF16)
                acc_ref[par] += jnp.dot(v_blk, p, preferred_element_type=F32)

    def finish(qi):
        lane = lax.broadcasted_iota(I32, (tq, LANES), 1)
        outs = []
        for par in range(2):
            acc = acc_ref[par]
            l = acc[64:65, :] if par == 0 else acc[0:1, :]
            outs.append((acc * (1.0 / l)).T)
        q0 = pl.multiple_of(qi * tq, tq)
        o_ref[pl.ds(q0, tq), :] = jnp.where(lane < 64, outs[0], outs[1]).astype(o_ref.dtype)
        reset()

    b_id = pl.program_id(0)
    g_id = pl.program_id(1)

    def first_block(qi):
        return first_ref[(b_id * pl.num_programs(1) + g_id) * n_q + qi]

    reset()
    scores(0, 0, FIRST)
    consume(0, 0, FIRST, True)
    finish(0)
    if n_q > 1:
        scores(1, first_block(1), FIRST)

    def query_block(qi, carry):
        base = first_block(qi)
        n_mid = qi - 1 - base
        nxt = jnp.minimum(qi + 1, n_q - 1)
        scores(qi, base + 1, 1)
        consume(qi, base, FIRST, False)

        def trip(t, c):
            j0 = base + 1 + ATTN_BLOCKS_PER_TRIP * t
            for u in range(ATTN_BLOCKS_PER_TRIP):
                scores(qi, j0 + u + 1, u % 2)
                consume(qi, j0 + u, (u + 1) % 2, False)
            return c

        lax.fori_loop(0, n_mid // ATTN_BLOCKS_PER_TRIP, trip, 0)
        j0 = base + 1 + (n_mid // ATTN_BLOCKS_PER_TRIP) * ATTN_BLOCKS_PER_TRIP
        for rem in range(ATTN_BLOCKS_PER_TRIP):
            @pl.when(n_mid % ATTN_BLOCKS_PER_TRIP == rem)
            def _(rem=rem):
                for u in range(rem):
                    scores(qi, j0 + u + 1, u % 2)
                    consume(qi, j0 + u, (u + 1) % 2, False)
                scores(nxt, first_block(nxt), FIRST)
                consume(qi, j0 + rem, (rem + 1) % 2, True)
                finish(qi)
        return carry

    lax.fori_loop(1, n_q, query_block, 0)


def _attn_call(first, qT, k, vT, tq, tk, online_max):
    B, H, _, S = qT.shape
    grid_spec = pltpu.PrefetchScalarGridSpec(
        num_scalar_prefetch=1,
        grid=(B, H // 2),
        in_specs=[
            pl.BlockSpec((None, 2, LANES, S), lambda b, g, first: (b, g, 0, 0)),
            pl.BlockSpec((None, 2, S, LANES), lambda b, g, first: (b, g, 0, 0)),
            pl.BlockSpec((None, 2, LANES, S), lambda b, g, first: (b, g, 0, 0)),
        ],
        out_specs=pl.BlockSpec((None, S, LANES), lambda b, g, first: (b, 0, g)),
        scratch_shapes=[pltpu.VMEM((2, 1, tq), F32), pltpu.VMEM((2, LANES, tq), F32),
                        pltpu.VMEM((3, 2, tk, tq), F32)],
    )
    return pl.pallas_call(
        functools.partial(_attn_kernel, tq=tq, tk=tk, n_q=S // tq, online_max=online_max),
        grid_spec=grid_spec,
        out_shape=jax.ShapeDtypeStruct((B, S, BRANCH_WIDTH), BF16),
        compiler_params=pltpu.CompilerParams(
            dimension_semantics=("parallel", "parallel"), vmem_limit_bytes=VMEM_LIMIT),
        name="attn_online" if online_max else "attn",
    )(first, qT, k, vT)


def _first_live_block(c, bound, tq, tk):
    c_q = c[:, 0::tq, 0:A_HEADS]
    c_k = c[:, tk - 1::tk, 0:A_HEADS]
    top = bound + LOG2E * (c_q[:, :, None, :] - c_k[:, None, :, :])
    dead = top < DEAD_SCORE
    dead_pair = dead[..., 0::2] & dead[..., 1::2]
    n_q = c_q.shape[1]
    n_dead = jnp.sum(jnp.cumprod(dead_pair.astype(I32), axis=2), axis=2)
    limit = jnp.maximum(jnp.arange(n_q, dtype=I32) - 1, 0)[None, :, None]
    first = jnp.minimum(n_dead, limit)
    return first.transpose(0, 2, 1).reshape(-1).astype(I32)


def _attention(qT, k, vT, c, g_q, g_k, tq, tk):
    bound = 16.0 * LOG2E * jnp.max(jnp.abs(g_q)) * jnp.max(jnp.abs(g_k))
    first = _first_live_block(c, bound, tq, tk)
    return lax.cond(bound <= SCORE_CAP,
                    lambda: _attn_call(first, qT, k, vT, tq, tk, False),
                    lambda: _attn_call(first, qT, k, vT, tq, tk, True))


def _merge_kernel(x_ref, ya_ref, yb_ref, yc_ref, gmix_ref, wg_ref, bg_ref, wb_ref, wo_ref,
                  gffn_ref, wr_ref, br_ref,
                  x1_ref, h2_ref, route_ref, route_t_ref, cnt_ref, carry_ref, logit_ref, *, T):
    i = pl.program_id(0)
    n_tiles = pl.num_programs(0) - 1

    @pl.when(i == 0)
    def _():
        carry_ref[...] = jnp.zeros_like(carry_ref)
        logit_ref[...] = jnp.zeros_like(logit_ref)

    def route_previous():
        live = (i > 0).astype(F32)
        logits = logit_ref[...]
        lane = lax.broadcasted_iota(I32, (T, LANES), 1).astype(F32)
        big = float(LANES)

        def first_argmax(v):
            m = jnp.max(v, axis=-1, keepdims=True)
            return m, jnp.min(jnp.where(v == m, lane, big), axis=-1, keepdims=True)

        lg = jnp.where(lane < N_EXPERT_GROUPS, logits, NEG)
        mg, grp = first_argmax(lg)
        p_grp = 1.0 / jnp.sum(jnp.exp(lg - mg), axis=-1, keepdims=True)
        lo_lane = N_EXPERT_GROUPS + grp * EXPERTS_PER_GROUP
        le = jnp.where((lane >= lo_lane) & (lane < lo_lane + EXPERTS_PER_GROUP), logits, NEG)
        m1, i1 = first_argmax(le)
        m2, i2 = first_argmax(jnp.where(lane == i1, NEG, le))
        e21 = jnp.exp(m2 - m1)
        g1 = p_grp / (1.0 + e21)
        g2 = p_grp * e21 / (1.0 + e21)
        e1 = i1 - N_EXPERT_GROUPS
        e2 = i2 - N_EXPERT_GROUPS

        oh1 = lane == e1
        oh2 = lane == e2
        sel = (oh1 | oh2).astype(F32)
        r_i = lax.broadcasted_iota(I32, (T, T), 0)
        c_i = lax.broadcasted_iota(I32, (T, T), 1)
        before = (r_i > c_i).astype(BF16)
        seen = jnp.dot(before, sel.astype(BF16), preferred_element_type=F32) + carry_ref[...]
        r1 = jnp.sum(jnp.where(oh1, seen, 0.0), axis=-1, keepdims=True)
        r2 = jnp.sum(jnp.where(oh2, seen, 0.0), axis=-1, keepdims=True)
        carry_ref[...] = carry_ref[...] + live * jnp.sum(sel, axis=0, keepdims=True)
        cnt_ref[...] = carry_ref[...]

        route = jnp.zeros((T, LANES), F32)
        for idx, val in enumerate((e1, e2, g1, g2, r1, r2)):
            route = jnp.where(lane == idx, val, route)
        route_ref[...] = route
        route_t_ref[...] = route.T[0:ROUTE_ROWS]

    @pl.when(i < n_tiles)
    def _():
        route_previous()
        x = x_ref[...]
        hb = _rms(x, gmix_ref[...]).astype(BF16)
        merged = None
        for bi, y_ref in enumerate((ya_ref, yb_ref, yc_ref)):
            sl = slice(bi * D_MODEL, (bi + 1) * D_MODEL)
            gate = jax.nn.sigmoid(jnp.dot(hb, wg_ref[:, sl], preferred_element_type=F32) + bg_ref[:, sl])
            term = gate * jnp.dot(y_ref[...], wb_ref[bi], preferred_element_type=F32)
            merged = term if merged is None else merged + term
        x1 = x + jnp.dot(merged.astype(BF16), wo_ref[...], preferred_element_type=F32)
        x1_ref[...] = x1
        h2 = _rms(x1, gffn_ref[...])
        _store_row_tiles(h2_ref, _pack_pairs(h2))

        h_hi = h2.astype(BF16)
        h_lo = (h2 - h_hi.astype(F32)).astype(BF16)
        hw = jnp.dot(h_hi, wr_ref[...], preferred_element_type=F32)
        lw = jnp.dot(h_lo, wr_ref[:, 0:LANES], preferred_element_type=F32)
        logit_ref[...] = hw[:, 0:LANES] + (hw[:, LANES:] + lw) + br_ref[...]

    @pl.when(i == n_tiles)
    def _():
        route_previous()


def _row_tile_rows(s, n_rows, row0=0):
    return pl.ds(row0 * ROW_TILE + s, n_rows, stride=ROW_TILE)


def _pack_pairs(v):
    hi = lax.bitcast_convert_type(v[:, :HALF_D].astype(BF16).astype(F32), U32)
    lo = lax.bitcast_convert_type(v[:, HALF_D:].astype(BF16).astype(F32), U32)
    return hi | (lo >> 16)


def _unpack_pairs(w):
    hi = lax.bitcast_convert_type(w & jnp.uint32(0xFFFF0000), F32)
    lo = lax.bitcast_convert_type(w << 16, F32)
    return jnp.concatenate([hi, lo], axis=-1)


def _store_row_tiles(ref, words, row0=0):
    for s in range(ROW_TILE):
        ref[_row_tile_rows(s, words.shape[0], row0), :] = words[:, s * LANES:(s + 1) * LANES]


def _load_row_tiles(ref, n_rows, row0=0):
    return jnp.concatenate([ref[_row_tile_rows(s, n_rows, row0), :] for s in range(ROW_TILE)], axis=-1)


def _merge_call(x2d, ya, yb, yc, w, layer, T):
    N, D = x2d.shape
    n_tiles = N // T
    const = functools.partial(_layer_spec, layer)
    cur = lambda i: jnp.minimum(i, n_tiles - 1)
    prev = lambda i: jnp.maximum(i - 1, 0)
    tile = lambda width: pl.BlockSpec((T, width), lambda i: (cur(i), 0))
    return pl.pallas_call(
        functools.partial(_merge_kernel, T=T),
        grid=(n_tiles + 1,),
        in_specs=[
            tile(D), tile(BRANCH_WIDTH), tile(BRANCH_WIDTH), tile(BRANCH_WIDTH),
            const(1, D), const(D, N_BRANCH * D), const(1, N_BRANCH * D),
            const(N_BRANCH, BRANCH_WIDTH, D), const(D, D), const(1, D), const(D, 2 * LANES), const(1, LANES),
        ],
        out_specs=[tile(D), pl.BlockSpec((T * ROW_TILE, LANES), lambda i: (cur(i), 0)),
                   pl.BlockSpec((T, LANES), lambda i: (prev(i), 0)),
                   pl.BlockSpec((ROUTE_ROWS, T), lambda i: (0, prev(i))),
                   pl.BlockSpec((1, LANES), lambda i: (0, 0))],
        out_shape=[
            jax.ShapeDtypeStruct((N, D), F32),
            jax.ShapeDtypeStruct((N * ROW_TILE, LANES), U32),
            jax.ShapeDtypeStruct((N, LANES), F32),
            jax.ShapeDtypeStruct((ROUTE_ROWS, N), F32),
            jax.ShapeDtypeStruct((1, LANES), F32),
        ],
        scratch_shapes=[pltpu.VMEM((1, LANES), F32), pltpu.VMEM((T, LANES), F32)],
        compiler_params=pltpu.CompilerParams(
            dimension_semantics=("arbitrary",), vmem_limit_bytes=VMEM_LIMIT),
        name="merge",
    )(x2d, ya, yb, yc, w["g_mix"], w["w_gate"], w["b_gate"], w["w_branch"], w["w_out"],
      w["g_ffn"], w["w_router"], w["b_router"])


def _dispatch_kernel(dest_ref, h2_ref, xs_ref, sem, *, T):
    def row_copy(t, d):
        return pltpu.make_async_copy(h2_ref.at[_row_tile(t)], xs_ref.at[_row_tile(d)], sem)

    def body(tb, carry):
        t0 = tb * ISSUE_UNROLL
        for u in range(ISSUE_UNROLL):
            for kk in range(TOP_K):
                row_copy(t0 + u, dest_ref[0, kk * T + t0 + u]).start(priority=kk % N_DMA_PRIORITIES)
        return carry

    lax.fori_loop(0, T // ISSUE_UNROLL, body, 0)
    for kk in range(TOP_K):
        pltpu.make_async_copy(h2_ref, xs_ref.at[pl.ds(0, T * ROW_TILE)], sem).wait()


def _row_tile(r):
    return pl.ds(pl.multiple_of(r * ROW_TILE, ROW_TILE), ROW_TILE)


def _dispatch_call(dest3, h2, T):
    N = h2.shape[0] // ROW_TILE
    return pl.pallas_call(
        functools.partial(_dispatch_kernel, T=T),
        grid=(N // T,),
        in_specs=[
            pl.BlockSpec((None, 1, TOP_K * T), lambda i: (i, 0, 0), memory_space=pltpu.SMEM),
            pl.BlockSpec((T * ROW_TILE, LANES), lambda i: (i, 0)),
        ],
        out_specs=pl.BlockSpec(memory_space=pl.ANY),
        out_shape=jax.ShapeDtypeStruct((TOP_K * N * ROW_TILE, LANES), U32),
        scratch_shapes=[pltpu.SemaphoreType.DMA],
        compiler_params=pltpu.CompilerParams(
            dimension_semantics=("arbitrary",), vmem_limit_bytes=VMEM_LIMIT),
        name="dispatch",
    )(dest3, h2)


def _gmm_kernel(blk_ref, exp_ref, lo_ref, hi_ref, xs_ref, w1_ref, w3_ref, w2_ref, y_ref,
                w13_bf, w2_bf, *, bm):
    i = pl.program_id(0)
    lo = lo_ref[i]
    hi = hi_ref[i]
    prev = jnp.maximum(i - 1, 0)
    first = jnp.logical_or(i == 0, blk_ref[i] != blk_ref[prev])
    new_expert = jnp.logical_or(i == 0, exp_ref[i] != exp_ref[prev])

    @pl.when(new_expert)
    def _():
        w13_bf[:, 0:D_EXPERT] = w1_ref[...].astype(BF16)
        w13_bf[:, D_EXPERT:] = w3_ref[...].astype(BF16)
        w2_bf[...] = w2_ref[...].astype(BF16)

    @pl.when(first)
    def _():
        y_ref[...] = jnp.zeros_like(y_ref)

    @pl.when(hi > lo)
    def _():
        n_part = 2
        part = bm // n_part
        for h in range(n_part):
            x = _unpack_pairs(_load_row_tiles(xs_ref, part, h * part)).astype(BF16)
            ab = jnp.dot(x, w13_bf[...], preferred_element_type=F32)
            a = ab[:, 0:D_EXPERT]
            b = ab[:, D_EXPERT:]
            mid = (a * jax.nn.sigmoid(a) * b).astype(BF16)
            out = _pack_pairs(jnp.dot(mid, w2_bf[...], preferred_element_type=F32))
            row = h * part + lax.broadcasted_iota(I32, (part, HALF_D), 0)
            mine = (row >= lo) & (row < hi)
            old = _load_row_tiles(y_ref, part, h * part)
            _store_row_tiles(y_ref, jnp.where(mine, out, old), h * part)


def _gmm_call(plan, xs, w1, w3, w2, layer, bm):
    A = xs.shape[0] // ROW_TILE
    D = D_MODEL
    n_items = plan[0].shape[0]
    grid_spec = pltpu.PrefetchScalarGridSpec(
        num_scalar_prefetch=4,
        grid=(n_items,),
        in_specs=[
            pl.BlockSpec((bm * ROW_TILE, LANES), lambda i, blk, ex, lo, hi: (blk[i], 0)),
            pl.BlockSpec((None, None, D, D_EXPERT), lambda i, blk, ex, lo, hi: (layer, ex[i], 0, 0)),
            pl.BlockSpec((None, None, D, D_EXPERT), lambda i, blk, ex, lo, hi: (layer, ex[i], 0, 0)),
            pl.BlockSpec((None, None, D_EXPERT, D), lambda i, blk, ex, lo, hi: (layer, ex[i], 0, 0)),
        ],
        out_specs=pl.BlockSpec((bm * ROW_TILE, LANES), lambda i, blk, ex, lo, hi: (blk[i], 0)),
        scratch_shapes=[pltpu.VMEM((D, 2 * D_EXPERT), BF16), pltpu.VMEM((D_EXPERT, D), BF16)],
    )
    return pl.pallas_call(
        functools.partial(_gmm_kernel, bm=bm),
        grid_spec=grid_spec,
        out_shape=jax.ShapeDtypeStruct((A * ROW_TILE, LANES), U32),
        compiler_params=pltpu.CompilerParams(
            dimension_semantics=("arbitrary",), vmem_limit_bytes=VMEM_LIMIT),
        name="gmm",
    )(*plan, xs, w1, w3, w2)


def _gmm_plan(counts, n_rows, bm):
    n_blk = n_rows // bm
    n_items = n_blk + N_EXPERTS - 1
    ends = jnp.cumsum(counts)
    starts = ends - counts
    first_blk = starts // bm
    n_it = jnp.where(counts > 0, (ends - 1) // bm - first_blk + 1, 0)
    item_end = jnp.cumsum(n_it)
    item_start = item_end - n_it
    total = item_end[-1]
    ids = jnp.arange(n_items, dtype=I32)
    valid = ids < total
    ex = jnp.minimum(jnp.sum(item_end[None, :] <= ids[:, None], axis=1).astype(I32), N_EXPERTS - 1)
    ex = jnp.where(valid, ex, ex[jnp.maximum(total - 1, 0)])
    blk = jnp.where(valid, first_blk[ex] + ids - item_start[ex], n_blk - 1)
    lo = jnp.where(valid, jnp.maximum(starts[ex], blk * bm) - blk * bm, 0)
    hi = jnp.where(valid, jnp.minimum(ends[ex], (blk + 1) * bm) - blk * bm, 0)
    return blk.astype(I32), ex.astype(I32), lo.astype(I32), hi.astype(I32)


COMBINE_SLOTS = 3
COMBINE_ROWS = 32


def _combine_kernel(dest_ref, dest_next_ref, dest_ahead_ref, x1_ref, route_ref, y_hbm, o_ref, buf, sem, *, T):
    i = pl.program_id(0)
    n_steps = pl.num_programs(0)
    n_trips = T // COMBINE_ROWS

    def issue_rows(d_ref, slot, t0):
        for u in range(COMBINE_ROWS):
            for kk in range(TOP_K):
                d = d_ref[0, kk * T + t0 + u]
                pltpu.make_async_copy(y_hbm.at[_row_tile(d)], buf.at[slot * TOP_K + kk, _row_tile(t0 + u)],
                                      sem.at[slot]).start(priority=kk % N_DMA_PRIORITIES)

    def combine_rows(slot, t0):
        rows = pl.ds(t0, COMBINE_ROWS)
        g1 = route_ref[rows, 2:3]
        g2 = route_ref[rows, 3:4]
        for s in range(ROW_TILE):
            words = pl.ds(t0 * ROW_TILE + s, COMBINE_ROWS, stride=ROW_TILE)
            y1 = _unpack_pairs(buf[slot * TOP_K, words, :])
            y2 = _unpack_pairs(buf[slot * TOP_K + 1, words, :])
            for half in range(2):
                src = slice(half * LANES, (half + 1) * LANES)
                dst = slice(half * HALF_D + s * LANES, half * HALF_D + (s + 1) * LANES)
                o_ref[rows, dst] = x1_ref[rows, dst] + (g1 * y1[:, src] + g2 * y2[:, src])

    def loop(body):
        def trip(tb, carry):
            body(pl.multiple_of(tb * COMBINE_ROWS, COMBINE_ROWS))
            return carry

        lax.fori_loop(0, n_trips, trip, 0)

    @pl.when(i == 0)
    def _():
        loop(lambda t0: issue_rows(dest_ref, 0, t0))

        @pl.when(n_steps > 1)
        def _():
            loop(lambda t0: issue_rows(dest_next_ref, 1, t0))

    for slot in range(COMBINE_SLOTS):
        @pl.when(i % COMBINE_SLOTS == slot)
        def _(slot=slot):
            for kk in range(TOP_K):
                pltpu.make_async_copy(y_hbm.at[pl.ds(0, T * ROW_TILE)], buf.at[slot * TOP_K + kk],
                                      sem.at[slot]).wait()
            ahead = (slot + 2) % COMBINE_SLOTS

            @pl.when(i + 2 < n_steps)
            def _():
                def both(t0):
                    issue_rows(dest_ahead_ref, ahead, t0)
                    combine_rows(slot, t0)

                loop(both)

            @pl.when(i + 2 >= n_steps)
            def _():
                loop(lambda t0: combine_rows(slot, t0))


def _combine_call(dest3, x1, route, y, T):
    N, D = x1.shape
    n_steps = N // T
    dest_spec = lambda ahead: pl.BlockSpec(
        (None, 1, TOP_K * T), lambda i: (jnp.minimum(i + ahead, n_steps - 1), 0, 0), memory_space=pltpu.SMEM)
    return pl.pallas_call(
        functools.partial(_combine_kernel, T=T),
        grid=(n_steps,),
        in_specs=[
            dest_spec(0), dest_spec(1), dest_spec(2),
            pl.BlockSpec((T, D), lambda i: (i, 0)),
            pl.BlockSpec((T, LANES), lambda i: (i, 0)),
            pl.BlockSpec(memory_space=pl.ANY),
        ],
        out_specs=pl.BlockSpec((T, D), lambda i: (i, 0)),
        out_shape=jax.ShapeDtypeStruct((N, D), F32),
        scratch_shapes=[pltpu.VMEM((COMBINE_SLOTS * TOP_K, T * ROW_TILE, LANES), U32),
                        pltpu.SemaphoreType.DMA((COMBINE_SLOTS,))],
        compiler_params=pltpu.CompilerParams(
            dimension_semantics=("arbitrary",), vmem_limit_bytes=VMEM_LIMIT),
        name="combine",
    )(dest3, dest3, dest3, x1, route, y)


def _block_diag(blocks):
    n = blocks.shape[1]
    eye = jnp.eye(n, dtype=blocks.dtype)
    out = blocks[:, :, :, None, :] * eye[None, :, None, :, None]
    return out.reshape(blocks.shape[0], n * blocks.shape[2], n * blocks.shape[3])


def _rep_forget(cols):
    lead = cols.shape[:-1]
    half = jnp.concatenate(
        [jnp.tile(cols, (1,) * len(lead) + (F_REP,)), jnp.zeros(lead + (64 - F_REP * A_HEADS,), cols.dtype)],
        axis=-1)
    return jnp.concatenate([half, half], axis=-1)


def _prep_all(g_mix, w_in, b_fgate, b_gate, g_q, g_k, w_pool, s_pool, g_sgu, w_sgu, b_sgu,
              w_branch, w_out, g_ffn, w_rg, b_rg, w_re, b_re):
    L = w_in.shape[0]
    order = jnp.argsort(b_fgate, axis=1)
    qkv = w_in[:, :, 0:OFF_F].reshape(L, D_MODEL, 3, A_HEADS, A_HEAD_DIM)
    qkv = jnp.take_along_axis(qkv, order[:, None, None, :, None], axis=3).reshape(L, D_MODEL, OFF_F)
    w_f = jnp.take_along_axis(w_in[:, :, OFF_F:OFF_P], order[:, None, :], axis=2)
    b_f = jnp.take_along_axis(b_fgate, order, axis=1)
    wb0 = w_branch[:, 0].reshape(L, A_HEADS, A_HEAD_DIM, D_MODEL)
    wb0 = jnp.take_along_axis(wb0, order[:, :, None, None], axis=1).reshape(L, 1, BRANCH_WIDTH, D_MODEL)
    pad_r = LANES - N_EXPERT_GROUPS - N_EXPERTS
    w_r = jnp.concatenate([w_rg, w_re, jnp.zeros((L, D_MODEL, pad_r), F32)], axis=2)
    w_r_hi = w_r.astype(BF16)
    w_r_lo = (w_r - w_r_hi.astype(F32)).astype(BF16)
    row = lambda v: v.reshape(L, 1, v.shape[-1])
    return dict(
        g_mix=row(g_mix),
        w_pack=jnp.concatenate([qkv, _rep_forget(w_f), w_in[:, :, OFF_P:OFF_G]], axis=2).astype(BF16),
        b_f=_rep_forget(row(b_f)),
        g_q=row(jnp.tile(g_q, (1, A_HEADS))),
        g_k=row(jnp.tile(g_k, (1, A_HEADS))),
        w_pool=_block_diag(w_pool).astype(BF16),
        s_pool=row(s_pool),
        g_sgu=row(g_sgu),
        w_sgu=w_sgu,
        b_sgu=jnp.broadcast_to(b_sgu[:, :, :, None], (L, N_GROUPS, C_CHUNK, GROUP_DIM)),
        w_gate=w_in[:, :, OFF_G:].astype(BF16),
        b_gate=b_gate.reshape(L, 1, N_BRANCH * D_MODEL),
        w_branch=jnp.concatenate([wb0, w_branch[:, 1:]], axis=1).astype(BF16),
        w_out=w_out.astype(BF16),
        g_ffn=row(g_ffn),
        w_router=jnp.concatenate([w_r_hi, w_r_lo], axis=2),
        b_router=row(jnp.concatenate([b_rg, b_re, jnp.zeros((L, pad_r), F32)], axis=1)),
    )


def kernel(x, g_mix, w_in, b_fgate, b_gate, g_q, g_k, w_pool, s_pool, g_sgu, w_sgu, b_sgu, w_branch, w_out,
           g_ffn, w_rg, b_rg, w_re, b_re, w1, w3, w2):
    B, S, D = x.shape
    assert D == D_MODEL and x.dtype == F32
    N = B * S
    T, tq, tk, bm, tm = _tiles(S)
    assert S % T == 0 and S % tq == 0 and tq == tk and T % C_CHUNK == 0
    assert (TOP_K * N) % bm == 0 and N % tm == 0 and tm % ISSUE_UNROLL == 0 and tm % COMBINE_ROWS == 0
    depth = w_in.shape[0]
    w = _prep_all(g_mix, w_in, b_fgate, b_gate, g_q, g_k, w_pool, s_pool, g_sgu, w_sgu, b_sgu,
                  w_branch, w_out, g_ffn, w_rg, b_rg, w_re, b_re)
    for l in range(depth):
        qT, k, vT, yb, yc, c = _proj_call(x, w, l, T)
        ya = _attention(qT, k, vT, c, g_q[l], g_k[l], tq, tk)
        x1, h2, route, route_t, cnt = _merge_call(
            x.reshape(N, D), ya.reshape(N, -1), yb.reshape(N, -1), yc.reshape(N, -1), w, l, T)
        counts = cnt[0, :N_EXPERTS].astype(I32)
        starts = jnp.cumsum(counts) - counts
        experts = route_t[0:TOP_K].astype(I32)
        expert_ids = jnp.arange(N_EXPERTS, dtype=I32)[:, None, None]
        start_of = jnp.sum(jnp.where(experts[None] == expert_ids, starts[:, None, None], 0), axis=0)
        dest = start_of + route_t[4:4 + TOP_K].astype(I32)
        dest3 = dest.reshape(TOP_K, N // tm, tm).transpose(1, 0, 2).reshape(N // tm, 1, TOP_K * tm)
        xs = _dispatch_call(dest3, h2, tm)
        y = _gmm_call(_gmm_plan(counts, TOP_K * N, bm), xs, w1, w3, w2, l, bm)
        x = _combine_call(dest3, x1, route, y, tm).reshape(B, S, D)
    return x
```

```python
import functools

import jax
import jax.numpy as jnp
from jax import lax
from jax.experimental import pallas as pl
from jax.experimental.pallas import tpu as pltpu

F32 = jnp.float32
BF16 = jnp.bfloat16
I32 = jnp.int32
U32 = jnp.uint32

D_MODEL = 1024
A_HEADS = 8
A_HEAD_DIM = 64
BRANCH_WIDTH = 512
N_BRANCH = 3
POOL_WINDOWS = (2, 4, 8, 16)
GROUP_DIM = 128
N_GROUPS = 4
C_CHUNK = 128
N_EXPERT_GROUPS = 4
EXPERTS_PER_GROUP = 8
N_EXPERTS = N_EXPERT_GROUPS * EXPERTS_PER_GROUP
TOP_K = 2
D_EXPERT = 256
RMS_EPS = 1e-6
OFF_F = 3 * BRANCH_WIDTH
OFF_P = OFF_F + A_HEADS
OFF_U = OFF_P + BRANCH_WIDTH
OFF_SV = OFF_U + BRANCH_WIDTH
OFF_G = OFF_SV + BRANCH_WIDTH

LANES = 128
SUBLANES = 8
V7X_VMEM_BYTES = 64 * 1024 * 1024
VMEM_LIMIT = V7X_VMEM_BYTES * 7 // 8
ROW_TILE = D_MODEL // (2 * LANES)
HALF_D = D_MODEL // 2
ROUTE_ROWS = SUBLANES
ISSUE_UNROLL = 8
N_DMA_PRIORITIES = 2
ATTN_BLOCKS_PER_TRIP = 4

PK_Q, PK_K, PK_V = 0, BRANCH_WIDTH, 2 * BRANCH_WIDTH
PK_F = 3 * BRANCH_WIDTH
PK_P = PK_F + LANES
PK_U = PK_P + BRANCH_WIDTH
PK_SV = PK_U + BRANCH_WIDTH
PK_W = PK_SV + BRANCH_WIDTH
F_REP = 6

LOG2E = 1.4426950408889634
SCORE_CAP = 96.0
DEAD_SCORE = -152.0
NEG = -1e30
HALO = 16


def _tiles(seq_len):
    t_proj = min(512, seq_len)
    t_q = min(512, seq_len)
    t_k = min(512, seq_len)
    moe_block = 512
    t_move = min(1024, seq_len)
    return t_proj, t_q, t_k, moe_block, t_move


def _layer_spec(layer, *shape):
    index = (layer,) + (0,) * len(shape)
    return pl.BlockSpec((None,) + shape, lambda *_: index, pipeline_mode=pl.Buffered(1))


def _rms(x, g):
    return x * lax.rsqrt(jnp.mean(x * x, axis=-1, keepdims=True) + RMS_EPS) * g


def _gelu_tanh(x):
    cdf = 0.5 * (1.0 + jnp.tanh(0.7978845608028654 * (x + 0.044715 * (x * x * x))))
    return x * cdf


def _log_sigmoid(x):
    return jnp.minimum(x, 0.0) - jnp.log1p(jnp.exp(-jnp.abs(x)))


def _split3(c):
    hi = c.astype(BF16).astype(F32)
    r = c - hi
    lo = r.astype(BF16).astype(F32)
    lolo = (r - lo).astype(BF16).astype(F32)
    return hi, lo, lolo


def _proj_kernel(x_ref, gmix_ref, w_ref, bf_ref, gq_ref, gk_ref, wpool_ref, spool_ref,
                 gsgu_ref, wsgu_ref, bsgu_ref,
                 qT_ref, k_ref, vT_ref, yb_ref, yc_ref, c_ref,
                 carry_ref, halo_ref, *, T):
    i = pl.program_id(1)

    @pl.when(i == 0)
    def _():
        carry_ref[...] = jnp.zeros_like(carry_ref)
        halo_ref[...] = jnp.zeros_like(halo_ref)

    hb = _rms(x_ref[...], gmix_ref[...]).astype(BF16)

    def proj(lo, width):
        return jnp.dot(hb, w_ref[:, lo:lo + width], preferred_element_type=F32)

    lane = lax.broadcasted_iota(I32, (T, LANES), 1)
    left = lane < A_HEAD_DIM

    def head_sums(v):
        cols = []
        for g in range(A_HEADS // 2):
            pair = v[:, g * LANES:(g + 1) * LANES]
            cols.append(jnp.sum(jnp.where(left, pair, 0.0), axis=-1, keepdims=True))
            cols.append(jnp.sum(jnp.where(left, 0.0, pair), axis=-1, keepdims=True))
        return cols

    def head_norm(z, g):
        cols = head_sums(z * z)
        ss = jnp.concatenate([jnp.where(left, cols[2 * g], cols[2 * g + 1]) for g in range(A_HEADS // 2)],
                             axis=-1)
        return z * lax.rsqrt(ss * (1.0 / A_HEAD_DIM) + RMS_EPS) * g

    qn = head_norm(proj(PK_Q, BRANCH_WIDTH), gq_ref[...]) * (A_HEAD_DIM ** -0.5 * LOG2E)
    kn = head_norm(proj(PK_K, BRANCH_WIDTH), gk_ref[...])
    zv = proj(PK_V, BRANCH_WIDTH)
    d_cols = head_sums(qn * kn)
    gate_head = lane & (A_HEADS - 1)
    diag = jnp.zeros((T, LANES), F32)
    for h in range(A_HEADS):
        diag = jnp.where(gate_head == h, d_cols[h], diag)

    logf = _log_sigmoid(proj(PK_F, LANES) + bf_ref[...])
    r_i = lax.broadcasted_iota(I32, (C_CHUNK, C_CHUNK), 0)
    c_i = lax.broadcasted_iota(I32, (C_CHUNK, C_CHUNK), 1)
    tri = (r_i >= c_i).astype(BF16)
    offset = carry_ref[...]
    c_chunks = []
    for ch in range(T // C_CHUNK):
        rows = logf[ch * C_CHUNK:(ch + 1) * C_CHUNK]
        local = sum(jnp.dot(tri, part.astype(BF16), preferred_element_type=F32) for part in _split3(rows))
        c_chunks.append(local + offset)
        offset = c_chunks[-1][C_CHUNK - 1:C_CHUNK, :]
    c = jnp.concatenate(c_chunks, axis=0)
    carry_ref[...] = offset
    c_ref[...] = c

    lm = lane & (A_HEAD_DIM - 1)
    grp = lm >> (A_HEADS.bit_length() - 1)
    hsel = gate_head
    first3 = grp < 3
    second3 = (grp >= 3) & (grp < F_REP)

    def parts(v):
        hi, lo, lolo = _split3(v)
        return jnp.where((grp == 0) | (grp == 3), hi, jnp.where((grp == 1) | (grp == 4), lo, lolo))

    k_aug = jnp.where(first3, -parts(c * LOG2E), 0.0)
    q_aug = jnp.where(second3, parts(c * LOG2E - diag), 0.0)

    for g in range(A_HEADS // 2):
        sl = slice(g * LANES, (g + 1) * LANES)
        zq2, zk2, zv2 = qn[:, sl], kn[:, sl], zv[:, sl]
        for par in range(2):
            h = 2 * g + par
            keep = left if par == 0 else jnp.logical_not(left)
            oh_k = (second3 & (hsel == h)).astype(F32)
            oh_q = (first3 & (hsel == h)).astype(F32)
            k_ref[h] = jnp.where(keep, zk2, k_aug + oh_k).astype(BF16)
            qT_ref[h] = jnp.where(keep, zq2, q_aug + oh_q).T.astype(BF16)
            ones_lane = A_HEAD_DIM if par == 0 else 0
            vT_ref[h] = jnp.where(keep, zv2, (lane == ones_lane).astype(F32)).T.astype(BF16)

    p = proj(PK_P, BRANCH_WIDTH)
    row8 = lax.broadcasted_iota(I32, (SUBLANES, GROUP_DIM), 0)
    pos = i * T + lax.broadcasted_iota(I32, (T, GROUP_DIM), 0)

    def shift_down(v, tail, d):
        r = pltpu.roll(v, d, 0)
        rt = pltpu.roll(tail, d, 0)
        top = jnp.where(row8 < d, rt[0:SUBLANES], r[0:SUBLANES])
        return jnp.concatenate([top, r[SUBLANES:]], axis=0)

    pooled = []
    for gi, w in enumerate(POOL_WINDOWS):
        sl = slice(gi * GROUP_DIM, (gi + 1) * GROUP_DIM)
        s = p[:, sl]
        for lv in range(gi + 1):
            tail = halo_ref[lv, :, sl]
            halo_ref[lv, :, sl] = s[T - HALO:T, :]
            s = s + shift_down(s, tail, 1 << lv)
        cnt = jnp.minimum(pos + 1, w).astype(F32)
        pooled.append(s / cnt - p[:, sl])
    pooled = jnp.concatenate(pooled, axis=1).astype(BF16)
    yb = jnp.dot(pooled, wpool_ref[...], preferred_element_type=F32) * spool_ref[...]
    yb_ref[...] = yb.astype(yb_ref.dtype)

    gu = _gelu_tanh(proj(PK_U, BRANCH_WIDTH))
    gv = _gelu_tanh(proj(PK_SV, BRANCH_WIDTH))
    t_r = lax.broadcasted_iota(I32, (C_CHUNK, C_CHUNK), 0)
    t_c = lax.broadcasted_iota(I32, (C_CHUNK, C_CHUNK), 1)
    causal = t_r >= t_c
    for g in range(N_GROUPS):
        sl = slice(g * GROUP_DIM, (g + 1) * GROUP_DIM)
        vn = _rms(gv[:, sl], gsgu_ref[:, sl]).astype(BF16)
        wc = jnp.where(causal, wsgu_ref[g], 0.0).astype(BF16)
        for ch in range(T // C_CHUNK):
            rows = slice(ch * C_CHUNK, (ch + 1) * C_CHUNK)
            mixed = jnp.dot(wc, vn[rows], preferred_element_type=F32) + bsgu_ref[g]
            yc_ref[rows, sl] = (gu[rows, sl] * mixed).astype(yc_ref.dtype)


def _proj_call(x, w, layer, T):
    B, S, D = x.shape
    H = A_HEADS
    per_layer = functools.partial(_layer_spec, layer)
    return pl.pallas_call(
        functools.partial(_proj_kernel, T=T),
        grid=(B, S // T),
        in_specs=[
            pl.BlockSpec((None, T, D), lambda b, i: (b, i, 0)),
            per_layer(1, D), per_layer(D, PK_W), per_layer(1, LANES), per_layer(1, BRANCH_WIDTH),
            per_layer(1, BRANCH_WIDTH),
            per_layer(BRANCH_WIDTH, BRANCH_WIDTH), per_layer(1, BRANCH_WIDTH), per_layer(1, BRANCH_WIDTH),
            per_layer(N_GROUPS, C_CHUNK, C_CHUNK), per_layer(N_GROUPS, C_CHUNK, GROUP_DIM),
        ],
        out_specs=[
            pl.BlockSpec((None, H, LANES, T), lambda b, i: (b, 0, 0, i)),
            pl.BlockSpec((None, H, T, LANES), lambda b, i: (b, 0, i, 0)),
            pl.BlockSpec((None, H, LANES, T), lambda b, i: (b, 0, 0, i)),
            pl.BlockSpec((None, T, BRANCH_WIDTH), lambda b, i: (b, i, 0)),
            pl.BlockSpec((None, T, BRANCH_WIDTH), lambda b, i: (b, i, 0)),
            pl.BlockSpec((None, T, LANES), lambda b, i: (b, i, 0)),
        ],
        out_shape=[
            jax.ShapeDtypeStruct((B, H, LANES, S), BF16),
            jax.ShapeDtypeStruct((B, H, S, LANES), BF16),
            jax.ShapeDtypeStruct((B, H, LANES, S), BF16),
            jax.ShapeDtypeStruct((B, S, BRANCH_WIDTH), BF16),
            jax.ShapeDtypeStruct((B, S, BRANCH_WIDTH), BF16),
            jax.ShapeDtypeStruct((B, S, LANES), F32),
        ],
        scratch_shapes=[pltpu.VMEM((1, LANES), F32), pltpu.VMEM((4, HALO, BRANCH_WIDTH), F32)],
        compiler_params=pltpu.CompilerParams(
            dimension_semantics=("arbitrary", "arbitrary"), vmem_limit_bytes=VMEM_LIMIT),
        name="proj",
    )(x, w["g_mix"], w["w_pack"], w["b_f"], w["g_q"], w["g_k"], w["w_pool"], w["s_pool"],
      w["g_sgu"], w["w_sgu"], w["b_sgu"])


def _attn_kernel(first_ref, qT_ref, k_ref, vT_ref, o_ref, m_ref, acc_ref, s_ref, *, tq, tk, n_q, online_max):
    assert tq == tk
    FIRST = 2

    def reset():
        acc_ref[...] = jnp.zeros_like(acc_ref)
        if online_max:
            m_ref[...] = jnp.full(m_ref.shape, NEG, F32)

    def scores(qi, j, slot):
        k0 = pl.multiple_of(j * tk, tk)
        q0 = pl.multiple_of(qi * tq, tq)
        for par in range(2):
            s_ref[slot, par] = jnp.dot(k_ref[par, pl.ds(k0, tk), :], qT_ref[par, :, pl.ds(q0, tq)],
                                       preferred_element_type=F32)

    def consume(qi, j, slot, masked):
        k0 = pl.multiple_of(j * tk, tk)
        q0 = qi * tq
        for par in range(2):
            s = s_ref[slot, par]
            if masked:
                kpos = k0 + lax.broadcasted_iota(I32, (tk, tq), 0)
                qpos = q0 + lax.broadcasted_iota(I32, (tk, tq), 1)
                s = jnp.where(kpos <= qpos, s, NEG)
            v_blk = vT_ref[par, :, pl.ds(k0, tk)]
            if online_max:
                m_old = m_ref[par]
                m_new = jnp.maximum(m_old, jnp.max(s, axis=0, keepdims=True))
                p = jnp.exp2(s - m_new).astype(BF16)
                pv = jnp.dot(v_blk, p, preferred_element_type=F32)
                acc_ref[par] = acc_ref[par] * jnp.exp2(m_old - m_new) + pv
                m_ref[par] = m_new
            else:
                p = jnp.exp2(s).astype(BF16)
                acc_ref[par] += jnp.dot(v_blk, p, preferred_element_type=F32)

    def finish(qi):
        lane = lax.broadcasted_iota(I32, (tq, LANES), 1)
        outs = []
        for par in range(2):
            acc = acc_ref[par]
            ones_row = A_HEAD_DIM if par == 0 else 0
            l = acc[ones_row:ones_row + 1, :]
            outs.append((acc * (1.0 / l)).T)
        q0 = pl.multiple_of(qi * tq, tq)
        o_ref[pl.ds(q0, tq), :] = jnp.where(lane < A_HEAD_DIM, outs[0], outs[1]).astype(o_ref.dtype)
        reset()

    b_id = pl.program_id(0)
    g_id = pl.program_id(1)

    def first_block(qi):
        return first_ref[(b_id * pl.num_programs(1) + g_id) * n_q + qi]

    reset()
    scores(0, 0, FIRST)
    consume(0, 0, FIRST, True)
    finish(0)
    if n_q > 1:
        scores(1, first_block(1), FIRST)

    def query_block(qi, carry):
        base = first_block(qi)
        n_mid = qi - 1 - base
        nxt = jnp.minimum(qi + 1, n_q - 1)
        scores(qi, base + 1, 1)
        consume(qi, base, FIRST, False)

        def trip(t, c):
            j0 = base + 1 + ATTN_BLOCKS_PER_TRIP * t
            for u in range(ATTN_BLOCKS_PER_TRIP):
                scores(qi, j0 + u + 1, u % 2)
                consume(qi, j0 + u, (u + 1) % 2, False)
            return c

        lax.fori_loop(0, n_mid // ATTN_BLOCKS_PER_TRIP, trip, 0)
        j0 = base + 1 + (n_mid // ATTN_BLOCKS_PER_TRIP) * ATTN_BLOCKS_PER_TRIP
        for rem in range(ATTN_BLOCKS_PER_TRIP):
            @pl.when(n_mid % ATTN_BLOCKS_PER_TRIP == rem)
            def _(rem=rem):
                for u in range(rem):
                    scores(qi, j0 + u + 1, u % 2)
                    consume(qi, j0 + u, (u + 1) % 2, False)
                scores(nxt, first_block(nxt), FIRST)
                consume(qi, j0 + rem, (rem + 1) % 2, True)
                finish(qi)
        return carry

    lax.fori_loop(1, n_q, query_block, 0)


def _attn_call(first, qT, k, vT, tq, tk, online_max):
    B, H, _, S = qT.shape
    grid_spec = pltpu.PrefetchScalarGridSpec(
        num_scalar_prefetch=1,
        grid=(B, H // 2),
        in_specs=[
            pl.BlockSpec((None, 2, LANES, S), lambda b, g, first: (b, g, 0, 0)),
            pl.BlockSpec((None, 2, S, LANES), lambda b, g, first: (b, g, 0, 0)),
            pl.BlockSpec((None, 2, LANES, S), lambda b, g, first: (b, g, 0, 0)),
        ],
        out_specs=pl.BlockSpec((None, S, LANES), lambda b, g, first: (b, 0, g)),
        scratch_shapes=[pltpu.VMEM((2, 1, tq), F32), pltpu.VMEM((2, LANES, tq), F32),
                        pltpu.VMEM((3, 2, tk, tq), F32)],
    )
    return pl.pallas_call(
        functools.partial(_attn_kernel, tq=tq, tk=tk, n_q=S // tq, online_max=online_max),
        grid_spec=grid_spec,
        out_shape=jax.ShapeDtypeStruct((B, S, BRANCH_WIDTH), BF16),
        compiler_params=pltpu.CompilerParams(
            dimension_semantics=("parallel", "parallel"), vmem_limit_bytes=VMEM_LIMIT),
        name="attn_online" if online_max else "attn",
    )(first, qT, k, vT)


def _first_live_block(c, bound, tq, tk):
    c_q = c[:, 0::tq, 0:A_HEADS]
    c_k = c[:, tk - 1::tk, 0:A_HEADS]
    top = bound + LOG2E * (c_q[:, :, None, :] - c_k[:, None, :, :])
    dead = top < DEAD_SCORE
    dead_pair = dead[..., 0::2] & dead[..., 1::2]
    n_q = c_q.shape[1]
    n_dead = jnp.sum(jnp.cumprod(dead_pair.astype(I32), axis=2), axis=2)
    limit = jnp.maximum(jnp.arange(n_q, dtype=I32) - 1, 0)[None, :, None]
    first = jnp.minimum(n_dead, limit)
    return first.transpose(0, 2, 1).reshape(-1).astype(I32)


def _attention(qT, k, vT, c, g_q, g_k, tq, tk):
    bound = 16.0 * LOG2E * jnp.max(jnp.abs(g_q)) * jnp.max(jnp.abs(g_k))
    first = _first_live_block(c, bound, tq, tk)
    return lax.cond(bound <= SCORE_CAP,
                    lambda: _attn_call(first, qT, k, vT, tq, tk, False),
                    lambda: _attn_call(first, qT, k, vT, tq, tk, True))


def _merge_kernel(x_ref, ya_ref, yb_ref, yc_ref, gmix_ref, wg_ref, bg_ref, wb_ref, wo_ref,
                  gffn_ref, wr_ref, br_ref,
                  x1_ref, h2_ref, route_ref, route_t_ref, cnt_ref, carry_ref, logit_ref, *, T):
    i = pl.program_id(0)
    n_tiles = pl.num_programs(0) - 1

    @pl.when(i == 0)
    def _():
        carry_ref[...] = jnp.zeros_like(carry_ref)
        logit_ref[...] = jnp.zeros_like(logit_ref)

    def route_previous():
        live = (i > 0).astype(F32)
        logits = logit_ref[...]
        lane = lax.broadcasted_iota(I32, (T, LANES), 1).astype(F32)
        big = float(LANES)

        def first_argmax(v):
            m = jnp.max(v, axis=-1, keepdims=True)
            return m, jnp.min(jnp.where(v == m, lane, big), axis=-1, keepdims=True)

        lg = jnp.where(lane < N_EXPERT_GROUPS, logits, NEG)
        mg, grp = first_argmax(lg)
        p_grp = 1.0 / jnp.sum(jnp.exp(lg - mg), axis=-1, keepdims=True)
        lo_lane = N_EXPERT_GROUPS + grp * EXPERTS_PER_GROUP
        le = jnp.where((lane >= lo_lane) & (lane < lo_lane + EXPERTS_PER_GROUP), logits, NEG)
        m1, i1 = first_argmax(le)
        m2, i2 = first_argmax(jnp.where(lane == i1, NEG, le))
        e21 = jnp.exp(m2 - m1)
        g1 = p_grp / (1.0 + e21)
        g2 = p_grp * e21 / (1.0 + e21)
        e1 = i1 - N_EXPERT_GROUPS
        e2 = i2 - N_EXPERT_GROUPS

        oh1 = lane == e1
        oh2 = lane == e2
        sel = (oh1 | oh2).astype(F32)
        r_i = lax.broadcasted_iota(I32, (T, T), 0)
        c_i = lax.broadcasted_iota(I32, (T, T), 1)
        before = (r_i > c_i).astype(BF16)
        seen = jnp.dot(before, sel.astype(BF16), preferred_element_type=F32) + carry_ref[...]
        r1 = jnp.sum(jnp.where(oh1, seen, 0.0), axis=-1, keepdims=True)
        r2 = jnp.sum(jnp.where(oh2, seen, 0.0), axis=-1, keepdims=True)
        carry_ref[...] = carry_ref[...] + live * jnp.sum(sel, axis=0, keepdims=True)
        cnt_ref[...] = carry_ref[...]

        route = jnp.zeros((T, LANES), F32)
        for idx, val in enumerate((e1, e2, g1, g2, r1, r2)):
            route = jnp.where(lane == idx, val, route)
        route_ref[...] = route
        route_t_ref[...] = route.T[0:ROUTE_ROWS]

    @pl.when(i < n_tiles)
    def _():
        route_previous()
        x = x_ref[...]
        hb = _rms(x, gmix_ref[...]).astype(BF16)
        merged = None
        for bi, y_ref in enumerate((ya_ref, yb_ref, yc_ref)):
            sl = slice(bi * D_MODEL, (bi + 1) * D_MODEL)
            gate = jax.nn.sigmoid(jnp.dot(hb, wg_ref[:, sl], preferred_element_type=F32) + bg_ref[:, sl])
            term = gate * jnp.dot(y_ref[...], wb_ref[bi], preferred_element_type=F32)
            merged = term if merged is None else merged + term
        x1 = x + jnp.dot(merged.astype(BF16), wo_ref[...], preferred_element_type=F32)
        x1_ref[...] = x1
        h2 = _rms(x1, gffn_ref[...])
        _store_row_tiles(h2_ref, _pack_pairs(h2))

        h_hi = h2.astype(BF16)
        h_lo = (h2 - h_hi.astype(F32)).astype(BF16)
        hw = jnp.dot(h_hi, wr_ref[...], preferred_element_type=F32)
        lw = jnp.dot(h_lo, wr_ref[:, 0:LANES], preferred_element_type=F32)
        logit_ref[...] = hw[:, 0:LANES] + (hw[:, LANES:] + lw) + br_ref[...]

    @pl.when(i == n_tiles)
    def _():
        route_previous()


def _row_tile_rows(s, n_rows, row0=0):
    return pl.ds(row0 * ROW_TILE + s, n_rows, stride=ROW_TILE)


def _pack_pairs(v):
    hi = lax.bitcast_convert_type(v[:, :HALF_D].astype(BF16).astype(F32), U32)
    lo = lax.bitcast_convert_type(v[:, HALF_D:].astype(BF16).astype(F32), U32)
    return hi | (lo >> 16)


def _unpack_pairs(w):
    hi = lax.bitcast_convert_type(w & jnp.uint32(0xFFFF0000), F32)
    lo = lax.bitcast_convert_type(w << 16, F32)
    return jnp.concatenate([hi, lo], axis=-1)


def _store_row_tiles(ref, words, row0=0):
    for s in range(ROW_TILE):
        ref[_row_tile_rows(s, words.shape[0], row0), :] = words[:, s * LANES:(s + 1) * LANES]


def _load_row_tiles(ref, n_rows, row0=0):
    return jnp.concatenate([ref[_row_tile_rows(s, n_rows, row0), :] for s in range(ROW_TILE)], axis=-1)


def _merge_call(x2d, ya, yb, yc, w, layer, T):
    N, D = x2d.shape
    n_tiles = N // T
    const = functools.partial(_layer_spec, layer)
    cur = lambda i: jnp.minimum(i, n_tiles - 1)
    prev = lambda i: jnp.maximum(i - 1, 0)
    tile = lambda width: pl.BlockSpec((T, width), lambda i: (cur(i), 0))
    return pl.pallas_call(
        functools.partial(_merge_kernel, T=T),
        grid=(n_tiles + 1,),
        in_specs=[
            tile(D), tile(BRANCH_WIDTH), tile(BRANCH_WIDTH), tile(BRANCH_WIDTH),
            const(1, D), const(D, N_BRANCH * D), const(1, N_BRANCH * D),
            const(N_BRANCH, BRANCH_WIDTH, D), const(D, D), const(1, D), const(D, 2 * LANES), const(1, LANES),
        ],
        out_specs=[tile(D), pl.BlockSpec((T * ROW_TILE, LANES), lambda i: (cur(i), 0)),
                   pl.BlockSpec((T, LANES), lambda i: (prev(i), 0)),
                   pl.BlockSpec((ROUTE_ROWS, T), lambda i: (0, prev(i))),
                   pl.BlockSpec((1, LANES), lambda i: (0, 0))],
        out_shape=[
            jax.ShapeDtypeStruct((N, D), F32),
            jax.ShapeDtypeStruct((N * ROW_TILE, LANES), U32),
            jax.ShapeDtypeStruct((N, LANES), F32),
            jax.ShapeDtypeStruct((ROUTE_ROWS, N), F32),
            jax.ShapeDtypeStruct((1, LANES), F32),
        ],
        scratch_shapes=[pltpu.VMEM((1, LANES), F32), pltpu.VMEM((T, LANES), F32)],
        compiler_params=pltpu.CompilerParams(
            dimension_semantics=("arbitrary",), vmem_limit_bytes=VMEM_LIMIT),
        name="merge",
    )(x2d, ya, yb, yc, w["g_mix"], w["w_gate"], w["b_gate"], w["w_branch"], w["w_out"],
      w["g_ffn"], w["w_router"], w["b_router"])


def _dispatch_kernel(dest_ref, h2_ref, xs_ref, sem, *, T):
    def row_copy(t, d):
        return pltpu.make_async_copy(h2_ref.at[_row_tile(t)], xs_ref.at[_row_tile(d)], sem)

    def body(tb, carry):
        t0 = tb * ISSUE_UNROLL
        for u in range(ISSUE_UNROLL):
            for kk in range(TOP_K):
                row_copy(t0 + u, dest_ref[0, kk * T + t0 + u]).start(priority=kk % N_DMA_PRIORITIES)
        return carry

    lax.fori_loop(0, T // ISSUE_UNROLL, body, 0)
    for kk in range(TOP_K):
        pltpu.make_async_copy(h2_ref, xs_ref.at[pl.ds(0, T * ROW_TILE)], sem).wait()


def _row_tile(r):
    return pl.ds(pl.multiple_of(r * ROW_TILE, ROW_TILE), ROW_TILE)


def _dispatch_call(dest3, h2, T):
    N = h2.shape[0] // ROW_TILE
    return pl.pallas_call(
        functools.partial(_dispatch_kernel, T=T),
        grid=(N // T,),
        in_specs=[
            pl.BlockSpec((None, 1, TOP_K * T), lambda i: (i, 0, 0), memory_space=pltpu.SMEM),
            pl.BlockSpec((T * ROW_TILE, LANES), lambda i: (i, 0)),
        ],
        out_specs=pl.BlockSpec(memory_space=pl.ANY),
        out_shape=jax.ShapeDtypeStruct((TOP_K * N * ROW_TILE, LANES), U32),
        scratch_shapes=[pltpu.SemaphoreType.DMA],
        compiler_params=pltpu.CompilerParams(
            dimension_semantics=("arbitrary",), vmem_limit_bytes=VMEM_LIMIT),
        name="dispatch",
    )(dest3, h2)


def _gmm_kernel(blk_ref, exp_ref, lo_ref, hi_ref, xs_ref, w1_ref, w3_ref, w2_ref, y_ref,
                w13_bf, w2_bf, *, bm):
    i = pl.program_id(0)
    lo = lo_ref[i]
    hi = hi_ref[i]
    prev = jnp.maximum(i - 1, 0)
    first = jnp.logical_or(i == 0, blk_ref[i] != blk_ref[prev])
    new_expert = jnp.logical_or(i == 0, exp_ref[i] != exp_ref[prev])

    @pl.when(new_expert)
    def _():
        w13_bf[:, 0:D_EXPERT] = w1_ref[...].astype(BF16)
        w13_bf[:, D_EXPERT:] = w3_ref[...].astype(BF16)
        w2_bf[...] = w2_ref[...].astype(BF16)

    @pl.when(first)
    def _():
        y_ref[...] = jnp.zeros_like(y_ref)

    @pl.when(hi > lo)
    def _():
        n_part = 2
        part = bm // n_part
        for h in range(n_part):
            x = _unpack_pairs(_load_row_tiles(xs_ref, part, h * part)).astype(BF16)
            ab = jnp.dot(x, w13_bf[...], preferred_element_type=F32)
            a = ab[:, 0:D_EXPERT]
            b = ab[:, D_EXPERT:]
            mid = (a * jax.nn.sigmoid(a) * b).astype(BF16)
            out = _pack_pairs(jnp.dot(mid, w2_bf[...], preferred_element_type=F32))
            row = h * part + lax.broadcasted_iota(I32, (part, HALF_D), 0)
            mine = (row >= lo) & (row < hi)
            old = _load_row_tiles(y_ref, part, h * part)
            _store_row_tiles(y_ref, jnp.where(mine, out, old), h * part)


def _gmm_call(plan, xs, w1, w3, w2, layer, bm):
    A = xs.shape[0] // ROW_TILE
    D = D_MODEL
    n_items = plan[0].shape[0]
    grid_spec = pltpu.PrefetchScalarGridSpec(
        num_scalar_prefetch=4,
        grid=(n_items,),
        in_specs=[
            pl.BlockSpec((bm * ROW_TILE, LANES), lambda i, blk, ex, lo, hi: (blk[i], 0)),
            pl.BlockSpec((None, None, D, D_EXPERT), lambda i, blk, ex, lo, hi: (layer, ex[i], 0, 0)),
            pl.BlockSpec((None, None, D, D_EXPERT), lambda i, blk, ex, lo, hi: (layer, ex[i], 0, 0)),
            pl.BlockSpec((None, None, D_EXPERT, D), lambda i, blk, ex, lo, hi: (layer, ex[i], 0, 0)),
        ],
        out_specs=pl.BlockSpec((bm * ROW_TILE, LANES), lambda i, blk, ex, lo, hi: (blk[i], 0)),
        scratch_shapes=[pltpu.VMEM((D, 2 * D_EXPERT), BF16), pltpu.VMEM((D_EXPERT, D), BF16)],
    )
    return pl.pallas_call(
        functools.partial(_gmm_kernel, bm=bm),
        grid_spec=grid_spec,
        out_shape=jax.ShapeDtypeStruct((A * ROW_TILE, LANES), U32),
        compiler_params=pltpu.CompilerParams(
            dimension_semantics=("arbitrary",), vmem_limit_bytes=VMEM_LIMIT),
        name="gmm",
    )(*plan, xs, w1, w3, w2)


def _gmm_plan(counts, n_rows, bm):
    n_blk = n_rows // bm
    n_items = n_blk + N_EXPERTS - 1
    ends = jnp.cumsum(counts)
    starts = ends - counts
    first_blk = starts // bm
    n_it = jnp.where(counts > 0, (ends - 1) // bm - first_blk + 1, 0)
    item_end = jnp.cumsum(n_it)
    item_start = item_end - n_it
    total = item_end[-1]
    ids = jnp.arange(n_items, dtype=I32)
    valid = ids < total
    ex = jnp.minimum(jnp.sum(item_end[None, :] <= ids[:, None], axis=1).astype(I32), N_EXPERTS - 1)
    ex = jnp.where(valid, ex, ex[jnp.maximum(total - 1, 0)])
    blk = jnp.where(valid, first_blk[ex] + ids - item_start[ex], n_blk - 1)
    lo = jnp.where(valid, jnp.maximum(starts[ex], blk * bm) - blk * bm, 0)
    hi = jnp.where(valid, jnp.minimum(ends[ex], (blk + 1) * bm) - blk * bm, 0)
    return blk.astype(I32), ex.astype(I32), lo.astype(I32), hi.astype(I32)


COMBINE_SLOTS = 3
COMBINE_ROWS = 32


def _combine_kernel(dest_ref, dest_next_ref, dest_ahead_ref, x1_ref, route_ref, y_hbm, o_ref, buf, sem, *, T):
    i = pl.program_id(0)
    n_steps = pl.num_programs(0)
    n_trips = T // COMBINE_ROWS

    def issue_rows(d_ref, slot, t0):
        for u in range(COMBINE_ROWS):
            for kk in range(TOP_K):
                d = d_ref[0, kk * T + t0 + u]
                pltpu.make_async_copy(y_hbm.at[_row_tile(d)], buf.at[slot * TOP_K + kk, _row_tile(t0 + u)],
                                      sem.at[slot]).start(priority=kk % N_DMA_PRIORITIES)

    def combine_rows(slot, t0):
        rows = pl.ds(t0, COMBINE_ROWS)
        g1 = route_ref[rows, 2:3]
        g2 = route_ref[rows, 3:4]
        for s in range(ROW_TILE):
            words = pl.ds(t0 * ROW_TILE + s, COMBINE_ROWS, stride=ROW_TILE)
            y1 = _unpack_pairs(buf[slot * TOP_K, words, :])
            y2 = _unpack_pairs(buf[slot * TOP_K + 1, words, :])
            for half in range(2):
                src = slice(half * LANES, (half + 1) * LANES)
                dst = slice(half * HALF_D + s * LANES, half * HALF_D + (s + 1) * LANES)
                o_ref[rows, dst] = x1_ref[rows, dst] + (g1 * y1[:, src] + g2 * y2[:, src])

    def loop(body):
        def trip(tb, carry):
            body(pl.multiple_of(tb * COMBINE_ROWS, COMBINE_ROWS))
            return carry

        lax.fori_loop(0, n_trips, trip, 0)

    @pl.when(i == 0)
    def _():
        loop(lambda t0: issue_rows(dest_ref, 0, t0))

        @pl.when(n_steps > 1)
        def _():
            loop(lambda t0: issue_rows(dest_next_ref, 1, t0))

    for slot in range(COMBINE_SLOTS):
        @pl.when(i % COMBINE_SLOTS == slot)
        def _(slot=slot):
            for kk in range(TOP_K):
                pltpu.make_async_copy(y_hbm.at[pl.ds(0, T * ROW_TILE)], buf.at[slot * TOP_K + kk],
                                      sem.at[slot]).wait()
            ahead = (slot + 2) % COMBINE_SLOTS

            @pl.when(i + 2 < n_steps)
            def _():
                def both(t0):
                    issue_rows(dest_ahead_ref, ahead, t0)
                    combine_rows(slot, t0)

                loop(both)

            @pl.when(i + 2 >= n_steps)
            def _():
                loop(lambda t0: combine_rows(slot, t0))


def _combine_call(dest3, x1, route, y, T):
    N, D = x1.shape
    n_steps = N // T
    dest_spec = lambda ahead: pl.BlockSpec(
        (None, 1, TOP_K * T), lambda i: (jnp.minimum(i + ahead, n_steps - 1), 0, 0), memory_space=pltpu.SMEM)
    return pl.pallas_call(
        functools.partial(_combine_kernel, T=T),
        grid=(n_steps,),
        in_specs=[
            dest_spec(0), dest_spec(1), dest_spec(2),
            pl.BlockSpec((T, D), lambda i: (i, 0)),
            pl.BlockSpec((T, LANES), lambda i: (i, 0)),
            pl.BlockSpec(memory_space=pl.ANY),
        ],
        out_specs=pl.BlockSpec((T, D), lambda i: (i, 0)),
        out_shape=jax.ShapeDtypeStruct((N, D), F32),
        scratch_shapes=[pltpu.VMEM((COMBINE_SLOTS * TOP_K, T * ROW_TILE, LANES), U32),
                        pltpu.SemaphoreType.DMA((COMBINE_SLOTS,))],
        compiler_params=pltpu.CompilerParams(
            dimension_semantics=("arbitrary",), vmem_limit_bytes=VMEM_LIMIT),
        name="combine",
    )(dest3, dest3, dest3, x1, route, y)


def _block_diag(blocks):
    n = blocks.shape[1]
    eye = jnp.eye(n, dtype=blocks.dtype)
    out = blocks[:, :, :, None, :] * eye[None, :, None, :, None]
    return out.reshape(blocks.shape[0], n * blocks.shape[2], n * blocks.shape[3])


def _rep_forget(cols):
    lead = cols.shape[:-1]
    half = jnp.concatenate(
        [jnp.tile(cols, (1,) * len(lead) + (F_REP,)), jnp.zeros(lead + (A_HEAD_DIM - F_REP * A_HEADS,), cols.dtype)],
        axis=-1)
    return jnp.concatenate([half, half], axis=-1)


def _prep_all(g_mix, w_in, b_fgate, b_gate, g_q, g_k, w_pool, s_pool, g_sgu, w_sgu, b_sgu,
              w_branch, w_out, g_ffn, w_rg, b_rg, w_re, b_re):
    L = w_in.shape[0]
    order = jnp.argsort(b_fgate, axis=1)
    qkv = w_in[:, :, 0:OFF_F].reshape(L, D_MODEL, 3, A_HEADS, A_HEAD_DIM)
    qkv = jnp.take_along_axis(qkv, order[:, None, None, :, None], axis=3).reshape(L, D_MODEL, OFF_F)
    w_f = jnp.take_along_axis(w_in[:, :, OFF_F:OFF_P], order[:, None, :], axis=2)
    b_f = jnp.take_along_axis(b_fgate, order, axis=1)
    wb0 = w_branch[:, 0].reshape(L, A_HEADS, A_HEAD_DIM, D_MODEL)
    wb0 = jnp.take_along_axis(wb0, order[:, :, None, None], axis=1).reshape(L, 1, BRANCH_WIDTH, D_MODEL)
    pad_r = LANES - N_EXPERT_GROUPS - N_EXPERTS
    w_r = jnp.concatenate([w_rg, w_re, jnp.zeros((L, D_MODEL, pad_r), F32)], axis=2)
    w_r_hi = w_r.astype(BF16)
    w_r_lo = (w_r - w_r_hi.astype(F32)).astype(BF16)
    row = lambda v: v.reshape(L, 1, v.shape[-1])
    return dict(
        g_mix=row(g_mix),
        w_pack=jnp.concatenate([qkv, _rep_forget(w_f), w_in[:, :, OFF_P:OFF_G]], axis=2).astype(BF16),
        b_f=_rep_forget(row(b_f)),
        g_q=row(jnp.tile(g_q, (1, A_HEADS))),
        g_k=row(jnp.tile(g_k, (1, A_HEADS))),
        w_pool=_block_diag(w_pool).astype(BF16),
        s_pool=row(s_pool),
        g_sgu=row(g_sgu),
        w_sgu=w_sgu,
        b_sgu=jnp.broadcast_to(b_sgu[:, :, :, None], (L, N_GROUPS, C_CHUNK, GROUP_DIM)),
        w_gate=w_in[:, :, OFF_G:].astype(BF16),
        b_gate=b_gate.reshape(L, 1, N_BRANCH * D_MODEL),
        w_branch=jnp.concatenate([wb0, w_branch[:, 1:]], axis=1).astype(BF16),
        w_out=w_out.astype(BF16),
        g_ffn=row(g_ffn),
        w_router=jnp.concatenate([w_r_hi, w_r_lo], axis=2),
        b_router=row(jnp.concatenate([b_rg, b_re, jnp.zeros((L, pad_r), F32)], axis=1)),
    )


def kernel(x, g_mix, w_in, b_fgate, b_gate, g_q, g_k, w_pool, s_pool, g_sgu, w_sgu, b_sgu, w_branch, w_out,
           g_ffn, w_rg, b_rg, w_re, b_re, w1, w3, w2):
    B, S, D = x.shape
    assert D == D_MODEL and x.dtype == F32
    N = B * S
    T, tq, tk, bm, tm = _tiles(S)
    assert S % T == 0 and S % tq == 0 and tq == tk and T % C_CHUNK == 0
    assert (TOP_K * N) % bm == 0 and N % tm == 0 and tm % ISSUE_UNROLL == 0 and tm % COMBINE_ROWS == 0
    depth = w_in.shape[0]
    w = _prep_all(g_mix, w_in, b_fgate, b_gate, g_q, g_k, w_pool, s_pool, g_sgu, w_sgu, b_sgu,
                  w_branch, w_out, g_ffn, w_rg, b_rg, w_re, b_re)
    for l in range(depth):
        qT, k, vT, yb, yc, c = _proj_call(x, w, l, T)
        ya = _attention(qT, k, vT, c, g_q[l], g_k[l], tq, tk)
        x1, h2, route, route_t, cnt = _merge_call(
            x.reshape(N, D), ya.reshape(N, -1), yb.reshape(N, -1), yc.reshape(N, -1), w, l, T)
        counts = cnt[0, :N_EXPERTS].astype(I32)
        starts = jnp.cumsum(counts) - counts
        experts = route_t[0:TOP_K].astype(I32)
        expert_ids = jnp.arange(N_EXPERTS, dtype=I32)[:, None, None]
        start_of = jnp.sum(jnp.where(experts[None] == expert_ids, starts[:, None, None], 0), axis=0)
        dest = start_of + route_t[4:4 + TOP_K].astype(I32)
        dest3 = dest.reshape(TOP_K, N // tm, tm).transpose(1, 0, 2).reshape(N // tm, 1, TOP_K * tm)
        xs = _dispatch_call(dest3, h2, tm)
        y = _gmm_call(_gmm_plan(counts, TOP_K * N, bm), xs, w1, w3, w2, l, bm)
        x = _combine_call(dest3, x1, route, y, tm).reshape(B, S, D)
    return x
```

```python
import functools

import jax
import jax.numpy as jnp
from jax import lax
from jax.experimental import pallas as pl
from jax.experimental.pallas import tpu as pltpu

F32 = jnp.float32
BF16 = jnp.bfloat16
I32 = jnp.int32
U32 = jnp.uint32

D_MODEL = 1024
A_HEADS = 8
A_HEAD_DIM = 64
BRANCH_WIDTH = 512
N_BRANCH = 3
POOL_WINDOWS = (2, 4, 8, 16)
GROUP_DIM = 128
N_GROUPS = 4
C_CHUNK = 128
N_EXPERT_GROUPS = 4
EXPERTS_PER_GROUP = 8
N_EXPERTS = N_EXPERT_GROUPS * EXPERTS_PER_GROUP
TOP_K = 2
D_EXPERT = 256
RMS_EPS = 1e-6
OFF_F = 3 * BRANCH_WIDTH
OFF_P = OFF_F + A_HEADS
OFF_U = OFF_P + BRANCH_WIDTH
OFF_SV = OFF_U + BRANCH_WIDTH
OFF_G = OFF_SV + BRANCH_WIDTH

LANES = 128
SUBLANES = 8
V7X_VMEM_BYTES = 64 * 1024 * 1024
VMEM_LIMIT = V7X_VMEM_BYTES * 7 // 8
ROW_TILE = D_MODEL // (2 * LANES)
HALF_D = D_MODEL // 2
ROUTE_ROWS = SUBLANES
ISSUE_UNROLL = 8
N_DMA_PRIORITIES = 2
ATTN_BLOCKS_PER_TRIP = 4

PK_Q, PK_K, PK_V = 0, BRANCH_WIDTH, 2 * BRANCH_WIDTH
PK_F = 3 * BRANCH_WIDTH
PK_P = PK_F + LANES
PK_U = PK_P + BRANCH_WIDTH
PK_SV = PK_U + BRANCH_WIDTH
PK_W = PK_SV + BRANCH_WIDTH
F_REP = 6

LOG2E = 1.4426950408889634
SCORE_CAP = 96.0
DEAD_SCORE = -152.0
NEG = -1e30
HALO = 16


def _tiles(seq_len):
    t_proj = min(512, seq_len)
    t_q = min(512, seq_len)
    t_k = min(512, seq_len)
    moe_block = 512
    t_move = min(1024, seq_len)
    return t_proj, t_q, t_k, moe_block, t_move


def _layer_spec(layer, *shape):
    index = (layer,) + (0,) * len(shape)
    return pl.BlockSpec((None,) + shape, lambda *_: index, pipeline_mode=pl.Buffered(1))


def _rms(x, g):
    return x * lax.rsqrt(jnp.mean(x * x, axis=-1, keepdims=True) + RMS_EPS) * g


def _gelu_tanh(x):
    cdf = 0.5 * (1.0 + jnp.tanh(0.7978845608028654 * (x + 0.044715 * (x * x * x))))
    return x * cdf


def _log_sigmoid(x):
    return jnp.minimum(x, 0.0) - jnp.log1p(jnp.exp(-jnp.abs(x)))


def _split3(c):
    hi = c.astype(BF16).astype(F32)
    r = c - hi
    lo = r.astype(BF16).astype(F32)
    lolo = (r - lo).astype(BF16).astype(F32)
    return hi, lo, lolo


def _proj_kernel(x_ref, gmix_ref, w_ref, bf_ref, gq_ref, gk_ref, wpool_ref, spool_ref,
                 gsgu_ref, wsgu_ref, bsgu_ref,
                 qT_ref, k_ref, vT_ref, yb_ref, yc_ref, c_ref,
                 carry_ref, halo_ref, *, T):
    i = pl.program_id(1)

    @pl.when(i == 0)
    def _():
        carry_ref[...] = jnp.zeros_like(carry_ref)
        halo_ref[...] = jnp.zeros_like(halo_ref)

    hb = _rms(x_ref[...], gmix_ref[...]).astype(BF16)

    def proj(lo, width):
        return jnp.dot(hb, w_ref[:, lo:lo + width], preferred_element_type=F32)

    lane = lax.broadcasted_iota(I32, (T, LANES), 1)
    left = lane < A_HEAD_DIM

    def head_sums(v):
        cols = []
        for g in range(A_HEADS // 2):
            pair = v[:, g * LANES:(g + 1) * LANES]
            cols.append(jnp.sum(jnp.where(left, pair, 0.0), axis=-1, keepdims=True))
            cols.append(jnp.sum(jnp.where(left, 0.0, pair), axis=-1, keepdims=True))
        return cols

    def head_norm(z, g):
        cols = head_sums(z * z)
        ss = jnp.concatenate([jnp.where(left, cols[2 * g], cols[2 * g + 1]) for g in range(A_HEADS // 2)],
                             axis=-1)
        return z * lax.rsqrt(ss * (1.0 / A_HEAD_DIM) + RMS_EPS) * g

    qn = head_norm(proj(PK_Q, BRANCH_WIDTH), gq_ref[...]) * (A_HEAD_DIM ** -0.5 * LOG2E)
    kn = head_norm(proj(PK_K, BRANCH_WIDTH), gk_ref[...])
    zv = proj(PK_V, BRANCH_WIDTH)
    d_cols = head_sums(qn * kn)
    gate_head = lane & (A_HEADS - 1)
    diag = jnp.zeros((T, LANES), F32)
    for h in range(A_HEADS):
        diag = jnp.where(gate_head == h, d_cols[h], diag)

    logf = _log_sigmoid(proj(PK_F, LANES) + bf_ref[...])
    r_i = lax.broadcasted_iota(I32, (C_CHUNK, C_CHUNK), 0)
    c_i = lax.broadcasted_iota(I32, (C_CHUNK, C_CHUNK), 1)
    tri = (r_i >= c_i).astype(BF16)
    offset = carry_ref[...]
    c_chunks = []
    for ch in range(T // C_CHUNK):
        rows = logf[ch * C_CHUNK:(ch + 1) * C_CHUNK]
        local = sum(jnp.dot(tri, part.astype(BF16), preferred_element_type=F32) for part in _split3(rows))
        c_chunks.append(local + offset)
        offset = c_chunks[-1][C_CHUNK - 1:C_CHUNK, :]
    c = jnp.concatenate(c_chunks, axis=0)
    carry_ref[...] = offset
    c_ref[...] = c

    lm = lane & (A_HEAD_DIM - 1)
    grp = lm >> (A_HEADS.bit_length() - 1)
    hsel = gate_head
    first3 = grp < 3
    second3 = (grp >= 3) & (grp < F_REP)

    def parts(v):
        hi, lo, lolo = _split3(v)
        return jnp.where((grp == 0) | (grp == 3), hi, jnp.where((grp == 1) | (grp == 4), lo, lolo))

    k_aug = jnp.where(first3, -parts(c * LOG2E), 0.0)
    q_aug = jnp.where(second3, parts(c * LOG2E - diag), 0.0)

    for g in range(A_HEADS // 2):
        sl = slice(g * LANES, (g + 1) * LANES)
        zq2, zk2, zv2 = qn[:, sl], kn[:, sl], zv[:, sl]
        for par in range(2):
            h = 2 * g + par
            keep = left if par == 0 else jnp.logical_not(left)
            oh_k = (second3 & (hsel == h)).astype(F32)
            oh_q = (first3 & (hsel == h)).astype(F32)
            k_ref[h] = jnp.where(keep, zk2, k_aug + oh_k).astype(BF16)
            qT_ref[h] = jnp.where(keep, zq2, q_aug + oh_q).T.astype(BF16)
            ones_lane = A_HEAD_DIM if par == 0 else 0
            vT_ref[h] = jnp.where(keep, zv2, (lane == ones_lane).astype(F32)).T.astype(BF16)

    p = proj(PK_P, BRANCH_WIDTH)
    row8 = lax.broadcasted_iota(I32, (SUBLANES, GROUP_DIM), 0)
    pos = i * T + lax.broadcasted_iota(I32, (T, GROUP_DIM), 0)

    def shift_down(v, tail, d):
        r = pltpu.roll(v, d, 0)
        rt = pltpu.roll(tail, d, 0)
        top = jnp.where(row8 < d, rt[0:SUBLANES], r[0:SUBLANES])
        return jnp.concatenate([top, r[SUBLANES:]], axis=0)

    pooled = []
    for gi, w in enumerate(POOL_WINDOWS):
        sl = slice(gi * GROUP_DIM, (gi + 1) * GROUP_DIM)
        s = p[:, sl]
        for lv in range(gi + 1):
            tail = halo_ref[lv, :, sl]
            halo_ref[lv, :, sl] = s[T - HALO:T, :]
            s = s + shift_down(s, tail, 1 << lv)
        cnt = jnp.minimum(pos + 1, w).astype(F32)
        pooled.append(s / cnt - p[:, sl])
    pooled = jnp.concatenate(pooled, axis=1).astype(BF16)
    yb = jnp.dot(pooled, wpool_ref[...], preferred_element_type=F32) * spool_ref[...]
    yb_ref[...] = yb.astype(yb_ref.dtype)

    gu = _gelu_tanh(proj(PK_U, BRANCH_WIDTH))
    gv = _gelu_tanh(proj(PK_SV, BRANCH_WIDTH))
    t_r = lax.broadcasted_iota(I32, (C_CHUNK, C_CHUNK), 0)
    t_c = lax.broadcasted_iota(I32, (C_CHUNK, C_CHUNK), 1)
    causal = t_r >= t_c
    for g in range(N_GROUPS):
        sl = slice(g * GROUP_DIM, (g + 1) * GROUP_DIM)
        vn = _rms(gv[:, sl], gsgu_ref[:, sl]).astype(BF16)
        wc = jnp.where(causal, wsgu_ref[g], 0.0).astype(BF16)
        for ch in range(T // C_CHUNK):
            rows = slice(ch * C_CHUNK, (ch + 1) * C_CHUNK)
            mixed = jnp.dot(wc, vn[rows], preferred_element_type=F32) + bsgu_ref[g]
            yc_ref[rows, sl] = (gu[rows, sl] * mixed).astype(yc_ref.dtype)


def _proj_call(x, w, layer, T):
    B, S, D = x.shape
    H = A_HEADS
    per_layer = functools.partial(_layer_spec, layer)
    return pl.pallas_call(
        functools.partial(_proj_kernel, T=T),
        grid=(B, S // T),
        in_specs=[
            pl.BlockSpec((None, T, D), lambda b, i: (b, i, 0)),
            per_layer(1, D), per_layer(D, PK_W), per_layer(1, LANES), per_layer(1, BRANCH_WIDTH),
            per_layer(1, BRANCH_WIDTH),
            per_layer(BRANCH_WIDTH, BRANCH_WIDTH), per_layer(1, BRANCH_WIDTH), per_layer(1, BRANCH_WIDTH),
            per_layer(N_GROUPS, C_CHUNK, C_CHUNK), per_layer(N_GROUPS, C_CHUNK, GROUP_DIM),
        ],
        out_specs=[
            pl.BlockSpec((None, H, LANES, T), lambda b, i: (b, 0, 0, i)),
            pl.BlockSpec((None, H, T, LANES), lambda b, i: (b, 0, i, 0)),
            pl.BlockSpec((None, H, LANES, T), lambda b, i: (b, 0, 0, i)),
            pl.BlockSpec((None, T, BRANCH_WIDTH), lambda b, i: (b, i, 0)),
            pl.BlockSpec((None, T, BRANCH_WIDTH), lambda b, i: (b, i, 0)),
            pl.BlockSpec((None, T, LANES), lambda b, i: (b, i, 0)),
        ],
        out_shape=[
            jax.ShapeDtypeStruct((B, H, LANES, S), BF16),
            jax.ShapeDtypeStruct((B, H, S, LANES), BF16),
            jax.ShapeDtypeStruct((B, H, LANES, S), BF16),
            jax.ShapeDtypeStruct((B, S, BRANCH_WIDTH), BF16),
            jax.ShapeDtypeStruct((B, S, BRANCH_WIDTH), BF16),
            jax.ShapeDtypeStruct((B, S, LANES), F32),
        ],
        scratch_shapes=[pltpu.VMEM((1, LANES), F32), pltpu.VMEM((4, HALO, BRANCH_WIDTH), F32)],
        compiler_params=pltpu.CompilerParams(
            dimension_semantics=("arbitrary", "arbitrary"), vmem_limit_bytes=VMEM_LIMIT),
        name="proj",
    )(x, w["g_mix"], w["w_pack"], w["b_f"], w["g_q"], w["g_k"], w["w_pool"], w["s_pool"],
      w["g_sgu"], w["w_sgu"], w["b_sgu"])


def _attn_kernel(first_ref, qT_ref, k_ref, vT_ref, o_ref, m_ref, acc_ref, s_ref, *, tq, tk, n_q, online_max):
    assert tq == tk
    FIRST = 2

    def reset():
        acc_ref[...] = jnp.zeros_like(acc_ref)
        if online_max:
            m_ref[...] = jnp.full(m_ref.shape, NEG, F32)

    def scores(qi, j, slot):
        k0 = pl.multiple_of(j * tk, tk)
        q0 = pl.multiple_of(qi * tq, tq)
        for par in range(2):
            s_ref[slot, par] = jnp.dot(k_ref[par, pl.ds(k0, tk), :], qT_ref[par, :, pl.ds(q0, tq)],
                                       preferred_element_type=F32)

    def consume(qi, j, slot, masked):
        k0 = pl.multiple_of(j * tk, tk)
        for par in range(2):
            s = s_ref[slot, par]
            if masked:
                s = jnp.where(lax.broadcasted_iota(I32, (tk, tq), 0) <= lax.broadcasted_iota(I32, (tk, tq), 1),
                              s, NEG)
            v_blk = vT_ref[par, :, pl.ds(k0, tk)]
            if online_max:
                m_old = m_ref[par]
                m_new = jnp.maximum(m_old, jnp.max(s, axis=0, keepdims=True))
                p = jnp.exp2(s - m_new).astype(BF16)
                pv = jnp.dot(v_blk, p, preferred_element_type=F32)
                acc_ref[par] = acc_ref[par] * jnp.exp2(m_old - m_new) + pv
                m_ref[par] = m_new
            else:
                p = jnp.exp2(s).astype(BF16)
                acc_ref[par] += jnp.dot(v_blk, p, preferred_element_type=F32)

    def finish(qi):
        lane = lax.broadcasted_iota(I32, (tq, LANES), 1)
        outs = []
        for par in range(2):
            acc = acc_ref[par]
            ones_row = A_HEAD_DIM if par == 0 else 0
            l = acc[ones_row:ones_row + 1, :]
            outs.append((acc * (1.0 / l)).T)
        q0 = pl.multiple_of(qi * tq, tq)
        o_ref[pl.ds(q0, tq), :] = jnp.where(lane < A_HEAD_DIM, outs[0], outs[1]).astype(o_ref.dtype)
        reset()

    b_id = pl.program_id(0)
    g_id = pl.program_id(1)

    def first_block(qi):
        return first_ref[(b_id * pl.num_programs(1) + g_id) * n_q + qi]

    reset()
    scores(0, 0, FIRST)
    consume(0, 0, FIRST, True)
    finish(0)
    if n_q > 1:
        scores(1, first_block(1), FIRST)

    def query_block(qi, carry):
        base = first_block(qi)
        n_mid = qi - 1 - base
        nxt = jnp.minimum(qi + 1, n_q - 1)
        scores(qi, base + 1, 1)
        consume(qi, base, FIRST, False)

        def trip(t, c):
            j0 = base + 1 + ATTN_BLOCKS_PER_TRIP * t
            for u in range(ATTN_BLOCKS_PER_TRIP):
                scores(qi, j0 + u + 1, u % 2)
                consume(qi, j0 + u, (u + 1) % 2, False)
            return c

        lax.fori_loop(0, n_mid // ATTN_BLOCKS_PER_TRIP, trip, 0)
        j0 = base + 1 + (n_mid // ATTN_BLOCKS_PER_TRIP) * ATTN_BLOCKS_PER_TRIP
        for rem in range(ATTN_BLOCKS_PER_TRIP):
            @pl.when(n_mid % ATTN_BLOCKS_PER_TRIP == rem)
            def _(rem=rem):
                for u in range(rem):
                    scores(qi, j0 + u + 1, u % 2)
                    consume(qi, j0 + u, (u + 1) % 2, False)
                scores(nxt, first_block(nxt), FIRST)
                consume(qi, j0 + rem, (rem + 1) % 2, True)
                finish(qi)
        return carry

    lax.fori_loop(1, n_q, query_block, 0)


def _attn_call(first, qT, k, vT, tq, tk, online_max):
    B, H, _, S = qT.shape
    grid_spec = pltpu.PrefetchScalarGridSpec(
        num_scalar_prefetch=1,
        grid=(B, H // 2),
        in_specs=[
            pl.BlockSpec((None, 2, LANES, S), lambda b, g, first: (b, g, 0, 0)),
            pl.BlockSpec((None, 2, S, LANES), lambda b, g, first: (b, g, 0, 0)),
            pl.BlockSpec((None, 2, LANES, S), lambda b, g, first: (b, g, 0, 0)),
        ],
        out_specs=pl.BlockSpec((None, S, LANES), lambda b, g, first: (b, 0, g)),
        scratch_shapes=[pltpu.VMEM((2, 1, tq), F32), pltpu.VMEM((2, LANES, tq), F32),
                        pltpu.VMEM((3, 2, tk, tq), F32)],
    )
    return pl.pallas_call(
        functools.partial(_attn_kernel, tq=tq, tk=tk, n_q=S // tq, online_max=online_max),
        grid_spec=grid_spec,
        out_shape=jax.ShapeDtypeStruct((B, S, BRANCH_WIDTH), BF16),
        compiler_params=pltpu.CompilerParams(
            dimension_semantics=("parallel", "parallel"), vmem_limit_bytes=VMEM_LIMIT),
        name="attn_online" if online_max else "attn",
    )(first, qT, k, vT)


def _first_live_block(c, bound, tq, tk):
    c_q = c[:, 0::tq, 0:A_HEADS]
    c_k = c[:, tk - 1::tk, 0:A_HEADS]
    top = bound + LOG2E * (c_q[:, :, None, :] - c_k[:, None, :, :])
    dead = top < DEAD_SCORE
    dead_pair = dead[..., 0::2] & dead[..., 1::2]
    n_q = c_q.shape[1]
    n_dead = jnp.sum(jnp.cumprod(dead_pair.astype(I32), axis=2), axis=2)
    limit = jnp.maximum(jnp.arange(n_q, dtype=I32) - 1, 0)[None, :, None]
    first = jnp.minimum(n_dead, limit)
    return first.transpose(0, 2, 1).reshape(-1).astype(I32)


def _attention(qT, k, vT, c, g_q, g_k, tq, tk):
    bound = 16.0 * LOG2E * jnp.max(jnp.abs(g_q)) * jnp.max(jnp.abs(g_k))
    first = _first_live_block(c, bound, tq, tk)
    return lax.cond(bound <= SCORE_CAP,
                    lambda: _attn_call(first, qT, k, vT, tq, tk, False),
                    lambda: _attn_call(first, qT, k, vT, tq, tk, True))


def _merge_kernel(x_ref, ya_ref, yb_ref, yc_ref, gmix_ref, wg_ref, bg_ref, wb_ref, wo_ref,
                  gffn_ref, wr_ref, br_ref,
                  x1_ref, h2_ref, route_ref, route_t_ref, cnt_ref, carry_ref, logit_ref, *, T):
    i = pl.program_id(0)
    n_tiles = pl.num_programs(0) - 1

    @pl.when(i == 0)
    def _():
        carry_ref[...] = jnp.zeros_like(carry_ref)
        logit_ref[...] = jnp.zeros_like(logit_ref)

    def route_previous():
        live = (i > 0).astype(F32)
        logits = logit_ref[...]
        lane = lax.broadcasted_iota(I32, (T, LANES), 1).astype(F32)
        big = float(LANES)

        def first_argmax(v):
            m = jnp.max(v, axis=-1, keepdims=True)
            return m, jnp.min(jnp.where(v == m, lane, big), axis=-1, keepdims=True)

        lg = jnp.where(lane < N_EXPERT_GROUPS, logits, NEG)
        mg, grp = first_argmax(lg)
        p_grp = 1.0 / jnp.sum(jnp.exp(lg - mg), axis=-1, keepdims=True)
        lo_lane = N_EXPERT_GROUPS + grp * EXPERTS_PER_GROUP
        le = jnp.where((lane >= lo_lane) & (lane < lo_lane + EXPERTS_PER_GROUP), logits, NEG)
        m1, i1 = first_argmax(le)
        m2, i2 = first_argmax(jnp.where(lane == i1, NEG, le))
        e21 = jnp.exp(m2 - m1)
        g1 = p_grp / (1.0 + e21)
        g2 = p_grp * e21 / (1.0 + e21)
        e1 = i1 - N_EXPERT_GROUPS
        e2 = i2 - N_EXPERT_GROUPS

        oh1 = lane == e1
        oh2 = lane == e2
        sel = (oh1 | oh2).astype(F32)
        r_i = lax.broadcasted_iota(I32, (T, T), 0)
        c_i = lax.broadcasted_iota(I32, (T, T), 1)
        before = (r_i > c_i).astype(BF16)
        seen = jnp.dot(before, sel.astype(BF16), preferred_element_type=F32) + carry_ref[...]
        r1 = jnp.sum(jnp.where(oh1, seen, 0.0), axis=-1, keepdims=True)
        r2 = jnp.sum(jnp.where(oh2, seen, 0.0), axis=-1, keepdims=True)
        carry_ref[...] = carry_ref[...] + live * jnp.sum(sel, axis=0, keepdims=True)
        cnt_ref[...] = carry_ref[...]

        route = jnp.zeros((T, LANES), F32)
        for idx, val in enumerate((e1, e2, g1, g2, r1, r2)):
            route = jnp.where(lane == idx, val, route)
        route_ref[...] = route
        route_t_ref[...] = route.T[0:ROUTE_ROWS]

    @pl.when(i < n_tiles)
    def _():
        route_previous()
        x = x_ref[...]
        hb = _rms(x, gmix_ref[...]).astype(BF16)
        merged = None
        for bi, y_ref in enumerate((ya_ref, yb_ref, yc_ref)):
            sl = slice(bi * D_MODEL, (bi + 1) * D_MODEL)
            gate = jax.nn.sigmoid(jnp.dot(hb, wg_ref[:, sl], preferred_element_type=F32) + bg_ref[:, sl])
            term = gate * jnp.dot(y_ref[...], wb_ref[bi], preferred_element_type=F32)
            merged = term if merged is None else merged + term
        x1 = x + jnp.dot(merged.astype(BF16), wo_ref[...], preferred_element_type=F32)
        x1_ref[...] = x1
        h2 = _rms(x1, gffn_ref[...])
        _store_row_tiles(h2_ref, _pack_pairs(h2))

        h_hi = h2.astype(BF16)
        h_lo = (h2 - h_hi.astype(F32)).astype(BF16)
        hw = jnp.dot(h_hi, wr_ref[...], preferred_element_type=F32)
        lw = jnp.dot(h_lo, wr_ref[:, 0:LANES], preferred_element_type=F32)
        logit_ref[...] = hw[:, 0:LANES] + (hw[:, LANES:] + lw) + br_ref[...]

    @pl.when(i == n_tiles)
    def _():
        route_previous()


def _row_tile_rows(s, n_rows, row0=0):
    return pl.ds(row0 * ROW_TILE + s, n_rows, stride=ROW_TILE)


def _pack_pairs(v):
    hi = lax.bitcast_convert_type(v[:, :HALF_D].astype(BF16).astype(F32), U32)
    lo = lax.bitcast_convert_type(v[:, HALF_D:].astype(BF16).astype(F32), U32)
    return hi | (lo >> 16)


def _unpack_pairs(w):
    hi = lax.bitcast_convert_type(w & jnp.uint32(0xFFFF0000), F32)
    lo = lax.bitcast_convert_type(w << 16, F32)
    return jnp.concatenate([hi, lo], axis=-1)


def _store_row_tiles(ref, words, row0=0):
    for s in range(ROW_TILE):
        ref[_row_tile_rows(s, words.shape[0], row0), :] = words[:, s * LANES:(s + 1) * LANES]


def _load_row_tiles(ref, n_rows, row0=0):
    return jnp.concatenate([ref[_row_tile_rows(s, n_rows, row0), :] for s in range(ROW_TILE)], axis=-1)


def _merge_call(x2d, ya, yb, yc, w, layer, T):
    N, D = x2d.shape
    n_tiles = N // T
    const = functools.partial(_layer_spec, layer)
    cur = lambda i: jnp.minimum(i, n_tiles - 1)
    prev = lambda i: jnp.maximum(i - 1, 0)
    tile = lambda width: pl.BlockSpec((T, width), lambda i: (cur(i), 0))
    return pl.pallas_call(
        functools.partial(_merge_kernel, T=T),
        grid=(n_tiles + 1,),
        in_specs=[
            tile(D), tile(BRANCH_WIDTH), tile(BRANCH_WIDTH), tile(BRANCH_WIDTH),
            const(1, D), const(D, N_BRANCH * D), const(1, N_BRANCH * D),
            const(N_BRANCH, BRANCH_WIDTH, D), const(D, D), const(1, D), const(D, 2 * LANES), const(1, LANES),
        ],
        out_specs=[tile(D), pl.BlockSpec((T * ROW_TILE, LANES), lambda i: (cur(i), 0)),
                   pl.BlockSpec((T, LANES), lambda i: (prev(i), 0)),
                   pl.BlockSpec((ROUTE_ROWS, T), lambda i: (0, prev(i))),
                   pl.BlockSpec((1, LANES), lambda i: (0, 0))],
        out_shape=[
            jax.ShapeDtypeStruct((N, D), F32),
            jax.ShapeDtypeStruct((N * ROW_TILE, LANES), U32),
            jax.ShapeDtypeStruct((N, LANES), F32),
            jax.ShapeDtypeStruct((ROUTE_ROWS, N), F32),
            jax.ShapeDtypeStruct((1, LANES), F32),
        ],
        scratch_shapes=[pltpu.VMEM((1, LANES), F32), pltpu.VMEM((T, LANES), F32)],
        compiler_params=pltpu.CompilerParams(
            dimension_semantics=("arbitrary",), vmem_limit_bytes=VMEM_LIMIT),
        name="merge",
    )(x2d, ya, yb, yc, w["g_mix"], w["w_gate"], w["b_gate"], w["w_branch"], w["w_out"],
      w["g_ffn"], w["w_router"], w["b_router"])


def _dispatch_kernel(dest_ref, h2_ref, xs_ref, sem, *, T):
    def row_copy(t, d):
        return pltpu.make_async_copy(h2_ref.at[_row_tile(t)], xs_ref.at[_row_tile(d)], sem)

    def body(tb, carry):
        t0 = tb * ISSUE_UNROLL
        for u in range(ISSUE_UNROLL):
            for kk in range(TOP_K):
                row_copy(t0 + u, dest_ref[0, kk * T + t0 + u]).start(priority=kk % N_DMA_PRIORITIES)
        return carry

    lax.fori_loop(0, T // ISSUE_UNROLL, body, 0)
    for kk in range(TOP_K):
        pltpu.make_async_copy(h2_ref, xs_ref.at[pl.ds(0, T * ROW_TILE)], sem).wait()


def _row_tile(r):
    return pl.ds(pl.multiple_of(r * ROW_TILE, ROW_TILE), ROW_TILE)


def _dispatch_call(dest3, h2, T):
    N = h2.shape[0] // ROW_TILE
    return pl.pallas_call(
        functools.partial(_dispatch_kernel, T=T),
        grid=(N // T,),
        in_specs=[
            pl.BlockSpec((None, 1, TOP_K * T), lambda i: (i, 0, 0), memory_space=pltpu.SMEM),
            pl.BlockSpec((T * ROW_TILE, LANES), lambda i: (i, 0)),
        ],
        out_specs=pl.BlockSpec(memory_space=pl.ANY),
        out_shape=jax.ShapeDtypeStruct((TOP_K * N * ROW_TILE, LANES), U32),
        scratch_shapes=[pltpu.SemaphoreType.DMA],
        compiler_params=pltpu.CompilerParams(
            dimension_semantics=("arbitrary",), vmem_limit_bytes=VMEM_LIMIT),
        name="dispatch",
    )(dest3, h2)


def _gmm_kernel(blk_ref, exp_ref, lo_ref, hi_ref, xs_ref, w1_ref, w3_ref, w2_ref, y_ref,
                w13_bf, w2_bf, *, bm):
    i = pl.program_id(0)
    lo = lo_ref[i]
    hi = hi_ref[i]
    prev = jnp.maximum(i - 1, 0)
    first = jnp.logical_or(i == 0, blk_ref[i] != blk_ref[prev])
    new_expert = jnp.logical_or(i == 0, exp_ref[i] != exp_ref[prev])

    @pl.when(new_expert)
    def _():
        w13_bf[:, 0:D_EXPERT] = w1_ref[...].astype(BF16)
        w13_bf[:, D_EXPERT:] = w3_ref[...].astype(BF16)
        w2_bf[...] = w2_ref[...].astype(BF16)

    @pl.when(first)
    def _():
        y_ref[...] = jnp.zeros_like(y_ref)

    @pl.when(hi > lo)
    def _():
        n_part = 2
        part = bm // n_part
        for h in range(n_part):
            x = _unpack_pairs(_load_row_tiles(xs_ref, part, h * part)).astype(BF16)
            ab = jnp.dot(x, w13_bf[...], preferred_element_type=F32)
            a = ab[:, 0:D_EXPERT]
            b = ab[:, D_EXPERT:]
            mid = (a * jax.nn.sigmoid(a) * b).astype(BF16)
            out = _pack_pairs(jnp.dot(mid, w2_bf[...], preferred_element_type=F32))
            row = h * part + lax.broadcasted_iota(I32, (part, HALF_D), 0)
            mine = (row >= lo) & (row < hi)
            old = _load_row_tiles(y_ref, part, h * part)
            _store_row_tiles(y_ref, jnp.where(mine, out, old), h * part)


def _gmm_call(plan, xs, w1, w3, w2, layer, bm):
    A = xs.shape[0] // ROW_TILE
    D = D_MODEL
    n_items = plan[0].shape[0]
    grid_spec = pltpu.PrefetchScalarGridSpec(
        num_scalar_prefetch=4,
        grid=(n_items,),
        in_specs=[
            pl.BlockSpec((bm * ROW_TILE, LANES), lambda i, blk, ex, lo, hi: (blk[i], 0)),
            pl.BlockSpec((None, None, D, D_EXPERT), lambda i, blk, ex, lo, hi: (layer, ex[i], 0, 0)),
            pl.BlockSpec((None, None, D, D_EXPERT), lambda i, blk, ex, lo, hi: (layer, ex[i], 0, 0)),
            pl.BlockSpec((None, None, D_EXPERT, D), lambda i, blk, ex, lo, hi: (layer, ex[i], 0, 0)),
        ],
        out_specs=pl.BlockSpec((bm * ROW_TILE, LANES), lambda i, blk, ex, lo, hi: (blk[i], 0)),
        scratch_shapes=[pltpu.VMEM((D, 2 * D_EXPERT), BF16), pltpu.VMEM((D_EXPERT, D), BF16)],
    )
    return pl.pallas_call(
        functools.partial(_gmm_kernel, bm=bm),
        grid_spec=grid_spec,
        out_shape=jax.ShapeDtypeStruct((A * ROW_TILE, LANES), U32),
        compiler_params=pltpu.CompilerParams(
            dimension_semantics=("arbitrary",), vmem_limit_bytes=VMEM_LIMIT),
        name="gmm",
    )(*plan, xs, w1, w3, w2)


def _gmm_plan(counts, n_rows, bm):
    n_blk = n_rows // bm
    n_items = n_blk + N_EXPERTS - 1
    ends = jnp.cumsum(counts)
    starts = ends - counts
    first_blk = starts // bm
    n_it = jnp.where(counts > 0, (ends - 1) // bm - first_blk + 1, 0)
    item_end = jnp.cumsum(n_it)
    item_start = item_end - n_it
    total = item_end[-1]
    ids = jnp.arange(n_items, dtype=I32)
    valid = ids < total
    ex = jnp.minimum(jnp.sum(item_end[None, :] <= ids[:, None], axis=1).astype(I32), N_EXPERTS - 1)
    ex = jnp.where(valid, ex, ex[jnp.maximum(total - 1, 0)])
    blk = jnp.where(valid, first_blk[ex] + ids - item_start[ex], n_blk - 1)
    lo = jnp.where(valid, jnp.maximum(starts[ex], blk * bm) - blk * bm, 0)
    hi = jnp.where(valid, jnp.minimum(ends[ex], (blk + 1) * bm) - blk * bm, 0)
    return blk.astype(I32), ex.astype(I32), lo.astype(I32), hi.astype(I32)


COMBINE_SLOTS = 3
COMBINE_ROWS = 32


def _combine_kernel(dest_ref, dest_next_ref, dest_ahead_ref, x1_ref, route_ref, y_hbm, o_ref, buf, sem, *, T):
    i = pl.program_id(0)
    n_steps = pl.num_programs(0)
    n_trips = T // COMBINE_ROWS

    def issue_rows(d_ref, slot, t0):
        for u in range(COMBINE_ROWS):
            for kk in range(TOP_K):
                d = d_ref[0, kk * T + t0 + u]
                pltpu.make_async_copy(y_hbm.at[_row_tile(d)], buf.at[slot * TOP_K + kk, _row_tile(t0 + u)],
                                      sem.at[slot]).start(priority=kk % N_DMA_PRIORITIES)

    def combine_rows(slot, t0):
        rows = pl.ds(t0, COMBINE_ROWS)
        g1 = route_ref[rows, 2:3]
        g2 = route_ref[rows, 3:4]
        for s in range(ROW_TILE):
            words = pl.ds(t0 * ROW_TILE + s, COMBINE_ROWS, stride=ROW_TILE)
            y1 = _unpack_pairs(buf[slot * TOP_K, words, :])
            y2 = _unpack_pairs(buf[slot * TOP_K + 1, words, :])
            for half in range(2):
                src = slice(half * LANES, (half + 1) * LANES)
                dst = slice(half * HALF_D + s * LANES, half * HALF_D + (s + 1) * LANES)
                o_ref[rows, dst] = x1_ref[rows, dst] + (g1 * y1[:, src] + g2 * y2[:, src])

    def loop(body):
        def trip(tb, carry):
            body(pl.multiple_of(tb * COMBINE_ROWS, COMBINE_ROWS))
            return carry

        lax.fori_loop(0, n_trips, trip, 0)

    @pl.when(i == 0)
    def _():
        loop(lambda t0: issue_rows(dest_ref, 0, t0))

        @pl.when(n_steps > 1)
        def _():
            loop(lambda t0: issue_rows(dest_next_ref, 1, t0))

    for slot in range(COMBINE_SLOTS):
        @pl.when(i % COMBINE_SLOTS == slot)
        def _(slot=slot):
            for kk in range(TOP_K):
                pltpu.make_async_copy(y_hbm.at[pl.ds(0, T * ROW_TILE)], buf.at[slot * TOP_K + kk],
                                      sem.at[slot]).wait()
            ahead = (slot + 2) % COMBINE_SLOTS

            @pl.when(i + 2 < n_steps)
            def _():
                def both(t0):
                    issue_rows(dest_ahead_ref, ahead, t0)
                    combine_rows(slot, t0)

                loop(both)

            @pl.when(i + 2 >= n_steps)
            def _():
                loop(lambda t0: combine_rows(slot, t0))


def _combine_call(dest3, x1, route, y, T):
    N, D = x1.shape
    n_steps = N // T
    dest_spec = lambda ahead: pl.BlockSpec(
        (None, 1, TOP_K * T), lambda i: (jnp.minimum(i + ahead, n_steps - 1), 0, 0), memory_space=pltpu.SMEM)
    return pl.pallas_call(
        functools.partial(_combine_kernel, T=T),
        grid=(n_steps,),
        in_specs=[
            dest_spec(0), dest_spec(1), dest_spec(2),
            pl.BlockSpec((T, D), lambda i: (i, 0)),
            pl.BlockSpec((T, LANES), lambda i: (i, 0)),
            pl.BlockSpec(memory_space=pl.ANY),
        ],
        out_specs=pl.BlockSpec((T, D), lambda i: (i, 0)),
        out_shape=jax.ShapeDtypeStruct((N, D), F32),
        scratch_shapes=[pltpu.VMEM((COMBINE_SLOTS * TOP_K, T * ROW_TILE, LANES), U32),
                        pltpu.SemaphoreType.DMA((COMBINE_SLOTS,))],
        compiler_params=pltpu.CompilerParams(
            dimension_semantics=("arbitrary",), vmem_limit_bytes=VMEM_LIMIT),
        name="combine",
    )(dest3, dest3, dest3, x1, route, y)


def _block_diag(blocks):
    n = blocks.shape[1]
    eye = jnp.eye(n, dtype=blocks.dtype)
    out = blocks[:, :, :, None, :] * eye[None, :, None, :, None]
    return out.reshape(blocks.shape[0], n * blocks.shape[2], n * blocks.shape[3])


def _rep_forget(cols):
    lead = cols.shape[:-1]
    half = jnp.concatenate(
        [jnp.tile(cols, (1,) * len(lead) + (F_REP,)), jnp.zeros(lead + (A_HEAD_DIM - F_REP * A_HEADS,), cols.dtype)],
        axis=-1)
    return jnp.concatenate([half, half], axis=-1)


def _prep_all(g_mix, w_in, b_fgate, b_gate, g_q, g_k, w_pool, s_pool, g_sgu, w_sgu, b_sgu,
              w_branch, w_out, g_ffn, w_rg, b_rg, w_re, b_re):
    L = w_in.shape[0]
    order = jnp.argsort(b_fgate, axis=1)
    qkv = w_in[:, :, 0:OFF_F].reshape(L, D_MODEL, 3, A_HEADS, A_HEAD_DIM)
    qkv = jnp.take_along_axis(qkv, order[:, None, None, :, None], axis=3).reshape(L, D_MODEL, OFF_F)
    w_f = jnp.take_along_axis(w_in[:, :, OFF_F:OFF_P], order[:, None, :], axis=2)
    b_f = jnp.take_along_axis(b_fgate, order, axis=1)
    wb0 = w_branch[:, 0].reshape(L, A_HEADS, A_HEAD_DIM, D_MODEL)
    wb0 = jnp.take_along_axis(wb0, order[:, :, None, None], axis=1).reshape(L, 1, BRANCH_WIDTH, D_MODEL)
    pad_r = LANES - N_EXPERT_GROUPS - N_EXPERTS
    w_r = jnp.concatenate([w_rg, w_re, jnp.zeros((L, D_MODEL, pad_r), F32)], axis=2)
    w_r_hi = w_r.astype(BF16)
    w_r_lo = (w_r - w_r_hi.astype(F32)).astype(BF16)
    row = lambda v: v.reshape(L, 1, v.shape[-1])
    return dict(
        g_mix=row(g_mix),
        w_pack=jnp.concatenate([qkv, _rep_forget(w_f), w_in[:, :, OFF_P:OFF_G]], axis=2).astype(BF16),
        b_f=_rep_forget(row(b_f)),
        g_q=row(jnp.tile(g_q, (1, A_HEADS))),
        g_k=row(jnp.tile(g_k, (1, A_HEADS))),
        w_pool=_block_diag(w_pool).astype(BF16),
        s_pool=row(s_pool),
        g_sgu=row(g_sgu),
        w_sgu=w_sgu,
        b_sgu=jnp.broadcast_to(b_sgu[:, :, :, None], (L, N_GROUPS, C_CHUNK, GROUP_DIM)),
        w_gate=w_in[:, :, OFF_G:].astype(BF16),
        b_gate=b_gate.reshape(L, 1, N_BRANCH * D_MODEL),
        w_branch=jnp.concatenate([wb0, w_branch[:, 1:]], axis=1).astype(BF16),
        w_out=w_out.astype(BF16),
        g_ffn=row(g_ffn),
        w_router=jnp.concatenate([w_r_hi, w_r_lo], axis=2),
        b_router=row(jnp.concatenate([b_rg, b_re, jnp.zeros((L, pad_r), F32)], axis=1)),
    )


def kernel(x, g_mix, w_in, b_fgate, b_gate, g_q, g_k, w_pool, s_pool, g_sgu, w_sgu, b_sgu, w_branch, w_out,
           g_ffn, w_rg, b_rg, w_re, b_re, w1, w3, w2):
    B, S, D = x.shape
    assert D == D_MODEL and x.dtype == F32
    N = B * S
    T, tq, tk, bm, tm = _tiles(S)
    assert S % T == 0 and S % tq == 0 and tq == tk and T % C_CHUNK == 0
    assert (TOP_K * N) % bm == 0 and N % tm == 0 and tm % ISSUE_UNROLL == 0 and tm % COMBINE_ROWS == 0
    depth = w_in.shape[0]
    w = _prep_all(g_mix, w_in, b_fgate, b_gate, g_q, g_k, w_pool, s_pool, g_sgu, w_sgu, b_sgu,
                  w_branch, w_out, g_ffn, w_rg, b_rg, w_re, b_re)
    for l in range(depth):
        qT, k, vT, yb, yc, c = _proj_call(x, w, l, T)
        ya = _attention(qT, k, vT, c, g_q[l], g_k[l], tq, tk)
        x1, h2, route, route_t, cnt = _merge_call(
            x.reshape(N, D), ya.reshape(N, -1), yb.reshape(N, -1), yc.reshape(N, -1), w, l, T)
        counts = cnt[0, :N_EXPERTS].astype(I32)
        starts = jnp.cumsum(counts) - counts
        experts = route_t[0:TOP_K].astype(I32)
        expert_ids = jnp.arange(N_EXPERTS, dtype=I32)[:, None, None]
        start_of = jnp.sum(jnp.where(experts[None] == expert_ids, starts[:, None, None], 0), axis=0)
        dest = start_of + route_t[4:4 + TOP_K].astype(I32)
        dest3 = dest.reshape(TOP_K, N // tm, tm).transpose(1, 0, 2).reshape(N // tm, 1, TOP_K * tm)
        xs = _dispatch_call(dest3, h2, tm)
        y = _gmm_call(_gmm_plan(counts, TOP_K * N, bm), xs, w1, w3, w2, l, bm)
        x = _combine_call(dest3, x1, route, y, tm).reshape(B, S, D)
    return x
```

```python
import functools

import jax
import jax.numpy as jnp
from jax import lax
from jax.experimental import pallas as pl
from jax.experimental.pallas import tpu as pltpu

F32 = jnp.float32
BF16 = jnp.bfloat16
I32 = jnp.int32
U32 = jnp.uint32

D_MODEL = 1024
A_HEADS = 8
A_HEAD_DIM = 64
BRANCH_WIDTH = 512
N_BRANCH = 3
POOL_WINDOWS = (2, 4, 8, 16)
GROUP_DIM = 128
N_GROUPS = 4
C_CHUNK = 128
N_EXPERT_GROUPS = 4
EXPERTS_PER_GROUP = 8
N_EXPERTS = N_EXPERT_GROUPS * EXPERTS_PER_GROUP
TOP_K = 2
D_EXPERT = 256
RMS_EPS = 1e-6
OFF_F = 3 * BRANCH_WIDTH
OFF_P = OFF_F + A_HEADS
OFF_U = OFF_P + BRANCH_WIDTH
OFF_SV = OFF_U + BRANCH_WIDTH
OFF_G = OFF_SV + BRANCH_WIDTH

LANES = 128
SUBLANES = 8
V7X_VMEM_BYTES = 64 * 1024 * 1024
VMEM_LIMIT = V7X_VMEM_BYTES * 7 // 8
ROW_TILE = D_MODEL // (2 * LANES)
HALF_D = D_MODEL // 2
ROUTE_ROWS = SUBLANES
ISSUE_UNROLL = 8
N_DMA_PRIORITIES = 2
ATTN_BLOCKS_PER_TRIP = 4

PK_Q, PK_K, PK_V = 0, BRANCH_WIDTH, 2 * BRANCH_WIDTH
PK_F = 3 * BRANCH_WIDTH
PK_P = PK_F + LANES
PK_U = PK_P + BRANCH_WIDTH
PK_SV = PK_U + BRANCH_WIDTH
PK_W = PK_SV + BRANCH_WIDTH
F_REP = 6

LOG2E = 1.4426950408889634
SCORE_CAP = 96.0
DEAD_SCORE = -152.0
NEG = -1e30
HALO = 16


def _tiles(seq_len):
    t_proj = min(512, seq_len)
    t_q = min(512, seq_len)
    t_k = min(512, seq_len)
    moe_block = 512
    t_move = min(1024, seq_len)
    return t_proj, t_q, t_k, moe_block, t_move


def _layer_spec(layer, *shape):
    index = (layer,) + (0,) * len(shape)
    return pl.BlockSpec((None,) + shape, lambda *_: index, pipeline_mode=pl.Buffered(1))


def _rms(x, g):
    return x * lax.rsqrt(jnp.mean(x * x, axis=-1, keepdims=True) + RMS_EPS) * g


def _gelu_tanh(x):
    cdf = 0.5 * (1.0 + jnp.tanh(0.7978845608028654 * (x + 0.044715 * (x * x * x))))
    return x * cdf


def _log_sigmoid(x):
    return jnp.minimum(x, 0.0) - jnp.log1p(jnp.exp(-jnp.abs(x)))


def _split3(c):
    hi = c.astype(BF16).astype(F32)
    r = c - hi
    lo = r.astype(BF16).astype(F32)
    lolo = (r - lo).astype(BF16).astype(F32)
    return hi, lo, lolo


def _proj_kernel(x_ref, gmix_ref, w_ref, bf_ref, gq_ref, gk_ref, wpool_ref, spool_ref,
                 gsgu_ref, wsgu_ref, bsgu_ref,
                 qT_ref, k_ref, vT_ref, yb_ref, yc_ref, c_ref,
                 carry_ref, halo_ref, *, T):
    i = pl.program_id(1)

    @pl.when(i == 0)
    def _():
        carry_ref[...] = jnp.zeros_like(carry_ref)
        halo_ref[...] = jnp.zeros_like(halo_ref)

    hb = _rms(x_ref[...], gmix_ref[...]).astype(BF16)

    def proj(lo, width):
        return jnp.dot(hb, w_ref[:, lo:lo + width], preferred_element_type=F32)

    lane = lax.broadcasted_iota(I32, (T, LANES), 1)
    left = lane < A_HEAD_DIM

    def head_sums(v):
        cols = []
        for g in range(A_HEADS // 2):
            pair = v[:, g * LANES:(g + 1) * LANES]
            cols.append(jnp.sum(jnp.where(left, pair, 0.0), axis=-1, keepdims=True))
            cols.append(jnp.sum(jnp.where(left, 0.0, pair), axis=-1, keepdims=True))
        return cols

    def head_norm(z, g):
        cols = head_sums(z * z)
        ss = jnp.concatenate([jnp.where(left, cols[2 * g], cols[2 * g + 1]) for g in range(A_HEADS // 2)],
                             axis=-1)
        return z * lax.rsqrt(ss * (1.0 / A_HEAD_DIM) + RMS_EPS) * g

    qn = head_norm(proj(PK_Q, BRANCH_WIDTH), gq_ref[...]) * (A_HEAD_DIM ** -0.5 * LOG2E)
    kn = head_norm(proj(PK_K, BRANCH_WIDTH), gk_ref[...])
    zv = proj(PK_V, BRANCH_WIDTH)
    d_cols = head_sums(qn * kn)
    gate_head = lane & (A_HEADS - 1)
    diag = jnp.zeros((T, LANES), F32)
    for h in range(A_HEADS):
        diag = jnp.where(gate_head == h, d_cols[h], diag)

    logf = _log_sigmoid(proj(PK_F, LANES) + bf_ref[...])
    r_i = lax.broadcasted_iota(I32, (C_CHUNK, C_CHUNK), 0)
    c_i = lax.broadcasted_iota(I32, (C_CHUNK, C_CHUNK), 1)
    tri = (r_i >= c_i).astype(BF16)
    offset = carry_ref[...]
    c_chunks = []
    for ch in range(T // C_CHUNK):
        rows = logf[ch * C_CHUNK:(ch + 1) * C_CHUNK]
        local = sum(jnp.dot(tri, part.astype(BF16), preferred_element_type=F32) for part in _split3(rows))
        c_chunks.append(local + offset)
        offset = c_chunks[-1][C_CHUNK - 1:C_CHUNK, :]
    c = jnp.concatenate(c_chunks, axis=0)
    carry_ref[...] = offset
    c_ref[...] = c

    lm = lane & (A_HEAD_DIM - 1)
    grp = lm >> (A_HEADS.bit_length() - 1)
    hsel = gate_head
    first3 = grp < 3
    second3 = (grp >= 3) & (grp < F_REP)

    def parts(v):
        hi, lo, lolo = _split3(v)
        return jnp.where((grp == 0) | (grp == 3), hi, jnp.where((grp == 1) | (grp == 4), lo, lolo))

    k_aug = jnp.where(first3, -parts(c * LOG2E), 0.0)
    q_aug = jnp.where(second3, parts(c * LOG2E - diag), 0.0)

    for g in range(A_HEADS // 2):
        sl = slice(g * LANES, (g + 1) * LANES)
        zq2, zk2, zv2 = qn[:, sl], kn[:, sl], zv[:, sl]
        for par in range(2):
            h = 2 * g + par
            keep = left if par == 0 else jnp.logical_not(left)
            oh_k = (second3 & (hsel == h)).astype(F32)
            oh_q = (first3 & (hsel == h)).astype(F32)
            k_ref[h] = jnp.where(keep, zk2, k_aug + oh_k).astype(BF16)
            qT_ref[h] = jnp.where(keep, zq2, q_aug + oh_q).T.astype(BF16)
            ones_lane = A_HEAD_DIM if par == 0 else 0
            vT_ref[h] = jnp.where(keep, zv2, (lane == ones_lane).astype(F32)).T.astype(BF16)

    p = proj(PK_P, BRANCH_WIDTH)
    row8 = lax.broadcasted_iota(I32, (SUBLANES, GROUP_DIM), 0)
    pos = i * T + lax.broadcasted_iota(I32, (T, GROUP_DIM), 0)

    def shift_down(v, tail, d):
        r = pltpu.roll(v, d, 0)
        rt = pltpu.roll(tail, d, 0)
        top = jnp.where(row8 < d, rt[0:SUBLANES], r[0:SUBLANES])
        return jnp.concatenate([top, r[SUBLANES:]], axis=0)

    pooled = []
    for gi, w in enumerate(POOL_WINDOWS):
        sl = slice(gi * GROUP_DIM, (gi + 1) * GROUP_DIM)
        s = p[:, sl]
        for lv in range(gi + 1):
            tail = halo_ref[lv, :, sl]
            halo_ref[lv, :, sl] = s[T - HALO:T, :]
            s = s + shift_down(s, tail, 1 << lv)
        cnt = jnp.minimum(pos + 1, w).astype(F32)
        pooled.append(s / cnt - p[:, sl])
    pooled = jnp.concatenate(pooled, axis=1).astype(BF16)
    yb = jnp.dot(pooled, wpool_ref[...], preferred_element_type=F32) * spool_ref[...]
    yb_ref[...] = yb.astype(yb_ref.dtype)

    gu = _gelu_tanh(proj(PK_U, BRANCH_WIDTH))
    gv = _gelu_tanh(proj(PK_SV, BRANCH_WIDTH))
    t_r = lax.broadcasted_iota(I32, (C_CHUNK, C_CHUNK), 0)
    t_c = lax.broadcasted_iota(I32, (C_CHUNK, C_CHUNK), 1)
    causal = t_r >= t_c
    for g in range(N_GROUPS):
        sl = slice(g * GROUP_DIM, (g + 1) * GROUP_DIM)
        vn = _rms(gv[:, sl], gsgu_ref[:, sl]).astype(BF16)
        wc = jnp.where(causal, wsgu_ref[g], 0.0).astype(BF16)
        for ch in range(T // C_CHUNK):
            rows = slice(ch * C_CHUNK, (ch + 1) * C_CHUNK)
            mixed = jnp.dot(wc, vn[rows], preferred_element_type=F32) + bsgu_ref[g]
            yc_ref[rows, sl] = (gu[rows, sl] * mixed).astype(yc_ref.dtype)


def _proj_call(x, w, layer, T):
    B, S, D = x.shape
    H = A_HEADS
    per_layer = functools.partial(_layer_spec, layer)
    return pl.pallas_call(
        functools.partial(_proj_kernel, T=T),
        grid=(B, S // T),
        in_specs=[
            pl.BlockSpec((None, T, D), lambda b, i: (b, i, 0)),
            per_layer(1, D), per_layer(D, PK_W), per_layer(1, LANES), per_layer(1, BRANCH_WIDTH),
            per_layer(1, BRANCH_WIDTH),
            per_layer(BRANCH_WIDTH, BRANCH_WIDTH), per_layer(1, BRANCH_WIDTH), per_layer(1, BRANCH_WIDTH),
            per_layer(N_GROUPS, C_CHUNK, C_CHUNK), per_layer(N_GROUPS, C_CHUNK, GROUP_DIM),
        ],
        out_specs=[
            pl.BlockSpec((None, H, LANES, T), lambda b, i: (b, 0, 0, i)),
            pl.BlockSpec((None, H, T, LANES), lambda b, i: (b, 0, i, 0)),
            pl.BlockSpec((None, H, LANES, T), lambda b, i: (b, 0, 0, i)),
            pl.BlockSpec((None, T, BRANCH_WIDTH), lambda b, i: (b, i, 0)),
            pl.BlockSpec((None, T, BRANCH_WIDTH), lambda b, i: (b, i, 0)),
            pl.BlockSpec((None, T, LANES), lambda b, i: (b, i, 0)),
        ],
        out_shape=[
            jax.ShapeDtypeStruct((B, H, LANES, S), BF16),
            jax.ShapeDtypeStruct((B, H, S, LANES), BF16),
            jax.ShapeDtypeStruct((B, H, LANES, S), BF16),
            jax.ShapeDtypeStruct((B, S, BRANCH_WIDTH), BF16),
            jax.ShapeDtypeStruct((B, S, BRANCH_WIDTH), BF16),
            jax.ShapeDtypeStruct((B, S, LANES), F32),
        ],
        scratch_shapes=[pltpu.VMEM((1, LANES), F32), pltpu.VMEM((4, HALO, BRANCH_WIDTH), F32)],
        compiler_params=pltpu.CompilerParams(
            dimension_semantics=("arbitrary", "arbitrary"), vmem_limit_bytes=VMEM_LIMIT),
        name="proj",
    )(x, w["g_mix"], w["w_pack"], w["b_f"], w["g_q"], w["g_k"], w["w_pool"], w["s_pool"],
      w["g_sgu"], w["w_sgu"], w["b_sgu"])


def _attn_kernel(first_ref, qT_ref, k_ref, vT_ref, o_ref, m_ref, acc_ref, s_ref, *, tq, tk, n_q, online_max):
    assert tq == tk
    FIRST = 2

    def reset():
        acc_ref[...] = jnp.zeros_like(acc_ref)
        if online_max:
            m_ref[...] = jnp.full(m_ref.shape, NEG, F32)

    def scores(qi, j, slot):
        k0 = pl.multiple_of(j * tk, tk)
        q0 = pl.multiple_of(qi * tq, tq)
        for par in range(2):
            s_ref[slot, par] = jnp.dot(k_ref[par, pl.ds(k0, tk), :], qT_ref[par, :, pl.ds(q0, tq)],
                                       preferred_element_type=F32)

    def consume(qi, j, slot, masked):
        k0 = pl.multiple_of(j * tk, tk)
        for par in range(2):
            s = s_ref[slot, par]
            if masked:
                s = jnp.where(lax.broadcasted_iota(I32, (tk, tq), 0) <= lax.broadcasted_iota(I32, (tk, tq), 1),
                              s, NEG)
            v_blk = vT_ref[par, :, pl.ds(k0, tk)]
            if online_max:
                m_old = m_ref[par]
                m_new = jnp.maximum(m_old, jnp.max(s, axis=0, keepdims=True))
                p = jnp.exp2(s - m_new).astype(BF16)
                pv = jnp.dot(v_blk, p, preferred_element_type=F32)
                acc_ref[par] = acc_ref[par] * jnp.exp2(m_old - m_new) + pv
                m_ref[par] = m_new
            else:
                p = jnp.exp2(s).astype(BF16)
                acc_ref[par] += jnp.dot(v_blk, p, preferred_element_type=F32)

    def finish(qi):
        halves = []
        for par in range(2):
            acc = acc_ref[par]
            ones_row = A_HEAD_DIM if par == 0 else 0
            out_rows = slice(par * A_HEAD_DIM, (par + 1) * A_HEAD_DIM)
            halves.append(acc[out_rows, :] * (1.0 / acc[ones_row:ones_row + 1, :]))
        q0 = pl.multiple_of(qi * tq, tq)
        o_ref[pl.ds(q0, tq), :] = jnp.concatenate(halves, axis=0).T.astype(o_ref.dtype)
        reset()

    b_id = pl.program_id(0)
    g_id = pl.program_id(1)

    def first_block(qi):
        return first_ref[(b_id * pl.num_programs(1) + g_id) * n_q + qi]

    reset()
    scores(0, 0, FIRST)
    consume(0, 0, FIRST, True)
    finish(0)
    if n_q > 1:
        scores(1, first_block(1), FIRST)

    def query_block(qi, carry):
        base = first_block(qi)
        n_mid = qi - 1 - base
        nxt = jnp.minimum(qi + 1, n_q - 1)
        scores(qi, base + 1, 1)
        consume(qi, base, FIRST, False)

        def trip(t, c):
            j0 = base + 1 + ATTN_BLOCKS_PER_TRIP * t
            for u in range(ATTN_BLOCKS_PER_TRIP):
                scores(qi, j0 + u + 1, u % 2)
                consume(qi, j0 + u, (u + 1) % 2, False)
            return c

        lax.fori_loop(0, n_mid // ATTN_BLOCKS_PER_TRIP, trip, 0)
        j0 = base + 1 + (n_mid // ATTN_BLOCKS_PER_TRIP) * ATTN_BLOCKS_PER_TRIP
        for rem in range(ATTN_BLOCKS_PER_TRIP):
            @pl.when(n_mid % ATTN_BLOCKS_PER_TRIP == rem)
            def _(rem=rem):
                for u in range(rem):
                    scores(qi, j0 + u + 1, u % 2)
                    consume(qi, j0 + u, (u + 1) % 2, False)
                scores(nxt, first_block(nxt), FIRST)
                consume(qi, j0 + rem, (rem + 1) % 2, True)
                finish(qi)
        return carry

    lax.fori_loop(1, n_q, query_block, 0)


def _attn_call(first, qT, k, vT, tq, tk, online_max):
    B, H, _, S = qT.shape
    grid_spec = pltpu.PrefetchScalarGridSpec(
        num_scalar_prefetch=1,
        grid=(B, H // 2),
        in_specs=[
            pl.BlockSpec((None, 2, LANES, S), lambda b, g, first: (b, g, 0, 0)),
            pl.BlockSpec((None, 2, S, LANES), lambda b, g, first: (b, g, 0, 0)),
            pl.BlockSpec((None, 2, LANES, S), lambda b, g, first: (b, g, 0, 0)),
        ],
        out_specs=pl.BlockSpec((None, S, LANES), lambda b, g, first: (b, 0, g)),
        scratch_shapes=[pltpu.VMEM((2, 1, tq), F32), pltpu.VMEM((2, LANES, tq), F32),
                        pltpu.VMEM((3, 2, tk, tq), F32)],
    )
    return pl.pallas_call(
        functools.partial(_attn_kernel, tq=tq, tk=tk, n_q=S // tq, online_max=online_max),
        grid_spec=grid_spec,
        out_shape=jax.ShapeDtypeStruct((B, S, BRANCH_WIDTH), BF16),
        compiler_params=pltpu.CompilerParams(
            dimension_semantics=("parallel", "parallel"), vmem_limit_bytes=VMEM_LIMIT),
        name="attn_online" if online_max else "attn",
    )(first, qT, k, vT)


def _first_live_block(c, bound, tq, tk):
    c_q = c[:, 0::tq, 0:A_HEADS]
    c_k = c[:, tk - 1::tk, 0:A_HEADS]
    top = bound + LOG2E * (c_q[:, :, None, :] - c_k[:, None, :, :])
    dead = top < DEAD_SCORE
    dead_pair = dead[..., 0::2] & dead[..., 1::2]
    n_q = c_q.shape[1]
    n_dead = jnp.sum(jnp.cumprod(dead_pair.astype(I32), axis=2), axis=2)
    limit = jnp.maximum(jnp.arange(n_q, dtype=I32) - 1, 0)[None, :, None]
    first = jnp.minimum(n_dead, limit)
    return first.transpose(0, 2, 1).reshape(-1).astype(I32)


def _attention(qT, k, vT, c, g_q, g_k, tq, tk):
    bound = 16.0 * LOG2E * jnp.max(jnp.abs(g_q)) * jnp.max(jnp.abs(g_k))
    first = _first_live_block(c, bound, tq, tk)
    return lax.cond(bound <= SCORE_CAP,
                    lambda: _attn_call(first, qT, k, vT, tq, tk, False),
                    lambda: _attn_call(first, qT, k, vT, tq, tk, True))


def _merge_kernel(x_ref, ya_ref, yb_ref, yc_ref, gmix_ref, wg_ref, bg_ref, wb_ref, wo_ref,
                  gffn_ref, wr_ref, br_ref,
                  x1_ref, h2_ref, route_ref, route_t_ref, cnt_ref, carry_ref, logit_ref, *, T):
    i = pl.program_id(0)
    n_tiles = pl.num_programs(0) - 1

    @pl.when(i == 0)
    def _():
        carry_ref[...] = jnp.zeros_like(carry_ref)
        logit_ref[...] = jnp.zeros_like(logit_ref)

    def route_previous():
        live = (i > 0).astype(F32)
        logits = logit_ref[...]
        lane = lax.broadcasted_iota(I32, (T, LANES), 1).astype(F32)
        big = float(LANES)

        def first_argmax(v):
            m = jnp.max(v, axis=-1, keepdims=True)
            return m, jnp.min(jnp.where(v == m, lane, big), axis=-1, keepdims=True)

        lg = jnp.where(lane < N_EXPERT_GROUPS, logits, NEG)
        mg, grp = first_argmax(lg)
        p_grp = 1.0 / jnp.sum(jnp.exp(lg - mg), axis=-1, keepdims=True)
        lo_lane = N_EXPERT_GROUPS + grp * EXPERTS_PER_GROUP
        le = jnp.where((lane >= lo_lane) & (lane < lo_lane + EXPERTS_PER_GROUP), logits, NEG)
        m1, i1 = first_argmax(le)
        m2, i2 = first_argmax(jnp.where(lane == i1, NEG, le))
        e21 = jnp.exp(m2 - m1)
        g1 = p_grp / (1.0 + e21)
        g2 = p_grp * e21 / (1.0 + e21)
        e1 = i1 - N_EXPERT_GROUPS
        e2 = i2 - N_EXPERT_GROUPS

        oh1 = lane == e1
        oh2 = lane == e2
        sel = (oh1 | oh2).astype(F32)
        r_i = lax.broadcasted_iota(I32, (T, T), 0)
        c_i = lax.broadcasted_iota(I32, (T, T), 1)
        before = (r_i > c_i).astype(BF16)
        seen = jnp.dot(before, sel.astype(BF16), preferred_element_type=F32) + carry_ref[...]
        r1 = jnp.sum(jnp.where(oh1, seen, 0.0), axis=-1, keepdims=True)
        r2 = jnp.sum(jnp.where(oh2, seen, 0.0), axis=-1, keepdims=True)
        carry_ref[...] = carry_ref[...] + live * jnp.sum(sel, axis=0, keepdims=True)
        cnt_ref[...] = carry_ref[...]

        route = jnp.zeros((T, LANES), F32)
        for idx, val in enumerate((e1, e2, g1, g2, r1, r2)):
            route = jnp.where(lane == idx, val, route)
        route_ref[...] = route
        route_t_ref[...] = route.T[0:ROUTE_ROWS]

    @pl.when(i < n_tiles)
    def _():
        route_previous()
        x = x_ref[...]
        hb = _rms(x, gmix_ref[...]).astype(BF16)
        merged = None
        for bi, y_ref in enumerate((ya_ref, yb_ref, yc_ref)):
            sl = slice(bi * D_MODEL, (bi + 1) * D_MODEL)
            gate = jax.nn.sigmoid(jnp.dot(hb, wg_ref[:, sl], preferred_element_type=F32) + bg_ref[:, sl])
            term = gate * jnp.dot(y_ref[...], wb_ref[bi], preferred_element_type=F32)
            merged = term if merged is None else merged + term
        x1 = x + jnp.dot(merged.astype(BF16), wo_ref[...], preferred_element_type=F32)
        x1_ref[...] = x1
        h2 = _rms(x1, gffn_ref[...])
        _store_row_tiles(h2_ref, _pack_pairs(h2))

        h_hi = h2.astype(BF16)
        h_lo = (h2 - h_hi.astype(F32)).astype(BF16)
        hw = jnp.dot(h_hi, wr_ref[...], preferred_element_type=F32)
        lw = jnp.dot(h_lo, wr_ref[:, 0:LANES], preferred_element_type=F32)
        logit_ref[...] = hw[:, 0:LANES] + (hw[:, LANES:] + lw) + br_ref[...]

    @pl.when(i == n_tiles)
    def _():
        route_previous()


def _row_tile_rows(s, n_rows, row0=0):
    return pl.ds(row0 * ROW_TILE + s, n_rows, stride=ROW_TILE)


def _pack_pairs(v):
    hi = lax.bitcast_convert_type(v[:, :HALF_D].astype(BF16).astype(F32), U32)
    lo = lax.bitcast_convert_type(v[:, HALF_D:].astype(BF16).astype(F32), U32)
    return hi | (lo >> 16)


def _unpack_pairs(w):
    hi = lax.bitcast_convert_type(w & jnp.uint32(0xFFFF0000), F32)
    lo = lax.bitcast_convert_type(w << 16, F32)
    return jnp.concatenate([hi, lo], axis=-1)


def _store_row_tiles(ref, words, row0=0):
    for s in range(ROW_TILE):
        ref[_row_tile_rows(s, words.shape[0], row0), :] = words[:, s * LANES:(s + 1) * LANES]


def _load_row_tiles(ref, n_rows, row0=0):
    return jnp.concatenate([ref[_row_tile_rows(s, n_rows, row0), :] for s in range(ROW_TILE)], axis=-1)


def _merge_call(x2d, ya, yb, yc, w, layer, T):
    N, D = x2d.shape
    n_tiles = N // T
    const = functools.partial(_layer_spec, layer)
    cur = lambda i: jnp.minimum(i, n_tiles - 1)
    prev = lambda i: jnp.maximum(i - 1, 0)
    tile = lambda width: pl.BlockSpec((T, width), lambda i: (cur(i), 0))
    return pl.pallas_call(
        functools.partial(_merge_kernel, T=T),
        grid=(n_tiles + 1,),
        in_specs=[
            tile(D), tile(BRANCH_WIDTH), tile(BRANCH_WIDTH), tile(BRANCH_WIDTH),
            const(1, D), const(D, N_BRANCH * D), const(1, N_BRANCH * D),
            const(N_BRANCH, BRANCH_WIDTH, D), const(D, D), const(1, D), const(D, 2 * LANES), const(1, LANES),
        ],
        out_specs=[tile(D), pl.BlockSpec((T * ROW_TILE, LANES), lambda i: (cur(i), 0)),
                   pl.BlockSpec((T, LANES), lambda i: (prev(i), 0)),
                   pl.BlockSpec((ROUTE_ROWS, T), lambda i: (0, prev(i))),
                   pl.BlockSpec((1, LANES), lambda i: (0, 0))],
        out_shape=[
            jax.ShapeDtypeStruct((N, D), F32),
            jax.ShapeDtypeStruct((N * ROW_TILE, LANES), U32),
            jax.ShapeDtypeStruct((N, LANES), F32),
            jax.ShapeDtypeStruct((ROUTE_ROWS, N), F32),
            jax.ShapeDtypeStruct((1, LANES), F32),
        ],
        scratch_shapes=[pltpu.VMEM((1, LANES), F32), pltpu.VMEM((T, LANES), F32)],
        compiler_params=pltpu.CompilerParams(
            dimension_semantics=("arbitrary",), vmem_limit_bytes=VMEM_LIMIT),
        name="merge",
    )(x2d, ya, yb, yc, w["g_mix"], w["w_gate"], w["b_gate"], w["w_branch"], w["w_out"],
      w["g_ffn"], w["w_router"], w["b_router"])


def _dispatch_kernel(dest_ref, h2_ref, xs_ref, sem, *, T):
    def row_copy(t, d):
        return pltpu.make_async_copy(h2_ref.at[_row_tile(t)], xs_ref.at[_row_tile(d)], sem)

    def body(tb, carry):
        t0 = tb * ISSUE_UNROLL
        for u in range(ISSUE_UNROLL):
            for kk in range(TOP_K):
                row_copy(t0 + u, dest_ref[0, kk * T + t0 + u]).start(priority=kk % N_DMA_PRIORITIES)
        return carry

    lax.fori_loop(0, T // ISSUE_UNROLL, body, 0)
    for kk in range(TOP_K):
        pltpu.make_async_copy(h2_ref, xs_ref.at[pl.ds(0, T * ROW_TILE)], sem).wait()


def _row_tile(r):
    return pl.ds(pl.multiple_of(r * ROW_TILE, ROW_TILE), ROW_TILE)


def _dispatch_call(dest3, h2, T):
    N = h2.shape[0] // ROW_TILE
    return pl.pallas_call(
        functools.partial(_dispatch_kernel, T=T),
        grid=(N // T,),
        in_specs=[
            pl.BlockSpec((None, 1, TOP_K * T), lambda i: (i, 0, 0), memory_space=pltpu.SMEM),
            pl.BlockSpec((T * ROW_TILE, LANES), lambda i: (i, 0)),
        ],
        out_specs=pl.BlockSpec(memory_space=pl.ANY),
        out_shape=jax.ShapeDtypeStruct((TOP_K * N * ROW_TILE, LANES), U32),
        scratch_shapes=[pltpu.SemaphoreType.DMA],
        compiler_params=pltpu.CompilerParams(
            dimension_semantics=("arbitrary",), vmem_limit_bytes=VMEM_LIMIT),
        name="dispatch",
    )(dest3, h2)


def _gmm_kernel(blk_ref, exp_ref, lo_ref, hi_ref, xs_ref, w1_ref, w3_ref, w2_ref, y_ref,
                w13_bf, w2_bf, *, bm):
    i = pl.program_id(0)
    lo = lo_ref[i]
    hi = hi_ref[i]
    prev = jnp.maximum(i - 1, 0)
    first = jnp.logical_or(i == 0, blk_ref[i] != blk_ref[prev])
    new_expert = jnp.logical_or(i == 0, exp_ref[i] != exp_ref[prev])

    @pl.when(new_expert)
    def _():
        w13_bf[:, 0:D_EXPERT] = w1_ref[...].astype(BF16)
        w13_bf[:, D_EXPERT:] = w3_ref[...].astype(BF16)
        w2_bf[...] = w2_ref[...].astype(BF16)

    @pl.when(first)
    def _():
        y_ref[...] = jnp.zeros_like(y_ref)

    @pl.when(hi > lo)
    def _():
        n_part = 2
        part = bm // n_part
        for h in range(n_part):
            x = _unpack_pairs(_load_row_tiles(xs_ref, part, h * part)).astype(BF16)
            ab = jnp.dot(x, w13_bf[...], preferred_element_type=F32)
            a = ab[:, 0:D_EXPERT]
            b = ab[:, D_EXPERT:]
            mid = (a * jax.nn.sigmoid(a) * b).astype(BF16)
            out = _pack_pairs(jnp.dot(mid, w2_bf[...], preferred_element_type=F32))
            row = h * part + lax.broadcasted_iota(I32, (part, HALF_D), 0)
            mine = (row >= lo) & (row < hi)
            old = _load_row_tiles(y_ref, part, h * part)
            _store_row_tiles(y_ref, jnp.where(mine, out, old), h * part)


def _gmm_call(plan, xs, w1, w3, w2, layer, bm):
    A = xs.shape[0] // ROW_TILE
    D = D_MODEL
    n_items = plan[0].shape[0]
    grid_spec = pltpu.PrefetchScalarGridSpec(
        num_scalar_prefetch=4,
        grid=(n_items,),
        in_specs=[
            pl.BlockSpec((bm * ROW_TILE, LANES), lambda i, blk, ex, lo, hi: (blk[i], 0)),
            pl.BlockSpec((None, None, D, D_EXPERT), lambda i, blk, ex, lo, hi: (layer, ex[i], 0, 0)),
            pl.BlockSpec((None, None, D, D_EXPERT), lambda i, blk, ex, lo, hi: (layer, ex[i], 0, 0)),
            pl.BlockSpec((None, None, D_EXPERT, D), lambda i, blk, ex, lo, hi: (layer, ex[i], 0, 0)),
        ],
        out_specs=pl.BlockSpec((bm * ROW_TILE, LANES), lambda i, blk, ex, lo, hi: (blk[i], 0)),
        scratch_shapes=[pltpu.VMEM((D, 2 * D_EXPERT), BF16), pltpu.VMEM((D_EXPERT, D), BF16)],
    )
    return pl.pallas_call(
        functools.partial(_gmm_kernel, bm=bm),
        grid_spec=grid_spec,
        out_shape=jax.ShapeDtypeStruct((A * ROW_TILE, LANES), U32),
        compiler_params=pltpu.CompilerParams(
            dimension_semantics=("arbitrary",), vmem_limit_bytes=VMEM_LIMIT),
        name="gmm",
    )(*plan, xs, w1, w3, w2)


def _gmm_plan(counts, n_rows, bm):
    n_blk = n_rows // bm
    n_items = n_blk + N_EXPERTS - 1
    ends = jnp.cumsum(counts)
    starts = ends - counts
    first_blk = starts // bm
    n_it = jnp.where(counts > 0, (ends - 1) // bm - first_blk + 1, 0)
    item_end = jnp.cumsum(n_it)
    item_start = item_end - n_it
    total = item_end[-1]
    ids = jnp.arange(n_items, dtype=I32)
    valid = ids < total
    ex = jnp.minimum(jnp.sum(item_end[None, :] <= ids[:, None], axis=1).astype(I32), N_EXPERTS - 1)
    ex = jnp.where(valid, ex, ex[jnp.maximum(total - 1, 0)])
    blk = jnp.where(valid, first_blk[ex] + ids - item_start[ex], n_blk - 1)
    lo = jnp.where(valid, jnp.maximum(starts[ex], blk * bm) - blk * bm, 0)
    hi = jnp.where(valid, jnp.minimum(ends[ex], (blk + 1) * bm) - blk * bm, 0)
    return blk.astype(I32), ex.astype(I32), lo.astype(I32), hi.astype(I32)


COMBINE_SLOTS = 3
COMBINE_ROWS = 32


def _combine_kernel(dest_ref, dest_next_ref, dest_ahead_ref, x1_ref, route_ref, y_hbm, o_ref, buf, sem, *, T):
    i = pl.program_id(0)
    n_steps = pl.num_programs(0)
    n_trips = T // COMBINE_ROWS

    def issue_rows(d_ref, slot, t0):
        for u in range(COMBINE_ROWS):
            for kk in range(TOP_K):
                d = d_ref[0, kk * T + t0 + u]
                pltpu.make_async_copy(y_hbm.at[_row_tile(d)], buf.at[slot * TOP_K + kk, _row_tile(t0 + u)],
                                      sem.at[slot]).start(priority=kk % N_DMA_PRIORITIES)

    def combine_rows(slot, t0):
        rows = pl.ds(t0, COMBINE_ROWS)
        g1 = route_ref[rows, 2:3]
        g2 = route_ref[rows, 3:4]
        for s in range(ROW_TILE):
            words = pl.ds(t0 * ROW_TILE + s, COMBINE_ROWS, stride=ROW_TILE)
            y1 = _unpack_pairs(buf[slot * TOP_K, words, :])
            y2 = _unpack_pairs(buf[slot * TOP_K + 1, words, :])
            for half in range(2):
                src = slice(half * LANES, (half + 1) * LANES)
                dst = slice(half * HALF_D + s * LANES, half * HALF_D + (s + 1) * LANES)
                o_ref[rows, dst] = x1_ref[rows, dst] + (g1 * y1[:, src] + g2 * y2[:, src])

    def loop(body):
        def trip(tb, carry):
            body(pl.multiple_of(tb * COMBINE_ROWS, COMBINE_ROWS))
            return carry

        lax.fori_loop(0, n_trips, trip, 0)

    @pl.when(i == 0)
    def _():
        loop(lambda t0: issue_rows(dest_ref, 0, t0))

        @pl.when(n_steps > 1)
        def _():
            loop(lambda t0: issue_rows(dest_next_ref, 1, t0))

    for slot in range(COMBINE_SLOTS):
        @pl.when(i % COMBINE_SLOTS == slot)
        def _(slot=slot):
            for kk in range(TOP_K):
                pltpu.make_async_copy(y_hbm.at[pl.ds(0, T * ROW_TILE)], buf.at[slot * TOP_K + kk],
                                      sem.at[slot]).wait()
            ahead = (slot + 2) % COMBINE_SLOTS

            @pl.when(i + 2 < n_steps)
            def _():
                def both(t0):
                    issue_rows(dest_ahead_ref, ahead, t0)
                    combine_rows(slot, t0)

                loop(both)

            @pl.when(i + 2 >= n_steps)
            def _():
                loop(lambda t0: combine_rows(slot, t0))


def _combine_call(dest3, x1, route, y, T):
    N, D = x1.shape
    n_steps = N // T
    dest_spec = lambda ahead: pl.BlockSpec(
        (None, 1, TOP_K * T), lambda i: (jnp.minimum(i + ahead, n_steps - 1), 0, 0), memory_space=pltpu.SMEM)
    return pl.pallas_call(
        functools.partial(_combine_kernel, T=T),
        grid=(n_steps,),
        in_specs=[
            dest_spec(0), dest_spec(1), dest_spec(2),
            pl.BlockSpec((T, D), lambda i: (i, 0)),
            pl.BlockSpec((T, LANES), lambda i: (i, 0)),
            pl.BlockSpec(memory_space=pl.ANY),
        ],
        out_specs=pl.BlockSpec((T, D), lambda i: (i, 0)),
        out_shape=jax.ShapeDtypeStruct((N, D), F32),
        scratch_shapes=[pltpu.VMEM((COMBINE_SLOTS * TOP_K, T * ROW_TILE, LANES), U32),
                        pltpu.SemaphoreType.DMA((COMBINE_SLOTS,))],
        compiler_params=pltpu.CompilerParams(
            dimension_semantics=("arbitrary",), vmem_limit_bytes=VMEM_LIMIT),
        name="combine",
    )(dest3, dest3, dest3, x1, route, y)


def _block_diag(blocks):
    n = blocks.shape[1]
    eye = jnp.eye(n, dtype=blocks.dtype)
    out = blocks[:, :, :, None, :] * eye[None, :, None, :, None]
    return out.reshape(blocks.shape[0], n * blocks.shape[2], n * blocks.shape[3])


def _rep_forget(cols):
    lead = cols.shape[:-1]
    half = jnp.concatenate(
        [jnp.tile(cols, (1,) * len(lead) + (F_REP,)), jnp.zeros(lead + (A_HEAD_DIM - F_REP * A_HEADS,), cols.dtype)],
        axis=-1)
    return jnp.concatenate([half, half], axis=-1)


def _prep_all(g_mix, w_in, b_fgate, b_gate, g_q, g_k, w_pool, s_pool, g_sgu, w_sgu, b_sgu,
              w_branch, w_out, g_ffn, w_rg, b_rg, w_re, b_re):
    L = w_in.shape[0]
    order = jnp.argsort(b_fgate, axis=1)
    qkv = w_in[:, :, 0:OFF_F].reshape(L, D_MODEL, 3, A_HEADS, A_HEAD_DIM)
    qkv = jnp.take_along_axis(qkv, order[:, None, None, :, None], axis=3).reshape(L, D_MODEL, OFF_F)
    w_f = jnp.take_along_axis(w_in[:, :, OFF_F:OFF_P], order[:, None, :], axis=2)
    b_f = jnp.take_along_axis(b_fgate, order, axis=1)
    wb0 = w_branch[:, 0].reshape(L, A_HEADS, A_HEAD_DIM, D_MODEL)
    wb0 = jnp.take_along_axis(wb0, order[:, :, None, None], axis=1).reshape(L, 1, BRANCH_WIDTH, D_MODEL)
    pad_r = LANES - N_EXPERT_GROUPS - N_EXPERTS
    w_r = jnp.concatenate([w_rg, w_re, jnp.zeros((L, D_MODEL, pad_r), F32)], axis=2)
    w_r_hi = w_r.astype(BF16)
    w_r_lo = (w_r - w_r_hi.astype(F32)).astype(BF16)
    row = lambda v: v.reshape(L, 1, v.shape[-1])
    return dict(
        g_mix=row(g_mix),
        w_pack=jnp.concatenate([qkv, _rep_forget(w_f), w_in[:, :, OFF_P:OFF_G]], axis=2).astype(BF16),
        b_f=_rep_forget(row(b_f)),
        g_q=row(jnp.tile(g_q, (1, A_HEADS))),
        g_k=row(jnp.tile(g_k, (1, A_HEADS))),
        w_pool=_block_diag(w_pool).astype(BF16),
        s_pool=row(s_pool),
        g_sgu=row(g_sgu),
        w_sgu=w_sgu,
        b_sgu=jnp.broadcast_to(b_sgu[:, :, :, None], (L, N_GROUPS, C_CHUNK, GROUP_DIM)),
        w_gate=w_in[:, :, OFF_G:].astype(BF16),
        b_gate=b_gate.reshape(L, 1, N_BRANCH * D_MODEL),
        w_branch=jnp.concatenate([wb0, w_branch[:, 1:]], axis=1).astype(BF16),
        w_out=w_out.astype(BF16),
        g_ffn=row(g_ffn),
        w_router=jnp.concatenate([w_r_hi, w_r_lo], axis=2),
        b_router=row(jnp.concatenate([b_rg, b_re, jnp.zeros((L, pad_r), F32)], axis=1)),
    )


def kernel(x, g_mix, w_in, b_fgate, b_gate, g_q, g_k, w_pool, s_pool, g_sgu, w_sgu, b_sgu, w_branch, w_out,
           g_ffn, w_rg, b_rg, w_re, b_re, w1, w3, w2):
    B, S, D = x.shape
    assert D == D_MODEL and x.dtype == F32
    N = B * S
    T, tq, tk, bm, tm = _tiles(S)
    assert S % T == 0 and S % tq == 0 and tq == tk and T % C_CHUNK == 0
    assert (TOP_K * N) % bm == 0 and N % tm == 0 and tm % ISSUE_UNROLL == 0 and tm % COMBINE_ROWS == 0
    depth = w_in.shape[0]
    w = _prep_all(g_mix, w_in, b_fgate, b_gate, g_q, g_k, w_pool, s_pool, g_sgu, w_sgu, b_sgu,
                  w_branch, w_out, g_ffn, w_rg, b_rg, w_re, b_re)
    for l in range(depth):
        qT, k, vT, yb, yc, c = _proj_call(x, w, l, T)
        ya = _attention(qT, k, vT, c, g_q[l], g_k[l], tq, tk)
        x1, h2, route, route_t, cnt = _merge_call(
            x.reshape(N, D), ya.reshape(N, -1), yb.reshape(N, -1), yc.reshape(N, -1), w, l, T)
        counts = cnt[0, :N_EXPERTS].astype(I32)
        starts = jnp.cumsum(counts) - counts
        experts = route_t[0:TOP_K].astype(I32)
        expert_ids = jnp.arange(N_EXPERTS, dtype=I32)[:, None, None]
        start_of = jnp.sum(jnp.where(experts[None] == expert_ids, starts[:, None, None], 0), axis=0)
        dest = start_of + route_t[4:4 + TOP_K].astype(I32)
        dest3 = dest.reshape(TOP_K, N // tm, tm).transpose(1, 0, 2).reshape(N // tm, 1, TOP_K * tm)
        xs = _dispatch_call(dest3, h2, tm)
        y = _gmm_call(_gmm_plan(counts, TOP_K * N, bm), xs, w1, w3, w2, l, bm)
        x = _combine_call(dest3, x1, route, y, tm).reshape(B, S, D)
    return x
```
